```python
import math
import jax, jax.numpy as jnp
from jax import lax
import numpy as np

D_MODEL = 1024
BATCH = 8
SEQ = 4096
DEPTH = 1
DEC_BATCH = 32
DEC_SEQ = 4
PAST_LEN = 16384
PAGE_SIZE = 128

ATTN_GROUPS = ((128, 1), (512, 4), (2048, 16))
N_GROUPS = 3
HEADS_PER_GROUP = 4
HEAD_DIM = 64
N_ATTN_HEADS = N_GROUPS * HEADS_PER_GROUP
ATTN_QKV_W = 3 * N_ATTN_HEADS * HEAD_DIM
ATTN_OUT_W = HEADS_PER_GROUP * HEAD_DIM

RWKV_HEADS = 8
RWKV_HEAD = 64
RWKV_W = RWKV_HEADS * RWKV_HEAD
DECAY_LORA = 64
AAA_LORA = 64
GATE_LORA = 128
RWKV_COLS = 3 * RWKV_W + DECAY_LORA + AAA_LORA + GATE_LORA
GN_EPS = 64e-5

GATE_COLS = 2 * D_MODEL
IN_COLS = ATTN_QKV_W + RWKV_COLS + GATE_COLS

N_EXPERTS = 32
TOP_K = 4
D_EXPERT = D_MODEL
SWIGLU_LIMIT = 7.0
SWIGLU_ALPHA = 1.702
MOE_BLOCK = 256

PLE_DIM = 256
RMS_EPS = 1e-6

kernel_name = 'dilated_attn_rwkv7_moe_hybrid_step'

F32 = jnp.float32


def rms_norm(x, g):
    xf = x.astype(F32)
    y = xf * lax.rsqrt(jnp.mean(xf * xf, axis=-1, keepdims=True) + RMS_EPS)
    return (y * g.astype(F32)).astype(x.dtype)


def alibi_slopes():
    h = jnp.arange(1, N_ATTN_HEADS + 1, dtype=F32)
    return (2.0 ** (-8.0 * h / N_ATTN_HEADS)).reshape(N_GROUPS, HEADS_PER_GROUP)


def dilated_band_attn_prompt(q, k, v, window, dilation, slopes):
    B, S, H, Dh = q.shape
    band = window // dilation
    blk = band
    unit = dilation * blk
    Sp = -(-S // unit) * unit
    L = Sp // dilation
    nb = L // blk

    def to_blocks(t):
        t = jnp.pad(t, ((0, 0), (0, Sp - S), (0, 0), (0, 0)))
        t = jnp.swapaxes(t.reshape(B, L, dilation, H, Dh), 1, 2)
        return t.reshape(B * dilation, nb, blk, H, Dh)

    def with_prev(t):
        prev = jnp.concatenate([jnp.zeros_like(t[:, :1]), t[:, :-1]], axis=1)
        return jnp.concatenate([prev, t], axis=2)

    qb = to_blocks(q)
    kk = with_prev(to_blocks(k))
    vv = with_prev(to_blocks(v))
    s = jnp.einsum('znqhd,znkhd->znhqk', qb, kk) / math.sqrt(Dh)
    qi = jnp.arange(blk) + blk
    ki = jnp.arange(2 * blk)
    rel = qi[:, None] - ki[None, :]
    key_pos = jnp.arange(nb)[:, None] * blk - blk + ki[None, :]
    valid = ((rel >= 0) & (rel <= band))[None] & (key_pos >= 0)[:, None, :]
    bias = -slopes[:, None, None] * (dilation * rel).astype(F32)[None]
    s = jnp.where(valid[None, :, None], s + bias[None, None], -jnp.inf)
    m = jnp.max(s, axis=-1, keepdims=True)
    e = jnp.exp(s - m)
    den = jnp.sum(e, axis=-1, keepdims=True)
    o = jnp.einsum('znhqk,znkhd->znqhd', e, vv) / jnp.swapaxes(den, 2, 3)
    lse = jnp.swapaxes((m + jnp.log(den))[..., 0], 2, 3)

    def from_blocks(t):
        t = jnp.swapaxes(t.reshape((B, dilation, L) + t.shape[3:]), 1, 2)
        return t.reshape((B, Sp) + t.shape[3:])[:, :S]

    return from_blocks(o), from_blocks(lse)


def dilated_attn_sample(q, k_all, v_all, window, dilation, slopes):
    B, T, H, Dh = q.shape
    Wc = k_all.shape[1] - T
    band = window // dilation
    steps = jnp.arange(band + 1)
    idx = Wc + jnp.arange(T)[:, None] - dilation * steps[None, :]
    valid = idx >= 0
    idx_c = jnp.maximum(idx, 0)
    kg = k_all[:, idx_c]
    vg = v_all[:, idx_c]
    s = jnp.einsum('bthd,btkhd->bhtk', q, kg) / math.sqrt(Dh)
    s = s - slopes[:, None, None] * (dilation * steps).astype(F32)[None, None, :]
    s = jnp.where(valid[None, None], s, -jnp.inf)
    m = jnp.max(s, axis=-1, keepdims=True)
    e = jnp.exp(s - m)
    den = jnp.sum(e, axis=-1, keepdims=True)
    o = jnp.einsum('bhtk,btkhd->bthd', e, vg) / jnp.swapaxes(den, 1, 2)
    lse = jnp.swapaxes((m + jnp.log(den))[..., 0], 1, 2)
    return o, lse


def wkv_scan(S0, r, w, k, v, a, b):
    xs = tuple(jnp.moveaxis(t, 1, 0) for t in (r, w, k, v, a, b))

    def step(S, inp):
        r_t, w_t, k_t, v_t, a_t, b_t = inp
        sa = jnp.einsum('bhij,bhj->bhi', S, a_t)
        S = S * w_t[:, :, None, :] + sa[..., None] * b_t[:, :, None, :] + v_t[..., None] * k_t[:, :, None, :]
        return S, jnp.einsum('bhij,bhj->bhi', S, r_t)

    S, ys = lax.scan(step, S0, xs)
    return jnp.moveaxis(ys, 0, 1), S


def rwkv7_mix(zr, shift0, wkv0, lp):
    B, T, _ = zr.shape
    prev = jnp.concatenate([shift0[:, None].astype(F32), zr[:, :-1]], axis=1)
    zm = zr + (prev - zr) * lp['rwkv_mu']
    cuts = [RWKV_W, 2 * RWKV_W, 3 * RWKV_W, 3 * RWKV_W + DECAY_LORA, 3 * RWKV_W + DECAY_LORA + AAA_LORA]
    r, k, v, xw, xa, xg = jnp.split(zm, cuts, axis=-1)
    w_log = -jax.nn.softplus(-(lp['rwkv_w0'] + jnp.tanh(xw) @ lp['rwkv_w2'])) - 0.5
    decay = jnp.exp(-jnp.exp(w_log))
    a = jax.nn.sigmoid(lp['rwkv_a0'] + xa @ lp['rwkv_a2'])
    g = jax.nn.sigmoid(xg) @ lp['rwkv_g2']

    def heads(t):
        return t.reshape(B, T, RWKV_HEADS, RWKV_HEAD)

    kk = heads(k * lp['rwkv_k_k'])
    kk = kk / jnp.maximum(jnp.sqrt(jnp.sum(kk * kk, axis=-1, keepdims=True)), 1e-12)
    k = k * (1.0 + (a - 1.0) * lp['rwkv_k_a'])
    r_h, k_h, v_h, w_h, a_h = heads(r), heads(k), heads(v), heads(decay), heads(a)
    y, wkv = wkv_scan(wkv0.astype(F32), r_h, w_h, k_h, v_h, -kk, kk * a_h)
    mu = jnp.mean(y, axis=-1, keepdims=True)
    var = jnp.mean((y - mu) ** 2, axis=-1, keepdims=True)
    y = ((y - mu) * lax.rsqrt(var + GN_EPS)).reshape(B, T, RWKV_W) * lp['rwkv_ln_w'] + lp['rwkv_ln_b']
    y = y + (jnp.sum(r_h * k_h * lp['rwkv_r_k'], axis=-1, keepdims=True) * v_h).reshape(B, T, RWKV_W)
    return y * g, zr[:, -1], wkv


def moe_ffn(h, router_w, router_b, w1, b1, w2, b2):
    B, T, D = h.shape
    xt = h.reshape(-1, D)
    n_tok = xt.shape[0]
    logits = xt.astype(F32) @ router_w.astype(F32) + router_b.astype(F32)
    top_v, top_e = lax.top_k(logits, TOP_K)
    gates = jax.nn.softmax(top_v, axis=-1)
    n_asg = n_tok * TOP_K
    flat_e = top_e.reshape(-1)
    flat_tok = jnp.arange(n_asg, dtype=jnp.int32) // TOP_K
    order = jnp.argsort(flat_e)
    se = flat_e[order]
    counts = jnp.bincount(flat_e, length=N_EXPERTS)
    padded = (counts + MOE_BLOCK - 1) // MOE_BLOCK * MOE_BLOCK
    pad_end = jnp.cumsum(padded)
    start = jnp.cumsum(counts) - counts
    dest = (pad_end - padded)[se] + jnp.arange(n_asg) - start[se]
    n_rows = -(-(n_asg + N_EXPERTS * (MOE_BLOCK - 1)) // MOE_BLOCK) * MOE_BLOCK
    n_blocks = n_rows // MOE_BLOCK
    row_tok = jnp.full((n_rows,), n_tok, jnp.int32).at[dest].set(flat_tok[order])
    row_gate = jnp.zeros((n_rows,), F32).at[dest].set(gates.reshape(-1)[order])
    block_e = jnp.minimum(jnp.searchsorted(pad_end, jnp.arange(n_blocks) * MOE_BLOCK, side='right'), N_EXPERTS - 1)
    x_rows = jnp.concatenate([xt, jnp.zeros((1, D), xt.dtype)], axis=0)[row_tok].reshape(n_blocks, MOE_BLOCK, D)

    def expert_block(args):
        xb, e = args
        u = xb @ w1[e] + b1[e]
        glu = jnp.minimum(u[:, :D_EXPERT], SWIGLU_LIMIT)
        lin = jnp.clip(u[:, D_EXPERT:], -SWIGLU_LIMIT, SWIGLU_LIMIT)
        act = glu * jax.nn.sigmoid(SWIGLU_ALPHA * glu) * (lin + 1.0)
        return act @ w2[e] + b2[e]

    y_rows = lax.map(expert_block, (x_rows, block_e)).reshape(n_rows, D).astype(F32)
    y = jnp.zeros((n_tok + 1, D), F32).at[row_tok].add(y_rows * row_gate[:, None])[:n_tok]
    return y.reshape(B, T, D).astype(h.dtype)


def trunk_layer(x, pe, kv_bufs, shift0, wkv0, lp, prompt):
    B, T, _ = x.shape
    h = rms_norm(x, lp['norm_mix_g'])
    z = (h @ lp['w_in']).astype(F32)
    z_att, z_rwkv, z_gate = jnp.split(z, [ATTN_QKV_W, ATTN_QKV_W + RWKV_COLS], axis=-1)
    qkv = z_att.reshape(B, T, 3, N_GROUPS, HEADS_PER_GROUP, HEAD_DIM)
    slopes = alibi_slopes()
    outs, lses, new_bufs = [], [], []
    for gi, (window, dil) in enumerate(ATTN_GROUPS):
        q, k, v = qkv[:, :, 0, gi], qkv[:, :, 1, gi], qkv[:, :, 2, gi]
        kv_new = jnp.stack([k, v], axis=2)
        if prompt:
            o, lse = dilated_band_attn_prompt(q, k, v, window, dil, slopes[gi])
            new_bufs.append(kv_new[:, T - min(window, T):].astype(x.dtype))
        else:
            kv_all = jnp.concatenate([kv_bufs[gi].astype(F32), kv_new], axis=1)
            o, lse = dilated_attn_sample(q, kv_all[:, :, 0], kv_all[:, :, 1], window, dil, slopes[gi])
            new_bufs.append(kv_all[:, T:].astype(kv_bufs[gi].dtype))
        outs.append(o)
        lses.append(lse)
    w_grp = jax.nn.softmax(jnp.stack(lses, axis=0), axis=0)
    o_att = jnp.sum(w_grp[..., None] * jnp.stack(outs, axis=0), axis=0).reshape(B, T, ATTN_OUT_W)
    o_rwkv, shift_new, wkv_new = rwkv7_mix(z_rwkv, shift0, wkv0, lp)
    gate_a = jax.nn.sigmoid(z_gate[..., :D_MODEL])
    gate_b = jax.nn.sigmoid(z_gate[..., D_MODEL:])
    merged = gate_a * (o_att @ lp['w_out_attn']) + gate_b * (o_rwkv @ lp['w_out_rwkv'])
    x = x + (merged @ lp['w_out']).astype(x.dtype)
    x = x + moe_ffn(rms_norm(x, lp['norm_ffn_g']), lp['router_w'], lp['router_b'],
                    lp['moe_w1'], lp['moe_b1'], lp['moe_w2'], lp['moe_b2'])
    ple_gate = jax.nn.sigmoid(rms_norm(x, lp['norm_ple_g']) @ lp['w_ple_gate'])
    x = x + (ple_gate * (pe @ lp['w_ple'])).astype(x.dtype)
    return x, new_bufs, shift_new, wkv_new


def setup_inputs(seed: int = 0) -> dict:
    key = jax.random.key(seed)
    ks = iter(jax.random.split(key, 48))

    def nrm(shape, scale):
        return scale * jax.random.normal(next(ks), shape, F32)

    def gain(shape):
        return 1.0 + 0.02 * jax.random.normal(next(ks), shape, F32)

    wc = [min(w, PAST_LEN) for w, _ in ATTN_GROUPS]
    return {
        'x_prompt': nrm((BATCH, SEQ, D_MODEL), 1.0),
        'x_sample': nrm((DEC_BATCH, DEC_SEQ, D_MODEL), 1.0),
        'p_prompt': nrm((DEPTH, BATCH, SEQ, PLE_DIM), 1.0),
        'p_sample': nrm((DEPTH, DEC_BATCH, DEC_SEQ, PLE_DIM), 1.0),
        'cache_kv_w128': nrm((DEPTH, DEC_BATCH, wc[0], 2, HEADS_PER_GROUP, HEAD_DIM), 1.0),
        'cache_kv_w512': nrm((DEPTH, DEC_BATCH, wc[1], 2, HEADS_PER_GROUP, HEAD_DIM), 1.0),
        'cache_kv_w2048': nrm((DEPTH, DEC_BATCH, wc[2], 2, HEADS_PER_GROUP, HEAD_DIM), 1.0),
        'state_rwkv_shift': nrm((DEPTH, DEC_BATCH, RWKV_COLS), 1.0),
        'state_rwkv_wkv': nrm((DEPTH, DEC_BATCH, RWKV_HEADS, RWKV_HEAD, RWKV_HEAD), 0.5),
        'norm_mix_g': gain((DEPTH, D_MODEL)),
        'w_in': nrm((DEPTH, D_MODEL, IN_COLS), D_MODEL ** -0.5),
        'rwkv_mu': jax.random.uniform(next(ks), (DEPTH, RWKV_COLS), F32),
        'rwkv_w0': jax.random.uniform(next(ks), (DEPTH, RWKV_W), F32, -4.0, 0.0),
        'rwkv_w2': nrm((DEPTH, DECAY_LORA, RWKV_W), 0.1),
        'rwkv_a0': nrm((DEPTH, RWKV_W), 0.5),
        'rwkv_a2': nrm((DEPTH, AAA_LORA, RWKV_W), 0.1),
        'rwkv_g2': nrm((DEPTH, GATE_LORA, RWKV_W), GATE_LORA ** -0.5),
        'rwkv_k_k': 0.85 + nrm((DEPTH, RWKV_W), 0.02),
        'rwkv_k_a': gain((DEPTH, RWKV_W)),
        'rwkv_r_k': nrm((DEPTH, RWKV_HEADS, RWKV_HEAD), 0.1),
        'rwkv_ln_w': gain((DEPTH, RWKV_W)),
        'rwkv_ln_b': nrm((DEPTH, RWKV_W), 0.01),
        'w_out_attn': nrm((DEPTH, ATTN_OUT_W, D_MODEL), ATTN_OUT_W ** -0.5),
        'w_out_rwkv': nrm((DEPTH, RWKV_W, D_MODEL), RWKV_W ** -0.5),
        'w_out': nrm((DEPTH, D_MODEL, D_MODEL), D_MODEL ** -0.5),
        'norm_ffn_g': gain((DEPTH, D_MODEL)),
        'router_w': nrm((DEPTH, D_MODEL, N_EXPERTS), D_MODEL ** -0.5),
        'router_b': nrm((DEPTH, N_EXPERTS), 0.01),
        'moe_w1': nrm((DEPTH, N_EXPERTS, D_MODEL, 2 * D_EXPERT), D_MODEL ** -0.5),
        'moe_b1': nrm((DEPTH, N_EXPERTS, 2 * D_EXPERT), 0.01),
        'moe_w2': nrm((DEPTH, N_EXPERTS, D_EXPERT, D_MODEL), D_EXPERT ** -0.5),
        'moe_b2': nrm((DEPTH, N_EXPERTS, D_MODEL), 0.01),
        'norm_ple_g': gain((DEPTH, D_MODEL)),
        'w_ple': nrm((DEPTH, PLE_DIM, D_MODEL), PLE_DIM ** -0.5),
        'w_ple_gate': nrm((DEPTH, D_MODEL, D_MODEL), D_MODEL ** -0.5),
        'norm_final_g': gain((D_MODEL,)),
    }


def reference(x_prompt, x_sample, p_prompt, p_sample, cache_kv_w128, cache_kv_w512, cache_kv_w2048,
              state_rwkv_shift, state_rwkv_wkv, norm_mix_g, w_in, rwkv_mu, rwkv_w0, rwkv_w2, rwkv_a0,
              rwkv_a2, rwkv_g2, rwkv_k_k, rwkv_k_a, rwkv_r_k, rwkv_ln_w, rwkv_ln_b, w_out_attn,
              w_out_rwkv, w_out, norm_ffn_g, router_w, router_b, moe_w1, moe_b1, moe_w2, moe_b2,
              norm_ple_g, w_ple, w_ple_gate, norm_final_g):
    xp, xs = x_prompt, x_sample
    Bp = x_prompt.shape[0]
    kvp = [[], [], []]
    kvs = [[], [], []]
    shp, wkp, shs, wks = [], [], [], []
    for i in range(DEPTH):
        lp = {
            'norm_mix_g': norm_mix_g[i], 'w_in': w_in[i], 'rwkv_mu': rwkv_mu[i], 'rwkv_w0': rwkv_w0[i],
            'rwkv_w2': rwkv_w2[i], 'rwkv_a0': rwkv_a0[i], 'rwkv_a2': rwkv_a2[i], 'rwkv_g2': rwkv_g2[i],
            'rwkv_k_k': rwkv_k_k[i], 'rwkv_k_a': rwkv_k_a[i], 'rwkv_r_k': rwkv_r_k[i],
            'rwkv_ln_w': rwkv_ln_w[i], 'rwkv_ln_b': rwkv_ln_b[i], 'w_out_attn': w_out_attn[i],
            'w_out_rwkv': w_out_rwkv[i], 'w_out': w_out[i], 'norm_ffn_g': norm_ffn_g[i],
            'router_w': router_w[i], 'router_b': router_b[i], 'moe_w1': moe_w1[i], 'moe_b1': moe_b1[i],
            'moe_w2': moe_w2[i], 'moe_b2': moe_b2[i], 'norm_ple_g': norm_ple_g[i], 'w_ple': w_ple[i],
            'w_ple_gate': w_ple_gate[i],
        }
        shift0 = jnp.zeros((Bp, RWKV_COLS), F32)
        wkv0 = jnp.zeros((Bp, RWKV_HEADS, RWKV_HEAD, RWKV_HEAD), F32)
        xp, bufs_p, sh_p, wk_p = trunk_layer(xp, p_prompt[i], None, shift0, wkv0, lp, True)
        xs, bufs_s, sh_s, wk_s = trunk_layer(
            xs, p_sample[i], [cache_kv_w128[i], cache_kv_w512[i], cache_kv_w2048[i]],
            state_rwkv_shift[i], state_rwkv_wkv[i], lp, False)
        for gi in range(N_GROUPS):
            kvp[gi].append(bufs_p[gi])
            kvs[gi].append(bufs_s[gi])
        shp.append(sh_p)
        wkp.append(wk_p)
        shs.append(sh_s)
        wks.append(wk_s)
    y_prompt = rms_norm(xp, norm_final_g)
    y_sample = rms_norm(xs, norm_final_g)
    kv128_p, kv512_p, kv2048_p = jnp.stack(kvp[0]), jnp.stack(kvp[1]), jnp.stack(kvp[2])
    kv128_s, kv512_s, kv2048_s = jnp.stack(kvs[0]), jnp.stack(kvs[1]), jnp.stack(kvs[2])
    shift_p, wkv_p = jnp.stack(shp), jnp.stack(wkp)
    shift_s, wkv_s = jnp.stack(shs), jnp.stack(wks)
    return (y_prompt, y_sample, kv128_p, kv512_p, kv2048_p, shift_p, wkv_p,
            kv128_s, kv512_s, kv2048_s, shift_s, wkv_s)
```

```python
import functools
import math

import numpy as np
import jax
import jax.numpy as jnp
from jax import lax
from jax.experimental import pallas as pl
from jax.experimental.pallas import tpu as pltpu

F32 = jnp.float32
BF16 = jnp.bfloat16

D_MODEL = 1024
N_GROUPS = 3
HEADS_PER_GROUP = 4
HEAD_DIM = 64
ATTN_GROUPS = ((128, 1), (512, 4), (2048, 16))
GROUP_W = HEADS_PER_GROUP * HEAD_DIM
Q_COLS = N_GROUPS * GROUP_W
KV_COLS = 2 * Q_COLS
BAND = 128

RWKV_HEADS = 8
RWKV_HEAD = 64
RWKV_W = 512
LORA_W = 128
GATE_LORA = 128
RWKV_COLS = 3 * RWKV_W + LORA_W + GATE_LORA
GN_EPS = 64e-5
GATE_COLS = 2 * D_MODEL
Z_RWKV0 = 3 * Q_COLS
Z_GATE0 = Z_RWKV0 + RWKV_COLS
IN_COLS = Z_GATE0 + GATE_COLS

N_EXPERTS = 32
TOP_K = 4
SWIGLU_LIMIT = 7.0
SWIGLU_ALPHA = 1.702
MOE_BLOCK = 256
IDX_REC = 1024
PLE_DIM = 256
RMS_EPS = 1e-6

NEG_BIG = -1e30
VMEM_LIMIT = 56 * 1024 * 1024


def _cparams(sem):
    return pltpu.CompilerParams(dimension_semantics=sem, vmem_limit_bytes=VMEM_LIMIT)


def _rms(x, g):
    return x * lax.rsqrt(jnp.mean(x * x, axis=-1, keepdims=True) + RMS_EPS) * g


def _sigmoid(x):
    return 1.0 / (1.0 + jnp.exp(-x))


def _dot(a, b):
    return jnp.dot(a, b, preferred_element_type=F32)


def _dot_nt(a, b):
    return lax.dot_general(a, b, (((1,), (1,)), ((), ())), preferred_element_type=F32)


def _dot_tn(a, b):
    return lax.dot_general(a, b, (((0,), (0,)), ((), ())), preferred_element_type=F32)


def _split(x):
    hi = x.astype(BF16)
    lo = (x - hi.astype(F32)).astype(BF16)
    return hi, lo


def _mm3(dot, a, b):
    ah, al = _split(a)
    bh, bl = _split(b)
    return dot(ah, bh) + (dot(ah, bl) + dot(al, bh))


def _in_proj_body(x_ref, g_ref, w_ref, q_ref, kv_ref, zr_ref, gate_ref):
    h = _rms(x_ref[...], g_ref[...]).astype(BF16)

    def proj(lo, width):
        return _dot(h, w_ref[:, lo:lo + width])

    q_ref[...] = (proj(0, Q_COLS) * (1.0 / math.sqrt(HEAD_DIM))).astype(BF16)
    for g in range(N_GROUPS):
        kv_ref[:, 2 * g * GROUP_W:(2 * g + 1) * GROUP_W] = proj(Q_COLS + g * GROUP_W, GROUP_W)
        kv_ref[:, (2 * g + 1) * GROUP_W:(2 * g + 2) * GROUP_W] = proj(2 * Q_COLS + g * GROUP_W, GROUP_W)
    zr_ref[...] = proj(Z_RWKV0, RWKV_COLS)
    gate_ref[...] = _sigmoid(proj(Z_GATE0, GATE_COLS)).astype(BF16)


def _in_proj(x, g, w_bf16, tm):
    n = x.shape[0]
    row = lambda i: (i, 0)
    const = lambda i: (0, 0)
    return pl.pallas_call(
        _in_proj_body,
        grid=(n // tm,),
        in_specs=[pl.BlockSpec((tm, D_MODEL), row),
                  pl.BlockSpec((1, D_MODEL), const),
                  pl.BlockSpec((D_MODEL, IN_COLS), const)],
        out_specs=[pl.BlockSpec((tm, Q_COLS), row),
                   pl.BlockSpec((tm, KV_COLS), row),
                   pl.BlockSpec((tm, RWKV_COLS), row),
                   pl.BlockSpec((tm, GATE_COLS), row)],
        out_shape=[jax.ShapeDtypeStruct((n, Q_COLS), BF16),
                   jax.ShapeDtypeStruct((n, KV_COLS), F32),
                   jax.ShapeDtypeStruct((n, RWKV_COLS), F32),
                   jax.ShapeDtypeStruct((n, GATE_COLS), BF16)],
        compiler_params=_cparams(("parallel",)),
        name="in_proj",
    )(x, g, w_bf16)


def _alibi_slopes(gi):
    return [2.0 ** (-8.0 * (gi * HEADS_PER_GROUP + h + 1) / (N_GROUPS * HEADS_PER_GROUP))
            for h in range(HEADS_PER_GROUP)]


def _head_of_lane(shape):
    return lax.broadcasted_iota(jnp.int32, shape, len(shape) - 1) // HEAD_DIM


def _stack_heads(q):
    hl = _head_of_lane(q.shape)
    return jnp.concatenate([jnp.where(hl == h, q, jnp.zeros_like(q)) for h in range(HEADS_PER_GROUP)], axis=0)


def _unstack_heads(x4, rows):
    hl = _head_of_lane((rows, GROUP_W))
    out = x4[0:rows]
    for h in range(1, HEADS_PER_GROUP):
        out = jnp.where(hl == h, x4[h * rows:(h + 1) * rows], out)
    return out


def _attn_prompt_body(q_ref, kvp_ref, kvc_ref, o_ref, lse_ref, kb_ref, bias_ref, *, dil, slopes, n_sub):
    c = pl.program_id(2)
    kb_ref[0:BAND, :] = kvp_ref[0].astype(BF16)
    kb_ref[BAND:, :] = kvc_ref[0].astype(BF16)

    qi = lax.broadcasted_iota(jnp.int32, (BAND, 2 * BAND), 0) + BAND
    ki = lax.broadcasted_iota(jnp.int32, (BAND, 2 * BAND), 1)
    rel = qi - ki
    dist = jnp.where((rel >= 0) & (rel <= BAND), (dil * rel).astype(F32), -NEG_BIG / slopes[-1])
    for h in range(HEADS_PER_GROUP):
        bias_ref[h * BAND:(h + 1) * BAND, :] = -slopes[h] * dist

    def sub_block(n, carry):
        r0 = pl.multiple_of(n * BAND, BAND)
        q4 = _stack_heads(q_ref[0, pl.ds(r0, BAND), :])
        kv = kb_ref[pl.ds(r0, 2 * BAND), :]
        s = _dot_nt(q4, kv[:, :GROUP_W]) + bias_ref[...]
        n_before_start = jnp.where((c == 0) & (n == 0), BAND, 0)
        kcol = lax.broadcasted_iota(jnp.int32, s.shape, 1)
        s = jnp.where(kcol < n_before_start, NEG_BIG, s)
        m = jnp.max(s, axis=-1, keepdims=True)
        e = jnp.exp(s - m)
        den = jnp.sum(e, axis=-1, keepdims=True)
        o4 = _dot(e.astype(BF16), kv[:, GROUP_W:]) * (1.0 / den)
        o_ref[0, pl.ds(r0, BAND), :] = _unstack_heads(o4, BAND).astype(o_ref.dtype)
        lse4 = jnp.broadcast_to(m + jnp.log(den), (HEADS_PER_GROUP * BAND, GROUP_W))
        lse_ref[0, pl.ds(r0, BAND), :] = _unstack_heads(lse4, BAND)
        return carry

    lax.fori_loop(0, n_sub, sub_block, 0)


def _attn_prompt(q, kv, gi, batch, seq):
    window, dil = ATTN_GROUPS[gi]
    assert window // dil == BAND
    sub_len = seq // dil
    chunk = min(sub_len, 1024)
    n_chunks = sub_len // chunk
    q3 = q.reshape(batch, sub_len, dil * Q_COLS)
    kv3 = kv.reshape(batch, sub_len, dil * KV_COLS)
    sub_per_chunk = chunk // BAND
    body = functools.partial(_attn_prompt_body, dil=dil, slopes=_alibi_slopes(gi), n_sub=sub_per_chunk)
    o, lse = pl.pallas_call(
        body,
        grid=(batch, dil, n_chunks),
        in_specs=[
            pl.BlockSpec((1, chunk, GROUP_W), lambda b, r, c: (b, c, N_GROUPS * r + gi)),
            pl.BlockSpec((1, BAND, 2 * GROUP_W),
                         lambda b, r, c: (b, jnp.maximum(c * sub_per_chunk - 1, 0), N_GROUPS * r + gi)),
            pl.BlockSpec((1, chunk, 2 * GROUP_W), lambda b, r, c: (b, c, N_GROUPS * r + gi)),
        ],
        out_specs=[pl.BlockSpec((1, chunk, GROUP_W), lambda b, r, c: (b, c, r)),
                   pl.BlockSpec((1, chunk, GROUP_W), lambda b, r, c: (b, c, r))],
        out_shape=[jax.ShapeDtypeStruct((batch, sub_len, dil * GROUP_W), BF16),
                   jax.ShapeDtypeStruct((batch, sub_len, dil * GROUP_W), F32)],
        scratch_shapes=[pltpu.VMEM((chunk + BAND, 2 * GROUP_W), BF16),
                        pltpu.VMEM((HEADS_PER_GROUP * BAND, 2 * BAND), F32)],
        compiler_params=_cparams(("parallel", "parallel", "arbitrary")),
        name=f"attn_prompt_g{gi}",
    )(q3, kv3, kv3)
    n = batch * seq
    return o.reshape(n, GROUP_W), lse.reshape(n, GROUP_W)


def _attn_sample_body(q_ref, kvn_ref, cache_ref, o_ref, lse_ref, newc_ref, *, dil, slopes, wc, t_new):
    tp = q_ref.shape[1]
    cache = cache_ref[0]
    new = kvn_ref[0]
    newc_ref[0, 0:wc - t_new, :] = cache[t_new:wc]
    newc_ref[0, wc - t_new:wc, :] = new[0:t_new]

    q4 = _stack_heads(q_ref[0])
    cb = cache.astype(BF16)
    nb = new.astype(BF16)
    t_row = lax.broadcasted_iota(jnp.int32, (tp, 1), 0)
    slope_col = jnp.concatenate([jnp.full((tp, 1), s, F32) for s in slopes], axis=0)
    t4 = jnp.concatenate([t_row] * HEADS_PER_GROUP, axis=0)

    def scores(keys_bf16, dist):
        ok = (dist >= 0) & ((dist & (dil - 1)) == 0) & (dist <= BAND * dil)
        s = _dot_nt(q4, keys_bf16[:, :GROUP_W])
        return jnp.where(ok, s - slope_col * dist.astype(F32), NEG_BIG)

    pc = lax.broadcasted_iota(jnp.int32, (HEADS_PER_GROUP * tp, wc), 1)
    pn = lax.broadcasted_iota(jnp.int32, (HEADS_PER_GROUP * tp, tp), 1)
    s_c = scores(cb, wc + t4 - pc)
    s_n = scores(nb, jnp.where(pn < t_new, t4 - pn, -1))
    m = jnp.maximum(jnp.max(s_c, axis=-1, keepdims=True), jnp.max(s_n, axis=-1, keepdims=True))
    e_c = jnp.exp(s_c - m)
    e_n = jnp.exp(s_n - m)
    den = jnp.sum(e_c, axis=-1, keepdims=True) + jnp.sum(e_n, axis=-1, keepdims=True)
    o4 = (_dot(e_c.astype(BF16), cb[:, GROUP_W:]) + _dot(e_n.astype(BF16), nb[:, GROUP_W:])) * (1.0 / den)
    o_ref[0] = _unstack_heads(o4, tp).astype(o_ref.dtype)
    lse4 = jnp.broadcast_to(m + jnp.log(den), (HEADS_PER_GROUP * tp, GROUP_W))
    lse_ref[0] = _unstack_heads(lse4, tp)


def _attn_sample(q, kv, cache, gi, t_new):
    window, dil = ATTN_GROUPS[gi]
    batch, tp, _ = q.shape
    wc = cache.shape[1]
    assert wc + 0 - dil * BAND >= 0
    body = functools.partial(_attn_sample_body, dil=dil, slopes=_alibi_slopes(gi), wc=wc, t_new=t_new)
    return pl.pallas_call(
        body,
        grid=(batch,),
        in_specs=[pl.BlockSpec((1, tp, GROUP_W), lambda b: (b, 0, gi)),
                  pl.BlockSpec((1, tp, 2 * GROUP_W), lambda b: (b, 0, gi)),
                  pl.BlockSpec((1, wc, 2 * GROUP_W), lambda b: (b, 0, 0))],
        out_specs=[pl.BlockSpec((1, tp, GROUP_W), lambda b: (b, 0, 0)),
                   pl.BlockSpec((1, tp, GROUP_W), lambda b: (b, 0, 0)),
                   pl.BlockSpec((1, wc, 2 * GROUP_W), lambda b: (b, 0, 0))],
        out_shape=[jax.ShapeDtypeStruct((batch, tp, GROUP_W), BF16),
                   jax.ShapeDtypeStruct((batch, tp, GROUP_W), F32),
                   jax.ShapeDtypeStruct((batch, wc, 2 * GROUP_W), F32)],
        compiler_params=_cparams(("parallel",)),
        name=f"attn_sample_g{gi}",
    )(q, kv, cache)


def _rwkv_body(zr_ref, shift0_ref, wkv0_ref, mu_ref, w0_ref, w2_ref, a0_ref, a2_ref, g2_ref, kk_ref, ka_ref,
               rk_ref, lnw_ref, lnb_ref, o_ref, wkv_ref, prev_ref, *, chunk, t_valid):
    c = pl.program_id(1)
    L = chunk
    mm = functools.partial(_mm3, _dot)
    mm_nt = functools.partial(_mm3, _dot_nt)
    mm_tn = functools.partial(_mm3, _dot_tn)

    @pl.when(c == 0)
    def _():
        prev_ref[...] = shift0_ref[0]
        wkv_ref[0] = wkv0_ref[0]

    zr = zr_ref[0]
    row = lax.broadcasted_iota(jnp.int32, (L, 1), 0)
    prev = jnp.where(row == 0, prev_ref[...], pltpu.roll(zr, 1, 0))
    prev_ref[...] = zr[L - 1:L, :]
    zm = zr + (prev - zr) * mu_ref[...]

    r = zm[:, 0:RWKV_W]
    k = zm[:, RWKV_W:2 * RWKV_W]
    v = zm[:, 2 * RWKV_W:3 * RWKV_W]
    xwa = zm[:, 3 * RWKV_W:3 * RWKV_W + LORA_W]
    xg = zm[:, 3 * RWKV_W + LORA_W:]
    w_pre = w0_ref[...] + mm(jnp.tanh(xwa), w2_ref[...])
    softplus = jnp.maximum(-w_pre, 0.0) + jnp.log(1.0 + jnp.exp(-jnp.abs(w_pre)))
    w_log = -softplus - 0.5
    lw = -jnp.exp(w_log)
    a = _sigmoid(a0_ref[...] + mm(xwa, a2_ref[...]))
    g = mm(_sigmoid(xg), g2_ref[...])
    kk_raw = k * kk_ref[...]
    k2 = k * (1.0 + (a - 1.0) * ka_ref[...])
    if t_valid < L:
        live = row < t_valid
        lw = jnp.where(live, lw, 0.0)
        kk_raw = jnp.where(live, kk_raw, 0.0)
        k2 = jnp.where(live, k2, 0.0)
        v = jnp.where(live, v, 0.0)

    ti = lax.broadcasted_iota(jnp.int32, (L, L), 0)
    si = lax.broadcasted_iota(jnp.int32, (L, L), 1)
    incl = ti >= si
    strict = ti > si
    clw = mm(jnp.where(incl, 1.0, 0.0), lw)
    p_incl = jnp.exp(clw)
    p_inv = jnp.exp(-clw)
    p_prev = jnp.exp(clw - lw)
    rt_all = r * p_incl
    kt_all = k2 * p_inv
    rk_all = r * k2 * rk_ref[...]

    n_double = max(int(math.log2(L)) - 1, 0)
    for h in range(RWKV_HEADS):
        sl = slice(h * RWKV_HEAD, (h + 1) * RWKV_HEAD)
        kkh = kk_raw[:, sl]
        kkn = kkh / jnp.maximum(jnp.sqrt(jnp.sum(kkh * kkh, axis=-1, keepdims=True)), 1e-12)
        at = -kkn * p_prev[:, sl]
        bt = kkn * a[:, sl] * p_inv[:, sl]
        rt = rt_all[:, sl]
        kt = kt_all[:, sl]
        vh = v[:, sl]
        ar = jnp.concatenate([at, rt], axis=0)
        g_b = mm_nt(ar, bt)
        g_k = mm_nt(ar, kt)
        a_ab = jnp.where(strict, g_b[0:L], 0.0)
        a_ak = jnp.where(strict, g_k[0:L], 0.0)
        a_rb = jnp.where(incl, g_b[L:2 * L], 0.0)
        a_rk = jnp.where(incl, g_k[L:2 * L], 0.0)
        x = jnp.concatenate([at, mm(a_ak, vh)], axis=1)
        p = a_ab
        x = x + mm(p, x)
        for _ in range(n_double):
            p = mm(p, p)
            x = x + mm(p, x)
        w_mat = x[:, 0:RWKV_HEAD]
        u0 = x[:, RWKV_HEAD:]
        qy = mm(a_rb, x)
        qh = rt + qy[:, 0:RWKV_HEAD]
        y0 = qy[:, RWKV_HEAD:] + mm(a_rk, vh)
        s0 = wkv_ref[0, h]
        y = mm_nt(qh, s0) + y0
        p_last = p_incl[L - 1:L, sl]
        s_new = (s0 + mm(s0, mm_tn(w_mat, bt)) + mm_tn(jnp.concatenate([u0, vh], axis=0),
                                                       jnp.concatenate([bt, kt], axis=0))) * p_last
        wkv_ref[0, h] = s_new
        mu_y = jnp.mean(y, axis=-1, keepdims=True)
        yc = y - mu_y
        var = jnp.mean(yc * yc, axis=-1, keepdims=True)
        yn = yc * lax.rsqrt(var + GN_EPS) * lnw_ref[:, sl] + lnb_ref[:, sl]
        bonus = jnp.sum(rk_all[:, sl], axis=-1, keepdims=True) * vh
        o_ref[0, :, sl] = ((yn + bonus) * g[:, sl]).astype(o_ref.dtype)


def _rwkv(zr, shift0, wkv0, wts, chunk, t_valid):
    batch, t, _ = zr.shape
    n_chunks = t // chunk
    assert t_valid == t or n_chunks == 1
    body = functools.partial(_rwkv_body, chunk=chunk, t_valid=min(t_valid, chunk))
    vec = lambda width: pl.BlockSpec((1, width), lambda b, c: (0, 0))
    mat = lambda rows: pl.BlockSpec((rows, RWKV_W), lambda b, c: (0, 0))
    return pl.pallas_call(
        body,
        grid=(batch, n_chunks),
        in_specs=[pl.BlockSpec((1, chunk, RWKV_COLS), lambda b, c: (b, c, 0)),
                  pl.BlockSpec((1, 1, RWKV_COLS), lambda b, c: (b, 0, 0)),
                  pl.BlockSpec((1, RWKV_HEADS, RWKV_HEAD, RWKV_HEAD), lambda b, c: (b, 0, 0, 0)),
                  vec(RWKV_COLS), vec(RWKV_W), mat(LORA_W), vec(RWKV_W), mat(LORA_W), mat(GATE_LORA),
                  vec(RWKV_W), vec(RWKV_W), vec(RWKV_W), vec(RWKV_W), vec(RWKV_W)],
        out_specs=[pl.BlockSpec((1, chunk, RWKV_W), lambda b, c: (b, c, 0)),
                   pl.BlockSpec((1, RWKV_HEADS, RWKV_HEAD, RWKV_HEAD), lambda b, c: (b, 0, 0, 0))],
        out_shape=[jax.ShapeDtypeStruct((batch, t, RWKV_W), BF16),
                   jax.ShapeDtypeStruct((batch, RWKV_HEADS, RWKV_HEAD, RWKV_HEAD), F32)],
        scratch_shapes=[pltpu.VMEM((1, RWKV_COLS), F32)],
        compiler_params=_cparams(("parallel", "arbitrary")),
        name="rwkv_scan",
    )(zr, shift0, wkv0, *wts)


def _mix_out_body(x_ref, o0_ref, o1_ref, o2_ref, l0_ref, l1_ref, l2_ref, orw_ref, gate_ref, woa_ref, wor_ref,
                  wo_ref, gffn_ref, rwt_ref, rb_ref, *rest):
    x1_ref, h2_ref, te_ref, tg_ref = rest[-4:]
    l0, l1, l2 = l0_ref[...], l1_ref[...], l2_ref[...]
    m = jnp.maximum(jnp.maximum(l0, l1), l2)
    e0, e1, e2 = jnp.exp(l0 - m), jnp.exp(l1 - m), jnp.exp(l2 - m)
    o_att = (e0 * o0_ref[...].astype(F32) + e1 * o1_ref[...].astype(F32) + e2 * o2_ref[...].astype(F32)) \
        * (1.0 / (e0 + e1 + e2))
    gates = gate_ref[...].astype(F32)
    merged = gates[:, :D_MODEL] * _dot(o_att.astype(BF16), woa_ref[...]) \
        + gates[:, D_MODEL:] * _dot(orw_ref[...], wor_ref[...])
    x1 = x_ref[...] + _dot(merged.astype(BF16), wo_ref[...])
    x1_ref[...] = x1
    h2 = _rms(x1, gffn_ref[...])
    h2_ref[...] = h2

    logits = _mm3(_dot_nt, rwt_ref[...], h2) + rb_ref[...]
    e_iota = lax.broadcasted_iota(jnp.int32, logits.shape, 0)
    vals, idxs = [], []
    for _ in range(TOP_K):
        top = jnp.max(logits, axis=0, keepdims=True)
        idx = jnp.min(jnp.where(logits == top, e_iota, N_EXPERTS), axis=0, keepdims=True)
        vals.append(top)
        idxs.append(idx)
        logits = jnp.where(e_iota == idx, -jnp.inf, logits)
    exps = [jnp.exp(t - vals[0]) for t in vals]
    inv = 1.0 / (exps[0] + exps[1] + exps[2] + exps[3])
    te_ref[...] = jnp.concatenate(idxs, axis=0)
    tg_ref[...] = jnp.concatenate([e * inv for e in exps], axis=0)


def _mix_out(x, o_g, lse_g, o_rwkv, gates, wts, tm, n_all, row0, bufs):
    n = x.shape[0]
    blk0 = row0 // tm
    row = lambda i: (i, 0)
    const = lambda i: (0, 0)
    out_row = lambda i: (i + blk0, 0)
    out_col = lambda i: (0, i + blk0)
    tok = lambda w: pl.BlockSpec((tm, w), row)
    in_specs = [tok(D_MODEL)] + [tok(GROUP_W)] * 6 + [tok(RWKV_W), tok(GATE_COLS),
                pl.BlockSpec((GROUP_W, D_MODEL), const), pl.BlockSpec((RWKV_W, D_MODEL), const),
                pl.BlockSpec((D_MODEL, D_MODEL), const), pl.BlockSpec((1, D_MODEL), const),
                pl.BlockSpec((N_EXPERTS, D_MODEL), const), pl.BlockSpec((N_EXPERTS, 1), const)]
    args = [x, *o_g, *lse_g, o_rwkv, gates, *wts]
    aliases = {}
    if bufs is not None:
        in_specs += [pl.BlockSpec(memory_space=pl.ANY)] * 4
        aliases = {len(args) + j: j for j in range(4)}
        args += list(bufs)
    return pl.pallas_call(
        _mix_out_body,
        grid=(n // tm,),
        in_specs=in_specs,
        out_specs=[pl.BlockSpec((tm, D_MODEL), out_row), pl.BlockSpec((tm, D_MODEL), out_row),
                   pl.BlockSpec((TOP_K, tm), out_col), pl.BlockSpec((TOP_K, tm), out_col)],
        out_shape=[jax.ShapeDtypeStruct((n_all, D_MODEL), F32), jax.ShapeDtypeStruct((n_all, D_MODEL), F32),
                   jax.ShapeDtypeStruct((TOP_K, n_all), jnp.int32), jax.ShapeDtypeStruct((TOP_K, n_all), F32)],
        input_output_aliases=aliases,
        compiler_params=_cparams(("parallel",)),
        name="mix_out",
    )(*args)


def _moe_body(be_ref, nused_ref, idx_hbm, h2_hbm, w1_ref, b1_ref, w2_ref, b2_ref, gate_ref, y_hbm,
              idx_smem, xbuf, ybuf, w1b, w2b, isem, gsem, ssem):
    i = pl.program_id(0)
    slot = i % 2
    nused = nused_ref[0]

    def idx_copy(rec, s):
        return pltpu.make_async_copy(idx_hbm.at[pl.ds(pl.multiple_of(rec * IDX_REC, IDX_REC), IDX_REC)],
                                     idx_smem.at[pl.ds(pl.multiple_of(s * IDX_REC, IDX_REC), IDX_REC)],
                                     isem.at[s])

    def gather_start(s_idx, s_buf):
        def one(r, carry):
            t = idx_smem[s_idx * IDX_REC + r]
            pltpu.make_async_copy(h2_hbm.at[pl.ds(t, 1)], xbuf.at[s_buf, pl.ds(r, 1)], gsem.at[s_buf]).start()
            return carry
        lax.fori_loop(0, MOE_BLOCK, one, 0, unroll=8)

    def gather_wait(s_buf):
        pltpu.make_async_copy(h2_hbm.at[pl.ds(0, MOE_BLOCK)], xbuf.at[s_buf], gsem.at[s_buf]).wait()

    def scatter_start(s_idx, s_buf):
        def one(r, carry):
            t = idx_smem[s_idx * IDX_REC + MOE_BLOCK + r]
            pltpu.make_async_copy(ybuf.at[s_buf, pl.ds(r, 1)], y_hbm.at[pl.ds(t, 1)], ssem.at[s_buf]).start()
            return carry
        lax.fori_loop(0, MOE_BLOCK, one, 0, unroll=8)

    def scatter_wait(s_buf):
        pltpu.make_async_copy(ybuf.at[s_buf], y_hbm.at[pl.ds(0, MOE_BLOCK)], ssem.at[s_buf]).wait()

    @pl.when(i == 0)
    def _():
        idx_copy(0, 1).start()
        idx_copy(0, 1).wait()
        gather_start(1, 0)
        idx_copy(1, 0).start()
        idx_copy(1, 0).wait()

    @pl.when(i < nused)
    def _():
        idx_copy(i + 2, 1 - slot).start()

        @pl.when(i + 1 < nused)
        def _():
            gather_start(slot, 1 - slot)

        @pl.when((i == 0) | (be_ref[i] != be_ref[jnp.maximum(i - 1, 0)]))
        def _():
            w1b[...] = w1_ref[0].astype(BF16)
            w2b[...] = w2_ref[0].astype(BF16)

        gather_wait(slot)
        u = _dot(xbuf[slot].astype(BF16), w1b[...]) + b1_ref[0]
        glu = jnp.minimum(u[:, :D_MODEL], SWIGLU_LIMIT)
        lin = jnp.clip(u[:, D_MODEL:], -SWIGLU_LIMIT, SWIGLU_LIMIT)
        act = glu * _sigmoid(SWIGLU_ALPHA * glu) * (lin + 1.0)
        y = _dot(act.astype(BF16), w2b[...]) + b2_ref[0]
        ybuf[slot] = y * gate_ref[...]
        scatter_start(slot, slot)

        @pl.when(i >= 1)
        def _():
            scatter_wait(1 - slot)

        @pl.when(i == nused - 1)
        def _():
            scatter_wait(slot)

        idx_copy(i + 2, 1 - slot).wait()


def _moe_experts(h2, idx_rec, block_e, n_used, row_gate, w1, b1, w2, b2, n_slots):
    n_blocks = block_e.shape[0]
    grid_spec = pltpu.PrefetchScalarGridSpec(
        num_scalar_prefetch=2,
        grid=(n_blocks,),
        in_specs=[pl.BlockSpec(memory_space=pl.ANY),
                  pl.BlockSpec(memory_space=pl.ANY),
                  pl.BlockSpec((1, D_MODEL, 2 * D_MODEL), lambda i, be, nu: (be[i], 0, 0)),
                  pl.BlockSpec((1, 1, 2 * D_MODEL), lambda i, be, nu: (be[i], 0, 0)),
                  pl.BlockSpec((1, D_MODEL, D_MODEL), lambda i, be, nu: (be[i], 0, 0)),
                  pl.BlockSpec((1, 1, D_MODEL), lambda i, be, nu: (be[i], 0, 0)),
                  pl.BlockSpec((MOE_BLOCK, 1), lambda i, be, nu: (i, 0))],
        out_specs=pl.BlockSpec(memory_space=pl.ANY),
        scratch_shapes=[pltpu.SMEM((2 * IDX_REC,), jnp.int32),
                        pltpu.VMEM((2, MOE_BLOCK, D_MODEL), F32),
                        pltpu.VMEM((2, MOE_BLOCK, D_MODEL), F32),
                        pltpu.VMEM((D_MODEL, 2 * D_MODEL), BF16),
                        pltpu.VMEM((D_MODEL, D_MODEL), BF16),
                        pltpu.SemaphoreType.DMA((2,)),
                        pltpu.SemaphoreType.DMA((2,)),
                        pltpu.SemaphoreType.DMA((2,))],
    )
    return pl.pallas_call(
        _moe_body,
        grid_spec=grid_spec,
        out_shape=jax.ShapeDtypeStruct((n_slots, D_MODEL), F32),
        compiler_params=_cparams(("arbitrary",)),
        name="moe_experts",
    )(block_e, n_used, idx_rec, h2, w1, b1, w2, b2, row_gate)


def _route(top_e, top_g):
    n_tok = top_e.shape[1]
    n_asg = n_tok * TOP_K
    flat_e = top_e.T.reshape(-1)
    order = jnp.argsort(flat_e).astype(jnp.int32)
    se = flat_e[order]
    counts = jnp.bincount(flat_e, length=N_EXPERTS).astype(jnp.int32)
    padded = (counts + MOE_BLOCK - 1) // MOE_BLOCK * MOE_BLOCK
    pad_end = jnp.cumsum(padded)
    start = jnp.cumsum(counts) - counts
    dest = (pad_end - padded)[se] + jnp.arange(n_asg, dtype=jnp.int32) - start[se]
    n_blocks = -(-(n_asg + N_EXPERTS * (MOE_BLOCK - 1)) // MOE_BLOCK)
    n_rows = n_blocks * MOE_BLOCK
    blk = jnp.arange(n_rows, dtype=jnp.int32) // MOE_BLOCK
    trash = n_asg + (blk % 2) * MOE_BLOCK + jnp.arange(n_rows, dtype=jnp.int32) % MOE_BLOCK
    row_tok = jnp.zeros((n_rows,), jnp.int32).at[dest].set(order // TOP_K)
    row_slot = trash.at[dest].set(order)
    row_gate = jnp.zeros((n_rows,), F32).at[dest].set(top_g.T.reshape(-1)[order])
    block_e = jnp.minimum(jnp.searchsorted(pad_end, jnp.arange(n_blocks, dtype=jnp.int32) * MOE_BLOCK,
                                           side='right'), N_EXPERTS - 1).astype(jnp.int32)
    n_used = (pad_end[-1] // MOE_BLOCK).astype(jnp.int32).reshape(1)
    tok_rec = jnp.concatenate([row_tok.reshape(n_blocks, MOE_BLOCK),
                               jnp.zeros((2, MOE_BLOCK), jnp.int32)], axis=0)
    slot_rec = jnp.concatenate([jnp.zeros((1, MOE_BLOCK), jnp.int32), row_slot.reshape(n_blocks, MOE_BLOCK),
                                jnp.zeros((1, MOE_BLOCK), jnp.int32)], axis=0)
    idx_rec = jnp.concatenate([tok_rec, slot_rec, jnp.zeros((n_blocks + 2, IDX_REC - 2 * MOE_BLOCK), jnp.int32)],
                              axis=1).reshape(-1)
    return idx_rec, block_e, n_used, row_gate.reshape(n_rows, 1), n_asg + 2 * MOE_BLOCK


def _tail_body(x1_ref, ys_ref, pe_ref, gple_ref, wpg_ref, wp_ref, gfin_ref, y_ref):
    ys = ys_ref[...]
    moe = (ys[:, 0:D_MODEL] + ys[:, D_MODEL:2 * D_MODEL]) + (ys[:, 2 * D_MODEL:3 * D_MODEL] + ys[:, 3 * D_MODEL:])
    x2 = x1_ref[...] + moe
    gate = _sigmoid(_dot(_rms(x2, gple_ref[...]).astype(BF16), wpg_ref[...]))
    x3 = x2 + gate * _dot(pe_ref[...].astype(BF16), wp_ref[...])
    y_ref[...] = _rms(x3, gfin_ref[...])


def _tail(x1_all, y_slots, pe, wts, tm, row0):
    n = pe.shape[0]
    blk0 = row0 // tm
    row = lambda i: (i, 0)
    src = lambda i: (i + blk0, 0)
    const = lambda i: (0, 0)
    ys = y_slots.reshape(-1, TOP_K * D_MODEL)
    return pl.pallas_call(
        _tail_body,
        grid=(n // tm,),
        in_specs=[pl.BlockSpec((tm, D_MODEL), src), pl.BlockSpec((tm, TOP_K * D_MODEL), src),
                  pl.BlockSpec((tm, PLE_DIM), row), pl.BlockSpec((1, D_MODEL), const),
                  pl.BlockSpec((D_MODEL, D_MODEL), const), pl.BlockSpec((PLE_DIM, D_MODEL), const),
                  pl.BlockSpec((1, D_MODEL), const)],
        out_specs=pl.BlockSpec((tm, D_MODEL), row),
        out_shape=jax.ShapeDtypeStruct((n, D_MODEL), F32),
        compiler_params=_cparams(("parallel",)),
        name="tail",
    )(x1_all, ys, pe, *wts)


RWKV_CHUNK = 64
TM_PROMPT = 256


def kernel(x_prompt, x_sample, p_prompt, p_sample, cache_kv_w128, cache_kv_w512, cache_kv_w2048, state_rwkv_shift, state_rwkv_wkv, norm_mix_g, w_in, rwkv_mu, rwkv_w0, rwkv_w2, rwkv_a0, rwkv_a2, rwkv_g2, rwkv_k_k, rwkv_k_a, rwkv_r_k, rwkv_ln_w, rwkv_ln_b, w_out_attn, w_out_rwkv, w_out, norm_ffn_g, router_w, router_b, moe_w1, moe_b1, moe_w2, moe_b2, norm_ple_g, w_ple, w_ple_gate, norm_final_g):
    bp, seq, _ = x_prompt.shape
    bs, t_s, _ = x_sample.shape
    n_p, n_s = bp * seq, bs * t_s
    n_all = n_p + n_s
    assert w_in.shape[0] == 1, "single layer"
    caches = (cache_kv_w128, cache_kv_w512, cache_kv_w2048)

    row = lambda a: a.reshape(1, -1)
    w_in_b = w_in[0].astype(BF16)
    zeros64 = jnp.zeros((LORA_W // 2, RWKV_W), F32)
    rwkv_wts = (row(rwkv_mu[0]), row(rwkv_w0[0]), jnp.concatenate([rwkv_w2[0], zeros64], axis=0),
                row(rwkv_a0[0]), jnp.concatenate([zeros64, rwkv_a2[0]], axis=0), rwkv_g2[0],
                row(rwkv_k_k[0]), row(rwkv_k_a[0]), row(rwkv_r_k[0]), row(rwkv_ln_w[0]), row(rwkv_ln_b[0]))
    mix_wts = (w_out_attn[0].astype(BF16), w_out_rwkv[0].astype(BF16), w_out[0].astype(BF16),
               row(norm_ffn_g[0]), router_w[0].T, router_b[0].reshape(N_EXPERTS, 1))
    tail_wts = (row(norm_ple_g[0]), w_ple_gate[0].astype(BF16), w_ple[0].astype(BF16), row(norm_final_g))
    g_mix = row(norm_mix_g[0])

    xp = x_prompt.reshape(n_p, D_MODEL)
    q_p, kv_p, zr_p, gate_p = _in_proj(xp, g_mix, w_in_b, TM_PROMPT)
    att_p = [_attn_prompt(q_p, kv_p, gi, bp, seq) for gi in range(N_GROUPS)]
    orw_p, wkv_p = _rwkv(zr_p.reshape(bp, seq, RWKV_COLS), jnp.zeros((bp, 1, RWKV_COLS), F32),
                         jnp.zeros((bp, RWKV_HEADS, RWKV_HEAD, RWKV_HEAD), F32), rwkv_wts, RWKV_CHUNK, seq)
    bufs = _mix_out(xp, [a[0] for a in att_p], [a[1] for a in att_p], orw_p.reshape(n_p, RWKV_W), gate_p,
                    mix_wts, TM_PROMPT, n_all, 0, None)

    xs = x_sample.reshape(n_s, D_MODEL)
    q_s, kv_s, zr_s, gate_s = _in_proj(xs, g_mix, w_in_b, n_s)
    t_pad = 8
    pad_t = lambda a: jnp.pad(a.reshape(bs, t_s, -1), ((0, 0), (0, t_pad - t_s), (0, 0)))
    q_s3, kv_s3 = pad_t(q_s), pad_t(kv_s)
    att_s, new_caches = [], []
    for gi in range(N_GROUPS):
        wc = caches[gi].shape[2]
        o, lse, newc = _attn_sample(q_s3, kv_s3, caches[gi].reshape(bs, wc, 2 * GROUP_W), gi, t_s)
        att_s.append((o[:, :t_s].reshape(n_s, GROUP_W), lse[:, :t_s].reshape(n_s, GROUP_W)))
        new_caches.append(newc.reshape(1, bs, wc, 2, HEADS_PER_GROUP, HEAD_DIM))
    zr_s3 = jnp.pad(zr_s.reshape(bs, t_s, RWKV_COLS), ((0, 0), (0, RWKV_CHUNK - t_s), (0, 0)))
    orw_s, wkv_s = _rwkv(zr_s3, state_rwkv_shift[0].reshape(bs, 1, RWKV_COLS), state_rwkv_wkv[0], rwkv_wts,
                         RWKV_CHUNK, t_s)
    bufs = _mix_out(xs, [a[0] for a in att_s], [a[1] for a in att_s], orw_s[:, :t_s].reshape(n_s, RWKV_W),
                    gate_s, mix_wts, n_s, n_all, n_p, bufs)
    x1_all, h2_all, top_e, top_g = bufs

    idx_rec, block_e, n_used, row_gate, n_slots = _route(top_e, top_g)
    y_slots = _moe_experts(h2_all, idx_rec, block_e, n_used, row_gate, moe_w1[0],
                           moe_b1[0].reshape(N_EXPERTS, 1, 2 * D_MODEL), moe_w2[0],
                           moe_b2[0].reshape(N_EXPERTS, 1, D_MODEL), n_slots)

    y_p = _tail(x1_all, y_slots, p_prompt[0].reshape(n_p, PLE_DIM), tail_wts, TM_PROMPT, 0)
    y_s = _tail(x1_all, y_slots, p_sample[0].reshape(n_s, PLE_DIM), tail_wts, n_s, n_p)

    kv_p4 = kv_p.reshape(bp, seq, N_GROUPS, 2, HEADS_PER_GROUP, HEAD_DIM)
    kv_out_p = [kv_p4[:, seq - min(w, seq):, gi][None] for gi, (w, _) in enumerate(ATTN_GROUPS)]
    shift_p = zr_p.reshape(bp, seq, RWKV_COLS)[:, -1][None]
    shift_s = zr_s.reshape(bs, t_s, RWKV_COLS)[:, -1][None]
    return (y_p.reshape(bp, seq, D_MODEL), y_s.reshape(bs, t_s, D_MODEL),
            kv_out_p[0], kv_out_p[1], kv_out_p[2], shift_p, wkv_p[None],
            new_caches[0], new_caches[1], new_caches[2], shift_s, wkv_s[None])
```

```python
import functools
import math

import numpy as np
import jax
import jax.numpy as jnp
from jax import lax
from jax.experimental import pallas as pl
from jax.experimental.pallas import tpu as pltpu

F32 = jnp.float32
BF16 = jnp.bfloat16

D_MODEL = 1024
N_GROUPS = 3
HEADS_PER_GROUP = 4
HEAD_DIM = 64
ATTN_GROUPS = ((128, 1), (512, 4), (2048, 16))
GROUP_W = HEADS_PER_GROUP * HEAD_DIM
Q_COLS = N_GROUPS * GROUP_W
KV_COLS = 2 * Q_COLS
BAND = 128

RWKV_HEADS = 8
RWKV_HEAD = 64
RWKV_W = 512
LORA_W = 128
GATE_LORA = 128
RWKV_COLS = 3 * RWKV_W + LORA_W + GATE_LORA
GN_EPS = 64e-5
GATE_COLS = 2 * D_MODEL
Z_RWKV0 = 3 * Q_COLS
Z_GATE0 = Z_RWKV0 + RWKV_COLS
IN_COLS = Z_GATE0 + GATE_COLS

N_EXPERTS = 32
TOP_K = 4
SWIGLU_LIMIT = 7.0
SWIGLU_ALPHA = 1.702
MOE_BLOCK = 256
IDX_REC = 1024
PLE_DIM = 256
RMS_EPS = 1e-6

NEG_BIG = -1e30
VMEM_LIMIT = 56 * 1024 * 1024


def _cparams(sem):
    return pltpu.CompilerParams(dimension_semantics=sem, vmem_limit_bytes=VMEM_LIMIT)


def _rms(x, g):
    return x * lax.rsqrt(jnp.mean(x * x, axis=-1, keepdims=True) + RMS_EPS) * g


def _sigmoid(x):
    return 1.0 / (1.0 + jnp.exp(-x))


def _dot(a, b):
    return jnp.dot(a, b, preferred_element_type=F32)


def _dot_nt(a, b):
    return lax.dot_general(a, b, (((1,), (1,)), ((), ())), preferred_element_type=F32)


def _dot_tn(a, b):
    return lax.dot_general(a, b, (((0,), (0,)), ((), ())), preferred_element_type=F32)


def _split(x):
    hi = x.astype(BF16)
    lo = (x - hi.astype(F32)).astype(BF16)
    return hi, lo


def _mm3(dot, a, b):
    ah, al = _split(a)
    bh, bl = _split(b)
    return dot(ah, bh) + (dot(ah, bl) + dot(al, bh))


def _parts(x, n):
    out = []
    for _ in range(n - 1):
        hi = x.astype(BF16)
        out.append(hi)
        x = x - hi.astype(F32)
    out.append(x.astype(BF16))
    return out


def _mmp(dot, ap, bp):
    order = max(len(ap), len(bp))
    acc = None
    for i, a in enumerate(ap):
        for j, b in enumerate(bp):
            if i + j < order:
                t = dot(a, b)
                acc = t if acc is None else acc + t
    return acc


def _in_proj_body(x_ref, g_ref, w_ref, q_ref, kv_ref, zr_ref, gate_ref):
    h = _rms(x_ref[...], g_ref[...]).astype(BF16)

    def proj(lo, width):
        return _dot(h, w_ref[:, lo:lo + width])

    q_ref[...] = (proj(0, Q_COLS) * (1.0 / math.sqrt(HEAD_DIM))).astype(BF16)
    for g in range(N_GROUPS):
        kv_ref[:, 2 * g * GROUP_W:(2 * g + 1) * GROUP_W] = proj(Q_COLS + g * GROUP_W, GROUP_W)
        kv_ref[:, (2 * g + 1) * GROUP_W:(2 * g + 2) * GROUP_W] = proj(2 * Q_COLS + g * GROUP_W, GROUP_W)
    zr_ref[...] = proj(Z_RWKV0, RWKV_COLS)
    gate_ref[...] = _sigmoid(proj(Z_GATE0, GATE_COLS)).astype(BF16)


def _in_proj(x, g, w_bf16, tm):
    n = x.shape[0]
    row = lambda i: (i, 0)
    const = lambda i: (0, 0)
    return pl.pallas_call(
        _in_proj_body,
        grid=(n // tm,),
        in_specs=[pl.BlockSpec((tm, D_MODEL), row),
                  pl.BlockSpec((1, D_MODEL), const),
                  pl.BlockSpec((D_MODEL, IN_COLS), const)],
        out_specs=[pl.BlockSpec((tm, Q_COLS), row),
                   pl.BlockSpec((tm, KV_COLS), row),
                   pl.BlockSpec((tm, RWKV_COLS), row),
                   pl.BlockSpec((tm, GATE_COLS), row)],
        out_shape=[jax.ShapeDtypeStruct((n, Q_COLS), BF16),
                   jax.ShapeDtypeStruct((n, KV_COLS), F32),
                   jax.ShapeDtypeStruct((n, RWKV_COLS), F32),
                   jax.ShapeDtypeStruct((n, GATE_COLS), BF16)],
        compiler_params=_cparams(("parallel",)),
        name="in_proj",
    )(x, g, w_bf16)


def _alibi_slopes(gi):
    return [2.0 ** (-8.0 * (gi * HEADS_PER_GROUP + h + 1) / (N_GROUPS * HEADS_PER_GROUP))
            for h in range(HEADS_PER_GROUP)]


def _head_of_lane(shape):
    return lax.broadcasted_iota(jnp.int32, shape, len(shape) - 1) // HEAD_DIM


def _stack_heads(q):
    hl = _head_of_lane(q.shape)
    return jnp.concatenate([jnp.where(hl == h, q, jnp.zeros_like(q)) for h in range(HEADS_PER_GROUP)], axis=0)


def _unstack_heads(x4, rows):
    hl = _head_of_lane((rows, GROUP_W))
    out = x4[0:rows]
    for h in range(1, HEADS_PER_GROUP):
        out = jnp.where(hl == h, x4[h * rows:(h + 1) * rows], out)
    return out


def _attn_prompt_body(q_ref, kvp_ref, kvc_ref, o_ref, lse_ref, kb_ref, bias_ref, *, dil, slopes, n_sub):
    c = pl.program_id(2)
    kb_ref[0:BAND, :] = kvp_ref[0].astype(BF16)
    kb_ref[BAND:, :] = kvc_ref[0].astype(BF16)

    qi = lax.broadcasted_iota(jnp.int32, (BAND, 2 * BAND), 0) + BAND
    ki = lax.broadcasted_iota(jnp.int32, (BAND, 2 * BAND), 1)
    rel = qi - ki
    dist = jnp.where((rel >= 0) & (rel <= BAND), (dil * rel).astype(F32), -NEG_BIG / slopes[-1])
    for h in range(HEADS_PER_GROUP):
        bias_ref[h * BAND:(h + 1) * BAND, :] = -slopes[h] * dist

    def sub_block(n, carry):
        r0 = pl.multiple_of(n * BAND, BAND)
        q4 = _stack_heads(q_ref[0, pl.ds(r0, BAND), :])
        kv = kb_ref[pl.ds(r0, 2 * BAND), :]
        s = _dot_nt(q4, kv[:, :GROUP_W]) + bias_ref[...]
        n_before_start = jnp.where((c == 0) & (n == 0), BAND, 0)
        kcol = lax.broadcasted_iota(jnp.int32, s.shape, 1)
        s = jnp.where(kcol < n_before_start, NEG_BIG, s)
        m = jnp.max(s, axis=-1, keepdims=True)
        e = jnp.exp(s - m)
        den = jnp.sum(e, axis=-1, keepdims=True)
        o4 = _dot(e.astype(BF16), kv[:, GROUP_W:]) * (1.0 / den)
        o_ref[0, pl.ds(r0, BAND), :] = _unstack_heads(o4, BAND).astype(o_ref.dtype)
        lse4 = jnp.broadcast_to(m + jnp.log(den), (HEADS_PER_GROUP * BAND, GROUP_W))
        lse_ref[0, pl.ds(r0, BAND), :] = _unstack_heads(lse4, BAND)
        return carry

    lax.fori_loop(0, n_sub, sub_block, 0)


def _attn_prompt(q, kv, gi, batch, seq):
    window, dil = ATTN_GROUPS[gi]
    assert window // dil == BAND
    sub_len = seq // dil
    chunk = min(sub_len, 1024)
    n_chunks = sub_len // chunk
    q3 = q.reshape(batch, sub_len, dil * Q_COLS)
    kv3 = kv.reshape(batch, sub_len, dil * KV_COLS)
    sub_per_chunk = chunk // BAND
    body = functools.partial(_attn_prompt_body, dil=dil, slopes=_alibi_slopes(gi), n_sub=sub_per_chunk)
    o, lse = pl.pallas_call(
        body,
        grid=(batch, dil, n_chunks),
        in_specs=[
            pl.BlockSpec((1, chunk, GROUP_W), lambda b, r, c: (b, c, N_GROUPS * r + gi)),
            pl.BlockSpec((1, BAND, 2 * GROUP_W),
                         lambda b, r, c: (b, jnp.maximum(c * sub_per_chunk - 1, 0), N_GROUPS * r + gi)),
            pl.BlockSpec((1, chunk, 2 * GROUP_W), lambda b, r, c: (b, c, N_GROUPS * r + gi)),
        ],
        out_specs=[pl.BlockSpec((1, chunk, GROUP_W), lambda b, r, c: (b, c, r)),
                   pl.BlockSpec((1, chunk, GROUP_W), lambda b, r, c: (b, c, r))],
        out_shape=[jax.ShapeDtypeStruct((batch, sub_len, dil * GROUP_W), BF16),
                   jax.ShapeDtypeStruct((batch, sub_len, dil * GROUP_W), F32)],
        scratch_shapes=[pltpu.VMEM((chunk + BAND, 2 * GROUP_W), BF16),
                        pltpu.VMEM((HEADS_PER_GROUP * BAND, 2 * BAND), F32)],
        compiler_params=_cparams(("parallel", "parallel", "arbitrary")),
        name=f"attn_prompt_g{gi}",
    )(q3, kv3, kv3)
    n = batch * seq
    return o.reshape(n, GROUP_W), lse.reshape(n, GROUP_W)


def _attn_sample_body(q_ref, kvn_ref, cache_ref, o_ref, lse_ref, newc_ref, *, dil, slopes, wc, t_new):
    tp = q_ref.shape[1]
    cache = cache_ref[0]
    new = kvn_ref[0]
    newc_ref[0, 0:wc - t_new, :] = cache[t_new:wc]
    newc_ref[0, wc - t_new:wc, :] = new[0:t_new]

    q4 = _stack_heads(q_ref[0])
    cb = cache.astype(BF16)
    nb = new.astype(BF16)
    t_row = lax.broadcasted_iota(jnp.int32, (tp, 1), 0)
    slope_col = jnp.concatenate([jnp.full((tp, 1), s, F32) for s in slopes], axis=0)
    t4 = jnp.concatenate([t_row] * HEADS_PER_GROUP, axis=0)

    def scores(keys_bf16, dist):
        ok = (dist >= 0) & ((dist & (dil - 1)) == 0) & (dist <= BAND * dil)
        s = _dot_nt(q4, keys_bf16[:, :GROUP_W])
        return jnp.where(ok, s - slope_col * dist.astype(F32), NEG_BIG)

    pc = lax.broadcasted_iota(jnp.int32, (HEADS_PER_GROUP * tp, wc), 1)
    pn = lax.broadcasted_iota(jnp.int32, (HEADS_PER_GROUP * tp, tp), 1)
    s_c = scores(cb, wc + t4 - pc)
    s_n = scores(nb, jnp.where(pn < t_new, t4 - pn, -1))
    m = jnp.maximum(jnp.max(s_c, axis=-1, keepdims=True), jnp.max(s_n, axis=-1, keepdims=True))
    e_c = jnp.exp(s_c - m)
    e_n = jnp.exp(s_n - m)
    den = jnp.sum(e_c, axis=-1, keepdims=True) + jnp.sum(e_n, axis=-1, keepdims=True)
    o4 = (_dot(e_c.astype(BF16), cb[:, GROUP_W:]) + _dot(e_n.astype(BF16), nb[:, GROUP_W:])) * (1.0 / den)
    o_ref[0] = _unstack_heads(o4, tp).astype(o_ref.dtype)
    lse4 = jnp.broadcast_to(m + jnp.log(den), (HEADS_PER_GROUP * tp, GROUP_W))
    lse_ref[0] = _unstack_heads(lse4, tp)


def _attn_sample(q, kv, cache, gi, t_new):
    window, dil = ATTN_GROUPS[gi]
    batch, tp, _ = q.shape
    wc = cache.shape[1]
    assert wc + 0 - dil * BAND >= 0
    body = functools.partial(_attn_sample_body, dil=dil, slopes=_alibi_slopes(gi), wc=wc, t_new=t_new)
    return pl.pallas_call(
        body,
        grid=(batch,),
        in_specs=[pl.BlockSpec((1, tp, GROUP_W), lambda b: (b, 0, gi)),
                  pl.BlockSpec((1, tp, 2 * GROUP_W), lambda b: (b, 0, gi)),
                  pl.BlockSpec((1, wc, 2 * GROUP_W), lambda b: (b, 0, 0))],
        out_specs=[pl.BlockSpec((1, tp, GROUP_W), lambda b: (b, 0, 0)),
                   pl.BlockSpec((1, tp, GROUP_W), lambda b: (b, 0, 0)),
                   pl.BlockSpec((1, wc, 2 * GROUP_W), lambda b: (b, 0, 0))],
        out_shape=[jax.ShapeDtypeStruct((batch, tp, GROUP_W), BF16),
                   jax.ShapeDtypeStruct((batch, tp, GROUP_W), F32),
                   jax.ShapeDtypeStruct((batch, wc, 2 * GROUP_W), F32)],
        compiler_params=_cparams(("parallel",)),
        name=f"attn_sample_g{gi}",
    )(q, kv, cache)


def _rwkv_body(zr_ref, shift0_ref, wkv0_ref, mu_ref, w0_ref, w2_ref, a0_ref, a2_ref, g2_ref, kk_ref, ka_ref,
               rk_ref, lnw_ref, lnb_ref, o_ref, wkv_ref, prev_ref, *, chunk, t_valid):
    c = pl.program_id(1)
    L = chunk
    mm = functools.partial(_mm3, _dot)
    heads = range(RWKV_HEADS)
    sls = [slice(h * RWKV_HEAD, (h + 1) * RWKV_HEAD) for h in heads]

    @pl.when(c == 0)
    def _():
        prev_ref[...] = shift0_ref[0]
        wkv_ref[0] = wkv0_ref[0]

    zr = zr_ref[0]
    row = lax.broadcasted_iota(jnp.int32, (L, 1), 0)
    prev = jnp.where(row == 0, prev_ref[...], pltpu.roll(zr, 1, 0))
    prev_ref[...] = zr[L - 1:L, :]
    zm = zr + (prev - zr) * mu_ref[...]

    r = zm[:, 0:RWKV_W]
    k = zm[:, RWKV_W:2 * RWKV_W]
    v = zm[:, 2 * RWKV_W:3 * RWKV_W]
    xwa = zm[:, 3 * RWKV_W:3 * RWKV_W + LORA_W]
    xg = zm[:, 3 * RWKV_W + LORA_W:]
    w_pre = w0_ref[...] + mm(jnp.tanh(xwa), w2_ref[...])
    softplus = jnp.maximum(-w_pre, 0.0) + jnp.log(1.0 + jnp.exp(-jnp.abs(w_pre)))
    w_log = -softplus - 0.5
    lw = -jnp.exp(w_log)
    a = _sigmoid(a0_ref[...] + mm(xwa, a2_ref[...]))
    g = mm(_sigmoid(xg), g2_ref[...])
    kk_raw = k * kk_ref[...]
    k2 = k * (1.0 + (a - 1.0) * ka_ref[...])
    if t_valid < L:
        live = row < t_valid
        lw = jnp.where(live, lw, 0.0)
        kk_raw = jnp.where(live, kk_raw, 0.0)
        k2 = jnp.where(live, k2, 0.0)
        v = jnp.where(live, v, 0.0)

    ti = lax.broadcasted_iota(jnp.int32, (L, L), 0)
    si = lax.broadcasted_iota(jnp.int32, (L, L), 1)
    incl = ti >= si
    strict = ti > si
    clw = _mmp(_dot, [jnp.where(incl, 1.0, 0.0).astype(BF16)], _parts(lw, 3))
    mid = max(L // 2 - 1, 0)
    clw_mid = clw[mid:mid + 1, :]
    rel = clw - clw_mid
    p_incl = jnp.exp(rel)
    p_inv = jnp.exp(-rel)
    p_prev = jnp.exp(rel - lw)
    p_mid = jnp.exp(clw_mid)
    rt_all = r * p_incl
    kt_all = k2 * p_inv
    rk_all = r * k2 * rk_ref[...]

    sp, yp = STATE_PASSES, Y_PASSES
    kkh = [kk_raw[:, s] for s in sls]
    kkn = [x / jnp.maximum(jnp.sqrt(jnp.sum(x * x, axis=-1, keepdims=True)), 1e-12) for x in kkh]
    at = [-kkn[h] * p_prev[:, sls[h]] for h in heads]
    bt = [kkn[h] * a[:, sls[h]] * p_inv[:, sls[h]] for h in heads]
    rt = [rt_all[:, s] for s in sls]
    kt = [kt_all[:, s] for s in sls]
    vh = [v[:, s] for s in sls]
    at_s = [_parts(x, sp) for x in at]
    bt_s = [_parts(x, sp) for x in bt]
    kt_s = [_parts(x, sp) for x in kt]
    vh_s = [_parts(x, sp) for x in vh]
    rt_y = [_parts(x, yp) for x in rt]
    a_ab = [jnp.where(strict, _mmp(_dot_nt, at_s[h], bt_s[h]), 0.0) for h in heads]
    a_ak = [jnp.where(strict, _mmp(_dot_nt, at_s[h], kt_s[h]), 0.0) for h in heads]
    a_rb = [jnp.where(incl, _mmp(_dot_nt, rt_y[h], bt_s[h][:yp]), 0.0) for h in heads]
    a_rk = [jnp.where(incl, _mmp(_dot_nt, rt_y[h], kt_s[h][:yp]), 0.0) for h in heads]
    x = [jnp.concatenate([at[h], _mmp(_dot, _parts(a_ak[h], sp), vh_s[h])], axis=1) for h in heads]
    p = a_ab
    for level in range(max(int(math.log2(L)), 1)):
        if level > 0:
            p = [_mmp(_dot, ps, ps) for ps in p_s]
        p_s = [_parts(q, sp) for q in p]
        x = [x[h] + _mmp(_dot, p_s[h], _parts(x[h], sp)) for h in heads]
    x_s = [_parts(q, sp) for q in x]
    qy = [_mmp(_dot, _parts(a_rb[h], yp), x_s[h][:yp]) for h in heads]
    y0 = [qy[h][:, RWKV_HEAD:] + _mmp(_dot, _parts(a_rk[h], yp), vh_s[h][:yp]) for h in heads]
    qh = [rt[h] + qy[h][:, :RWKV_HEAD] for h in heads]
    s0 = [wkv_ref[0, h] * p_mid[:, sls[h]] for h in heads]
    s0_s = [_parts(q, sp) for q in s0]
    y = [_mmp(_dot_nt, _parts(qh[h], yp), s0_s[h][:yp]) + y0[h] for h in heads]
    wtb = [_mmp(_dot_tn, [q[:, :RWKV_HEAD] for q in x_s[h]], bt_s[h]) for h in heads]
    uv_s = [[jnp.concatenate([x_s[h][i][:, RWKV_HEAD:], vh_s[h][i]], axis=0) for i in range(sp)] for h in heads]
    bk_s = [[jnp.concatenate([bt_s[h][i], kt_s[h][i]], axis=0) for i in range(sp)] for h in heads]
    for h in heads:
        s_new = s0[h] + _mmp(_dot, s0_s[h], _parts(wtb[h], sp)) + _mmp(_dot_tn, uv_s[h], bk_s[h])
        wkv_ref[0, h] = s_new * p_incl[L - 1:L, sls[h]]
    for h in heads:
        mu_y = jnp.mean(y[h], axis=-1, keepdims=True)
        yc = y[h] - mu_y
        var = jnp.mean(yc * yc, axis=-1, keepdims=True)
        yn = yc * lax.rsqrt(var + GN_EPS) * lnw_ref[:, sls[h]] + lnb_ref[:, sls[h]]
        bonus = jnp.sum(rk_all[:, sls[h]], axis=-1, keepdims=True) * vh[h]
        o_ref[0, :, sls[h]] = ((yn + bonus) * g[:, sls[h]]).astype(o_ref.dtype)


def _rwkv(zr, shift0, wkv0, wts, chunk, t_valid):
    batch, t, _ = zr.shape
    n_chunks = t // chunk
    assert t_valid == t or n_chunks == 1
    body = functools.partial(_rwkv_body, chunk=chunk, t_valid=min(t_valid, chunk))
    vec = lambda width: pl.BlockSpec((1, width), lambda b, c: (0, 0))
    mat = lambda rows: pl.BlockSpec((rows, RWKV_W), lambda b, c: (0, 0))
    return pl.pallas_call(
        body,
        grid=(batch, n_chunks),
        in_specs=[pl.BlockSpec((1, chunk, RWKV_COLS), lambda b, c: (b, c, 0)),
                  pl.BlockSpec((1, 1, RWKV_COLS), lambda b, c: (b, 0, 0)),
                  pl.BlockSpec((1, RWKV_HEADS, RWKV_HEAD, RWKV_HEAD), lambda b, c: (b, 0, 0, 0)),
                  vec(RWKV_COLS), vec(RWKV_W), mat(LORA_W), vec(RWKV_W), mat(LORA_W), mat(GATE_LORA),
                  vec(RWKV_W), vec(RWKV_W), vec(RWKV_W), vec(RWKV_W), vec(RWKV_W)],
        out_specs=[pl.BlockSpec((1, chunk, RWKV_W), lambda b, c: (b, c, 0)),
                   pl.BlockSpec((1, RWKV_HEADS, RWKV_HEAD, RWKV_HEAD), lambda b, c: (b, 0, 0, 0))],
        out_shape=[jax.ShapeDtypeStruct((batch, t, RWKV_W), BF16),
                   jax.ShapeDtypeStruct((batch, RWKV_HEADS, RWKV_HEAD, RWKV_HEAD), F32)],
        scratch_shapes=[pltpu.VMEM((1, RWKV_COLS), F32)],
        compiler_params=_cparams(("parallel", "arbitrary")),
        name="rwkv_scan",
    )(zr, shift0, wkv0, *wts)


def _mix_out_body(x_ref, o0_ref, o1_ref, o2_ref, l0_ref, l1_ref, l2_ref, orw_ref, gate_ref, woa_ref, wor_ref,
                  wo_ref, gffn_ref, rwt_ref, rb_ref, *rest):
    x1_ref, h2_ref, te_ref, tg_ref = rest[-4:]
    l0, l1, l2 = l0_ref[...], l1_ref[...], l2_ref[...]
    m = jnp.maximum(jnp.maximum(l0, l1), l2)
    e0, e1, e2 = jnp.exp(l0 - m), jnp.exp(l1 - m), jnp.exp(l2 - m)
    o_att = (e0 * o0_ref[...].astype(F32) + e1 * o1_ref[...].astype(F32) + e2 * o2_ref[...].astype(F32)) \
        * (1.0 / (e0 + e1 + e2))
    gates = gate_ref[...].astype(F32)
    merged = gates[:, :D_MODEL] * _dot(o_att.astype(BF16), woa_ref[...]) \
        + gates[:, D_MODEL:] * _dot(orw_ref[...], wor_ref[...])
    x1 = x_ref[...] + _dot(merged.astype(BF16), wo_ref[...])
    x1_ref[...] = x1
    h2 = _rms(x1, gffn_ref[...])
    h2_ref[...] = h2

    logits = _mm3(_dot_nt, rwt_ref[...], h2) + rb_ref[...]
    e_iota = lax.broadcasted_iota(jnp.int32, logits.shape, 0)
    vals, idxs = [], []
    for _ in range(TOP_K):
        top = jnp.max(logits, axis=0, keepdims=True)
        idx = jnp.min(jnp.where(logits == top, e_iota, N_EXPERTS), axis=0, keepdims=True)
        vals.append(top)
        idxs.append(idx)
        logits = jnp.where(e_iota == idx, -jnp.inf, logits)
    exps = [jnp.exp(t - vals[0]) for t in vals]
    inv = 1.0 / (exps[0] + exps[1] + exps[2] + exps[3])
    te_ref[...] = jnp.concatenate(idxs, axis=0)
    tg_ref[...] = jnp.concatenate([e * inv for e in exps], axis=0)


def _mix_out(x, o_g, lse_g, o_rwkv, gates, wts, tm, n_all, row0, bufs):
    n = x.shape[0]
    blk0 = row0 // tm
    row = lambda i: (i, 0)
    const = lambda i: (0, 0)
    out_row = lambda i: (i + blk0, 0)
    out_col = lambda i: (0, i + blk0)
    tok = lambda w: pl.BlockSpec((tm, w), row)
    in_specs = [tok(D_MODEL)] + [tok(GROUP_W)] * 6 + [tok(RWKV_W), tok(GATE_COLS),
                pl.BlockSpec((GROUP_W, D_MODEL), const), pl.BlockSpec((RWKV_W, D_MODEL), const),
                pl.BlockSpec((D_MODEL, D_MODEL), const), pl.BlockSpec((1, D_MODEL), const),
                pl.BlockSpec((N_EXPERTS, D_MODEL), const), pl.BlockSpec((N_EXPERTS, 1), const)]
    args = [x, *o_g, *lse_g, o_rwkv, gates, *wts]
    aliases = {}
    if bufs is not None:
        in_specs += [pl.BlockSpec(memory_space=pl.ANY)] * 4
        aliases = {len(args) + j: j for j in range(4)}
        args += list(bufs)
    return pl.pallas_call(
        _mix_out_body,
        grid=(n // tm,),
        in_specs=in_specs,
        out_specs=[pl.BlockSpec((tm, D_MODEL), out_row), pl.BlockSpec((tm, D_MODEL), out_row),
                   pl.BlockSpec((TOP_K, tm), out_col), pl.BlockSpec((TOP_K, tm), out_col)],
        out_shape=[jax.ShapeDtypeStruct((n_all, D_MODEL), F32), jax.ShapeDtypeStruct((n_all, D_MODEL), F32),
                   jax.ShapeDtypeStruct((TOP_K, n_all), jnp.int32), jax.ShapeDtypeStruct((TOP_K, n_all), F32)],
        input_output_aliases=aliases,
        compiler_params=_cparams(("parallel",)),
        name="mix_out",
    )(*args)


def _moe_body(be_ref, nused_ref, idx_hbm, h2_hbm, w1_ref, b1_ref, w2_ref, b2_ref, gate_ref, y_hbm,
              idx_smem, xbuf, ybuf, w1b, w2b, isem, gsem, ssem):
    i = pl.program_id(0)
    slot = i % 2
    nused = nused_ref[0]

    def idx_copy(rec, s):
        return pltpu.make_async_copy(idx_hbm.at[pl.ds(pl.multiple_of(rec * IDX_REC, IDX_REC), IDX_REC)],
                                     idx_smem.at[pl.ds(pl.multiple_of(s * IDX_REC, IDX_REC), IDX_REC)],
                                     isem.at[s])

    def gather_start(s_idx, s_buf):
        def one(r, carry):
            t = idx_smem[s_idx * IDX_REC + r]
            pltpu.make_async_copy(h2_hbm.at[pl.ds(t, 1)], xbuf.at[s_buf, pl.ds(r, 1)], gsem.at[s_buf]).start()
            return carry
        lax.fori_loop(0, MOE_BLOCK, one, 0, unroll=8)

    def gather_wait(s_buf):
        pltpu.make_async_copy(h2_hbm.at[pl.ds(0, MOE_BLOCK)], xbuf.at[s_buf], gsem.at[s_buf]).wait()

    def scatter_start(s_idx, s_buf):
        def one(r, carry):
            t = idx_smem[s_idx * IDX_REC + MOE_BLOCK + r]
            pltpu.make_async_copy(ybuf.at[s_buf, pl.ds(r, 1)], y_hbm.at[pl.ds(t, 1)], ssem.at[s_buf]).start()
            return carry
        lax.fori_loop(0, MOE_BLOCK, one, 0, unroll=8)

    def scatter_wait(s_buf):
        pltpu.make_async_copy(ybuf.at[s_buf], y_hbm.at[pl.ds(0, MOE_BLOCK)], ssem.at[s_buf]).wait()

    @pl.when(i == 0)
    def _():
        idx_copy(0, 1).start()
        idx_copy(0, 1).wait()
        gather_start(1, 0)
        idx_copy(1, 0).start()
        idx_copy(1, 0).wait()

    @pl.when(i < nused)
    def _():
        idx_copy(i + 2, 1 - slot).start()

        @pl.when(i + 1 < nused)
        def _():
            gather_start(slot, 1 - slot)

        @pl.when((i == 0) | (be_ref[i] != be_ref[jnp.maximum(i - 1, 0)]))
        def _():
            w1b[...] = w1_ref[0].astype(BF16)
            w2b[...] = w2_ref[0].astype(BF16)

        gather_wait(slot)
        u = _dot(xbuf[slot].astype(BF16), w1b[...]) + b1_ref[0]
        glu = jnp.minimum(u[:, :D_MODEL], SWIGLU_LIMIT)
        lin = jnp.clip(u[:, D_MODEL:], -SWIGLU_LIMIT, SWIGLU_LIMIT)
        act = glu * _sigmoid(SWIGLU_ALPHA * glu) * (lin + 1.0)
        y = _dot(act.astype(BF16), w2b[...]) + b2_ref[0]
        ybuf[slot] = y * gate_ref[...]
        scatter_start(slot, slot)

        @pl.when(i >= 1)
        def _():
            scatter_wait(1 - slot)

        @pl.when(i == nused - 1)
        def _():
            scatter_wait(slot)

        idx_copy(i + 2, 1 - slot).wait()


def _moe_experts(h2, idx_rec, block_e, n_used, row_gate, w1, b1, w2, b2, n_slots):
    n_blocks = block_e.shape[0]
    grid_spec = pltpu.PrefetchScalarGridSpec(
        num_scalar_prefetch=2,
        grid=(n_blocks,),
        in_specs=[pl.BlockSpec(memory_space=pl.ANY),
                  pl.BlockSpec(memory_space=pl.ANY),
                  pl.BlockSpec((1, D_MODEL, 2 * D_MODEL), lambda i, be, nu: (be[i], 0, 0)),
                  pl.BlockSpec((1, 1, 2 * D_MODEL), lambda i, be, nu: (be[i], 0, 0)),
                  pl.BlockSpec((1, D_MODEL, D_MODEL), lambda i, be, nu: (be[i], 0, 0)),
                  pl.BlockSpec((1, 1, D_MODEL), lambda i, be, nu: (be[i], 0, 0)),
                  pl.BlockSpec((MOE_BLOCK, 1), lambda i, be, nu: (i, 0))],
        out_specs=pl.BlockSpec(memory_space=pl.ANY),
        scratch_shapes=[pltpu.SMEM((2 * IDX_REC,), jnp.int32),
                        pltpu.VMEM((2, MOE_BLOCK, D_MODEL), F32),
                        pltpu.VMEM((2, MOE_BLOCK, D_MODEL), F32),
                        pltpu.VMEM((D_MODEL, 2 * D_MODEL), BF16),
                        pltpu.VMEM((D_MODEL, D_MODEL), BF16),
                        pltpu.SemaphoreType.DMA((2,)),
                        pltpu.SemaphoreType.DMA((2,)),
                        pltpu.SemaphoreType.DMA((2,))],
    )
    return pl.pallas_call(
        _moe_body,
        grid_spec=grid_spec,
        out_shape=jax.ShapeDtypeStruct((n_slots, D_MODEL), F32),
        compiler_params=_cparams(("arbitrary",)),
        name="moe_experts",
    )(block_e, n_used, idx_rec, h2, w1, b1, w2, b2, row_gate)


def _route(top_e, top_g):
    n_tok = top_e.shape[1]
    n_asg = n_tok * TOP_K
    flat_e = top_e.T.reshape(-1)
    order = jnp.argsort(flat_e).astype(jnp.int32)
    se = flat_e[order]
    counts = jnp.bincount(flat_e, length=N_EXPERTS).astype(jnp.int32)
    padded = (counts + MOE_BLOCK - 1) // MOE_BLOCK * MOE_BLOCK
    pad_end = jnp.cumsum(padded)
    start = jnp.cumsum(counts) - counts
    dest = (pad_end - padded)[se] + jnp.arange(n_asg, dtype=jnp.int32) - start[se]
    n_blocks = -(-(n_asg + N_EXPERTS * (MOE_BLOCK - 1)) // MOE_BLOCK)
    n_rows = n_blocks * MOE_BLOCK
    blk = jnp.arange(n_rows, dtype=jnp.int32) // MOE_BLOCK
    trash = n_asg + (blk % 2) * MOE_BLOCK + jnp.arange(n_rows, dtype=jnp.int32) % MOE_BLOCK
    row_tok = jnp.zeros((n_rows,), jnp.int32).at[dest].set(order // TOP_K)
    row_slot = trash.at[dest].set(order)
    row_gate = jnp.zeros((n_rows,), F32).at[dest].set(top_g.T.reshape(-1)[order])
    block_e = jnp.minimum(jnp.searchsorted(pad_end, jnp.arange(n_blocks, dtype=jnp.int32) * MOE_BLOCK,
                                           side='right'), N_EXPERTS - 1).astype(jnp.int32)
    n_used = (pad_end[-1] // MOE_BLOCK).astype(jnp.int32).reshape(1)
    tok_rec = jnp.concatenate([row_tok.reshape(n_blocks, MOE_BLOCK),
                               jnp.zeros((2, MOE_BLOCK), jnp.int32)], axis=0)
    slot_rec = jnp.concatenate([jnp.zeros((1, MOE_BLOCK), jnp.int32), row_slot.reshape(n_blocks, MOE_BLOCK),
                                jnp.zeros((1, MOE_BLOCK), jnp.int32)], axis=0)
    idx_rec = jnp.concatenate([tok_rec, slot_rec, jnp.zeros((n_blocks + 2, IDX_REC - 2 * MOE_BLOCK), jnp.int32)],
                              axis=1).reshape(-1)
    return idx_rec, block_e, n_used, row_gate.reshape(n_rows, 1), n_asg + 2 * MOE_BLOCK


def _tail_body(x1_ref, ys_ref, pe_ref, gple_ref, wpg_ref, wp_ref, gfin_ref, y_ref):
    ys = ys_ref[...]
    moe = (ys[:, 0:D_MODEL] + ys[:, D_MODEL:2 * D_MODEL]) + (ys[:, 2 * D_MODEL:3 * D_MODEL] + ys[:, 3 * D_MODEL:])
    x2 = x1_ref[...] + moe
    gate = _sigmoid(_dot(_rms(x2, gple_ref[...]).astype(BF16), wpg_ref[...]))
    x3 = x2 + gate * _dot(pe_ref[...].astype(BF16), wp_ref[...])
    y_ref[...] = _rms(x3, gfin_ref[...])


def _tail(x1_all, y_slots, pe, wts, tm, row0):
    n = pe.shape[0]
    blk0 = row0 // tm
    row = lambda i: (i, 0)
    src = lambda i: (i + blk0, 0)
    const = lambda i: (0, 0)
    ys = y_slots.reshape(-1, TOP_K * D_MODEL)
    return pl.pallas_call(
        _tail_body,
        grid=(n // tm,),
        in_specs=[pl.BlockSpec((tm, D_MODEL), src), pl.BlockSpec((tm, TOP_K * D_MODEL), src),
                  pl.BlockSpec((tm, PLE_DIM), row), pl.BlockSpec((1, D_MODEL), const),
                  pl.BlockSpec((D_MODEL, D_MODEL), const), pl.BlockSpec((PLE_DIM, D_MODEL), const),
                  pl.BlockSpec((1, D_MODEL), const)],
        out_specs=pl.BlockSpec((tm, D_MODEL), row),
        out_shape=jax.ShapeDtypeStruct((n, D_MODEL), F32),
        compiler_params=_cparams(("parallel",)),
        name="tail",
    )(x1_all, ys, pe, *wts)


RWKV_CHUNK = 128
RWKV_CHUNK_SAMPLE = 8
STATE_PASSES = 1
Y_PASSES = 1
TM_PROMPT = 256


def kernel(x_prompt, x_sample, p_prompt, p_sample, cache_kv_w128, cache_kv_w512, cache_kv_w2048, state_rwkv_shift, state_rwkv_wkv, norm_mix_g, w_in, rwkv_mu, rwkv_w0, rwkv_w2, rwkv_a0, rwkv_a2, rwkv_g2, rwkv_k_k, rwkv_k_a, rwkv_r_k, rwkv_ln_w, rwkv_ln_b, w_out_attn, w_out_rwkv, w_out, norm_ffn_g, router_w, router_b, moe_w1, moe_b1, moe_w2, moe_b2, norm_ple_g, w_ple, w_ple_gate, norm_final_g):
    bp, seq, _ = x_prompt.shape
    bs, t_s, _ = x_sample.shape
    n_p, n_s = bp * seq, bs * t_s
    n_all = n_p + n_s
    assert w_in.shape[0] == 1, "single layer"
    caches = (cache_kv_w128, cache_kv_w512, cache_kv_w2048)

    row = lambda a: a.reshape(1, -1)
    w_in_b = w_in[0].astype(BF16)
    zeros64 = jnp.zeros((LORA_W // 2, RWKV_W), F32)
    rwkv_wts = (row(rwkv_mu[0]), row(rwkv_w0[0]), jnp.concatenate([rwkv_w2[0], zeros64], axis=0),
                row(rwkv_a0[0]), jnp.concatenate([zeros64, rwkv_a2[0]], axis=0), rwkv_g2[0],
                row(rwkv_k_k[0]), row(rwkv_k_a[0]), row(rwkv_r_k[0]), row(rwkv_ln_w[0]), row(rwkv_ln_b[0]))
    mix_wts = (w_out_attn[0].astype(BF16), w_out_rwkv[0].astype(BF16), w_out[0].astype(BF16),
               row(norm_ffn_g[0]), router_w[0].T, router_b[0].reshape(N_EXPERTS, 1))
    tail_wts = (row(norm_ple_g[0]), w_ple_gate[0].astype(BF16), w_ple[0].astype(BF16), row(norm_final_g))
    g_mix = row(norm_mix_g[0])

    xp = x_prompt.reshape(n_p, D_MODEL)
    q_p, kv_p, zr_p, gate_p = _in_proj(xp, g_mix, w_in_b, TM_PROMPT)
    att_p = [_attn_prompt(q_p, kv_p, gi, bp, seq) for gi in range(N_GROUPS)]
    orw_p, wkv_p = _rwkv(zr_p.reshape(bp, seq, RWKV_COLS), jnp.zeros((bp, 1, RWKV_COLS), F32),
                         jnp.zeros((bp, RWKV_HEADS, RWKV_HEAD, RWKV_HEAD), F32), rwkv_wts, RWKV_CHUNK, seq)
    bufs = _mix_out(xp, [a[0] for a in att_p], [a[1] for a in att_p], orw_p.reshape(n_p, RWKV_W), gate_p,
                    mix_wts, TM_PROMPT, n_all, 0, None)

    xs = x_sample.reshape(n_s, D_MODEL)
    q_s, kv_s, zr_s, gate_s = _in_proj(xs, g_mix, w_in_b, n_s)
    t_pad = 8
    pad_t = lambda a: jnp.pad(a.reshape(bs, t_s, -1), ((0, 0), (0, t_pad - t_s), (0, 0)))
    q_s3, kv_s3 = pad_t(q_s), pad_t(kv_s)
    att_s, new_caches = [], []
    for gi in range(N_GROUPS):
        wc = caches[gi].shape[2]
        o, lse, newc = _attn_sample(q_s3, kv_s3, caches[gi].reshape(bs, wc, 2 * GROUP_W), gi, t_s)
        att_s.append((o[:, :t_s].reshape(n_s, GROUP_W), lse[:, :t_s].reshape(n_s, GROUP_W)))
        new_caches.append(newc.reshape(1, bs, wc, 2, HEADS_PER_GROUP, HEAD_DIM))
    zr_s3 = jnp.pad(zr_s.reshape(bs, t_s, RWKV_COLS), ((0, 0), (0, RWKV_CHUNK_SAMPLE - t_s), (0, 0)))
    orw_s, wkv_s = _rwkv(zr_s3, state_rwkv_shift[0].reshape(bs, 1, RWKV_COLS), state_rwkv_wkv[0], rwkv_wts,
                         RWKV_CHUNK_SAMPLE, t_s)
    bufs = _mix_out(xs, [a[0] for a in att_s], [a[1] for a in att_s], orw_s[:, :t_s].reshape(n_s, RWKV_W),
                    gate_s, mix_wts, n_s, n_all, n_p, bufs)
    x1_all, h2_all, top_e, top_g = bufs

    idx_rec, block_e, n_used, row_gate, n_slots = _route(top_e, top_g)
    y_slots = _moe_experts(h2_all, idx_rec, block_e, n_used, row_gate, moe_w1[0],
                           moe_b1[0].reshape(N_EXPERTS, 1, 2 * D_MODEL), moe_w2[0],
                           moe_b2[0].reshape(N_EXPERTS, 1, D_MODEL), n_slots)

    y_p = _tail(x1_all, y_slots, p_prompt[0].reshape(n_p, PLE_DIM), tail_wts, TM_PROMPT, 0)
    y_s = _tail(x1_all, y_slots, p_sample[0].reshape(n_s, PLE_DIM), tail_wts, n_s, n_p)

    kv_p4 = kv_p.reshape(bp, seq, N_GROUPS, 2, HEADS_PER_GROUP, HEAD_DIM)
    kv_out_p = [kv_p4[:, seq - min(w, seq):, gi][None] for gi, (w, _) in enumerate(ATTN_GROUPS)]
    shift_p = zr_p.reshape(bp, seq, RWKV_COLS)[:, -1][None]
    shift_s = zr_s.reshape(bs, t_s, RWKV_COLS)[:, -1][None]
    return (y_p.reshape(bp, seq, D_MODEL), y_s.reshape(bs, t_s, D_MODEL),
            kv_out_p[0], kv_out_p[1], kv_out_p[2], shift_p, wkv_p[None],
            new_caches[0], new_caches[1], new_caches[2], shift_s, wkv_s[None])
```

```python
import functools
import math

import numpy as np
import jax
import jax.numpy as jnp
from jax import lax
from jax.experimental import pallas as pl
from jax.experimental.pallas import tpu as pltpu

F32 = jnp.float32
BF16 = jnp.bfloat16

LANE = 128
D_MODEL = 1024
N_GROUPS = 3
HEADS_PER_GROUP = 4
HEAD_DIM = 64
ATTN_GROUPS = ((128, 1), (512, 4), (2048, 16))
GROUP_W = HEADS_PER_GROUP * HEAD_DIM
Q_COLS = N_GROUPS * GROUP_W
KV_COLS = 2 * Q_COLS
BAND = 128

RWKV_HEADS = 8
RWKV_HEAD = 64
RWKV_W = 512
LORA_W = 128
GATE_LORA = 128
RWKV_COLS = 3 * RWKV_W + LORA_W + GATE_LORA
GN_EPS = 64e-5
GATE_COLS = 2 * D_MODEL
Z_RWKV0 = 3 * Q_COLS
Z_GATE0 = Z_RWKV0 + RWKV_COLS
IN_COLS = Z_GATE0 + GATE_COLS

N_EXPERTS = 32
TOP_K = 4
SWIGLU_LIMIT = 7.0
SWIGLU_ALPHA = 1.702
MOE_BLOCK = 256
MOE_TILE = 128
IDX_REC = 1024
PLE_DIM = 256
RMS_EPS = 1e-6

NEG_BIG = -1e30
VMEM_LIMIT = 56 * 1024 * 1024


def _cparams(sem):
    return pltpu.CompilerParams(dimension_semantics=sem, vmem_limit_bytes=VMEM_LIMIT)


def _rms(x, g):
    return x * lax.rsqrt(jnp.mean(x * x, axis=-1, keepdims=True) + RMS_EPS) * g


def _sigmoid(x):
    return 1.0 / (1.0 + jnp.exp(-x))


def _dot(a, b):
    return jnp.dot(a, b, preferred_element_type=F32)


def _dot_nt(a, b):
    return lax.dot_general(a, b, (((1,), (1,)), ((), ())), preferred_element_type=F32)


def _dot_tn(a, b):
    return lax.dot_general(a, b, (((0,), (0,)), ((), ())), preferred_element_type=F32)


def _split(x):
    hi = x.astype(BF16)
    lo = (x - hi.astype(F32)).astype(BF16)
    return hi, lo


def _mm3(dot, a, b):
    ah, al = _split(a)
    bh, bl = _split(b)
    return dot(ah, bh) + (dot(ah, bl) + dot(al, bh))


def _parts(x, n):
    out = []
    for _ in range(n - 1):
        hi = x.astype(BF16)
        out.append(hi)
        x = x - hi.astype(F32)
    out.append(x.astype(BF16))
    return out


def _mmp(dot, ap, bp):
    order = max(len(ap), len(bp))
    acc = None
    for i, a in enumerate(ap):
        for j, b in enumerate(bp):
            if i + j < order:
                t = dot(a, b)
                acc = t if acc is None else acc + t
    return acc


def _in_proj_body(x_ref, g_ref, w_ref, q_ref, kv_ref, zr_ref, gate_ref):
    h = _rms(x_ref[...], g_ref[...]).astype(BF16)

    def proj(lo, width):
        return _dot(h, w_ref[:, lo:lo + width])

    q_ref[...] = (proj(0, Q_COLS) * (1.0 / math.sqrt(HEAD_DIM))).astype(BF16)
    for g in range(N_GROUPS):
        kv_ref[:, 2 * g * GROUP_W:(2 * g + 1) * GROUP_W] = proj(Q_COLS + g * GROUP_W, GROUP_W)
        kv_ref[:, (2 * g + 1) * GROUP_W:(2 * g + 2) * GROUP_W] = proj(2 * Q_COLS + g * GROUP_W, GROUP_W)
    zr_ref[...] = proj(Z_RWKV0, RWKV_COLS)
    gate_ref[...] = _sigmoid(proj(Z_GATE0, GATE_COLS)).astype(BF16)


def _in_proj_dilated_body(x_ref, g_ref, w_ref, *rest):
    qd_refs, kvd_refs = rest[0:N_GROUPS], rest[N_GROUPS:2 * N_GROUPS]
    kv_ref, zr_ref, gate_ref, st_ref = rest[2 * N_GROUPS:]
    tm = x_ref.shape[0]
    h = _rms(x_ref[...], g_ref[...]).astype(BF16)

    def proj(lo, width):
        return _dot(h, w_ref[:, lo:lo + width])

    n_q = Q_COLS // LANE

    def stage(slab0, val):
        for s in range(val.shape[1] // LANE):
            st_ref[slab0 + s] = val[:, s * LANE:(s + 1) * LANE]

    stage(0, proj(0, Q_COLS) * (1.0 / math.sqrt(HEAD_DIM)))
    for g in range(N_GROUPS):
        for part, src in ((0, Q_COLS), (1, 2 * Q_COLS)):
            col = (2 * g + part) * GROUP_W
            val = proj(src + g * GROUP_W, GROUP_W)
            kv_ref[:, col:col + GROUP_W] = val
            stage(n_q + col // LANE, val)
    for g, (_, dil) in enumerate(ATTN_GROUPS):
        for r in range(dil):
            rows = pl.ds(r, tm // dil, stride=dil) if dil > 1 else slice(None)
            for s in range(GROUP_W // LANE):
                qd_refs[g][0, r, :, s * LANE:(s + 1) * LANE] = st_ref[g * GROUP_W // LANE + s, rows, :].astype(BF16)
            for s in range(2 * GROUP_W // LANE):
                kvd_refs[g][0, r, :, s * LANE:(s + 1) * LANE] = \
                    st_ref[n_q + 2 * g * GROUP_W // LANE + s, rows, :].astype(BF16)
    zr_ref[...] = proj(Z_RWKV0, RWKV_COLS)
    gate_ref[...] = _sigmoid(proj(Z_GATE0, GATE_COLS)).astype(BF16)


def _in_proj_dilated(x, g, w_bf16, tm, batch, seq):
    n = x.shape[0]
    tiles = seq // tm
    row = lambda i: (i, 0)
    const = lambda i: (0, 0)
    sub = lambda i: (i // tiles, 0, i % tiles, 0)
    dils = [d for _, d in ATTN_GROUPS]
    assert all(tm % (16 * d) == 0 for d in dils)
    outs = pl.pallas_call(
        _in_proj_dilated_body,
        grid=(n // tm,),
        in_specs=[pl.BlockSpec((tm, D_MODEL), row),
                  pl.BlockSpec((1, D_MODEL), const),
                  pl.BlockSpec((D_MODEL, IN_COLS), const)],
        out_specs=[pl.BlockSpec((1, d, tm // d, GROUP_W), sub) for d in dils]
                  + [pl.BlockSpec((1, d, tm // d, 2 * GROUP_W), sub) for d in dils]
                  + [pl.BlockSpec((tm, KV_COLS), row),
                     pl.BlockSpec((tm, RWKV_COLS), row),
                     pl.BlockSpec((tm, GATE_COLS), row)],
        out_shape=[jax.ShapeDtypeStruct((batch, d, seq // d, GROUP_W), BF16) for d in dils]
                  + [jax.ShapeDtypeStruct((batch, d, seq // d, 2 * GROUP_W), BF16) for d in dils]
                  + [jax.ShapeDtypeStruct((n, KV_COLS), F32),
                     jax.ShapeDtypeStruct((n, RWKV_COLS), F32),
                     jax.ShapeDtypeStruct((n, GATE_COLS), BF16)],
        scratch_shapes=[pltpu.VMEM(((Q_COLS + KV_COLS) // LANE, tm, LANE), F32)],
        compiler_params=_cparams(("parallel",)),
        name="in_proj_dilated",
    )(x, g, w_bf16)
    return outs[0:N_GROUPS], outs[N_GROUPS:2 * N_GROUPS], outs[2 * N_GROUPS], outs[2 * N_GROUPS + 1], outs[2 * N_GROUPS + 2]


def _in_proj(x, g, w_bf16, tm):
    n = x.shape[0]
    row = lambda i: (i, 0)
    const = lambda i: (0, 0)
    return pl.pallas_call(
        _in_proj_body,
        grid=(n // tm,),
        in_specs=[pl.BlockSpec((tm, D_MODEL), row),
                  pl.BlockSpec((1, D_MODEL), const),
                  pl.BlockSpec((D_MODEL, IN_COLS), const)],
        out_specs=[pl.BlockSpec((tm, Q_COLS), row),
                   pl.BlockSpec((tm, KV_COLS), row),
                   pl.BlockSpec((tm, RWKV_COLS), row),
                   pl.BlockSpec((tm, GATE_COLS), row)],
        out_shape=[jax.ShapeDtypeStruct((n, Q_COLS), BF16),
                   jax.ShapeDtypeStruct((n, KV_COLS), F32),
                   jax.ShapeDtypeStruct((n, RWKV_COLS), F32),
                   jax.ShapeDtypeStruct((n, GATE_COLS), BF16)],
        compiler_params=_cparams(("parallel",)),
        name="in_proj",
    )(x, g, w_bf16)


def _alibi_slopes(gi):
    return [2.0 ** (-8.0 * (gi * HEADS_PER_GROUP + h + 1) / (N_GROUPS * HEADS_PER_GROUP))
            for h in range(HEADS_PER_GROUP)]


def _head_of_lane(shape):
    return lax.broadcasted_iota(jnp.int32, shape, len(shape) - 1) // HEAD_DIM


def _stack_heads(q):
    hl = _head_of_lane(q.shape)
    return jnp.concatenate([jnp.where(hl == h, q, jnp.zeros_like(q)) for h in range(HEADS_PER_GROUP)], axis=0)


def _unstack_heads(x4, rows):
    hl = _head_of_lane((rows, GROUP_W))
    out = x4[0:rows]
    for h in range(1, HEADS_PER_GROUP):
        out = jnp.where(hl == h, x4[h * rows:(h + 1) * rows], out)
    return out


def _attn_prompt_body(q_ref, kvp_ref, kvc_ref, o_ref, lse_ref, kb_ref, bias_ref, *, dil, slopes, n_sub):
    c = pl.program_id(2)
    kb_ref[0:BAND, :] = kvp_ref[0, 0]
    kb_ref[BAND:, :] = kvc_ref[0, 0]

    qi = lax.broadcasted_iota(jnp.int32, (BAND, 2 * BAND), 0) + BAND
    ki = lax.broadcasted_iota(jnp.int32, (BAND, 2 * BAND), 1)
    rel = qi - ki
    dist = jnp.where((rel >= 0) & (rel <= BAND), (dil * rel).astype(F32), -NEG_BIG / slopes[-1])
    for h in range(HEADS_PER_GROUP):
        bias_ref[h * BAND:(h + 1) * BAND, :] = -slopes[h] * dist

    def sub_block(n, carry):
        r0 = pl.multiple_of(n * BAND, BAND)
        q4 = _stack_heads(q_ref[0, 0, pl.ds(r0, BAND), :])
        kv = kb_ref[pl.ds(r0, 2 * BAND), :]
        s = _dot_nt(q4, kv[:, :GROUP_W]) + bias_ref[...]
        n_before_start = jnp.where((c == 0) & (n == 0), BAND, 0)
        kcol = lax.broadcasted_iota(jnp.int32, s.shape, 1)
        s = jnp.where(kcol < n_before_start, NEG_BIG, s)
        m = jnp.max(s, axis=-1, keepdims=True)
        e = jnp.exp(s - m)
        den = jnp.sum(e, axis=-1, keepdims=True)
        o4 = _dot(e.astype(BF16), kv[:, GROUP_W:]) * (1.0 / den)
        o_ref[0, 0, pl.ds(r0, BAND), :] = _unstack_heads(o4, BAND).astype(o_ref.dtype)
        lse4 = jnp.broadcast_to(m + jnp.log(den), (HEADS_PER_GROUP * BAND, GROUP_W))
        lse_ref[0, 0, pl.ds(r0, BAND), :] = _unstack_heads(lse4, BAND)
        return carry

    lax.fori_loop(0, n_sub, sub_block, 0)


def _attn_prompt(qd, kvd, gi):
    window, dil = ATTN_GROUPS[gi]
    assert window // dil == BAND
    batch, _, sub_len, _ = qd.shape
    chunk = min(sub_len, 1024)
    n_chunks = sub_len // chunk
    sub_per_chunk = chunk // BAND
    body = functools.partial(_attn_prompt_body, dil=dil, slopes=_alibi_slopes(gi), n_sub=sub_per_chunk)
    cur = lambda b, r, c: (b, r, c, 0)
    return pl.pallas_call(
        body,
        grid=(batch, dil, n_chunks),
        in_specs=[
            pl.BlockSpec((1, 1, chunk, GROUP_W), cur),
            pl.BlockSpec((1, 1, BAND, 2 * GROUP_W),
                         lambda b, r, c: (b, r, jnp.maximum(c * sub_per_chunk - 1, 0), 0)),
            pl.BlockSpec((1, 1, chunk, 2 * GROUP_W), cur),
        ],
        out_specs=[pl.BlockSpec((1, 1, chunk, GROUP_W), cur), pl.BlockSpec((1, 1, chunk, GROUP_W), cur)],
        out_shape=[jax.ShapeDtypeStruct(qd.shape, BF16), jax.ShapeDtypeStruct(qd.shape, F32)],
        scratch_shapes=[pltpu.VMEM((chunk + BAND, 2 * GROUP_W), BF16),
                        pltpu.VMEM((HEADS_PER_GROUP * BAND, 2 * BAND), F32)],
        compiler_params=_cparams(("parallel", "parallel", "arbitrary")),
        name=f"attn_prompt_g{gi}",
    )(qd, kvd, kvd)


def _attn_sample_body(q_ref, kvn_ref, cache_ref, o_ref, lse_ref, newc_ref, *, dil, slopes, wc, t_new):
    tp = q_ref.shape[1]
    cache = cache_ref[0]
    new = kvn_ref[0]
    newc_ref[0, 0:wc - t_new, :] = cache[t_new:wc]
    newc_ref[0, wc - t_new:wc, :] = new[0:t_new]

    q4 = _stack_heads(q_ref[0])
    cb = cache.astype(BF16)
    nb = new.astype(BF16)
    t_row = lax.broadcasted_iota(jnp.int32, (tp, 1), 0)
    slope_col = jnp.concatenate([jnp.full((tp, 1), s, F32) for s in slopes], axis=0)
    t4 = jnp.concatenate([t_row] * HEADS_PER_GROUP, axis=0)

    def scores(keys_bf16, dist):
        ok = (dist >= 0) & ((dist & (dil - 1)) == 0) & (dist <= BAND * dil)
        s = _dot_nt(q4, keys_bf16[:, :GROUP_W])
        return jnp.where(ok, s - slope_col * dist.astype(F32), NEG_BIG)

    pc = lax.broadcasted_iota(jnp.int32, (HEADS_PER_GROUP * tp, wc), 1)
    pn = lax.broadcasted_iota(jnp.int32, (HEADS_PER_GROUP * tp, tp), 1)
    s_c = scores(cb, wc + t4 - pc)
    s_n = scores(nb, jnp.where(pn < t_new, t4 - pn, -1))
    m = jnp.maximum(jnp.max(s_c, axis=-1, keepdims=True), jnp.max(s_n, axis=-1, keepdims=True))
    e_c = jnp.exp(s_c - m)
    e_n = jnp.exp(s_n - m)
    den = jnp.sum(e_c, axis=-1, keepdims=True) + jnp.sum(e_n, axis=-1, keepdims=True)
    o4 = (_dot(e_c.astype(BF16), cb[:, GROUP_W:]) + _dot(e_n.astype(BF16), nb[:, GROUP_W:])) * (1.0 / den)
    o_ref[0] = _unstack_heads(o4, tp).astype(o_ref.dtype)
    lse4 = jnp.broadcast_to(m + jnp.log(den), (HEADS_PER_GROUP * tp, GROUP_W))
    lse_ref[0] = _unstack_heads(lse4, tp)


def _attn_sample(q, kv, cache, gi, t_new):
    window, dil = ATTN_GROUPS[gi]
    batch, tp, _ = q.shape
    wc = cache.shape[1]
    assert wc + 0 - dil * BAND >= 0
    body = functools.partial(_attn_sample_body, dil=dil, slopes=_alibi_slopes(gi), wc=wc, t_new=t_new)
    return pl.pallas_call(
        body,
        grid=(batch,),
        in_specs=[pl.BlockSpec((1, tp, GROUP_W), lambda b: (b, 0, gi)),
                  pl.BlockSpec((1, tp, 2 * GROUP_W), lambda b: (b, 0, gi)),
                  pl.BlockSpec((1, wc, 2 * GROUP_W), lambda b: (b, 0, 0))],
        out_specs=[pl.BlockSpec((1, tp, GROUP_W), lambda b: (b, 0, 0)),
                   pl.BlockSpec((1, tp, GROUP_W), lambda b: (b, 0, 0)),
                   pl.BlockSpec((1, wc, 2 * GROUP_W), lambda b: (b, 0, 0))],
        out_shape=[jax.ShapeDtypeStruct((batch, tp, GROUP_W), BF16),
                   jax.ShapeDtypeStruct((batch, tp, GROUP_W), F32),
                   jax.ShapeDtypeStruct((batch, wc, 2 * GROUP_W), F32)],
        compiler_params=_cparams(("parallel",)),
        name=f"attn_sample_g{gi}",
    )(q, kv, cache)


def _rwkv_body(zr_ref, shift0_ref, wkv0_ref, mu_ref, w0_ref, w2_ref, a0_ref, a2_ref, g2_ref, kk_ref, ka_ref,
               rk_ref, lnw_ref, lnb_ref, o_ref, wkv_ref, prev_ref, *, chunk, t_valid):
    c = pl.program_id(1)
    L = chunk
    mm = functools.partial(_mm3, _dot)
    heads = range(RWKV_HEADS)
    sls = [slice(h * RWKV_HEAD, (h + 1) * RWKV_HEAD) for h in heads]

    @pl.when(c == 0)
    def _():
        prev_ref[...] = shift0_ref[0]
        wkv_ref[0] = wkv0_ref[0]

    zr = zr_ref[0]
    row = lax.broadcasted_iota(jnp.int32, (L, 1), 0)
    prev = jnp.where(row == 0, prev_ref[...], pltpu.roll(zr, 1, 0))
    prev_ref[...] = zr[L - 1:L, :]
    zm = zr + (prev - zr) * mu_ref[...]

    r = zm[:, 0:RWKV_W]
    k = zm[:, RWKV_W:2 * RWKV_W]
    v = zm[:, 2 * RWKV_W:3 * RWKV_W]
    xwa = zm[:, 3 * RWKV_W:3 * RWKV_W + LORA_W]
    xg = zm[:, 3 * RWKV_W + LORA_W:]
    w_pre = w0_ref[...] + mm(jnp.tanh(xwa), w2_ref[...])
    softplus = jnp.maximum(-w_pre, 0.0) + jnp.log(1.0 + jnp.exp(-jnp.abs(w_pre)))
    w_log = -softplus - 0.5
    lw = -jnp.exp(w_log)
    a = _sigmoid(a0_ref[...] + mm(xwa, a2_ref[...]))
    g = mm(_sigmoid(xg), g2_ref[...])
    kk_raw = k * kk_ref[...]
    k2 = k * (1.0 + (a - 1.0) * ka_ref[...])
    if t_valid < L:
        live = row < t_valid
        lw = jnp.where(live, lw, 0.0)
        kk_raw = jnp.where(live, kk_raw, 0.0)
        k2 = jnp.where(live, k2, 0.0)
        v = jnp.where(live, v, 0.0)

    ti = lax.broadcasted_iota(jnp.int32, (L, L), 0)
    si = lax.broadcasted_iota(jnp.int32, (L, L), 1)
    incl = ti >= si
    strict = ti > si
    clw = _mmp(_dot, [jnp.where(incl, 1.0, 0.0).astype(BF16)], _parts(lw, 3))
    mid = max(L // 2 - 1, 0)
    clw_mid = clw[mid:mid + 1, :]
    rel = clw - clw_mid
    p_incl = jnp.exp(rel)
    p_inv = jnp.exp(-rel)
    p_prev = jnp.exp(rel - lw)
    p_mid = jnp.exp(clw_mid)
    rt_all = r * p_incl
    kt_all = k2 * p_inv
    rk_all = r * k2 * rk_ref[...]

    sp, yp = STATE_PASSES, Y_PASSES
    kkh = [kk_raw[:, s] for s in sls]
    kkn = [x / jnp.maximum(jnp.sqrt(jnp.sum(x * x, axis=-1, keepdims=True)), 1e-12) for x in kkh]
    at = [-kkn[h] * p_prev[:, sls[h]] for h in heads]
    bt = [kkn[h] * a[:, sls[h]] * p_inv[:, sls[h]] for h in heads]
    rt = [rt_all[:, s] for s in sls]
    kt = [kt_all[:, s] for s in sls]
    vh = [v[:, s] for s in sls]
    at_s = [_parts(x, sp) for x in at]
    bt_s = [_parts(x, sp) for x in bt]
    kt_s = [_parts(x, sp) for x in kt]
    vh_s = [_parts(x, sp) for x in vh]
    rt_y = [_parts(x, yp) for x in rt]
    a_ab = [jnp.where(strict, _mmp(_dot_nt, at_s[h], bt_s[h]), 0.0) for h in heads]
    a_ak = [jnp.where(strict, _mmp(_dot_nt, at_s[h], kt_s[h]), 0.0) for h in heads]
    a_rb = [jnp.where(incl, _mmp(_dot_nt, rt_y[h], bt_s[h][:yp]), 0.0) for h in heads]
    a_rk = [jnp.where(incl, _mmp(_dot_nt, rt_y[h], kt_s[h][:yp]), 0.0) for h in heads]
    x = [jnp.concatenate([at[h], _mmp(_dot, _parts(a_ak[h], sp), vh_s[h])], axis=1) for h in heads]
    p = a_ab
    for level in range(max(int(math.log2(L)), 1)):
        if level > 0:
            p = [_mmp(_dot, ps, ps) for ps in p_s]
        p_s = [_parts(q, sp) for q in p]
        x = [x[h] + _mmp(_dot, p_s[h], _parts(x[h], sp)) for h in heads]
    x_s = [_parts(q, sp) for q in x]
    qy = [_mmp(_dot, _parts(a_rb[h], yp), x_s[h][:yp]) for h in heads]
    y0 = [qy[h][:, RWKV_HEAD:] + _mmp(_dot, _parts(a_rk[h], yp), vh_s[h][:yp]) for h in heads]
    qh = [rt[h] + qy[h][:, :RWKV_HEAD] for h in heads]
    s0 = [wkv_ref[0, h] * p_mid[:, sls[h]] for h in heads]
    s0_s = [_parts(q, sp) for q in s0]
    y = [_mmp(_dot_nt, _parts(qh[h], yp), s0_s[h][:yp]) + y0[h] for h in heads]
    wtb = [_mmp(_dot_tn, [q[:, :RWKV_HEAD] for q in x_s[h]], bt_s[h]) for h in heads]
    uv_s = [[jnp.concatenate([x_s[h][i][:, RWKV_HEAD:], vh_s[h][i]], axis=0) for i in range(sp)] for h in heads]
    bk_s = [[jnp.concatenate([bt_s[h][i], kt_s[h][i]], axis=0) for i in range(sp)] for h in heads]
    for h in heads:
        s_new = s0[h] + _mmp(_dot, s0_s[h], _parts(wtb[h], sp)) + _mmp(_dot_tn, uv_s[h], bk_s[h])
        wkv_ref[0, h] = s_new * p_incl[L - 1:L, sls[h]]
    for h in heads:
        mu_y = jnp.mean(y[h], axis=-1, keepdims=True)
        yc = y[h] - mu_y
        var = jnp.mean(yc * yc, axis=-1, keepdims=True)
        yn = yc * lax.rsqrt(var + GN_EPS) * lnw_ref[:, sls[h]] + lnb_ref[:, sls[h]]
        bonus = jnp.sum(rk_all[:, sls[h]], axis=-1, keepdims=True) * vh[h]
        o_ref[0, :, sls[h]] = ((yn + bonus) * g[:, sls[h]]).astype(o_ref.dtype)


def _rwkv(zr, shift0, wkv0, wts, chunk, t_valid):
    batch, t, _ = zr.shape
    n_chunks = t // chunk
    assert t_valid == t or n_chunks == 1
    body = functools.partial(_rwkv_body, chunk=chunk, t_valid=min(t_valid, chunk))
    vec = lambda width: pl.BlockSpec((1, width), lambda b, c: (0, 0))
    mat = lambda rows: pl.BlockSpec((rows, RWKV_W), lambda b, c: (0, 0))
    return pl.pallas_call(
        body,
        grid=(batch, n_chunks),
        in_specs=[pl.BlockSpec((1, chunk, RWKV_COLS), lambda b, c: (b, c, 0)),
                  pl.BlockSpec((1, 1, RWKV_COLS), lambda b, c: (b, 0, 0)),
                  pl.BlockSpec((1, RWKV_HEADS, RWKV_HEAD, RWKV_HEAD), lambda b, c: (b, 0, 0, 0)),
                  vec(RWKV_COLS), vec(RWKV_W), mat(LORA_W), vec(RWKV_W), mat(LORA_W), mat(GATE_LORA),
                  vec(RWKV_W), vec(RWKV_W), vec(RWKV_W), vec(RWKV_W), vec(RWKV_W)],
        out_specs=[pl.BlockSpec((1, chunk, RWKV_W), lambda b, c: (b, c, 0)),
                   pl.BlockSpec((1, RWKV_HEADS, RWKV_HEAD, RWKV_HEAD), lambda b, c: (b, 0, 0, 0))],
        out_shape=[jax.ShapeDtypeStruct((batch, t, RWKV_W), BF16),
                   jax.ShapeDtypeStruct((batch, RWKV_HEADS, RWKV_HEAD, RWKV_HEAD), F32)],
        scratch_shapes=[pltpu.VMEM((1, RWKV_COLS), F32)],
        compiler_params=_cparams(("parallel", "arbitrary")),
        name="rwkv_scan",
    )(zr, shift0, wkv0, *wts)


def _mix_out_body(x_ref, o0_ref, o1_ref, o2_ref, l0_ref, l1_ref, l2_ref, orw_ref, gate_ref, woa_ref, wor_ref,
                  wo_ref, gffn_ref, rwt_ref, rb_ref, cnt0_ref, *rest, dilated, n_alias):
    x1_ref, h2_ref, te_ref, tg_ref, rk_ref, cnt_ref = rest[n_alias:n_alias + 6]
    stage = list(rest[n_alias + 6:])

    def token_major(ref, gi):
        if not dilated:
            return ref[...].astype(F32)
        dil = ATTN_GROUPS[gi][1]
        if dil == 1:
            return ref[0, 0].astype(F32)
        st_ref = stage.pop()
        for r in range(dil):
            sub = ref[0, r].astype(F32)
            for s in range(GROUP_W // LANE):
                st_ref[s, pl.ds(r, ref.shape[2], stride=dil), :] = sub[:, s * LANE:(s + 1) * LANE]
        return jnp.concatenate([st_ref[s] for s in range(GROUP_W // LANE)], axis=1)

    l0, l1, l2 = token_major(l0_ref, 0), token_major(l1_ref, 1), token_major(l2_ref, 2)
    m = jnp.maximum(jnp.maximum(l0, l1), l2)
    e0, e1, e2 = jnp.exp(l0 - m), jnp.exp(l1 - m), jnp.exp(l2 - m)
    o_att = (e0 * token_major(o0_ref, 0) + e1 * token_major(o1_ref, 1) + e2 * token_major(o2_ref, 2)) \
        * (1.0 / (e0 + e1 + e2))
    gates = gate_ref[...].astype(F32)
    merged = gates[:, :D_MODEL] * _dot(o_att.astype(BF16), woa_ref[...]) \
        + gates[:, D_MODEL:] * _dot(orw_ref[...], wor_ref[...])
    x1 = x_ref[...] + _dot(merged.astype(BF16), wo_ref[...])
    x1_ref[...] = x1
    h2 = _rms(x1, gffn_ref[...])
    h2_ref[...] = h2

    logits = _mm3(_dot_nt, rwt_ref[...], h2) + rb_ref[...]
    e_iota = lax.broadcasted_iota(jnp.int32, logits.shape, 0)
    vals, idxs = [], []
    for _ in range(TOP_K):
        top = jnp.max(logits, axis=0, keepdims=True)
        idx = jnp.min(jnp.where(logits == top, e_iota, N_EXPERTS), axis=0, keepdims=True)
        vals.append(top)
        idxs.append(idx)
        logits = jnp.where(e_iota == idx, -jnp.inf, logits)
    exps = [jnp.exp(t - vals[0]) for t in vals]
    inv = 1.0 / (exps[0] + exps[1] + exps[2] + exps[3])
    te_ref[...] = jnp.concatenate(idxs, axis=0)
    tg_ref[...] = jnp.concatenate([e * inv for e in exps], axis=0)

    @pl.when(pl.program_id(0) == 0)
    def _():
        cnt_ref[...] = cnt0_ref[...]

    hits = [e_iota == idx for idx in idxs]
    onehot = jnp.where(hits[0] | hits[1] | hits[2] | hits[3], 1.0, 0.0)
    tm = onehot.shape[1]
    earlier = lax.broadcasted_iota(jnp.int32, (tm, tm), 0) < lax.broadcasted_iota(jnp.int32, (tm, tm), 1)
    before = cnt_ref[...] + _dot(onehot.astype(BF16), jnp.where(earlier, 1.0, 0.0).astype(BF16))
    rk_ref[...] = jnp.concatenate([jnp.sum(jnp.where(h, before, 0.0), axis=0, keepdims=True) for h in hits],
                                  axis=0).astype(jnp.int32)
    cnt_ref[...] += jnp.sum(onehot, axis=1, keepdims=True)


def _mix_out(x, o_g, lse_g, o_rwkv, gates, wts, cnt0, tm, n_all, row0, bufs):
    n = x.shape[0]
    blk0 = row0 // tm
    row = lambda i: (i, 0)
    const = lambda i: (0, 0)
    out_row = lambda i: (i + blk0, 0)
    out_col = lambda i: (0, i + blk0)
    tok = lambda w: pl.BlockSpec((tm, w), row)
    dilated = o_g[0].ndim == 4
    if dilated:
        tiles = o_g[0].shape[1] * o_g[0].shape[2] // tm
        sub = lambda i: (i // tiles, 0, i % tiles, 0)
        att_specs = [pl.BlockSpec((1, d, tm // d, GROUP_W), sub) for _, d in ATTN_GROUPS] * 2
        stage = [pltpu.VMEM((GROUP_W // LANE, tm, LANE), F32) for _, d in ATTN_GROUPS if d > 1] * 2
    else:
        att_specs, stage = [tok(GROUP_W)] * 6, []
    in_specs = [tok(D_MODEL)] + att_specs + [tok(RWKV_W), tok(GATE_COLS),
                pl.BlockSpec((GROUP_W, D_MODEL), const), pl.BlockSpec((RWKV_W, D_MODEL), const),
                pl.BlockSpec((D_MODEL, D_MODEL), const), pl.BlockSpec((1, D_MODEL), const),
                pl.BlockSpec((N_EXPERTS, D_MODEL), const), pl.BlockSpec((N_EXPERTS, 1), const),
                pl.BlockSpec((N_EXPERTS, 1), const)]
    args = [x, *o_g, *lse_g, o_rwkv, gates, *wts, cnt0]
    aliases = {}
    if bufs is not None:
        in_specs += [pl.BlockSpec(memory_space=pl.ANY)] * 5
        aliases = {len(args) + j: j for j in range(5)}
        args += list(bufs)
    *new_bufs, cnt = pl.pallas_call(
        functools.partial(_mix_out_body, dilated=dilated, n_alias=len(aliases)),
        grid=(n // tm,),
        scratch_shapes=stage,
        in_specs=in_specs,
        out_specs=[pl.BlockSpec((tm, D_MODEL), out_row), pl.BlockSpec((tm, D_MODEL), out_row),
                   pl.BlockSpec((TOP_K, tm), out_col), pl.BlockSpec((TOP_K, tm), out_col),
                   pl.BlockSpec((TOP_K, tm), out_col), pl.BlockSpec((N_EXPERTS, 1), const)],
        out_shape=[jax.ShapeDtypeStruct((n_all, D_MODEL), F32), jax.ShapeDtypeStruct((n_all, D_MODEL), F32),
                   jax.ShapeDtypeStruct((TOP_K, n_all), jnp.int32), jax.ShapeDtypeStruct((TOP_K, n_all), F32),
                   jax.ShapeDtypeStruct((TOP_K, n_all), jnp.int32), jax.ShapeDtypeStruct((N_EXPERTS, 1), F32)],
        input_output_aliases=aliases,
        compiler_params=_cparams(("arbitrary",)),
        name="mix_out",
    )(*args)
    return new_bufs, cnt


def _moe_dispatch_body(tab_ref, idx_hbm, h2_hbm, x_hbm, idx_smem, zrow, isem, dsem, zsem, *, n_tiles):
    i = pl.program_id(0)
    slot = i % 2

    def idx_copy(rec, s):
        return pltpu.make_async_copy(idx_hbm.at[pl.ds(pl.multiple_of(rec * IDX_REC, IDX_REC), IDX_REC)],
                                     idx_smem.at[pl.ds(pl.multiple_of(s * IDX_REC, IDX_REC), IDX_REC)],
                                     isem.at[s])

    def row_copies_start(s):
        def one(t, carry):
            for k in range(TOP_K):
                d = idx_smem[s * IDX_REC + k * MOE_TILE + t]
                pltpu.make_async_copy(h2_hbm.at[pl.ds(i * MOE_TILE + t, 1)], x_hbm.at[pl.ds(d, 1)],
                                      dsem.at[s]).start()
            return carry
        lax.fori_loop(0, MOE_TILE, one, 0, unroll=4)

    @pl.when(i == 0)
    def _():
        idx_copy(0, 0).start()

    @pl.when(i + 1 < n_tiles)
    def _():
        idx_copy(i + 1, 1 - slot).start()

    idx_copy(i, slot).wait()
    row_copies_start(slot)

    def wait_rows(s):
        pltpu.make_async_copy(x_hbm.at[pl.ds(0, TOP_K * MOE_TILE)], x_hbm.at[pl.ds(0, TOP_K * MOE_TILE)],
                              dsem.at[s]).wait()

    @pl.when(i >= 1)
    def _():
        wait_rows(1 - slot)

    @pl.when(i == n_tiles - 1)
    def _():
        wait_rows(slot)
        zrow[...] = jnp.zeros_like(zrow)

        def zero_copy(dst_row):
            return pltpu.make_async_copy(zrow.at[pl.ds(0, 1)], x_hbm.at[pl.ds(dst_row, 1)], zsem)

        def per_expert(e, carry):
            first = tab_ref[e] + tab_ref[N_EXPERTS + e]
            last = tab_ref[e] + tab_ref[2 * N_EXPERTS + e]
            lax.fori_loop(first, last, lambda r, c: (zero_copy(r).start(), c)[1], 0)
            lax.fori_loop(first, last, lambda r, c: (zero_copy(r).wait(), c)[1], 0)
            return carry
        lax.fori_loop(0, N_EXPERTS, per_expert, 0)


def _moe_dispatch(h2, idx_rec, tables, n_rows):
    n_tiles = h2.shape[0] // MOE_TILE
    grid_spec = pltpu.PrefetchScalarGridSpec(
        num_scalar_prefetch=1,
        grid=(n_tiles,),
        in_specs=[pl.BlockSpec(memory_space=pl.ANY), pl.BlockSpec(memory_space=pl.ANY)],
        out_specs=pl.BlockSpec(memory_space=pl.ANY),
        scratch_shapes=[pltpu.SMEM((2 * IDX_REC,), jnp.int32),
                        pltpu.VMEM((8, D_MODEL), F32),
                        pltpu.SemaphoreType.DMA((2,)),
                        pltpu.SemaphoreType.DMA((2,)),
                        pltpu.SemaphoreType.DMA],
    )
    return pl.pallas_call(
        functools.partial(_moe_dispatch_body, n_tiles=n_tiles),
        grid_spec=grid_spec,
        out_shape=jax.ShapeDtypeStruct((n_rows, D_MODEL), F32),
        compiler_params=_cparams(("arbitrary",)),
        name="moe_dispatch",
    )(tables, idx_rec, h2)


def _moe_experts_body(be_ref, nused_ref, x_ref, w1_ref, b1_ref, w2_ref, b2_ref, y_ref, w1b, w2b):
    i = pl.program_id(0)

    @pl.when(i < nused_ref[0])
    def _():
        @pl.when((i == 0) | (be_ref[i] != be_ref[jnp.maximum(i - 1, 0)]))
        def _():
            w1b[...] = w1_ref[0].astype(BF16)
            w2b[...] = w2_ref[0].astype(BF16)

        u = _dot(x_ref[...].astype(BF16), w1b[...]) + b1_ref[0]
        glu = jnp.minimum(u[:, :D_MODEL], SWIGLU_LIMIT)
        lin = jnp.clip(u[:, D_MODEL:], -SWIGLU_LIMIT, SWIGLU_LIMIT)
        act = glu * _sigmoid(SWIGLU_ALPHA * glu) * (lin + 1.0)
        y_ref[...] = _dot(act.astype(BF16), w2b[...]) + b2_ref[0]

    @pl.when(i >= nused_ref[0])
    def _():
        y_ref[...] = jnp.zeros_like(y_ref)


def _moe_experts(x_rows, block_e, n_used, w1, b1, w2, b2):
    n_blocks = block_e.shape[0]
    by_expert = lambda i, be, nu: (be[i], 0, 0)
    grid_spec = pltpu.PrefetchScalarGridSpec(
        num_scalar_prefetch=2,
        grid=(n_blocks,),
        in_specs=[pl.BlockSpec((MOE_BLOCK, D_MODEL), lambda i, be, nu: (jnp.minimum(i, nu[0] - 1), 0)),
                  pl.BlockSpec((1, D_MODEL, 2 * D_MODEL), by_expert),
                  pl.BlockSpec((1, 1, 2 * D_MODEL), by_expert),
                  pl.BlockSpec((1, D_MODEL, D_MODEL), by_expert),
                  pl.BlockSpec((1, 1, D_MODEL), by_expert)],
        out_specs=pl.BlockSpec((MOE_BLOCK, D_MODEL), lambda i, be, nu: (i, 0)),
        scratch_shapes=[pltpu.VMEM((D_MODEL, 2 * D_MODEL), BF16),
                        pltpu.VMEM((D_MODEL, D_MODEL), BF16)],
    )
    return pl.pallas_call(
        _moe_experts_body,
        grid_spec=grid_spec,
        out_shape=jax.ShapeDtypeStruct((n_blocks * MOE_BLOCK, D_MODEL), F32),
        compiler_params=_cparams(("arbitrary",)),
        name="moe_experts",
    )(block_e, n_used, x_rows, w1, b1, w2, b2)


def _route(top_e, rank, counts):
    n_tok = top_e.shape[1]
    counts = counts.reshape(N_EXPERTS).astype(jnp.int32)
    padded = (counts + MOE_BLOCK - 1) // MOE_BLOCK * MOE_BLOCK
    pad_end = jnp.cumsum(padded)
    pad_start = pad_end - padded
    experts = jnp.arange(N_EXPERTS, dtype=jnp.int32)
    dest = rank + jnp.sum(jnp.where(top_e[..., None] == experts, pad_start, 0), axis=-1)
    n_blocks = -(-(n_tok * TOP_K + N_EXPERTS * (MOE_BLOCK - 1)) // MOE_BLOCK)
    blk_row0 = jnp.arange(n_blocks, dtype=jnp.int32) * MOE_BLOCK
    block_e = jnp.minimum(jnp.sum(blk_row0[:, None] >= pad_end[None, :], axis=1), N_EXPERTS - 1).astype(jnp.int32)
    n_used = (pad_end[-1] // MOE_BLOCK).astype(jnp.int32).reshape(1)
    n_tiles = n_tok // MOE_TILE
    rec = dest.reshape(TOP_K, n_tiles, MOE_TILE).transpose(1, 0, 2).reshape(n_tiles, TOP_K * MOE_TILE)
    idx_rec = jnp.concatenate([rec, jnp.zeros((n_tiles, IDX_REC - TOP_K * MOE_TILE), jnp.int32)], axis=1).reshape(-1)
    tables = jnp.concatenate([pad_start, counts, padded]).astype(jnp.int32)
    return idx_rec, tables, block_e, n_used, n_blocks * MOE_BLOCK


def _tail_body(idx_hbm, y_hbm, x1_ref, tg_ref, pe_ref, gple_ref, wpg_ref, wp_ref, gfin_ref, yp_ref, ys_ref,
               idx_smem, gbuf, isem, gsem, *, n_tiles, n_prompt_tiles):
    i = pl.program_id(0)
    slot = i % 2

    def idx_copy(rec, s):
        return pltpu.make_async_copy(idx_hbm.at[pl.ds(pl.multiple_of(rec * IDX_REC, IDX_REC), IDX_REC)],
                                     idx_smem.at[pl.ds(pl.multiple_of(s * IDX_REC, IDX_REC), IDX_REC)],
                                     isem.at[s])

    def gather_start(s):
        def one(t, carry):
            for k in range(TOP_K):
                d = idx_smem[s * IDX_REC + k * MOE_TILE + t]
                pltpu.make_async_copy(y_hbm.at[pl.ds(d, 1)], gbuf.at[s, k, pl.ds(t, 1)], gsem.at[s]).start()
            return carry
        lax.fori_loop(0, MOE_TILE, one, 0, unroll=4)

    @pl.when(i == 0)
    def _():
        idx_copy(0, 0).start()
        idx_copy(0, 0).wait()
        gather_start(0)
        idx_copy(1, 1).start()

    @pl.when(i + 1 < n_tiles)
    def _():
        idx_copy(i + 1, 1 - slot).wait()
        gather_start(1 - slot)

        @pl.when(i + 2 < n_tiles)
        def _():
            idx_copy(i + 2, slot).start()

    for k in range(TOP_K):
        pltpu.make_async_copy(y_hbm.at[pl.ds(0, MOE_TILE)], gbuf.at[slot, k], gsem.at[slot]).wait()
    tg = tg_ref[...]
    moe = tg[:, 0:1] * gbuf[slot, 0]
    for k in range(1, TOP_K):
        moe = moe + tg[:, k:k + 1] * gbuf[slot, k]
    x2 = x1_ref[...] + moe
    gate = _sigmoid(_dot(_rms(x2, gple_ref[...]).astype(BF16), wpg_ref[...]))
    x3 = x2 + gate * _dot(pe_ref[...].astype(BF16), wp_ref[...])
    y = _rms(x3, gfin_ref[...])

    @pl.when(i < n_prompt_tiles)
    def _():
        yp_ref[...] = y

    @pl.when(i >= n_prompt_tiles)
    def _():
        ys_ref[...] = y


def _tail(x1_all, y_rows, idx_rec, tg_t, pe_all, wts, n_p):
    n_all = x1_all.shape[0]
    n_tiles = n_all // MOE_TILE
    n_prompt_tiles = n_p // MOE_TILE
    row = lambda i: (i, 0)
    const = lambda i: (0, 0)
    body = functools.partial(_tail_body, n_tiles=n_tiles, n_prompt_tiles=n_prompt_tiles)
    return pl.pallas_call(
        body,
        grid=(n_tiles,),
        in_specs=[pl.BlockSpec(memory_space=pl.ANY), pl.BlockSpec(memory_space=pl.ANY),
                  pl.BlockSpec((MOE_TILE, D_MODEL), row), pl.BlockSpec((MOE_TILE, TOP_K), row),
                  pl.BlockSpec((MOE_TILE, PLE_DIM), row), pl.BlockSpec((1, D_MODEL), const),
                  pl.BlockSpec((D_MODEL, D_MODEL), const), pl.BlockSpec((PLE_DIM, D_MODEL), const),
                  pl.BlockSpec((1, D_MODEL), const)],
        out_specs=[pl.BlockSpec((MOE_TILE, D_MODEL), lambda i: (jnp.minimum(i, n_prompt_tiles - 1), 0)),
                   pl.BlockSpec((MOE_TILE, D_MODEL), lambda i: (jnp.maximum(i - n_prompt_tiles, 0), 0))],
        out_shape=[jax.ShapeDtypeStruct((n_p, D_MODEL), F32),
                   jax.ShapeDtypeStruct((n_all - n_p, D_MODEL), F32)],
        scratch_shapes=[pltpu.SMEM((2 * IDX_REC,), jnp.int32),
                        pltpu.VMEM((2, TOP_K, MOE_TILE, D_MODEL), F32),
                        pltpu.SemaphoreType.DMA((2,)),
                        pltpu.SemaphoreType.DMA((2,))],
        compiler_params=_cparams(("arbitrary",)),
        name="tail",
    )(idx_rec, y_rows, x1_all, tg_t, pe_all, *wts)


RWKV_CHUNK = 128
RWKV_CHUNK_SAMPLE = 8
STATE_PASSES = 1
Y_PASSES = 1
TM_PROMPT = 256


def kernel(x_prompt, x_sample, p_prompt, p_sample, cache_kv_w128, cache_kv_w512, cache_kv_w2048, state_rwkv_shift, state_rwkv_wkv, norm_mix_g, w_in, rwkv_mu, rwkv_w0, rwkv_w2, rwkv_a0, rwkv_a2, rwkv_g2, rwkv_k_k, rwkv_k_a, rwkv_r_k, rwkv_ln_w, rwkv_ln_b, w_out_attn, w_out_rwkv, w_out, norm_ffn_g, router_w, router_b, moe_w1, moe_b1, moe_w2, moe_b2, norm_ple_g, w_ple, w_ple_gate, norm_final_g):
    bp, seq, _ = x_prompt.shape
    bs, t_s, _ = x_sample.shape
    n_p, n_s = bp * seq, bs * t_s
    n_all = n_p + n_s
    assert w_in.shape[0] == 1, "single layer"
    caches = (cache_kv_w128, cache_kv_w512, cache_kv_w2048)

    row = lambda a: a.reshape(1, -1)
    w_in_b = w_in[0].astype(BF16)
    zeros64 = jnp.zeros((LORA_W // 2, RWKV_W), F32)
    rwkv_wts = (row(rwkv_mu[0]), row(rwkv_w0[0]), jnp.concatenate([rwkv_w2[0], zeros64], axis=0),
                row(rwkv_a0[0]), jnp.concatenate([zeros64, rwkv_a2[0]], axis=0), rwkv_g2[0],
                row(rwkv_k_k[0]), row(rwkv_k_a[0]), row(rwkv_r_k[0]), row(rwkv_ln_w[0]), row(rwkv_ln_b[0]))
    mix_wts = (w_out_attn[0].astype(BF16), w_out_rwkv[0].astype(BF16), w_out[0].astype(BF16),
               row(norm_ffn_g[0]), router_w[0].T, router_b[0].reshape(N_EXPERTS, 1))
    tail_wts = (row(norm_ple_g[0]), w_ple_gate[0].astype(BF16), w_ple[0].astype(BF16), row(norm_final_g))
    g_mix = row(norm_mix_g[0])

    xp = x_prompt.reshape(n_p, D_MODEL)
    qd_p, kvd_p, kv_p, zr_p, gate_p = _in_proj_dilated(xp, g_mix, w_in_b, TM_PROMPT, bp, seq)
    att_p = [_attn_prompt(qd_p[gi], kvd_p[gi], gi) for gi in range(N_GROUPS)]
    orw_p, wkv_p = _rwkv(zr_p.reshape(bp, seq, RWKV_COLS), jnp.zeros((bp, 1, RWKV_COLS), F32),
                         jnp.zeros((bp, RWKV_HEADS, RWKV_HEAD, RWKV_HEAD), F32), rwkv_wts, RWKV_CHUNK, seq)
    bufs, counts = _mix_out(xp, [a[0] for a in att_p], [a[1] for a in att_p], orw_p.reshape(n_p, RWKV_W), gate_p,
                            mix_wts, jnp.zeros((N_EXPERTS, 1), F32), TM_PROMPT, n_all, 0, None)

    xs = x_sample.reshape(n_s, D_MODEL)
    q_s, kv_s, zr_s, gate_s = _in_proj(xs, g_mix, w_in_b, n_s)
    t_pad = 8
    pad_t = lambda a: jnp.pad(a.reshape(bs, t_s, -1), ((0, 0), (0, t_pad - t_s), (0, 0)))
    q_s3, kv_s3 = pad_t(q_s), pad_t(kv_s)
    att_s, new_caches = [], []
    for gi in range(N_GROUPS):
        wc = caches[gi].shape[2]
        o, lse, newc = _attn_sample(q_s3, kv_s3, caches[gi].reshape(bs, wc, 2 * GROUP_W), gi, t_s)
        att_s.append((o[:, :t_s].reshape(n_s, GROUP_W), lse[:, :t_s].reshape(n_s, GROUP_W)))
        new_caches.append(newc.reshape(1, bs, wc, 2, HEADS_PER_GROUP, HEAD_DIM))
    zr_s3 = jnp.pad(zr_s.reshape(bs, t_s, RWKV_COLS), ((0, 0), (0, RWKV_CHUNK_SAMPLE - t_s), (0, 0)))
    orw_s, wkv_s = _rwkv(zr_s3, state_rwkv_shift[0].reshape(bs, 1, RWKV_COLS), state_rwkv_wkv[0], rwkv_wts,
                         RWKV_CHUNK_SAMPLE, t_s)
    bufs, counts = _mix_out(xs, [a[0] for a in att_s], [a[1] for a in att_s], orw_s[:, :t_s].reshape(n_s, RWKV_W),
                            gate_s, mix_wts, counts, n_s, n_all, n_p, bufs)
    x1_all, h2_all, top_e, top_g, rank = bufs

    idx_rec, tables, block_e, n_used, n_rows = _route(top_e, rank, counts)
    x_rows = _moe_dispatch(h2_all, idx_rec, tables, n_rows)
    y_rows = _moe_experts(x_rows, block_e, n_used, moe_w1[0], moe_b1[0].reshape(N_EXPERTS, 1, 2 * D_MODEL),
                          moe_w2[0], moe_b2[0].reshape(N_EXPERTS, 1, D_MODEL))
    pe_all = jnp.concatenate([p_prompt[0].reshape(n_p, PLE_DIM), p_sample[0].reshape(n_s, PLE_DIM)], axis=0)
    y_p, y_s = _tail(x1_all, y_rows, idx_rec, top_g.T, pe_all, tail_wts, n_p)

    kv_p4 = kv_p.reshape(bp, seq, N_GROUPS, 2, HEADS_PER_GROUP, HEAD_DIM)
    kv_out_p = [kv_p4[:, seq - min(w, seq):, gi][None] for gi, (w, _) in enumerate(ATTN_GROUPS)]
    shift_p = zr_p.reshape(bp, seq, RWKV_COLS)[:, -1][None]
    shift_s = zr_s.reshape(bs, t_s, RWKV_COLS)[:, -1][None]
    return (y_p.reshape(bp, seq, D_MODEL), y_s.reshape(bs, t_s, D_MODEL),
            kv_out_p[0], kv_out_p[1], kv_out_p[2], shift_p, wkv_p[None],
            new_caches[0], new_caches[1], new_caches[2], shift_s, wkv_s[None])
```

```python
import functools
import math

import numpy as np
import jax
import jax.numpy as jnp
from jax import lax
from jax.experimental import pallas as pl
from jax.experimental.pallas import tpu as pltpu

F32 = jnp.float32
BF16 = jnp.bfloat16

LANE = 128
D_MODEL = 1024
N_GROUPS = 3
HEADS_PER_GROUP = 4
HEAD_DIM = 64
ATTN_GROUPS = ((128, 1), (512, 4), (2048, 16))
GROUP_W = HEADS_PER_GROUP * HEAD_DIM
Q_COLS = N_GROUPS * GROUP_W
KV_COLS = 2 * Q_COLS
BAND = 128

RWKV_HEADS = 8
RWKV_HEAD = 64
RWKV_W = 512
LORA_W = 128
GATE_LORA = 128
RWKV_COLS = 3 * RWKV_W + LORA_W + GATE_LORA
GN_EPS = 64e-5
GATE_COLS = 2 * D_MODEL
Z_RWKV0 = 3 * Q_COLS
Z_GATE0 = Z_RWKV0 + RWKV_COLS
IN_COLS = Z_GATE0 + GATE_COLS

N_EXPERTS = 32
TOP_K = 4
SWIGLU_LIMIT = 7.0
SWIGLU_ALPHA = 1.702
MOE_BLOCK = 256
MOE_TILE = 128
IDX_REC = 1024
PLE_DIM = 256
RMS_EPS = 1e-6

NEG_BIG = -1e30
VMEM_LIMIT = 56 * 1024 * 1024


def _cparams(sem):
    return pltpu.CompilerParams(dimension_semantics=sem, vmem_limit_bytes=VMEM_LIMIT)


def _rms(x, g):
    return x * lax.rsqrt(jnp.mean(x * x, axis=-1, keepdims=True) + RMS_EPS) * g


def _sigmoid(x):
    return 1.0 / (1.0 + jnp.exp(-x))


def _dot(a, b):
    return jnp.dot(a, b, preferred_element_type=F32)


def _dot_nt(a, b):
    return lax.dot_general(a, b, (((1,), (1,)), ((), ())), preferred_element_type=F32)


def _dot_tn(a, b):
    return lax.dot_general(a, b, (((0,), (0,)), ((), ())), preferred_element_type=F32)


def _split(x):
    hi = x.astype(BF16)
    lo = (x - hi.astype(F32)).astype(BF16)
    return hi, lo


def _mm3(dot, a, b):
    ah, al = _split(a)
    bh, bl = _split(b)
    return dot(ah, bh) + (dot(ah, bl) + dot(al, bh))


def _parts(x, n):
    out = []
    for _ in range(n - 1):
        hi = x.astype(BF16)
        out.append(hi)
        x = x - hi.astype(F32)
    out.append(x.astype(BF16))
    return out


def _mmp(dot, ap, bp):
    order = max(len(ap), len(bp))
    acc = None
    for i, a in enumerate(ap):
        for j, b in enumerate(bp):
            if i + j < order:
                t = dot(a, b)
                acc = t if acc is None else acc + t
    return acc


def _in_proj_body(x_ref, g_ref, w_ref, q_ref, kv_ref, zr_ref, gate_ref):
    h = _rms(x_ref[...], g_ref[...]).astype(BF16)

    def proj(lo, width):
        return _dot(h, w_ref[:, lo:lo + width])

    q_ref[...] = (proj(0, Q_COLS) * (1.0 / math.sqrt(HEAD_DIM))).astype(BF16)
    for g in range(N_GROUPS):
        kv_ref[:, 2 * g * GROUP_W:(2 * g + 1) * GROUP_W] = proj(Q_COLS + g * GROUP_W, GROUP_W)
        kv_ref[:, (2 * g + 1) * GROUP_W:(2 * g + 2) * GROUP_W] = proj(2 * Q_COLS + g * GROUP_W, GROUP_W)
    zr_ref[...] = proj(Z_RWKV0, RWKV_COLS)
    gate_ref[...] = _sigmoid(proj(Z_GATE0, GATE_COLS)).astype(BF16)


def _in_proj_dilated_body(x_ref, g_ref, w_ref, *rest):
    qd_refs, kvd_refs = rest[0:N_GROUPS], rest[N_GROUPS:2 * N_GROUPS]
    kv_ref, zr_ref, gate_ref, st_ref = rest[2 * N_GROUPS:]
    tm = x_ref.shape[0]
    h = _rms(x_ref[...], g_ref[...]).astype(BF16)

    def proj(lo, width):
        return _dot(h, w_ref[:, lo:lo + width])

    n_q = Q_COLS // LANE

    def stage(slab0, val):
        for s in range(val.shape[1] // LANE):
            st_ref[slab0 + s] = val[:, s * LANE:(s + 1) * LANE]

    stage(0, proj(0, Q_COLS) * (1.0 / math.sqrt(HEAD_DIM)))
    for g in range(N_GROUPS):
        for part, src in ((0, Q_COLS), (1, 2 * Q_COLS)):
            col = (2 * g + part) * GROUP_W
            val = proj(src + g * GROUP_W, GROUP_W)
            kv_ref[:, col:col + GROUP_W] = val
            stage(n_q + col // LANE, val)
    for g, (_, dil) in enumerate(ATTN_GROUPS):
        for r in range(dil):
            rows = pl.ds(r, tm // dil, stride=dil) if dil > 1 else slice(None)
            for s in range(GROUP_W // LANE):
                qd_refs[g][0, r, :, s * LANE:(s + 1) * LANE] = st_ref[g * GROUP_W // LANE + s, rows, :].astype(BF16)
            for s in range(2 * GROUP_W // LANE):
                kvd_refs[g][0, r, :, s * LANE:(s + 1) * LANE] = \
                    st_ref[n_q + 2 * g * GROUP_W // LANE + s, rows, :].astype(BF16)
    zr_ref[...] = proj(Z_RWKV0, RWKV_COLS)
    gate_ref[...] = _sigmoid(proj(Z_GATE0, GATE_COLS)).astype(BF16)


def _in_proj_dilated(x, g, w_bf16, tm, batch, seq):
    n = x.shape[0]
    tiles = seq // tm
    row = lambda i: (i, 0)
    const = lambda i: (0, 0)
    sub = lambda i: (i // tiles, 0, i % tiles, 0)
    dils = [d for _, d in ATTN_GROUPS]
    assert all(tm % (16 * d) == 0 for d in dils)
    outs = pl.pallas_call(
        _in_proj_dilated_body,
        grid=(n // tm,),
        in_specs=[pl.BlockSpec((tm, D_MODEL), row),
                  pl.BlockSpec((1, D_MODEL), const),
                  pl.BlockSpec((D_MODEL, IN_COLS), const)],
        out_specs=[pl.BlockSpec((1, d, tm // d, GROUP_W), sub) for d in dils]
                  + [pl.BlockSpec((1, d, tm // d, 2 * GROUP_W), sub) for d in dils]
                  + [pl.BlockSpec((tm, KV_COLS), row),
                     pl.BlockSpec((tm, RWKV_COLS), row),
                     pl.BlockSpec((tm, GATE_COLS), row)],
        out_shape=[jax.ShapeDtypeStruct((batch, d, seq // d, GROUP_W), BF16) for d in dils]
                  + [jax.ShapeDtypeStruct((batch, d, seq // d, 2 * GROUP_W), BF16) for d in dils]
                  + [jax.ShapeDtypeStruct((n, KV_COLS), F32),
                     jax.ShapeDtypeStruct((n, RWKV_COLS), F32),
                     jax.ShapeDtypeStruct((n, GATE_COLS), BF16)],
        scratch_shapes=[pltpu.VMEM(((Q_COLS + KV_COLS) // LANE, tm, LANE), F32)],
        compiler_params=_cparams(("parallel",)),
        name="in_proj_dilated",
    )(x, g, w_bf16)
    return outs[0:N_GROUPS], outs[N_GROUPS:2 * N_GROUPS], outs[2 * N_GROUPS], outs[2 * N_GROUPS + 1], outs[2 * N_GROUPS + 2]


def _in_proj(x, g, w_bf16, tm):
    n = x.shape[0]
    row = lambda i: (i, 0)
    const = lambda i: (0, 0)
    return pl.pallas_call(
        _in_proj_body,
        grid=(n // tm,),
        in_specs=[pl.BlockSpec((tm, D_MODEL), row),
                  pl.BlockSpec((1, D_MODEL), const),
                  pl.BlockSpec((D_MODEL, IN_COLS), const)],
        out_specs=[pl.BlockSpec((tm, Q_COLS), row),
                   pl.BlockSpec((tm, KV_COLS), row),
                   pl.BlockSpec((tm, RWKV_COLS), row),
                   pl.BlockSpec((tm, GATE_COLS), row)],
        out_shape=[jax.ShapeDtypeStruct((n, Q_COLS), BF16),
                   jax.ShapeDtypeStruct((n, KV_COLS), F32),
                   jax.ShapeDtypeStruct((n, RWKV_COLS), F32),
                   jax.ShapeDtypeStruct((n, GATE_COLS), BF16)],
        compiler_params=_cparams(("parallel",)),
        name="in_proj",
    )(x, g, w_bf16)


def _alibi_slopes(gi):
    return [2.0 ** (-8.0 * (gi * HEADS_PER_GROUP + h + 1) / (N_GROUPS * HEADS_PER_GROUP))
            for h in range(HEADS_PER_GROUP)]


def _head_of_lane(shape):
    return lax.broadcasted_iota(jnp.int32, shape, len(shape) - 1) // HEAD_DIM


def _stack_heads(q):
    hl = _head_of_lane(q.shape)
    return jnp.concatenate([jnp.where(hl == h, q, jnp.zeros_like(q)) for h in range(HEADS_PER_GROUP)], axis=0)


def _unstack_heads(x4, rows):
    hl = _head_of_lane((rows, GROUP_W))
    out = x4[0:rows]
    for h in range(1, HEADS_PER_GROUP):
        out = jnp.where(hl == h, x4[h * rows:(h + 1) * rows], out)
    return out


def _attn_prompt_body(q_ref, kvp_ref, kvc_ref, o_ref, lse_ref, kb_ref, bias_ref, *, dil, slopes, n_sub):
    c = pl.program_id(2)
    kb_ref[0:BAND, :] = kvp_ref[0, 0]
    kb_ref[BAND:, :] = kvc_ref[0, 0]

    qi = lax.broadcasted_iota(jnp.int32, (BAND, 2 * BAND), 0) + BAND
    ki = lax.broadcasted_iota(jnp.int32, (BAND, 2 * BAND), 1)
    rel = qi - ki
    dist = jnp.where((rel >= 0) & (rel <= BAND), (dil * rel).astype(F32), -NEG_BIG / slopes[-1])
    for h in range(HEADS_PER_GROUP):
        bias_ref[h * BAND:(h + 1) * BAND, :] = -slopes[h] * dist

    def sub_block(n, carry):
        r0 = pl.multiple_of(n * BAND, BAND)
        q4 = _stack_heads(q_ref[0, 0, pl.ds(r0, BAND), :])
        kv = kb_ref[pl.ds(r0, 2 * BAND), :]
        s = _dot_nt(q4, kv[:, :GROUP_W]) + bias_ref[...]
        n_before_start = jnp.where((c == 0) & (n == 0), BAND, 0)
        kcol = lax.broadcasted_iota(jnp.int32, s.shape, 1)
        s = jnp.where(kcol < n_before_start, NEG_BIG, s)
        m = jnp.max(s, axis=-1, keepdims=True)
        e = jnp.exp(s - m)
        den = jnp.sum(e, axis=-1, keepdims=True)
        o4 = _dot(e.astype(BF16), kv[:, GROUP_W:]) * (1.0 / den)
        o_ref[0, 0, pl.ds(r0, BAND), :] = _unstack_heads(o4, BAND).astype(o_ref.dtype)
        lse4 = jnp.broadcast_to(m + jnp.log(den), (HEADS_PER_GROUP * BAND, GROUP_W))
        lse_ref[0, 0, pl.ds(r0, BAND), :] = _unstack_heads(lse4, BAND)
        return carry

    lax.fori_loop(0, n_sub, sub_block, 0)


def _attn_prompt(qd, kvd, gi):
    window, dil = ATTN_GROUPS[gi]
    assert window // dil == BAND
    batch, _, sub_len, _ = qd.shape
    chunk = min(sub_len, 1024)
    n_chunks = sub_len // chunk
    sub_per_chunk = chunk // BAND
    body = functools.partial(_attn_prompt_body, dil=dil, slopes=_alibi_slopes(gi), n_sub=sub_per_chunk)
    cur = lambda b, r, c: (b, r, c, 0)
    return pl.pallas_call(
        body,
        grid=(batch, dil, n_chunks),
        in_specs=[
            pl.BlockSpec((1, 1, chunk, GROUP_W), cur),
            pl.BlockSpec((1, 1, BAND, 2 * GROUP_W),
                         lambda b, r, c: (b, r, jnp.maximum(c * sub_per_chunk - 1, 0), 0)),
            pl.BlockSpec((1, 1, chunk, 2 * GROUP_W), cur),
        ],
        out_specs=[pl.BlockSpec((1, 1, chunk, GROUP_W), cur), pl.BlockSpec((1, 1, chunk, GROUP_W), cur)],
        out_shape=[jax.ShapeDtypeStruct(qd.shape, BF16), jax.ShapeDtypeStruct(qd.shape, F32)],
        scratch_shapes=[pltpu.VMEM((chunk + BAND, 2 * GROUP_W), BF16),
                        pltpu.VMEM((HEADS_PER_GROUP * BAND, 2 * BAND), F32)],
        compiler_params=_cparams(("parallel", "parallel", "arbitrary")),
        name=f"attn_prompt_g{gi}",
    )(qd, kvd, kvd)


def _attn_sample_body(q_ref, kvn_ref, cache_ref, o_ref, lse_ref, newc_ref, *, dil, slopes, wc, t_new):
    tp = q_ref.shape[1]
    cache = cache_ref[0]
    new = kvn_ref[0]
    newc_ref[0, 0:wc - t_new, :] = cache[t_new:wc]
    newc_ref[0, wc - t_new:wc, :] = new[0:t_new]

    q4 = _stack_heads(q_ref[0])
    cb = cache.astype(BF16)
    nb = new.astype(BF16)
    t_row = lax.broadcasted_iota(jnp.int32, (tp, 1), 0)
    slope_col = jnp.concatenate([jnp.full((tp, 1), s, F32) for s in slopes], axis=0)
    t4 = jnp.concatenate([t_row] * HEADS_PER_GROUP, axis=0)

    def scores(keys_bf16, dist):
        ok = (dist >= 0) & ((dist & (dil - 1)) == 0) & (dist <= BAND * dil)
        s = _dot_nt(q4, keys_bf16[:, :GROUP_W])
        return jnp.where(ok, s - slope_col * dist.astype(F32), NEG_BIG)

    pc = lax.broadcasted_iota(jnp.int32, (HEADS_PER_GROUP * tp, wc), 1)
    pn = lax.broadcasted_iota(jnp.int32, (HEADS_PER_GROUP * tp, tp), 1)
    s_c = scores(cb, wc + t4 - pc)
    s_n = scores(nb, jnp.where(pn < t_new, t4 - pn, -1))
    m = jnp.maximum(jnp.max(s_c, axis=-1, keepdims=True), jnp.max(s_n, axis=-1, keepdims=True))
    e_c = jnp.exp(s_c - m)
    e_n = jnp.exp(s_n - m)
    den = jnp.sum(e_c, axis=-1, keepdims=True) + jnp.sum(e_n, axis=-1, keepdims=True)
    o4 = (_dot(e_c.astype(BF16), cb[:, GROUP_W:]) + _dot(e_n.astype(BF16), nb[:, GROUP_W:])) * (1.0 / den)
    o_ref[0] = _unstack_heads(o4, tp).astype(o_ref.dtype)
    lse4 = jnp.broadcast_to(m + jnp.log(den), (HEADS_PER_GROUP * tp, GROUP_W))
    lse_ref[0] = _unstack_heads(lse4, tp)


def _attn_sample(q, kv, cache, gi, t_new):
    window, dil = ATTN_GROUPS[gi]
    batch, tp, _ = q.shape
    wc = cache.shape[1]
    assert wc + 0 - dil * BAND >= 0
    body = functools.partial(_attn_sample_body, dil=dil, slopes=_alibi_slopes(gi), wc=wc, t_new=t_new)
    return pl.pallas_call(
        body,
        grid=(batch,),
        in_specs=[pl.BlockSpec((1, tp, GROUP_W), lambda b: (b, 0, gi)),
                  pl.BlockSpec((1, tp, 2 * GROUP_W), lambda b: (b, 0, gi)),
                  pl.BlockSpec((1, wc, 2 * GROUP_W), lambda b: (b, 0, 0))],
        out_specs=[pl.BlockSpec((1, tp, GROUP_W), lambda b: (b, 0, 0)),
                   pl.BlockSpec((1, tp, GROUP_W), lambda b: (b, 0, 0)),
                   pl.BlockSpec((1, wc, 2 * GROUP_W), lambda b: (b, 0, 0))],
        out_shape=[jax.ShapeDtypeStruct((batch, tp, GROUP_W), BF16),
                   jax.ShapeDtypeStruct((batch, tp, GROUP_W), F32),
                   jax.ShapeDtypeStruct((batch, wc, 2 * GROUP_W), F32)],
        compiler_params=_cparams(("parallel",)),
        name=f"attn_sample_g{gi}",
    )(q, kv, cache)


def _rwkv_body(zr_ref, shift0_ref, wkv0_ref, mu_ref, w0_ref, w2_ref, a0_ref, a2_ref, g2_ref, kk_ref, ka_ref,
               rk_ref, lnw_ref, lnb_ref, o_ref, wkv_ref, prev_ref, *, chunk, t_valid):
    c = pl.program_id(1)
    L = chunk
    mm = functools.partial(_mm3, _dot)
    heads = range(RWKV_HEADS)
    sls = [slice(h * RWKV_HEAD, (h + 1) * RWKV_HEAD) for h in heads]

    @pl.when(c == 0)
    def _():
        prev_ref[...] = shift0_ref[0]
        wkv_ref[0] = wkv0_ref[0]

    zr = zr_ref[0]
    row = lax.broadcasted_iota(jnp.int32, (L, 1), 0)
    prev = jnp.where(row == 0, prev_ref[...], pltpu.roll(zr, 1, 0))
    prev_ref[...] = zr[L - 1:L, :]
    zm = zr + (prev - zr) * mu_ref[...]

    r = zm[:, 0:RWKV_W]
    k = zm[:, RWKV_W:2 * RWKV_W]
    v = zm[:, 2 * RWKV_W:3 * RWKV_W]
    xwa = zm[:, 3 * RWKV_W:3 * RWKV_W + LORA_W]
    xg = zm[:, 3 * RWKV_W + LORA_W:]
    w_pre = w0_ref[...] + mm(jnp.tanh(xwa), w2_ref[...])
    softplus = jnp.maximum(-w_pre, 0.0) + jnp.log(1.0 + jnp.exp(-jnp.abs(w_pre)))
    w_log = -softplus - 0.5
    lw = -jnp.exp(w_log)
    a = _sigmoid(a0_ref[...] + mm(xwa, a2_ref[...]))
    g = mm(_sigmoid(xg), g2_ref[...])
    kk_raw = k * kk_ref[...]
    k2 = k * (1.0 + (a - 1.0) * ka_ref[...])
    if t_valid < L:
        live = row < t_valid
        lw = jnp.where(live, lw, 0.0)
        kk_raw = jnp.where(live, kk_raw, 0.0)
        k2 = jnp.where(live, k2, 0.0)
        v = jnp.where(live, v, 0.0)

    ti = lax.broadcasted_iota(jnp.int32, (L, L), 0)
    si = lax.broadcasted_iota(jnp.int32, (L, L), 1)
    incl = ti >= si
    strict = ti > si
    clw = _mmp(_dot, [jnp.where(incl, 1.0, 0.0).astype(BF16)], _parts(lw, 3))
    mid = max(L // 2 - 1, 0)
    clw_mid = clw[mid:mid + 1, :]
    rel = clw - clw_mid
    p_incl = jnp.exp(rel)
    p_inv = jnp.exp(-rel)
    p_prev = jnp.exp(rel - lw)
    p_mid = jnp.exp(clw_mid)
    rt_all = r * p_incl
    kt_all = k2 * p_inv
    rk_all = r * k2 * rk_ref[...]

    sp, yp = STATE_PASSES, Y_PASSES
    kkh = [kk_raw[:, s] for s in sls]
    kkn = [x / jnp.maximum(jnp.sqrt(jnp.sum(x * x, axis=-1, keepdims=True)), 1e-12) for x in kkh]
    at = [-kkn[h] * p_prev[:, sls[h]] for h in heads]
    bt = [kkn[h] * a[:, sls[h]] * p_inv[:, sls[h]] for h in heads]
    rt = [rt_all[:, s] for s in sls]
    kt = [kt_all[:, s] for s in sls]
    vh = [v[:, s] for s in sls]
    at_s = [_parts(x, sp) for x in at]
    bt_s = [_parts(x, sp) for x in bt]
    kt_s = [_parts(x, sp) for x in kt]
    vh_s = [_parts(x, sp) for x in vh]
    rt_y = [_parts(x, yp) for x in rt]
    a_ab = [jnp.where(strict, _mmp(_dot_nt, at_s[h], bt_s[h]), 0.0) for h in heads]
    a_ak = [jnp.where(strict, _mmp(_dot_nt, at_s[h], kt_s[h]), 0.0) for h in heads]
    a_rb = [jnp.where(incl, _mmp(_dot_nt, rt_y[h], bt_s[h][:yp]), 0.0) for h in heads]
    a_rk = [jnp.where(incl, _mmp(_dot_nt, rt_y[h], kt_s[h][:yp]), 0.0) for h in heads]
    x = [jnp.concatenate([at[h], _mmp(_dot, _parts(a_ak[h], sp), vh_s[h])], axis=1) for h in heads]
    p = a_ab
    for level in range(max(int(math.log2(L)), 1)):
        if level > 0:
            p = [_mmp(_dot, ps, ps) for ps in p_s]
        p_s = [_parts(q, sp) for q in p]
        x = [x[h] + _mmp(_dot, p_s[h], _parts(x[h], sp)) for h in heads]
    x_s = [_parts(q, sp) for q in x]
    qy = [_mmp(_dot, _parts(a_rb[h], yp), x_s[h][:yp]) for h in heads]
    y0 = [qy[h][:, RWKV_HEAD:] + _mmp(_dot, _parts(a_rk[h], yp), vh_s[h][:yp]) for h in heads]
    qh = [rt[h] + qy[h][:, :RWKV_HEAD] for h in heads]
    s0 = [wkv_ref[0, h] * p_mid[:, sls[h]] for h in heads]
    s0_s = [_parts(q, sp) for q in s0]
    y = [_mmp(_dot_nt, _parts(qh[h], yp), s0_s[h][:yp]) + y0[h] for h in heads]
    wtb = [_mmp(_dot_tn, [q[:, :RWKV_HEAD] for q in x_s[h]], bt_s[h]) for h in heads]
    uv_s = [[jnp.concatenate([x_s[h][i][:, RWKV_HEAD:], vh_s[h][i]], axis=0) for i in range(sp)] for h in heads]
    bk_s = [[jnp.concatenate([bt_s[h][i], kt_s[h][i]], axis=0) for i in range(sp)] for h in heads]
    for h in heads:
        s_new = s0[h] + _mmp(_dot, s0_s[h], _parts(wtb[h], sp)) + _mmp(_dot_tn, uv_s[h], bk_s[h])
        wkv_ref[0, h] = s_new * p_incl[L - 1:L, sls[h]]
    for h in heads:
        mu_y = jnp.mean(y[h], axis=-1, keepdims=True)
        yc = y[h] - mu_y
        var = jnp.mean(yc * yc, axis=-1, keepdims=True)
        yn = yc * lax.rsqrt(var + GN_EPS) * lnw_ref[:, sls[h]] + lnb_ref[:, sls[h]]
        bonus = jnp.sum(rk_all[:, sls[h]], axis=-1, keepdims=True) * vh[h]
        o_ref[0, :, sls[h]] = ((yn + bonus) * g[:, sls[h]]).astype(o_ref.dtype)


def _rwkv(zr, shift0, wkv0, wts, chunk, t_valid):
    batch, t, _ = zr.shape
    n_chunks = t // chunk
    assert t_valid == t or n_chunks == 1
    body = functools.partial(_rwkv_body, chunk=chunk, t_valid=min(t_valid, chunk))
    vec = lambda width: pl.BlockSpec((1, width), lambda b, c: (0, 0))
    mat = lambda rows: pl.BlockSpec((rows, RWKV_W), lambda b, c: (0, 0))
    return pl.pallas_call(
        body,
        grid=(batch, n_chunks),
        in_specs=[pl.BlockSpec((1, chunk, RWKV_COLS), lambda b, c: (b, c, 0)),
                  pl.BlockSpec((1, 1, RWKV_COLS), lambda b, c: (b, 0, 0)),
                  pl.BlockSpec((1, RWKV_HEADS, RWKV_HEAD, RWKV_HEAD), lambda b, c: (b, 0, 0, 0)),
                  vec(RWKV_COLS), vec(RWKV_W), mat(LORA_W), vec(RWKV_W), mat(LORA_W), mat(GATE_LORA),
                  vec(RWKV_W), vec(RWKV_W), vec(RWKV_W), vec(RWKV_W), vec(RWKV_W)],
        out_specs=[pl.BlockSpec((1, chunk, RWKV_W), lambda b, c: (b, c, 0)),
                   pl.BlockSpec((1, RWKV_HEADS, RWKV_HEAD, RWKV_HEAD), lambda b, c: (b, 0, 0, 0))],
        out_shape=[jax.ShapeDtypeStruct((batch, t, RWKV_W), BF16),
                   jax.ShapeDtypeStruct((batch, RWKV_HEADS, RWKV_HEAD, RWKV_HEAD), F32)],
        scratch_shapes=[pltpu.VMEM((1, RWKV_COLS), F32)],
        compiler_params=_cparams(("parallel", "arbitrary")),
        name="rwkv_scan",
    )(zr, shift0, wkv0, *wts)


def _mix_out_body(x_ref, o0_ref, o1_ref, o2_ref, l0_ref, l1_ref, l2_ref, orw_ref, gate_ref, woa_ref, wor_ref,
                  wo_ref, gffn_ref, rwt_ref, rb_ref, cnt0_ref, *rest, dilated, n_alias):
    x1_ref, h2_ref, te_ref, tg_ref, rk_ref, cnt_ref = rest[n_alias:n_alias + 6]
    stage = list(rest[n_alias + 6:])

    def token_major(ref, gi):
        if not dilated:
            return ref[...].astype(F32)
        dil = ATTN_GROUPS[gi][1]
        if dil == 1:
            return ref[0, 0].astype(F32)
        st_ref = stage.pop()
        for r in range(dil):
            sub = ref[0, r].astype(F32)
            for s in range(GROUP_W // LANE):
                st_ref[s, pl.ds(r, ref.shape[2], stride=dil), :] = sub[:, s * LANE:(s + 1) * LANE]
        return jnp.concatenate([st_ref[s] for s in range(GROUP_W // LANE)], axis=1)

    l0, l1, l2 = token_major(l0_ref, 0), token_major(l1_ref, 1), token_major(l2_ref, 2)
    m = jnp.maximum(jnp.maximum(l0, l1), l2)
    e0, e1, e2 = jnp.exp(l0 - m), jnp.exp(l1 - m), jnp.exp(l2 - m)
    o_att = (e0 * token_major(o0_ref, 0) + e1 * token_major(o1_ref, 1) + e2 * token_major(o2_ref, 2)) \
        * (1.0 / (e0 + e1 + e2))
    gates = gate_ref[...].astype(F32)
    merged = gates[:, :D_MODEL] * _dot(o_att.astype(BF16), woa_ref[...]) \
        + gates[:, D_MODEL:] * _dot(orw_ref[...], wor_ref[...])
    x1 = x_ref[...] + _dot(merged.astype(BF16), wo_ref[...])
    x1_ref[...] = x1
    h2 = _rms(x1, gffn_ref[...])
    h2_ref[...] = h2

    logits = _mm3(_dot_nt, rwt_ref[...], h2) + rb_ref[...]
    e_iota = lax.broadcasted_iota(jnp.int32, logits.shape, 0)
    vals, idxs = [], []
    for _ in range(TOP_K):
        top = jnp.max(logits, axis=0, keepdims=True)
        idx = jnp.min(jnp.where(logits == top, e_iota, N_EXPERTS), axis=0, keepdims=True)
        vals.append(top)
        idxs.append(idx)
        logits = jnp.where(e_iota == idx, -jnp.inf, logits)
    exps = [jnp.exp(t - vals[0]) for t in vals]
    inv = 1.0 / (exps[0] + exps[1] + exps[2] + exps[3])
    te_ref[...] = jnp.concatenate(idxs, axis=0)
    tg_ref[...] = jnp.concatenate([e * inv for e in exps], axis=0)

    @pl.when(pl.program_id(0) == 0)
    def _():
        cnt_ref[...] = cnt0_ref[...]

    hits = [e_iota == idx for idx in idxs]
    onehot = jnp.where(hits[0] | hits[1] | hits[2] | hits[3], 1.0, 0.0)
    tm = onehot.shape[1]
    earlier = lax.broadcasted_iota(jnp.int32, (tm, tm), 0) < lax.broadcasted_iota(jnp.int32, (tm, tm), 1)
    before = cnt_ref[...] + _dot(onehot.astype(BF16), jnp.where(earlier, 1.0, 0.0).astype(BF16))
    rk_ref[...] = jnp.concatenate([jnp.sum(jnp.where(h, before, 0.0), axis=0, keepdims=True) for h in hits],
                                  axis=0).astype(jnp.int32)
    cnt_ref[...] += jnp.sum(onehot, axis=1, keepdims=True)


def _mix_out(x, o_g, lse_g, o_rwkv, gates, wts, cnt0, tm, n_all, row0, bufs):
    n = x.shape[0]
    blk0 = row0 // tm
    row = lambda i: (i, 0)
    const = lambda i: (0, 0)
    out_row = lambda i: (i + blk0, 0)
    out_col = lambda i: (0, i + blk0)
    tok = lambda w: pl.BlockSpec((tm, w), row)
    dilated = o_g[0].ndim == 4
    if dilated:
        tiles = o_g[0].shape[1] * o_g[0].shape[2] // tm
        sub = lambda i: (i // tiles, 0, i % tiles, 0)
        att_specs = [pl.BlockSpec((1, d, tm // d, GROUP_W), sub) for _, d in ATTN_GROUPS] * 2
        stage = [pltpu.VMEM((GROUP_W // LANE, tm, LANE), F32) for _, d in ATTN_GROUPS if d > 1] * 2
    else:
        att_specs, stage = [tok(GROUP_W)] * 6, []
    in_specs = [tok(D_MODEL)] + att_specs + [tok(RWKV_W), tok(GATE_COLS),
                pl.BlockSpec((GROUP_W, D_MODEL), const), pl.BlockSpec((RWKV_W, D_MODEL), const),
                pl.BlockSpec((D_MODEL, D_MODEL), const), pl.BlockSpec((1, D_MODEL), const),
                pl.BlockSpec((N_EXPERTS, D_MODEL), const), pl.BlockSpec((N_EXPERTS, 1), const),
                pl.BlockSpec((N_EXPERTS, 1), const)]
    args = [x, *o_g, *lse_g, o_rwkv, gates, *wts, cnt0]
    aliases = {}
    if bufs is not None:
        in_specs += [pl.BlockSpec(memory_space=pl.ANY)] * 5
        aliases = {len(args) + j: j for j in range(5)}
        args += list(bufs)
    *new_bufs, cnt = pl.pallas_call(
        functools.partial(_mix_out_body, dilated=dilated, n_alias=len(aliases)),
        grid=(n // tm,),
        scratch_shapes=stage,
        in_specs=in_specs,
        out_specs=[pl.BlockSpec((tm, D_MODEL), out_row), pl.BlockSpec((tm, D_MODEL), out_row),
                   pl.BlockSpec((TOP_K, tm), out_col), pl.BlockSpec((TOP_K, tm), out_col),
                   pl.BlockSpec((TOP_K, tm), out_col), pl.BlockSpec((N_EXPERTS, 1), const)],
        out_shape=[jax.ShapeDtypeStruct((n_all, D_MODEL), F32), jax.ShapeDtypeStruct((n_all, D_MODEL), F32),
                   jax.ShapeDtypeStruct((TOP_K, n_all), jnp.int32), jax.ShapeDtypeStruct((TOP_K, n_all), F32),
                   jax.ShapeDtypeStruct((TOP_K, n_all), jnp.int32), jax.ShapeDtypeStruct((N_EXPERTS, 1), F32)],
        input_output_aliases=aliases,
        compiler_params=_cparams(("arbitrary",)),
        name="mix_out",
    )(*args)
    return new_bufs, cnt


def _moe_dispatch_body(tab_ref, idx_hbm, h2_ref, x_hbm, idx_smem, zrow, isem, dsem, zsem, *, n_tiles):
    i = pl.program_id(0)
    slot = i % 2

    def idx_copy(rec, s):
        return pltpu.make_async_copy(idx_hbm.at[pl.ds(pl.multiple_of(rec * IDX_REC, IDX_REC), IDX_REC)],
                                     idx_smem.at[pl.ds(pl.multiple_of(s * IDX_REC, IDX_REC), IDX_REC)],
                                     isem.at[s])

    def row_copies_start(s):
        def one(t, carry):
            for k in range(TOP_K):
                d = idx_smem[s * IDX_REC + k * MOE_TILE + t]
                pltpu.make_async_copy(h2_ref.at[pl.ds(t, 1)], x_hbm.at[pl.ds(d, 1)], dsem.at[s]).start()
            return carry
        lax.fori_loop(0, MOE_TILE, one, 0, unroll=4)

    @pl.when(i == 0)
    def _():
        idx_copy(0, 0).start()

    @pl.when(i + 1 < n_tiles)
    def _():
        idx_copy(i + 1, 1 - slot).start()

    idx_copy(i, slot).wait()
    row_copies_start(slot)
    for _ in range(TOP_K):
        pltpu.make_async_copy(h2_ref, x_hbm.at[pl.ds(0, MOE_TILE)], dsem.at[slot]).wait()

    @pl.when(i == n_tiles - 1)
    def _():
        zrow[...] = jnp.zeros_like(zrow)

        def zero_copy(dst_row):
            return pltpu.make_async_copy(zrow.at[pl.ds(0, 1)], x_hbm.at[pl.ds(dst_row, 1)], zsem)

        def per_expert(e, carry):
            first = tab_ref[e] + tab_ref[N_EXPERTS + e]
            last = tab_ref[e] + tab_ref[2 * N_EXPERTS + e]
            lax.fori_loop(first, last, lambda r, c: (zero_copy(r).start(), c)[1], 0)
            lax.fori_loop(first, last, lambda r, c: (zero_copy(r).wait(), c)[1], 0)
            return carry
        lax.fori_loop(0, N_EXPERTS, per_expert, 0)


def _moe_dispatch(h2, idx_rec, tables, n_rows):
    n_tiles = h2.shape[0] // MOE_TILE
    grid_spec = pltpu.PrefetchScalarGridSpec(
        num_scalar_prefetch=1,
        grid=(n_tiles,),
        in_specs=[pl.BlockSpec(memory_space=pl.ANY),
                  pl.BlockSpec((MOE_TILE, D_MODEL), lambda i, tab: (i, 0))],
        out_specs=pl.BlockSpec(memory_space=pl.ANY),
        scratch_shapes=[pltpu.SMEM((2 * IDX_REC,), jnp.int32),
                        pltpu.VMEM((8, D_MODEL), F32),
                        pltpu.SemaphoreType.DMA((2,)),
                        pltpu.SemaphoreType.DMA((2,)),
                        pltpu.SemaphoreType.DMA],
    )
    return pl.pallas_call(
        functools.partial(_moe_dispatch_body, n_tiles=n_tiles),
        grid_spec=grid_spec,
        out_shape=jax.ShapeDtypeStruct((n_rows, D_MODEL), F32),
        compiler_params=_cparams(("arbitrary",)),
        name="moe_dispatch",
    )(tables, idx_rec, h2)


def _moe_experts_body(be_ref, nused_ref, x_ref, w1_ref, b1_ref, w2_ref, b2_ref, y_ref, w1b, w2b):
    i = pl.program_id(0)

    @pl.when(i < nused_ref[0])
    def _():
        @pl.when((i == 0) | (be_ref[i] != be_ref[jnp.maximum(i - 1, 0)]))
        def _():
            w1b[...] = w1_ref[0].astype(BF16)
            w2b[...] = w2_ref[0].astype(BF16)

        u = _dot(x_ref[...].astype(BF16), w1b[...]) + b1_ref[0]
        glu = jnp.minimum(u[:, :D_MODEL], SWIGLU_LIMIT)
        lin = jnp.clip(u[:, D_MODEL:], -SWIGLU_LIMIT, SWIGLU_LIMIT)
        act = glu * _sigmoid(SWIGLU_ALPHA * glu) * (lin + 1.0)
        y_ref[...] = _dot(act.astype(BF16), w2b[...]) + b2_ref[0]

    @pl.when(i >= nused_ref[0])
    def _():
        y_ref[...] = jnp.zeros_like(y_ref)


def _moe_experts(x_rows, block_e, n_used, w1, b1, w2, b2):
    n_blocks = block_e.shape[0]
    by_expert = lambda i, be, nu: (be[i], 0, 0)
    grid_spec = pltpu.PrefetchScalarGridSpec(
        num_scalar_prefetch=2,
        grid=(n_blocks,),
        in_specs=[pl.BlockSpec((MOE_BLOCK, D_MODEL), lambda i, be, nu: (jnp.minimum(i, nu[0] - 1), 0)),
                  pl.BlockSpec((1, D_MODEL, 2 * D_MODEL), by_expert),
                  pl.BlockSpec((1, 1, 2 * D_MODEL), by_expert),
                  pl.BlockSpec((1, D_MODEL, D_MODEL), by_expert),
                  pl.BlockSpec((1, 1, D_MODEL), by_expert)],
        out_specs=pl.BlockSpec((MOE_BLOCK, D_MODEL), lambda i, be, nu: (i, 0)),
        scratch_shapes=[pltpu.VMEM((D_MODEL, 2 * D_MODEL), BF16),
                        pltpu.VMEM((D_MODEL, D_MODEL), BF16)],
    )
    return pl.pallas_call(
        _moe_experts_body,
        grid_spec=grid_spec,
        out_shape=jax.ShapeDtypeStruct((n_blocks * MOE_BLOCK, D_MODEL), F32),
        compiler_params=_cparams(("arbitrary",)),
        name="moe_experts",
    )(block_e, n_used, x_rows, w1, b1, w2, b2)


def _route(top_e, rank, counts):
    n_tok = top_e.shape[1]
    counts = counts.reshape(N_EXPERTS).astype(jnp.int32)
    padded = (counts + MOE_BLOCK - 1) // MOE_BLOCK * MOE_BLOCK
    pad_end = jnp.cumsum(padded)
    pad_start = pad_end - padded
    experts = jnp.arange(N_EXPERTS, dtype=jnp.int32)
    dest = rank + jnp.sum(jnp.where(top_e[..., None] == experts, pad_start, 0), axis=-1)
    n_blocks = -(-(n_tok * TOP_K + N_EXPERTS * (MOE_BLOCK - 1)) // MOE_BLOCK)
    blk_row0 = jnp.arange(n_blocks, dtype=jnp.int32) * MOE_BLOCK
    block_e = jnp.minimum(jnp.sum(blk_row0[:, None] >= pad_end[None, :], axis=1), N_EXPERTS - 1).astype(jnp.int32)
    n_used = (pad_end[-1] // MOE_BLOCK).astype(jnp.int32).reshape(1)
    n_tiles = n_tok // MOE_TILE
    rec = dest.reshape(TOP_K, n_tiles, MOE_TILE).transpose(1, 0, 2).reshape(n_tiles, TOP_K * MOE_TILE)
    idx_rec = jnp.concatenate([rec, jnp.zeros((n_tiles, IDX_REC - TOP_K * MOE_TILE), jnp.int32)], axis=1).reshape(-1)
    tables = jnp.concatenate([pad_start, counts, padded]).astype(jnp.int32)
    return idx_rec, tables, block_e, n_used, n_blocks * MOE_BLOCK


def _tail_body(idx_hbm, y_hbm, x1_ref, tg_ref, pe_ref, gple_ref, wpg_ref, wp_ref, gfin_ref, yp_ref, ys_ref,
               idx_smem, gbuf, isem, gsem, *, n_tiles, n_prompt_tiles):
    i = pl.program_id(0)
    slot = i % 2

    def idx_copy(rec, s):
        return pltpu.make_async_copy(idx_hbm.at[pl.ds(pl.multiple_of(rec * IDX_REC, IDX_REC), IDX_REC)],
                                     idx_smem.at[pl.ds(pl.multiple_of(s * IDX_REC, IDX_REC), IDX_REC)],
                                     isem.at[s])

    def gather_start(s):
        def one(t, carry):
            for k in range(TOP_K):
                d = idx_smem[s * IDX_REC + k * MOE_TILE + t]
                pltpu.make_async_copy(y_hbm.at[pl.ds(d, 1)], gbuf.at[s, k, pl.ds(t, 1)], gsem.at[s]).start()
            return carry
        lax.fori_loop(0, MOE_TILE, one, 0, unroll=4)

    @pl.when(i == 0)
    def _():
        idx_copy(0, 0).start()
        idx_copy(0, 0).wait()
        gather_start(0)
        idx_copy(1, 1).start()

    @pl.when(i + 1 < n_tiles)
    def _():
        idx_copy(i + 1, 1 - slot).wait()
        gather_start(1 - slot)

        @pl.when(i + 2 < n_tiles)
        def _():
            idx_copy(i + 2, slot).start()

    for k in range(TOP_K):
        pltpu.make_async_copy(y_hbm.at[pl.ds(0, MOE_TILE)], gbuf.at[slot, k], gsem.at[slot]).wait()
    tg = tg_ref[...]
    moe = tg[:, 0:1] * gbuf[slot, 0]
    for k in range(1, TOP_K):
        moe = moe + tg[:, k:k + 1] * gbuf[slot, k]
    x2 = x1_ref[...] + moe
    gate = _sigmoid(_dot(_rms(x2, gple_ref[...]).astype(BF16), wpg_ref[...]))
    x3 = x2 + gate * _dot(pe_ref[...].astype(BF16), wp_ref[...])
    y = _rms(x3, gfin_ref[...])

    @pl.when(i < n_prompt_tiles)
    def _():
        yp_ref[...] = y

    @pl.when(i >= n_prompt_tiles)
    def _():
        ys_ref[...] = y


def _tail(x1_all, y_rows, idx_rec, tg_t, pe_all, wts, n_p):
    n_all = x1_all.shape[0]
    n_tiles = n_all // MOE_TILE
    n_prompt_tiles = n_p // MOE_TILE
    row = lambda i: (i, 0)
    const = lambda i: (0, 0)
    body = functools.partial(_tail_body, n_tiles=n_tiles, n_prompt_tiles=n_prompt_tiles)
    return pl.pallas_call(
        body,
        grid=(n_tiles,),
        in_specs=[pl.BlockSpec(memory_space=pl.ANY), pl.BlockSpec(memory_space=pl.ANY),
                  pl.BlockSpec((MOE_TILE, D_MODEL), row), pl.BlockSpec((MOE_TILE, TOP_K), row),
                  pl.BlockSpec((MOE_TILE, PLE_DIM), row), pl.BlockSpec((1, D_MODEL), const),
                  pl.BlockSpec((D_MODEL, D_MODEL), const), pl.BlockSpec((PLE_DIM, D_MODEL), const),
                  pl.BlockSpec((1, D_MODEL), const)],
        out_specs=[pl.BlockSpec((MOE_TILE, D_MODEL), lambda i: (jnp.minimum(i, n_prompt_tiles - 1), 0)),
                   pl.BlockSpec((MOE_TILE, D_MODEL), lambda i: (jnp.maximum(i - n_prompt_tiles, 0), 0))],
        out_shape=[jax.ShapeDtypeStruct((n_p, D_MODEL), F32),
                   jax.ShapeDtypeStruct((n_all - n_p, D_MODEL), F32)],
        scratch_shapes=[pltpu.SMEM((2 * IDX_REC,), jnp.int32),
                        pltpu.VMEM((2, TOP_K, MOE_TILE, D_MODEL), F32),
                        pltpu.SemaphoreType.DMA((2,)),
                        pltpu.SemaphoreType.DMA((2,))],
        compiler_params=_cparams(("arbitrary",)),
        name="tail",
    )(idx_rec, y_rows, x1_all, tg_t, pe_all, *wts)


RWKV_CHUNK = 128
RWKV_CHUNK_SAMPLE = 8
STATE_PASSES = 1
Y_PASSES = 1
TM_PROMPT = 256


def kernel(x_prompt, x_sample, p_prompt, p_sample, cache_kv_w128, cache_kv_w512, cache_kv_w2048, state_rwkv_shift, state_rwkv_wkv, norm_mix_g, w_in, rwkv_mu, rwkv_w0, rwkv_w2, rwkv_a0, rwkv_a2, rwkv_g2, rwkv_k_k, rwkv_k_a, rwkv_r_k, rwkv_ln_w, rwkv_ln_b, w_out_attn, w_out_rwkv, w_out, norm_ffn_g, router_w, router_b, moe_w1, moe_b1, moe_w2, moe_b2, norm_ple_g, w_ple, w_ple_gate, norm_final_g):
    bp, seq, _ = x_prompt.shape
    bs, t_s, _ = x_sample.shape
    n_p, n_s = bp * seq, bs * t_s
    n_all = n_p + n_s
    assert w_in.shape[0] == 1, "single layer"
    caches = (cache_kv_w128, cache_kv_w512, cache_kv_w2048)

    row = lambda a: a.reshape(1, -1)
    w_in_b = w_in[0].astype(BF16)
    zeros64 = jnp.zeros((LORA_W // 2, RWKV_W), F32)
    rwkv_wts = (row(rwkv_mu[0]), row(rwkv_w0[0]), jnp.concatenate([rwkv_w2[0], zeros64], axis=0),
                row(rwkv_a0[0]), jnp.concatenate([zeros64, rwkv_a2[0]], axis=0), rwkv_g2[0],
                row(rwkv_k_k[0]), row(rwkv_k_a[0]), row(rwkv_r_k[0]), row(rwkv_ln_w[0]), row(rwkv_ln_b[0]))
    mix_wts = (w_out_attn[0].astype(BF16), w_out_rwkv[0].astype(BF16), w_out[0].astype(BF16),
               row(norm_ffn_g[0]), router_w[0].T, router_b[0].reshape(N_EXPERTS, 1))
    tail_wts = (row(norm_ple_g[0]), w_ple_gate[0].astype(BF16), w_ple[0].astype(BF16), row(norm_final_g))
    g_mix = row(norm_mix_g[0])

    xp = x_prompt.reshape(n_p, D_MODEL)
    qd_p, kvd_p, kv_p, zr_p, gate_p = _in_proj_dilated(xp, g_mix, w_in_b, TM_PROMPT, bp, seq)
    att_p = [_attn_prompt(qd_p[gi], kvd_p[gi], gi) for gi in range(N_GROUPS)]
    orw_p, wkv_p = _rwkv(zr_p.reshape(bp, seq, RWKV_COLS), jnp.zeros((bp, 1, RWKV_COLS), F32),
                         jnp.zeros((bp, RWKV_HEADS, RWKV_HEAD, RWKV_HEAD), F32), rwkv_wts, RWKV_CHUNK, seq)
    bufs, counts = _mix_out(xp, [a[0] for a in att_p], [a[1] for a in att_p], orw_p.reshape(n_p, RWKV_W), gate_p,
                            mix_wts, jnp.zeros((N_EXPERTS, 1), F32), TM_PROMPT, n_all, 0, None)

    xs = x_sample.reshape(n_s, D_MODEL)
    q_s, kv_s, zr_s, gate_s = _in_proj(xs, g_mix, w_in_b, n_s)
    t_pad = 8
    pad_t = lambda a: jnp.pad(a.reshape(bs, t_s, -1), ((0, 0), (0, t_pad - t_s), (0, 0)))
    q_s3, kv_s3 = pad_t(q_s), pad_t(kv_s)
    att_s, new_caches = [], []
    for gi in range(N_GROUPS):
        wc = caches[gi].shape[2]
        o, lse, newc = _attn_sample(q_s3, kv_s3, caches[gi].reshape(bs, wc, 2 * GROUP_W), gi, t_s)
        att_s.append((o[:, :t_s].reshape(n_s, GROUP_W), lse[:, :t_s].reshape(n_s, GROUP_W)))
        new_caches.append(newc.reshape(1, bs, wc, 2, HEADS_PER_GROUP, HEAD_DIM))
    zr_s3 = jnp.pad(zr_s.reshape(bs, t_s, RWKV_COLS), ((0, 0), (0, RWKV_CHUNK_SAMPLE - t_s), (0, 0)))
    orw_s, wkv_s = _rwkv(zr_s3, state_rwkv_shift[0].reshape(bs, 1, RWKV_COLS), state_rwkv_wkv[0], rwkv_wts,
                         RWKV_CHUNK_SAMPLE, t_s)
    bufs, counts = _mix_out(xs, [a[0] for a in att_s], [a[1] for a in att_s], orw_s[:, :t_s].reshape(n_s, RWKV_W),
                            gate_s, mix_wts, counts, n_s, n_all, n_p, bufs)
    x1_all, h2_all, top_e, top_g, rank = bufs

    idx_rec, tables, block_e, n_used, n_rows = _route(top_e, rank, counts)
    x_rows = _moe_dispatch(h2_all, idx_rec, tables, n_rows)
    y_rows = _moe_experts(x_rows, block_e, n_used, moe_w1[0], moe_b1[0].reshape(N_EXPERTS, 1, 2 * D_MODEL),
                          moe_w2[0], moe_b2[0].reshape(N_EXPERTS, 1, D_MODEL))
    pe_all = jnp.concatenate([p_prompt[0].reshape(n_p, PLE_DIM), p_sample[0].reshape(n_s, PLE_DIM)], axis=0)
    y_p, y_s = _tail(x1_all, y_rows, idx_rec, top_g.T, pe_all, tail_wts, n_p)

    kv_p4 = kv_p.reshape(bp, seq, N_GROUPS, 2, HEADS_PER_GROUP, HEAD_DIM)
    kv_out_p = [kv_p4[:, seq - min(w, seq):, gi][None] for gi, (w, _) in enumerate(ATTN_GROUPS)]
    shift_p = zr_p.reshape(bp, seq, RWKV_COLS)[:, -1][None]
    shift_s = zr_s.reshape(bs, t_s, RWKV_COLS)[:, -1][None]
    return (y_p.reshape(bp, seq, D_MODEL), y_s.reshape(bs, t_s, D_MODEL),
            kv_out_p[0], kv_out_p[1], kv_out_p[2], shift_p, wkv_p[None],
            new_caches[0], new_caches[1], new_caches[2], shift_s, wkv_s[None])
```

```python
import functools
import math

import numpy as np
import jax
import jax.numpy as jnp
from jax import lax
from jax.experimental import pallas as pl
from jax.experimental.pallas import tpu as pltpu

F32 = jnp.float32
BF16 = jnp.bfloat16

LANE = 128
D_MODEL = 1024
N_GROUPS = 3
HEADS_PER_GROUP = 4
HEAD_DIM = 64
ATTN_GROUPS = ((128, 1), (512, 4), (2048, 16))
GROUP_W = HEADS_PER_GROUP * HEAD_DIM
Q_COLS = N_GROUPS * GROUP_W
KV_COLS = 2 * Q_COLS
BAND = 128

RWKV_HEADS = 8
RWKV_HEAD = 64
RWKV_W = 512
LORA_W = 128
GATE_LORA = 128
RWKV_COLS = 3 * RWKV_W + LORA_W + GATE_LORA
GN_EPS = 64e-5
GATE_COLS = 2 * D_MODEL
Z_RWKV0 = 3 * Q_COLS
Z_GATE0 = Z_RWKV0 + RWKV_COLS
IN_COLS = Z_GATE0 + GATE_COLS

N_EXPERTS = 32
TOP_K = 4
SWIGLU_LIMIT = 7.0
SWIGLU_ALPHA = 1.702
MOE_BLOCK = 512
MOE_TILE = 128
IDX_REC = 1024
PLE_DIM = 256
RMS_EPS = 1e-6

NEG_BIG = -1e30
VMEM_LIMIT = 56 * 1024 * 1024


def _cparams(sem):
    return pltpu.CompilerParams(dimension_semantics=sem, vmem_limit_bytes=VMEM_LIMIT)


def _rms(x, g):
    return x * lax.rsqrt(jnp.mean(x * x, axis=-1, keepdims=True) + RMS_EPS) * g


def _sigmoid(x):
    return 1.0 / (1.0 + jnp.exp(-x))


def _dot(a, b):
    return jnp.dot(a, b, preferred_element_type=F32)


def _dot_nt(a, b):
    return lax.dot_general(a, b, (((1,), (1,)), ((), ())), preferred_element_type=F32)


def _dot_tn(a, b):
    return lax.dot_general(a, b, (((0,), (0,)), ((), ())), preferred_element_type=F32)


def _split(x):
    hi = x.astype(BF16)
    lo = (x - hi.astype(F32)).astype(BF16)
    return hi, lo


def _mm3(dot, a, b):
    ah, al = _split(a)
    bh, bl = _split(b)
    return dot(ah, bh) + (dot(ah, bl) + dot(al, bh))


def _parts(x, n):
    out = []
    for _ in range(n - 1):
        hi = x.astype(BF16)
        out.append(hi)
        x = x - hi.astype(F32)
    out.append(x.astype(BF16))
    return out


def _mmp(dot, ap, bp):
    order = max(len(ap), len(bp))
    acc = None
    for i, a in enumerate(ap):
        for j, b in enumerate(bp):
            if i + j < order:
                t = dot(a, b)
                acc = t if acc is None else acc + t
    return acc


def _in_proj_body(x_ref, g_ref, w_ref, q_ref, kv_ref, zr_ref, gate_ref):
    h = _rms(x_ref[...], g_ref[...]).astype(BF16)

    def proj(lo, width):
        return _dot(h, w_ref[:, lo:lo + width])

    q_ref[...] = (proj(0, Q_COLS) * (1.0 / math.sqrt(HEAD_DIM))).astype(BF16)
    for g in range(N_GROUPS):
        kv_ref[:, 2 * g * GROUP_W:(2 * g + 1) * GROUP_W] = proj(Q_COLS + g * GROUP_W, GROUP_W)
        kv_ref[:, (2 * g + 1) * GROUP_W:(2 * g + 2) * GROUP_W] = proj(2 * Q_COLS + g * GROUP_W, GROUP_W)
    zr_ref[...] = proj(Z_RWKV0, RWKV_COLS)
    gate_ref[...] = _sigmoid(proj(Z_GATE0, GATE_COLS)).astype(BF16)


def _in_proj_dilated_body(x_ref, g_ref, w_ref, *rest):
    qd_refs, kvd_refs = rest[0:N_GROUPS], rest[N_GROUPS:2 * N_GROUPS]
    kv_ref, zr_ref, gate_ref, st_ref = rest[2 * N_GROUPS:]
    tm = x_ref.shape[0]
    h = _rms(x_ref[...], g_ref[...]).astype(BF16)

    def proj(lo, width):
        return _dot(h, w_ref[:, lo:lo + width])

    n_q = Q_COLS // LANE

    def stage(slab0, val):
        for s in range(val.shape[1] // LANE):
            st_ref[slab0 + s] = val[:, s * LANE:(s + 1) * LANE]

    stage(0, proj(0, Q_COLS) * (1.0 / math.sqrt(HEAD_DIM)))
    for g in range(N_GROUPS):
        for part, src in ((0, Q_COLS), (1, 2 * Q_COLS)):
            col = (2 * g + part) * GROUP_W
            val = proj(src + g * GROUP_W, GROUP_W)
            kv_ref[:, col:col + GROUP_W] = val
            stage(n_q + col // LANE, val)
    for g, (_, dil) in enumerate(ATTN_GROUPS):
        for r in range(dil):
            rows = pl.ds(r, tm // dil, stride=dil) if dil > 1 else slice(None)
            for s in range(GROUP_W // LANE):
                qd_refs[g][0, r, :, s * LANE:(s + 1) * LANE] = st_ref[g * GROUP_W // LANE + s, rows, :].astype(BF16)
            for s in range(2 * GROUP_W // LANE):
                kvd_refs[g][0, r, :, s * LANE:(s + 1) * LANE] = \
                    st_ref[n_q + 2 * g * GROUP_W // LANE + s, rows, :].astype(BF16)
    zr_ref[...] = proj(Z_RWKV0, RWKV_COLS)
    gate_ref[...] = _sigmoid(proj(Z_GATE0, GATE_COLS)).astype(BF16)


def _in_proj_dilated(x, g, w_bf16, tm, batch, seq):
    n = x.shape[0]
    tiles = seq // tm
    row = lambda i: (i, 0)
    const = lambda i: (0, 0)
    sub = lambda i: (i // tiles, 0, i % tiles, 0)
    dils = [d for _, d in ATTN_GROUPS]
    assert all(tm % (16 * d) == 0 for d in dils)
    outs = pl.pallas_call(
        _in_proj_dilated_body,
        grid=(n // tm,),
        in_specs=[pl.BlockSpec((tm, D_MODEL), row),
                  pl.BlockSpec((1, D_MODEL), const),
                  pl.BlockSpec((D_MODEL, IN_COLS), const, pipeline_mode=pl.Buffered(1))],
        out_specs=[pl.BlockSpec((1, d, tm // d, GROUP_W), sub) for d in dils]
                  + [pl.BlockSpec((1, d, tm // d, 2 * GROUP_W), sub) for d in dils]
                  + [pl.BlockSpec((tm, KV_COLS), row),
                     pl.BlockSpec((tm, RWKV_COLS), row),
                     pl.BlockSpec((tm, GATE_COLS), row)],
        out_shape=[jax.ShapeDtypeStruct((batch, d, seq // d, GROUP_W), BF16) for d in dils]
                  + [jax.ShapeDtypeStruct((batch, d, seq // d, 2 * GROUP_W), BF16) for d in dils]
                  + [jax.ShapeDtypeStruct((n, KV_COLS), F32),
                     jax.ShapeDtypeStruct((n, RWKV_COLS), F32),
                     jax.ShapeDtypeStruct((n, GATE_COLS), BF16)],
        scratch_shapes=[pltpu.VMEM(((Q_COLS + KV_COLS) // LANE, tm, LANE), F32)],
        compiler_params=_cparams(("parallel",)),
        name="in_proj_dilated",
    )(x, g, w_bf16)
    return outs[0:N_GROUPS], outs[N_GROUPS:2 * N_GROUPS], outs[2 * N_GROUPS], outs[2 * N_GROUPS + 1], outs[2 * N_GROUPS + 2]


def _in_proj(x, g, w_bf16, tm):
    n = x.shape[0]
    row = lambda i: (i, 0)
    const = lambda i: (0, 0)
    return pl.pallas_call(
        _in_proj_body,
        grid=(n // tm,),
        in_specs=[pl.BlockSpec((tm, D_MODEL), row),
                  pl.BlockSpec((1, D_MODEL), const),
                  pl.BlockSpec((D_MODEL, IN_COLS), const)],
        out_specs=[pl.BlockSpec((tm, Q_COLS), row),
                   pl.BlockSpec((tm, KV_COLS), row),
                   pl.BlockSpec((tm, RWKV_COLS), row),
                   pl.BlockSpec((tm, GATE_COLS), row)],
        out_shape=[jax.ShapeDtypeStruct((n, Q_COLS), BF16),
                   jax.ShapeDtypeStruct((n, KV_COLS), F32),
                   jax.ShapeDtypeStruct((n, RWKV_COLS), F32),
                   jax.ShapeDtypeStruct((n, GATE_COLS), BF16)],
        compiler_params=_cparams(("parallel",)),
        name="in_proj",
    )(x, g, w_bf16)


def _alibi_slopes(gi):
    return [2.0 ** (-8.0 * (gi * HEADS_PER_GROUP + h + 1) / (N_GROUPS * HEADS_PER_GROUP))
            for h in range(HEADS_PER_GROUP)]


def _head_of_lane(shape):
    return lax.broadcasted_iota(jnp.int32, shape, len(shape) - 1) // HEAD_DIM


def _stack_heads(q):
    hl = _head_of_lane(q.shape)
    return jnp.concatenate([jnp.where(hl == h, q, jnp.zeros_like(q)) for h in range(HEADS_PER_GROUP)], axis=0)


def _unstack_heads(x4, rows):
    hl = _head_of_lane((rows, GROUP_W))
    out = x4[0:rows]
    for h in range(1, HEADS_PER_GROUP):
        out = jnp.where(hl == h, x4[h * rows:(h + 1) * rows], out)
    return out


def _attn_prompt_body(q_ref, kvp_ref, kvc_ref, o_ref, lse_ref, kb_ref, bias_ref, *, dil, slopes, n_sub):
    c = pl.program_id(2)
    kb_ref[0:BAND, :] = kvp_ref[0, 0]
    kb_ref[BAND:, :] = kvc_ref[0, 0]

    qi = lax.broadcasted_iota(jnp.int32, (BAND, 2 * BAND), 0) + BAND
    ki = lax.broadcasted_iota(jnp.int32, (BAND, 2 * BAND), 1)
    rel = qi - ki
    dist = jnp.where((rel >= 0) & (rel <= BAND), (dil * rel).astype(F32), -NEG_BIG / slopes[-1])
    for h in range(HEADS_PER_GROUP):
        bias_ref[h * BAND:(h + 1) * BAND, :] = -slopes[h] * dist

    def sub_block(n, carry):
        r0 = pl.multiple_of(n * BAND, BAND)
        q4 = _stack_heads(q_ref[0, 0, pl.ds(r0, BAND), :])
        kv = kb_ref[pl.ds(r0, 2 * BAND), :]
        s = _dot_nt(q4, kv[:, :GROUP_W]) + bias_ref[...]
        n_before_start = jnp.where((c == 0) & (n == 0), BAND, 0)
        kcol = lax.broadcasted_iota(jnp.int32, s.shape, 1)
        s = jnp.where(kcol < n_before_start, NEG_BIG, s)
        m = jnp.max(s, axis=-1, keepdims=True)
        e = jnp.exp(s - m)
        den = jnp.sum(e, axis=-1, keepdims=True)
        o4 = _dot(e.astype(BF16), kv[:, GROUP_W:]) * (1.0 / den)
        o_ref[0, 0, pl.ds(r0, BAND), :] = _unstack_heads(o4, BAND).astype(o_ref.dtype)
        lse4 = jnp.broadcast_to(m + jnp.log(den), (HEADS_PER_GROUP * BAND, GROUP_W))
        lse_ref[0, 0, pl.ds(r0, BAND), :] = _unstack_heads(lse4, BAND)
        return carry

    lax.fori_loop(0, n_sub, sub_block, 0)


def _attn_prompt(qd, kvd, gi):
    window, dil = ATTN_GROUPS[gi]
    assert window // dil == BAND
    batch, _, sub_len, _ = qd.shape
    chunk = min(sub_len, 1024)
    n_chunks = sub_len // chunk
    sub_per_chunk = chunk // BAND
    body = functools.partial(_attn_prompt_body, dil=dil, slopes=_alibi_slopes(gi), n_sub=sub_per_chunk)
    cur = lambda b, r, c: (b, r, c, 0)
    return pl.pallas_call(
        body,
        grid=(batch, dil, n_chunks),
        in_specs=[
            pl.BlockSpec((1, 1, chunk, GROUP_W), cur),
            pl.BlockSpec((1, 1, BAND, 2 * GROUP_W),
                         lambda b, r, c: (b, r, jnp.maximum(c * sub_per_chunk - 1, 0), 0)),
            pl.BlockSpec((1, 1, chunk, 2 * GROUP_W), cur),
        ],
        out_specs=[pl.BlockSpec((1, 1, chunk, GROUP_W), cur), pl.BlockSpec((1, 1, chunk, GROUP_W), cur)],
        out_shape=[jax.ShapeDtypeStruct(qd.shape, BF16), jax.ShapeDtypeStruct(qd.shape, F32)],
        scratch_shapes=[pltpu.VMEM((chunk + BAND, 2 * GROUP_W), BF16),
                        pltpu.VMEM((HEADS_PER_GROUP * BAND, 2 * BAND), F32)],
        compiler_params=_cparams(("parallel", "parallel", "arbitrary")),
        name=f"attn_prompt_g{gi}",
    )(qd, kvd, kvd)


def _attn_sample_body(q_ref, kvn_ref, cache_ref, o_ref, lse_ref, newc_ref, *, dil, slopes, wc, t_new):
    tp = q_ref.shape[1]
    cache = cache_ref[0]
    new = kvn_ref[0]
    newc_ref[0, 0:wc - t_new, :] = cache[t_new:wc]
    newc_ref[0, wc - t_new:wc, :] = new[0:t_new]

    q4 = _stack_heads(q_ref[0])
    cb = cache.astype(BF16)
    nb = new.astype(BF16)
    t_row = lax.broadcasted_iota(jnp.int32, (tp, 1), 0)
    slope_col = jnp.concatenate([jnp.full((tp, 1), s, F32) for s in slopes], axis=0)
    t4 = jnp.concatenate([t_row] * HEADS_PER_GROUP, axis=0)

    def scores(keys_bf16, dist):
        ok = (dist >= 0) & ((dist & (dil - 1)) == 0) & (dist <= BAND * dil)
        s = _dot_nt(q4, keys_bf16[:, :GROUP_W])
        return jnp.where(ok, s - slope_col * dist.astype(F32), NEG_BIG)

    pc = lax.broadcasted_iota(jnp.int32, (HEADS_PER_GROUP * tp, wc), 1)
    pn = lax.broadcasted_iota(jnp.int32, (HEADS_PER_GROUP * tp, tp), 1)
    s_c = scores(cb, wc + t4 - pc)
    s_n = scores(nb, jnp.where(pn < t_new, t4 - pn, -1))
    m = jnp.maximum(jnp.max(s_c, axis=-1, keepdims=True), jnp.max(s_n, axis=-1, keepdims=True))
    e_c = jnp.exp(s_c - m)
    e_n = jnp.exp(s_n - m)
    den = jnp.sum(e_c, axis=-1, keepdims=True) + jnp.sum(e_n, axis=-1, keepdims=True)
    o4 = (_dot(e_c.astype(BF16), cb[:, GROUP_W:]) + _dot(e_n.astype(BF16), nb[:, GROUP_W:])) * (1.0 / den)
    o_ref[0] = _unstack_heads(o4, tp).astype(o_ref.dtype)
    lse4 = jnp.broadcast_to(m + jnp.log(den), (HEADS_PER_GROUP * tp, GROUP_W))
    lse_ref[0] = _unstack_heads(lse4, tp)


def _attn_sample(q, kv, cache, gi, t_new):
    window, dil = ATTN_GROUPS[gi]
    batch, tp, _ = q.shape
    wc = cache.shape[1]
    assert wc + 0 - dil * BAND >= 0
    body = functools.partial(_attn_sample_body, dil=dil, slopes=_alibi_slopes(gi), wc=wc, t_new=t_new)
    return pl.pallas_call(
        body,
        grid=(batch,),
        in_specs=[pl.BlockSpec((1, tp, GROUP_W), lambda b: (b, 0, gi)),
                  pl.BlockSpec((1, tp, 2 * GROUP_W), lambda b: (b, 0, gi)),
                  pl.BlockSpec((1, wc, 2 * GROUP_W), lambda b: (b, 0, 0))],
        out_specs=[pl.BlockSpec((1, tp, GROUP_W), lambda b: (b, 0, 0)),
                   pl.BlockSpec((1, tp, GROUP_W), lambda b: (b, 0, 0)),
                   pl.BlockSpec((1, wc, 2 * GROUP_W), lambda b: (b, 0, 0))],
        out_shape=[jax.ShapeDtypeStruct((batch, tp, GROUP_W), BF16),
                   jax.ShapeDtypeStruct((batch, tp, GROUP_W), F32),
                   jax.ShapeDtypeStruct((batch, wc, 2 * GROUP_W), F32)],
        compiler_params=_cparams(("parallel",)),
        name=f"attn_sample_g{gi}",
    )(q, kv, cache)


def _rwkv_body(zr_ref, shift0_ref, wkv0_ref, mu_ref, w0_ref, w2_ref, a0_ref, a2_ref, g2_ref, kk_ref, ka_ref,
               rk_ref, lnw_ref, lnb_ref, o_ref, wkv_ref, prev_ref, *, chunk, t_valid):
    c = pl.program_id(1)
    L = chunk
    mm = functools.partial(_mm3, _dot)
    heads = range(RWKV_HEADS)
    sls = [slice(h * RWKV_HEAD, (h + 1) * RWKV_HEAD) for h in heads]

    @pl.when(c == 0)
    def _():
        prev_ref[...] = shift0_ref[0]
        wkv_ref[0] = wkv0_ref[0]

    zr = zr_ref[0]
    row = lax.broadcasted_iota(jnp.int32, (L, 1), 0)
    prev = jnp.where(row == 0, prev_ref[...], pltpu.roll(zr, 1, 0))
    prev_ref[...] = zr[L - 1:L, :]
    zm = zr + (prev - zr) * mu_ref[...]

    r = zm[:, 0:RWKV_W]
    k = zm[:, RWKV_W:2 * RWKV_W]
    v = zm[:, 2 * RWKV_W:3 * RWKV_W]
    xwa = zm[:, 3 * RWKV_W:3 * RWKV_W + LORA_W]
    xg = zm[:, 3 * RWKV_W + LORA_W:]
    w_pre = w0_ref[...] + mm(jnp.tanh(xwa), w2_ref[...])
    softplus = jnp.maximum(-w_pre, 0.0) + jnp.log(1.0 + jnp.exp(-jnp.abs(w_pre)))
    w_log = -softplus - 0.5
    lw = -jnp.exp(w_log)
    a = _sigmoid(a0_ref[...] + mm(xwa, a2_ref[...]))
    g = mm(_sigmoid(xg), g2_ref[...])
    kk_raw = k * kk_ref[...]
    k2 = k * (1.0 + (a - 1.0) * ka_ref[...])
    if t_valid < L:
        live = row < t_valid
        lw = jnp.where(live, lw, 0.0)
        kk_raw = jnp.where(live, kk_raw, 0.0)
        k2 = jnp.where(live, k2, 0.0)
        v = jnp.where(live, v, 0.0)

    ti = lax.broadcasted_iota(jnp.int32, (L, L), 0)
    si = lax.broadcasted_iota(jnp.int32, (L, L), 1)
    incl = ti >= si
    strict = ti > si
    clw = _mmp(_dot, [jnp.where(incl, 1.0, 0.0).astype(BF16)], _parts(lw, 3))
    mid = max(L // 2 - 1, 0)
    clw_mid = clw[mid:mid + 1, :]
    rel = clw - clw_mid
    p_incl = jnp.exp(rel)
    p_inv = jnp.exp(-rel)
    p_prev = jnp.exp(rel - lw)
    p_mid = jnp.exp(clw_mid)
    rt_all = r * p_incl
    kt_all = k2 * p_inv
    rk_all = r * k2 * rk_ref[...]

    sp, yp = STATE_PASSES, Y_PASSES
    kkh = [kk_raw[:, s] for s in sls]
    kkn = [x / jnp.maximum(jnp.sqrt(jnp.sum(x * x, axis=-1, keepdims=True)), 1e-12) for x in kkh]
    at = [-kkn[h] * p_prev[:, sls[h]] for h in heads]
    bt = [kkn[h] * a[:, sls[h]] * p_inv[:, sls[h]] for h in heads]
    rt = [rt_all[:, s] for s in sls]
    kt = [kt_all[:, s] for s in sls]
    vh = [v[:, s] for s in sls]
    at_s = [_parts(x, sp) for x in at]
    bt_s = [_parts(x, sp) for x in bt]
    kt_s = [_parts(x, sp) for x in kt]
    vh_s = [_parts(x, sp) for x in vh]
    rt_y = [_parts(x, yp) for x in rt]
    a_ab = [jnp.where(strict, _mmp(_dot_nt, at_s[h], bt_s[h]), 0.0) for h in heads]
    a_ak = [jnp.where(strict, _mmp(_dot_nt, at_s[h], kt_s[h]), 0.0) for h in heads]
    a_rb = [jnp.where(incl, _mmp(_dot_nt, rt_y[h], bt_s[h][:yp]), 0.0) for h in heads]
    a_rk = [jnp.where(incl, _mmp(_dot_nt, rt_y[h], kt_s[h][:yp]), 0.0) for h in heads]
    x = [jnp.concatenate([at[h], _mmp(_dot, _parts(a_ak[h], sp), vh_s[h])], axis=1) for h in heads]
    p = a_ab
    for level in range(max(int(math.log2(L)), 1)):
        if level > 0:
            p = [_mmp(_dot, ps, ps) for ps in p_s]
        p_s = [_parts(q, sp) for q in p]
        x = [x[h] + _mmp(_dot, p_s[h], _parts(x[h], sp)) for h in heads]
    x_s = [_parts(q, sp) for q in x]
    qy = [_mmp(_dot, _parts(a_rb[h], yp), x_s[h][:yp]) for h in heads]
    y0 = [qy[h][:, RWKV_HEAD:] + _mmp(_dot, _parts(a_rk[h], yp), vh_s[h][:yp]) for h in heads]
    qh = [rt[h] + qy[h][:, :RWKV_HEAD] for h in heads]
    s0 = [wkv_ref[0, h] * p_mid[:, sls[h]] for h in heads]
    s0_s = [_parts(q, sp) for q in s0]
    y = [_mmp(_dot_nt, _parts(qh[h], yp), s0_s[h][:yp]) + y0[h] for h in heads]
    wtb = [_mmp(_dot_tn, [q[:, :RWKV_HEAD] for q in x_s[h]], bt_s[h]) for h in heads]
    uv_s = [[jnp.concatenate([x_s[h][i][:, RWKV_HEAD:], vh_s[h][i]], axis=0) for i in range(sp)] for h in heads]
    bk_s = [[jnp.concatenate([bt_s[h][i], kt_s[h][i]], axis=0) for i in range(sp)] for h in heads]
    for h in heads:
        s_new = s0[h] + _mmp(_dot, s0_s[h], _parts(wtb[h], sp)) + _mmp(_dot_tn, uv_s[h], bk_s[h])
        wkv_ref[0, h] = s_new * p_incl[L - 1:L, sls[h]]
    for h in heads:
        mu_y = jnp.mean(y[h], axis=-1, keepdims=True)
        yc = y[h] - mu_y
        var = jnp.mean(yc * yc, axis=-1, keepdims=True)
        yn = yc * lax.rsqrt(var + GN_EPS) * lnw_ref[:, sls[h]] + lnb_ref[:, sls[h]]
        bonus = jnp.sum(rk_all[:, sls[h]], axis=-1, keepdims=True) * vh[h]
        o_ref[0, :, sls[h]] = ((yn + bonus) * g[:, sls[h]]).astype(o_ref.dtype)


def _rwkv(zr, shift0, wkv0, wts, chunk, t_valid):
    batch, t, _ = zr.shape
    n_chunks = t // chunk
    assert t_valid == t or n_chunks == 1
    body = functools.partial(_rwkv_body, chunk=chunk, t_valid=min(t_valid, chunk))
    vec = lambda width: pl.BlockSpec((1, width), lambda b, c: (0, 0))
    mat = lambda rows: pl.BlockSpec((rows, RWKV_W), lambda b, c: (0, 0))
    return pl.pallas_call(
        body,
        grid=(batch, n_chunks),
        in_specs=[pl.BlockSpec((1, chunk, RWKV_COLS), lambda b, c: (b, c, 0)),
                  pl.BlockSpec((1, 1, RWKV_COLS), lambda b, c: (b, 0, 0)),
                  pl.BlockSpec((1, RWKV_HEADS, RWKV_HEAD, RWKV_HEAD), lambda b, c: (b, 0, 0, 0)),
                  vec(RWKV_COLS), vec(RWKV_W), mat(LORA_W), vec(RWKV_W), mat(LORA_W), mat(GATE_LORA),
                  vec(RWKV_W), vec(RWKV_W), vec(RWKV_W), vec(RWKV_W), vec(RWKV_W)],
        out_specs=[pl.BlockSpec((1, chunk, RWKV_W), lambda b, c: (b, c, 0)),
                   pl.BlockSpec((1, RWKV_HEADS, RWKV_HEAD, RWKV_HEAD), lambda b, c: (b, 0, 0, 0))],
        out_shape=[jax.ShapeDtypeStruct((batch, t, RWKV_W), BF16),
                   jax.ShapeDtypeStruct((batch, RWKV_HEADS, RWKV_HEAD, RWKV_HEAD), F32)],
        scratch_shapes=[pltpu.VMEM((1, RWKV_COLS), F32)],
        compiler_params=_cparams(("parallel", "arbitrary")),
        name="rwkv_scan",
    )(zr, shift0, wkv0, *wts)


def _mix_out_body(x_ref, o0_ref, o1_ref, o2_ref, l0_ref, l1_ref, l2_ref, orw_ref, gate_ref, woa_ref, wor_ref,
                  wo_ref, gffn_ref, rwt_ref, rb_ref, cnt0_ref, *rest, dilated, n_alias):
    x1_ref, h2_ref, te_ref, tg_ref, rk_ref, cnt_ref = rest[n_alias:n_alias + 6]
    stage = list(rest[n_alias + 6:])

    def token_major(ref, gi):
        if not dilated:
            return ref[...].astype(F32)
        dil = ATTN_GROUPS[gi][1]
        if dil == 1:
            return ref[0, 0].astype(F32)
        st_ref = stage.pop()
        for r in range(dil):
            sub = ref[0, r].astype(F32)
            for s in range(GROUP_W // LANE):
                st_ref[s, pl.ds(r, ref.shape[2], stride=dil), :] = sub[:, s * LANE:(s + 1) * LANE]
        return jnp.concatenate([st_ref[s] for s in range(GROUP_W // LANE)], axis=1)

    l0, l1, l2 = token_major(l0_ref, 0), token_major(l1_ref, 1), token_major(l2_ref, 2)
    m = jnp.maximum(jnp.maximum(l0, l1), l2)
    e0, e1, e2 = jnp.exp(l0 - m), jnp.exp(l1 - m), jnp.exp(l2 - m)
    o_att = (e0 * token_major(o0_ref, 0) + e1 * token_major(o1_ref, 1) + e2 * token_major(o2_ref, 2)) \
        * (1.0 / (e0 + e1 + e2))
    gates = gate_ref[...].astype(F32)
    merged = gates[:, :D_MODEL] * _dot(o_att.astype(BF16), woa_ref[...]) \
        + gates[:, D_MODEL:] * _dot(orw_ref[...], wor_ref[...])
    x1 = x_ref[...] + _dot(merged.astype(BF16), wo_ref[...])
    x1_ref[...] = x1
    h2 = _rms(x1, gffn_ref[...])
    h2_ref[...] = h2

    logits = _mm3(_dot_nt, rwt_ref[...], h2) + rb_ref[...]
    e_iota = lax.broadcasted_iota(jnp.int32, logits.shape, 0)
    vals, idxs = [], []
    for _ in range(TOP_K):
        top = jnp.max(logits, axis=0, keepdims=True)
        idx = jnp.min(jnp.where(logits == top, e_iota, N_EXPERTS), axis=0, keepdims=True)
        vals.append(top)
        idxs.append(idx)
        logits = jnp.where(e_iota == idx, -jnp.inf, logits)
    exps = [jnp.exp(t - vals[0]) for t in vals]
    inv = 1.0 / (exps[0] + exps[1] + exps[2] + exps[3])
    te_ref[...] = jnp.concatenate(idxs, axis=0)
    tg_ref[...] = jnp.concatenate([e * inv for e in exps], axis=0)

    @pl.when(pl.program_id(0) == 0)
    def _():
        cnt_ref[...] = cnt0_ref[...]

    hits = [e_iota == idx for idx in idxs]
    onehot = jnp.where(hits[0] | hits[1] | hits[2] | hits[3], 1.0, 0.0)
    tm = onehot.shape[1]
    earlier = lax.broadcasted_iota(jnp.int32, (tm, tm), 0) < lax.broadcasted_iota(jnp.int32, (tm, tm), 1)
    before = cnt_ref[...] + _dot(onehot.astype(BF16), jnp.where(earlier, 1.0, 0.0).astype(BF16))
    rk_ref[...] = jnp.concatenate([jnp.sum(jnp.where(h, before, 0.0), axis=0, keepdims=True) for h in hits],
                                  axis=0).astype(jnp.int32)
    cnt_ref[...] += jnp.sum(onehot, axis=1, keepdims=True)


def _mix_out(x, o_g, lse_g, o_rwkv, gates, wts, cnt0, tm, n_all, row0, bufs):
    n = x.shape[0]
    blk0 = row0 // tm
    row = lambda i: (i, 0)
    const = lambda i: (0, 0)
    out_row = lambda i: (i + blk0, 0)
    out_col = lambda i: (0, i + blk0)
    tok = lambda w: pl.BlockSpec((tm, w), row)
    dilated = o_g[0].ndim == 4
    if dilated:
        tiles = o_g[0].shape[1] * o_g[0].shape[2] // tm
        sub = lambda i: (i // tiles, 0, i % tiles, 0)
        att_specs = [pl.BlockSpec((1, d, tm // d, GROUP_W), sub) for _, d in ATTN_GROUPS] * 2
        stage = [pltpu.VMEM((GROUP_W // LANE, tm, LANE), F32) for _, d in ATTN_GROUPS if d > 1] * 2
    else:
        att_specs, stage = [tok(GROUP_W)] * 6, []
    in_specs = [tok(D_MODEL)] + att_specs + [tok(RWKV_W), tok(GATE_COLS),
                pl.BlockSpec((GROUP_W, D_MODEL), const), pl.BlockSpec((RWKV_W, D_MODEL), const),
                pl.BlockSpec((D_MODEL, D_MODEL), const), pl.BlockSpec((1, D_MODEL), const),
                pl.BlockSpec((N_EXPERTS, D_MODEL), const), pl.BlockSpec((N_EXPERTS, 1), const),
                pl.BlockSpec((N_EXPERTS, 1), const)]
    args = [x, *o_g, *lse_g, o_rwkv, gates, *wts, cnt0]
    aliases = {}
    if bufs is not None:
        in_specs += [pl.BlockSpec(memory_space=pl.ANY)] * 5
        aliases = {len(args) + j: j for j in range(5)}
        args += list(bufs)
    *new_bufs, cnt = pl.pallas_call(
        functools.partial(_mix_out_body, dilated=dilated, n_alias=len(aliases)),
        grid=(n // tm,),
        scratch_shapes=stage,
        in_specs=in_specs,
        out_specs=[pl.BlockSpec((tm, D_MODEL), out_row), pl.BlockSpec((tm, D_MODEL), out_row),
                   pl.BlockSpec((TOP_K, tm), out_col), pl.BlockSpec((TOP_K, tm), out_col),
                   pl.BlockSpec((TOP_K, tm), out_col), pl.BlockSpec((N_EXPERTS, 1), const)],
        out_shape=[jax.ShapeDtypeStruct((n_all, D_MODEL), F32), jax.ShapeDtypeStruct((n_all, D_MODEL), F32),
                   jax.ShapeDtypeStruct((TOP_K, n_all), jnp.int32), jax.ShapeDtypeStruct((TOP_K, n_all), F32),
                   jax.ShapeDtypeStruct((TOP_K, n_all), jnp.int32), jax.ShapeDtypeStruct((N_EXPERTS, 1), F32)],
        input_output_aliases=aliases,
        compiler_params=_cparams(("arbitrary",)),
        name="mix_out",
    )(*args)
    return new_bufs, cnt


def _moe_dispatch_body(tab_ref, idx_hbm, h2_ref, x_hbm, idx_smem, zrow, isem, dsem, zsem, *, n_tiles):
    i = pl.program_id(0)
    slot = i % 2

    def idx_copy(rec, s):
        return pltpu.make_async_copy(idx_hbm.at[pl.ds(pl.multiple_of(rec * IDX_REC, IDX_REC), IDX_REC)],
                                     idx_smem.at[pl.ds(pl.multiple_of(s * IDX_REC, IDX_REC), IDX_REC)],
                                     isem.at[s])

    def row_copies_start(s):
        def one(t, carry):
            for k in range(TOP_K):
                d = idx_smem[s * IDX_REC + k * MOE_TILE + t]
                pltpu.make_async_copy(h2_ref.at[pl.ds(t, 1)], x_hbm.at[pl.ds(d, 1)], dsem.at[s]).start()
            return carry
        lax.fori_loop(0, MOE_TILE, one, 0, unroll=4)

    @pl.when(i == 0)
    def _():
        idx_copy(0, 0).start()

    @pl.when(i + 1 < n_tiles)
    def _():
        idx_copy(i + 1, 1 - slot).start()

    idx_copy(i, slot).wait()
    row_copies_start(slot)
    for _ in range(TOP_K):
        pltpu.make_async_copy(h2_ref, x_hbm.at[pl.ds(0, MOE_TILE)], dsem.at[slot]).wait()

    @pl.when(i == n_tiles - 1)
    def _():
        zrow[...] = jnp.zeros_like(zrow)

        def zero_copy(dst_row):
            return pltpu.make_async_copy(zrow.at[pl.ds(0, 1)], x_hbm.at[pl.ds(dst_row, 1)], zsem)

        def per_expert(e, carry):
            first = tab_ref[e] + tab_ref[N_EXPERTS + e]
            last = tab_ref[e] + tab_ref[2 * N_EXPERTS + e]
            lax.fori_loop(first, last, lambda r, c: (zero_copy(r).start(), c)[1], 0)
            lax.fori_loop(first, last, lambda r, c: (zero_copy(r).wait(), c)[1], 0)
            return carry
        lax.fori_loop(0, N_EXPERTS, per_expert, 0)


def _moe_dispatch(h2, idx_rec, tables, n_rows):
    n_tiles = h2.shape[0] // MOE_TILE
    grid_spec = pltpu.PrefetchScalarGridSpec(
        num_scalar_prefetch=1,
        grid=(n_tiles,),
        in_specs=[pl.BlockSpec(memory_space=pl.ANY),
                  pl.BlockSpec((MOE_TILE, D_MODEL), lambda i, tab: (i, 0))],
        out_specs=pl.BlockSpec(memory_space=pl.ANY),
        scratch_shapes=[pltpu.SMEM((2 * IDX_REC,), jnp.int32),
                        pltpu.VMEM((8, D_MODEL), F32),
                        pltpu.SemaphoreType.DMA((2,)),
                        pltpu.SemaphoreType.DMA((2,)),
                        pltpu.SemaphoreType.DMA],
    )
    return pl.pallas_call(
        functools.partial(_moe_dispatch_body, n_tiles=n_tiles),
        grid_spec=grid_spec,
        out_shape=jax.ShapeDtypeStruct((n_rows, D_MODEL), F32),
        compiler_params=_cparams(("arbitrary",)),
        name="moe_dispatch",
    )(tables, idx_rec, h2)


def _moe_experts_body(be_ref, nused_ref, x_ref, w1_ref, b1_ref, w2_ref, b2_ref, y_ref, w1b, w2b):
    i = pl.program_id(0)

    @pl.when(i < nused_ref[0])
    def _():
        @pl.when((i == 0) | (be_ref[i] != be_ref[jnp.maximum(i - 1, 0)]))
        def _():
            w1b[...] = w1_ref[0].astype(BF16)
            w2b[...] = w2_ref[0].astype(BF16)

        u = _dot(x_ref[...].astype(BF16), w1b[...]) + b1_ref[0]
        glu = jnp.minimum(u[:, :D_MODEL], SWIGLU_LIMIT)
        lin = jnp.clip(u[:, D_MODEL:], -SWIGLU_LIMIT, SWIGLU_LIMIT)
        act = glu * _sigmoid(SWIGLU_ALPHA * glu) * (lin + 1.0)
        y_ref[...] = _dot(act.astype(BF16), w2b[...]) + b2_ref[0]

    @pl.when(i >= nused_ref[0])
    def _():
        y_ref[...] = jnp.zeros_like(y_ref)


def _moe_experts(x_rows, block_e, n_used, w1, b1, w2, b2):
    n_blocks = block_e.shape[0]
    by_expert = lambda i, be, nu: (be[i], 0, 0)
    grid_spec = pltpu.PrefetchScalarGridSpec(
        num_scalar_prefetch=2,
        grid=(n_blocks,),
        in_specs=[pl.BlockSpec((MOE_BLOCK, D_MODEL), lambda i, be, nu: (jnp.minimum(i, nu[0] - 1), 0)),
                  pl.BlockSpec((1, D_MODEL, 2 * D_MODEL), by_expert),
                  pl.BlockSpec((1, 1, 2 * D_MODEL), by_expert),
                  pl.BlockSpec((1, D_MODEL, D_MODEL), by_expert),
                  pl.BlockSpec((1, 1, D_MODEL), by_expert)],
        out_specs=pl.BlockSpec((MOE_BLOCK, D_MODEL), lambda i, be, nu: (i, 0)),
        scratch_shapes=[pltpu.VMEM((D_MODEL, 2 * D_MODEL), BF16),
                        pltpu.VMEM((D_MODEL, D_MODEL), BF16)],
    )
    return pl.pallas_call(
        _moe_experts_body,
        grid_spec=grid_spec,
        out_shape=jax.ShapeDtypeStruct((n_blocks * MOE_BLOCK, D_MODEL), F32),
        compiler_params=_cparams(("arbitrary",)),
        name="moe_experts",
    )(block_e, n_used, x_rows, w1, b1, w2, b2)


def _route(top_e, rank, counts):
    n_tok = top_e.shape[1]
    counts = counts.reshape(N_EXPERTS).astype(jnp.int32)
    padded = (counts + MOE_BLOCK - 1) // MOE_BLOCK * MOE_BLOCK
    pad_end = jnp.cumsum(padded)
    pad_start = pad_end - padded
    experts = jnp.arange(N_EXPERTS, dtype=jnp.int32)
    dest = rank + jnp.sum(jnp.where(top_e[..., None] == experts, pad_start, 0), axis=-1)
    n_blocks = -(-(n_tok * TOP_K + N_EXPERTS * (MOE_BLOCK - 1)) // MOE_BLOCK)
    blk_row0 = jnp.arange(n_blocks, dtype=jnp.int32) * MOE_BLOCK
    block_e = jnp.minimum(jnp.sum(blk_row0[:, None] >= pad_end[None, :], axis=1), N_EXPERTS - 1).astype(jnp.int32)
    n_used = (pad_end[-1] // MOE_BLOCK).astype(jnp.int32).reshape(1)
    n_tiles = n_tok // MOE_TILE
    rec = dest.reshape(TOP_K, n_tiles, MOE_TILE).transpose(1, 0, 2).reshape(n_tiles, TOP_K * MOE_TILE)
    idx_rec = jnp.concatenate([rec, jnp.zeros((n_tiles, IDX_REC - TOP_K * MOE_TILE), jnp.int32)], axis=1).reshape(-1)
    tables = jnp.concatenate([pad_start, counts, padded]).astype(jnp.int32)
    return idx_rec, tables, block_e, n_used, n_blocks * MOE_BLOCK


def _tail_body(idx_hbm, y_hbm, x1_ref, tg_ref, pe_ref, gple_ref, wpg_ref, wp_ref, gfin_ref, yp_ref, ys_ref,
               idx_smem, gbuf, isem, gsem, *, n_tiles, n_prompt_tiles):
    i = pl.program_id(0)
    slot = i % 2

    def idx_copy(rec, s):
        return pltpu.make_async_copy(idx_hbm.at[pl.ds(pl.multiple_of(rec * IDX_REC, IDX_REC), IDX_REC)],
                                     idx_smem.at[pl.ds(pl.multiple_of(s * IDX_REC, IDX_REC), IDX_REC)],
                                     isem.at[s])

    def gather_start(s):
        def one(t, carry):
            for k in range(TOP_K):
                d = idx_smem[s * IDX_REC + k * MOE_TILE + t]
                pltpu.make_async_copy(y_hbm.at[pl.ds(d, 1)], gbuf.at[s, k, pl.ds(t, 1)], gsem.at[s]).start()
            return carry
        lax.fori_loop(0, MOE_TILE, one, 0, unroll=4)

    @pl.when(i == 0)
    def _():
        idx_copy(0, 0).start()
        idx_copy(0, 0).wait()
        gather_start(0)
        idx_copy(1, 1).start()

    @pl.when(i + 1 < n_tiles)
    def _():
        idx_copy(i + 1, 1 - slot).wait()
        gather_start(1 - slot)

        @pl.when(i + 2 < n_tiles)
        def _():
            idx_copy(i + 2, slot).start()

    for k in range(TOP_K):
        pltpu.make_async_copy(y_hbm.at[pl.ds(0, MOE_TILE)], gbuf.at[slot, k], gsem.at[slot]).wait()
    tg = tg_ref[...]
    moe = tg[:, 0:1] * gbuf[slot, 0]
    for k in range(1, TOP_K):
        moe = moe + tg[:, k:k + 1] * gbuf[slot, k]
    x2 = x1_ref[...] + moe
    gate = _sigmoid(_dot(_rms(x2, gple_ref[...]).astype(BF16), wpg_ref[...]))
    x3 = x2 + gate * _dot(pe_ref[...].astype(BF16), wp_ref[...])
    y = _rms(x3, gfin_ref[...])

    @pl.when(i < n_prompt_tiles)
    def _():
        yp_ref[...] = y

    @pl.when(i >= n_prompt_tiles)
    def _():
        ys_ref[...] = y


def _tail(x1_all, y_rows, idx_rec, tg_t, pe_all, wts, n_p):
    n_all = x1_all.shape[0]
    n_tiles = n_all // MOE_TILE
    n_prompt_tiles = n_p // MOE_TILE
    row = lambda i: (i, 0)
    const = lambda i: (0, 0)
    body = functools.partial(_tail_body, n_tiles=n_tiles, n_prompt_tiles=n_prompt_tiles)
    return pl.pallas_call(
        body,
        grid=(n_tiles,),
        in_specs=[pl.BlockSpec(memory_space=pl.ANY), pl.BlockSpec(memory_space=pl.ANY),
                  pl.BlockSpec((MOE_TILE, D_MODEL), row), pl.BlockSpec((MOE_TILE, TOP_K), row),
                  pl.BlockSpec((MOE_TILE, PLE_DIM), row), pl.BlockSpec((1, D_MODEL), const),
                  pl.BlockSpec((D_MODEL, D_MODEL), const), pl.BlockSpec((PLE_DIM, D_MODEL), const),
                  pl.BlockSpec((1, D_MODEL), const)],
        out_specs=[pl.BlockSpec((MOE_TILE, D_MODEL), lambda i: (jnp.minimum(i, n_prompt_tiles - 1), 0)),
                   pl.BlockSpec((MOE_TILE, D_MODEL), lambda i: (jnp.maximum(i - n_prompt_tiles, 0), 0))],
        out_shape=[jax.ShapeDtypeStruct((n_p, D_MODEL), F32),
                   jax.ShapeDtypeStruct((n_all - n_p, D_MODEL), F32)],
        scratch_shapes=[pltpu.SMEM((2 * IDX_REC,), jnp.int32),
                        pltpu.VMEM((2, TOP_K, MOE_TILE, D_MODEL), F32),
                        pltpu.SemaphoreType.DMA((2,)),
                        pltpu.SemaphoreType.DMA((2,))],
        compiler_params=_cparams(("arbitrary",)),
        name="tail",
    )(idx_rec, y_rows, x1_all, tg_t, pe_all, *wts)


RWKV_CHUNK = 128
RWKV_CHUNK_SAMPLE = 8
STATE_PASSES = 1
Y_PASSES = 1
TM_PROMPT = 512


def kernel(x_prompt, x_sample, p_prompt, p_sample, cache_kv_w128, cache_kv_w512, cache_kv_w2048, state_rwkv_shift, state_rwkv_wkv, norm_mix_g, w_in, rwkv_mu, rwkv_w0, rwkv_w2, rwkv_a0, rwkv_a2, rwkv_g2, rwkv_k_k, rwkv_k_a, rwkv_r_k, rwkv_ln_w, rwkv_ln_b, w_out_attn, w_out_rwkv, w_out, norm_ffn_g, router_w, router_b, moe_w1, moe_b1, moe_w2, moe_b2, norm_ple_g, w_ple, w_ple_gate, norm_final_g):
    bp, seq, _ = x_prompt.shape
    bs, t_s, _ = x_sample.shape
    n_p, n_s = bp * seq, bs * t_s
    n_all = n_p + n_s
    assert w_in.shape[0] == 1, "single layer"
    caches = (cache_kv_w128, cache_kv_w512, cache_kv_w2048)

    row = lambda a: a.reshape(1, -1)
    w_in_b = w_in[0].astype(BF16)
    zeros64 = jnp.zeros((LORA_W // 2, RWKV_W), F32)
    rwkv_wts = (row(rwkv_mu[0]), row(rwkv_w0[0]), jnp.concatenate([rwkv_w2[0], zeros64], axis=0),
                row(rwkv_a0[0]), jnp.concatenate([zeros64, rwkv_a2[0]], axis=0), rwkv_g2[0],
                row(rwkv_k_k[0]), row(rwkv_k_a[0]), row(rwkv_r_k[0]), row(rwkv_ln_w[0]), row(rwkv_ln_b[0]))
    mix_wts = (w_out_attn[0].astype(BF16), w_out_rwkv[0].astype(BF16), w_out[0].astype(BF16),
               row(norm_ffn_g[0]), router_w[0].T, router_b[0].reshape(N_EXPERTS, 1))
    tail_wts = (row(norm_ple_g[0]), w_ple_gate[0].astype(BF16), w_ple[0].astype(BF16), row(norm_final_g))
    g_mix = row(norm_mix_g[0])

    xp = x_prompt.reshape(n_p, D_MODEL)
    qd_p, kvd_p, kv_p, zr_p, gate_p = _in_proj_dilated(xp, g_mix, w_in_b, TM_PROMPT, bp, seq)
    att_p = [_attn_prompt(qd_p[gi], kvd_p[gi], gi) for gi in range(N_GROUPS)]
    orw_p, wkv_p = _rwkv(zr_p.reshape(bp, seq, RWKV_COLS), jnp.zeros((bp, 1, RWKV_COLS), F32),
                         jnp.zeros((bp, RWKV_HEADS, RWKV_HEAD, RWKV_HEAD), F32), rwkv_wts, RWKV_CHUNK, seq)
    bufs, counts = _mix_out(xp, [a[0] for a in att_p], [a[1] for a in att_p], orw_p.reshape(n_p, RWKV_W), gate_p,
                            mix_wts, jnp.zeros((N_EXPERTS, 1), F32), TM_PROMPT, n_all, 0, None)

    xs = x_sample.reshape(n_s, D_MODEL)
    q_s, kv_s, zr_s, gate_s = _in_proj(xs, g_mix, w_in_b, n_s)
    t_pad = 8
    pad_t = lambda a: jnp.pad(a.reshape(bs, t_s, -1), ((0, 0), (0, t_pad - t_s), (0, 0)))
    q_s3, kv_s3 = pad_t(q_s), pad_t(kv_s)
    att_s, new_caches = [], []
    for gi in range(N_GROUPS):
        wc = caches[gi].shape[2]
        o, lse, newc = _attn_sample(q_s3, kv_s3, caches[gi].reshape(bs, wc, 2 * GROUP_W), gi, t_s)
        att_s.append((o[:, :t_s].reshape(n_s, GROUP_W), lse[:, :t_s].reshape(n_s, GROUP_W)))
        new_caches.append(newc.reshape(1, bs, wc, 2, HEADS_PER_GROUP, HEAD_DIM))
    zr_s3 = jnp.pad(zr_s.reshape(bs, t_s, RWKV_COLS), ((0, 0), (0, RWKV_CHUNK_SAMPLE - t_s), (0, 0)))
    orw_s, wkv_s = _rwkv(zr_s3, state_rwkv_shift[0].reshape(bs, 1, RWKV_COLS), state_rwkv_wkv[0], rwkv_wts,
                         RWKV_CHUNK_SAMPLE, t_s)
    bufs, counts = _mix_out(xs, [a[0] for a in att_s], [a[1] for a in att_s], orw_s[:, :t_s].reshape(n_s, RWKV_W),
                            gate_s, mix_wts, counts, n_s, n_all, n_p, bufs)
    x1_all, h2_all, top_e, top_g, rank = bufs

    idx_rec, tables, block_e, n_used, n_rows = _route(top_e, rank, counts)
    x_rows = _moe_dispatch(h2_all, idx_rec, tables, n_rows)
    y_rows = _moe_experts(x_rows, block_e, n_used, moe_w1[0], moe_b1[0].reshape(N_EXPERTS, 1, 2 * D_MODEL),
                          moe_w2[0], moe_b2[0].reshape(N_EXPERTS, 1, D_MODEL))
    pe_all = jnp.concatenate([p_prompt[0].reshape(n_p, PLE_DIM), p_sample[0].reshape(n_s, PLE_DIM)], axis=0)
    y_p, y_s = _tail(x1_all, y_rows, idx_rec, top_g.T, pe_all, tail_wts, n_p)

    kv_p4 = kv_p.reshape(bp, seq, N_GROUPS, 2, HEADS_PER_GROUP, HEAD_DIM)
    kv_out_p = [kv_p4[:, seq - min(w, seq):, gi][None] for gi, (w, _) in enumerate(ATTN_GROUPS)]
    shift_p = zr_p.reshape(bp, seq, RWKV_COLS)[:, -1][None]
    shift_s = zr_s.reshape(bs, t_s, RWKV_COLS)[:, -1][None]
    return (y_p.reshape(bp, seq, D_MODEL), y_s.reshape(bs, t_s, D_MODEL),
            kv_out_p[0], kv_out_p[1], kv_out_p[2], shift_p, wkv_p[None],
            new_caches[0], new_caches[1], new_caches[2], shift_s, wkv_s[None])
```

```python
import functools
import math

import numpy as np
import jax
import jax.numpy as jnp
from jax import lax
from jax.experimental import pallas as pl
from jax.experimental.pallas import tpu as pltpu

F32 = jnp.float32
BF16 = jnp.bfloat16

LANE = 128
D_MODEL = 1024
N_GROUPS = 3
HEADS_PER_GROUP = 4
HEAD_DIM = 64
ATTN_GROUPS = ((128, 1), (512, 4), (2048, 16))
GROUP_W = HEADS_PER_GROUP * HEAD_DIM
Q_COLS = N_GROUPS * GROUP_W
KV_COLS = 2 * Q_COLS
BAND = 128

RWKV_HEADS = 8
RWKV_HEAD = 64
RWKV_W = 512
LORA_W = 128
GATE_LORA = 128
RWKV_COLS = 3 * RWKV_W + LORA_W + GATE_LORA
GN_EPS = 64e-5
GATE_COLS = 2 * D_MODEL
Z_RWKV0 = 3 * Q_COLS
Z_GATE0 = Z_RWKV0 + RWKV_COLS
IN_COLS = Z_GATE0 + GATE_COLS

N_EXPERTS = 32
TOP_K = 4
SWIGLU_LIMIT = 7.0
SWIGLU_ALPHA = 1.702
MOE_BLOCK = 512
MOE_TILE = 128
IDX_REC = 1024
PLE_DIM = 256
RMS_EPS = 1e-6

NEG_BIG = -1e30
VMEM_LIMIT = 56 * 1024 * 1024


def _cparams(sem):
    return pltpu.CompilerParams(dimension_semantics=sem, vmem_limit_bytes=VMEM_LIMIT)


def _rms(x, g):
    return x * lax.rsqrt(jnp.mean(x * x, axis=-1, keepdims=True) + RMS_EPS) * g


def _sigmoid(x):
    return 1.0 / (1.0 + jnp.exp(-x))


def _dot(a, b):
    return jnp.dot(a, b, preferred_element_type=F32)


def _dot_nt(a, b):
    return lax.dot_general(a, b, (((1,), (1,)), ((), ())), preferred_element_type=F32)


def _dot_tn(a, b):
    return lax.dot_general(a, b, (((0,), (0,)), ((), ())), preferred_element_type=F32)


def _split(x):
    hi = x.astype(BF16)
    lo = (x - hi.astype(F32)).astype(BF16)
    return hi, lo


def _mm3(dot, a, b):
    ah, al = _split(a)
    bh, bl = _split(b)
    return dot(ah, bh) + (dot(ah, bl) + dot(al, bh))


def _parts(x, n):
    out = []
    for _ in range(n - 1):
        hi = x.astype(BF16)
        out.append(hi)
        x = x - hi.astype(F32)
    out.append(x.astype(BF16))
    return out


def _mmp(dot, ap, bp):
    order = max(len(ap), len(bp))
    acc = None
    for i, a in enumerate(ap):
        for j, b in enumerate(bp):
            if i + j < order:
                t = dot(a, b)
                acc = t if acc is None else acc + t
    return acc


def _in_proj_body(x_ref, g_ref, w_ref, q_ref, kv_ref, zr_ref, gate_ref):
    h = _rms(x_ref[...], g_ref[...]).astype(BF16)

    def proj(lo, width):
        return _dot(h, w_ref[:, lo:lo + width])

    q_ref[...] = (proj(0, Q_COLS) * (1.0 / math.sqrt(HEAD_DIM))).astype(BF16)
    for g in range(N_GROUPS):
        kv_ref[:, 2 * g * GROUP_W:(2 * g + 1) * GROUP_W] = proj(Q_COLS + g * GROUP_W, GROUP_W)
        kv_ref[:, (2 * g + 1) * GROUP_W:(2 * g + 2) * GROUP_W] = proj(2 * Q_COLS + g * GROUP_W, GROUP_W)
    zr_ref[...] = proj(Z_RWKV0, RWKV_COLS)
    gate_ref[...] = _sigmoid(proj(Z_GATE0, GATE_COLS)).astype(BF16)


def _in_proj_dilated_body(x_ref, g_ref, w_ref, *rest):
    qd_refs, kvd_refs = rest[0:N_GROUPS], rest[N_GROUPS:2 * N_GROUPS]
    kv_ref, zr_ref, gate_ref, st_ref = rest[2 * N_GROUPS:]
    tm = x_ref.shape[0]
    h = _rms(x_ref[...], g_ref[...]).astype(BF16)

    def proj(lo, width):
        return _dot(h, w_ref[:, lo:lo + width])

    n_q = Q_COLS // LANE

    def stage(slab0, val):
        for s in range(val.shape[1] // LANE):
            st_ref[slab0 + s] = val[:, s * LANE:(s + 1) * LANE]

    stage(0, proj(0, Q_COLS) * (1.0 / math.sqrt(HEAD_DIM)))
    for g in range(N_GROUPS):
        for part, src in ((0, Q_COLS), (1, 2 * Q_COLS)):
            col = (2 * g + part) * GROUP_W
            val = proj(src + g * GROUP_W, GROUP_W)
            kv_ref[:, col:col + GROUP_W] = val
            stage(n_q + col // LANE, val)
    for g, (_, dil) in enumerate(ATTN_GROUPS):
        for r in range(dil):
            rows = pl.ds(r, tm // dil, stride=dil) if dil > 1 else slice(None)
            for s in range(GROUP_W // LANE):
                qd_refs[g][0, r, :, s * LANE:(s + 1) * LANE] = st_ref[g * GROUP_W // LANE + s, rows, :].astype(BF16)
            for s in range(2 * GROUP_W // LANE):
                kvd_refs[g][0, r, :, s * LANE:(s + 1) * LANE] = \
                    st_ref[n_q + 2 * g * GROUP_W // LANE + s, rows, :].astype(BF16)
    zr_ref[...] = proj(Z_RWKV0, RWKV_COLS)
    gate_ref[...] = _sigmoid(proj(Z_GATE0, GATE_COLS)).astype(BF16)


def _in_proj_dilated(x, g, w_bf16, tm, batch, seq):
    n = x.shape[0]
    tiles = seq // tm
    row = lambda i: (i, 0)
    const = lambda i: (0, 0)
    sub = lambda i: (i // tiles, 0, i % tiles, 0)
    dils = [d for _, d in ATTN_GROUPS]
    assert all(tm % (16 * d) == 0 for d in dils)
    outs = pl.pallas_call(
        _in_proj_dilated_body,
        grid=(n // tm,),
        in_specs=[pl.BlockSpec((tm, D_MODEL), row),
                  pl.BlockSpec((1, D_MODEL), const),
                  pl.BlockSpec((D_MODEL, IN_COLS), const, pipeline_mode=pl.Buffered(1))],
        out_specs=[pl.BlockSpec((1, d, tm // d, GROUP_W), sub) for d in dils]
                  + [pl.BlockSpec((1, d, tm // d, 2 * GROUP_W), sub) for d in dils]
                  + [pl.BlockSpec((tm, KV_COLS), row),
                     pl.BlockSpec((tm, RWKV_COLS), row),
                     pl.BlockSpec((tm, GATE_COLS), row)],
        out_shape=[jax.ShapeDtypeStruct((batch, d, seq // d, GROUP_W), BF16) for d in dils]
                  + [jax.ShapeDtypeStruct((batch, d, seq // d, 2 * GROUP_W), BF16) for d in dils]
                  + [jax.ShapeDtypeStruct((n, KV_COLS), F32),
                     jax.ShapeDtypeStruct((n, RWKV_COLS), F32),
                     jax.ShapeDtypeStruct((n, GATE_COLS), BF16)],
        scratch_shapes=[pltpu.VMEM(((Q_COLS + KV_COLS) // LANE, tm, LANE), F32)],
        compiler_params=_cparams(("parallel",)),
        name="in_proj_dilated",
    )(x, g, w_bf16)
    return outs[0:N_GROUPS], outs[N_GROUPS:2 * N_GROUPS], outs[2 * N_GROUPS], outs[2 * N_GROUPS + 1], outs[2 * N_GROUPS + 2]


def _in_proj(x, g, w_bf16, tm):
    n = x.shape[0]
    row = lambda i: (i, 0)
    const = lambda i: (0, 0)
    return pl.pallas_call(
        _in_proj_body,
        grid=(n // tm,),
        in_specs=[pl.BlockSpec((tm, D_MODEL), row),
                  pl.BlockSpec((1, D_MODEL), const),
                  pl.BlockSpec((D_MODEL, IN_COLS), const)],
        out_specs=[pl.BlockSpec((tm, Q_COLS), row),
                   pl.BlockSpec((tm, KV_COLS), row),
                   pl.BlockSpec((tm, RWKV_COLS), row),
                   pl.BlockSpec((tm, GATE_COLS), row)],
        out_shape=[jax.ShapeDtypeStruct((n, Q_COLS), BF16),
                   jax.ShapeDtypeStruct((n, KV_COLS), F32),
                   jax.ShapeDtypeStruct((n, RWKV_COLS), F32),
                   jax.ShapeDtypeStruct((n, GATE_COLS), BF16)],
        compiler_params=_cparams(("parallel",)),
        name="in_proj",
    )(x, g, w_bf16)


def _alibi_slopes(gi):
    return [2.0 ** (-8.0 * (gi * HEADS_PER_GROUP + h + 1) / (N_GROUPS * HEADS_PER_GROUP))
            for h in range(HEADS_PER_GROUP)]


def _head_of_lane(shape):
    return lax.broadcasted_iota(jnp.int32, shape, len(shape) - 1) // HEAD_DIM


def _stack_heads(q):
    hl = _head_of_lane(q.shape)
    return jnp.concatenate([jnp.where(hl == h, q, jnp.zeros_like(q)) for h in range(HEADS_PER_GROUP)], axis=0)


def _unstack_heads(x4, rows):
    hl = _head_of_lane((rows, GROUP_W))
    out = x4[0:rows]
    for h in range(1, HEADS_PER_GROUP):
        out = jnp.where(hl == h, x4[h * rows:(h + 1) * rows], out)
    return out


def _attn_prompt_body(q_ref, kvp_ref, kvc_ref, o_ref, lse_ref, kb_ref, bias_ref, *, dil, slopes, n_sub):
    c = pl.program_id(2)
    kb_ref[0:BAND, :] = kvp_ref[0, 0]
    kb_ref[BAND:, :] = kvc_ref[0, 0]

    qi = lax.broadcasted_iota(jnp.int32, (BAND, 2 * BAND), 0) + BAND
    ki = lax.broadcasted_iota(jnp.int32, (BAND, 2 * BAND), 1)
    rel = qi - ki
    dist = jnp.where((rel >= 0) & (rel <= BAND), (dil * rel).astype(F32), -NEG_BIG / slopes[-1])
    for h in range(HEADS_PER_GROUP):
        bias_ref[h * BAND:(h + 1) * BAND, :] = -slopes[h] * dist

    def sub_block(n, carry):
        r0 = pl.multiple_of(n * BAND, BAND)
        q4 = _stack_heads(q_ref[0, 0, pl.ds(r0, BAND), :])
        kv = kb_ref[pl.ds(r0, 2 * BAND), :]
        s = _dot_nt(q4, kv[:, :GROUP_W]) + bias_ref[...]
        n_before_start = jnp.where((c == 0) & (n == 0), BAND, 0)
        kcol = lax.broadcasted_iota(jnp.int32, s.shape, 1)
        s = jnp.where(kcol < n_before_start, NEG_BIG, s)
        m = jnp.max(s, axis=-1, keepdims=True)
        e = jnp.exp(s - m)
        den = jnp.sum(e, axis=-1, keepdims=True)
        o4 = _dot(e.astype(BF16), kv[:, GROUP_W:]) * (1.0 / den)
        o_ref[0, 0, pl.ds(r0, BAND), :] = _unstack_heads(o4, BAND).astype(o_ref.dtype)
        lse4 = jnp.broadcast_to(m + jnp.log(den), (HEADS_PER_GROUP * BAND, GROUP_W))
        lse_ref[0, 0, pl.ds(r0, BAND), :] = _unstack_heads(lse4, BAND)
        return carry

    lax.fori_loop(0, n_sub, sub_block, 0)


def _attn_prompt(qd, kvd, gi):
    window, dil = ATTN_GROUPS[gi]
    assert window // dil == BAND
    batch, _, sub_len, _ = qd.shape
    chunk = min(sub_len, 1024)
    n_chunks = sub_len // chunk
    sub_per_chunk = chunk // BAND
    body = functools.partial(_attn_prompt_body, dil=dil, slopes=_alibi_slopes(gi), n_sub=sub_per_chunk)
    cur = lambda b, r, c: (b, r, c, 0)
    return pl.pallas_call(
        body,
        grid=(batch, dil, n_chunks),
        in_specs=[
            pl.BlockSpec((1, 1, chunk, GROUP_W), cur),
            pl.BlockSpec((1, 1, BAND, 2 * GROUP_W),
                         lambda b, r, c: (b, r, jnp.maximum(c * sub_per_chunk - 1, 0), 0)),
            pl.BlockSpec((1, 1, chunk, 2 * GROUP_W), cur),
        ],
        out_specs=[pl.BlockSpec((1, 1, chunk, GROUP_W), cur), pl.BlockSpec((1, 1, chunk, GROUP_W), cur)],
        out_shape=[jax.ShapeDtypeStruct(qd.shape, BF16), jax.ShapeDtypeStruct(qd.shape, F32)],
        scratch_shapes=[pltpu.VMEM((chunk + BAND, 2 * GROUP_W), BF16),
                        pltpu.VMEM((HEADS_PER_GROUP * BAND, 2 * BAND), F32)],
        compiler_params=_cparams(("parallel", "parallel", "arbitrary")),
        name=f"attn_prompt_g{gi}",
    )(qd, kvd, kvd)


def _attn_sample_body(q_ref, kvn_ref, cache_ref, o_ref, lse_ref, newc_ref, *, dil, slopes, wc, t_new):
    tp = q_ref.shape[1]
    cache = cache_ref[0]
    new = kvn_ref[0]
    newc_ref[0, 0:wc - t_new, :] = cache[t_new:wc]
    newc_ref[0, wc - t_new:wc, :] = new[0:t_new]

    q4 = _stack_heads(q_ref[0])
    cb = cache.astype(BF16)
    nb = new.astype(BF16)
    t_row = lax.broadcasted_iota(jnp.int32, (tp, 1), 0)
    slope_col = jnp.concatenate([jnp.full((tp, 1), s, F32) for s in slopes], axis=0)
    t4 = jnp.concatenate([t_row] * HEADS_PER_GROUP, axis=0)

    def scores(keys_bf16, dist):
        ok = (dist >= 0) & ((dist & (dil - 1)) == 0) & (dist <= BAND * dil)
        s = _dot_nt(q4, keys_bf16[:, :GROUP_W])
        return jnp.where(ok, s - slope_col * dist.astype(F32), NEG_BIG)

    pc = lax.broadcasted_iota(jnp.int32, (HEADS_PER_GROUP * tp, wc), 1)
    pn = lax.broadcasted_iota(jnp.int32, (HEADS_PER_GROUP * tp, tp), 1)
    s_c = scores(cb, wc + t4 - pc)
    s_n = scores(nb, jnp.where(pn < t_new, t4 - pn, -1))
    m = jnp.maximum(jnp.max(s_c, axis=-1, keepdims=True), jnp.max(s_n, axis=-1, keepdims=True))
    e_c = jnp.exp(s_c - m)
    e_n = jnp.exp(s_n - m)
    den = jnp.sum(e_c, axis=-1, keepdims=True) + jnp.sum(e_n, axis=-1, keepdims=True)
    o4 = (_dot(e_c.astype(BF16), cb[:, GROUP_W:]) + _dot(e_n.astype(BF16), nb[:, GROUP_W:])) * (1.0 / den)
    o_ref[0] = _unstack_heads(o4, tp).astype(o_ref.dtype)
    lse4 = jnp.broadcast_to(m + jnp.log(den), (HEADS_PER_GROUP * tp, GROUP_W))
    lse_ref[0] = _unstack_heads(lse4, tp)


def _attn_sample(q, kv, cache, gi, t_new):
    window, dil = ATTN_GROUPS[gi]
    batch, tp, _ = q.shape
    wc = cache.shape[1]
    assert wc + 0 - dil * BAND >= 0
    body = functools.partial(_attn_sample_body, dil=dil, slopes=_alibi_slopes(gi), wc=wc, t_new=t_new)
    return pl.pallas_call(
        body,
        grid=(batch,),
        in_specs=[pl.BlockSpec((1, tp, GROUP_W), lambda b: (b, 0, gi)),
                  pl.BlockSpec((1, tp, 2 * GROUP_W), lambda b: (b, 0, gi)),
                  pl.BlockSpec((1, wc, 2 * GROUP_W), lambda b: (b, 0, 0))],
        out_specs=[pl.BlockSpec((1, tp, GROUP_W), lambda b: (b, 0, 0)),
                   pl.BlockSpec((1, tp, GROUP_W), lambda b: (b, 0, 0)),
                   pl.BlockSpec((1, wc, 2 * GROUP_W), lambda b: (b, 0, 0))],
        out_shape=[jax.ShapeDtypeStruct((batch, tp, GROUP_W), BF16),
                   jax.ShapeDtypeStruct((batch, tp, GROUP_W), F32),
                   jax.ShapeDtypeStruct((batch, wc, 2 * GROUP_W), F32)],
        compiler_params=_cparams(("parallel",)),
        name=f"attn_sample_g{gi}",
    )(q, kv, cache)


def _rwkv_body(zr_ref, shift0_ref, wkv0_ref, mu_ref, w0_ref, w2_ref, a0_ref, a2_ref, g2_ref, kk_ref, ka_ref,
               rk_ref, lnw_ref, lnb_ref, o_ref, wkv_ref, prev_ref, *, chunk, t_valid):
    c = pl.program_id(1)
    L = chunk
    mm = functools.partial(_mm3, _dot)
    heads = range(RWKV_HEADS)
    sls = [slice(h * RWKV_HEAD, (h + 1) * RWKV_HEAD) for h in heads]

    @pl.when(c == 0)
    def _():
        prev_ref[...] = shift0_ref[0]
        wkv_ref[0] = wkv0_ref[0]

    zr = zr_ref[0]
    row = lax.broadcasted_iota(jnp.int32, (L, 1), 0)
    prev = jnp.where(row == 0, prev_ref[...], pltpu.roll(zr, 1, 0))
    prev_ref[...] = zr[L - 1:L, :]
    zm = zr + (prev - zr) * mu_ref[...]

    r = zm[:, 0:RWKV_W]
    k = zm[:, RWKV_W:2 * RWKV_W]
    v = zm[:, 2 * RWKV_W:3 * RWKV_W]
    xwa = zm[:, 3 * RWKV_W:3 * RWKV_W + LORA_W]
    xg = zm[:, 3 * RWKV_W + LORA_W:]
    w_pre = w0_ref[...] + mm(jnp.tanh(xwa), w2_ref[...])
    softplus = jnp.maximum(-w_pre, 0.0) + jnp.log(1.0 + jnp.exp(-jnp.abs(w_pre)))
    w_log = -softplus - 0.5
    lw = -jnp.exp(w_log)
    a = _sigmoid(a0_ref[...] + mm(xwa, a2_ref[...]))
    g = mm(_sigmoid(xg), g2_ref[...])
    kk_raw = k * kk_ref[...]
    k2 = k * (1.0 + (a - 1.0) * ka_ref[...])
    if t_valid < L:
        live = row < t_valid
        lw = jnp.where(live, lw, 0.0)
        kk_raw = jnp.where(live, kk_raw, 0.0)
        k2 = jnp.where(live, k2, 0.0)
        v = jnp.where(live, v, 0.0)

    ti = lax.broadcasted_iota(jnp.int32, (L, L), 0)
    si = lax.broadcasted_iota(jnp.int32, (L, L), 1)
    incl = ti >= si
    strict = ti > si
    clw = _mmp(_dot, [jnp.where(incl, 1.0, 0.0).astype(BF16)], _parts(lw, 3))
    mid = max(L // 2 - 1, 0)
    clw_mid = clw[mid:mid + 1, :]
    rel = clw - clw_mid
    p_incl = jnp.exp(rel)
    p_inv = jnp.exp(-rel)
    p_prev = jnp.exp(rel - lw)
    p_mid = jnp.exp(clw_mid)
    rt_all = r * p_incl
    kt_all = k2 * p_inv
    rk_all = r * k2 * rk_ref[...]

    sp, yp = STATE_PASSES, Y_PASSES
    kkh = [kk_raw[:, s] for s in sls]
    kkn = [x / jnp.maximum(jnp.sqrt(jnp.sum(x * x, axis=-1, keepdims=True)), 1e-12) for x in kkh]
    at = [-kkn[h] * p_prev[:, sls[h]] for h in heads]
    bt = [kkn[h] * a[:, sls[h]] * p_inv[:, sls[h]] for h in heads]
    rt = [rt_all[:, s] for s in sls]
    kt = [kt_all[:, s] for s in sls]
    vh = [v[:, s] for s in sls]
    at_s = [_parts(x, sp) for x in at]
    bt_s = [_parts(x, sp) for x in bt]
    kt_s = [_parts(x, sp) for x in kt]
    vh_s = [_parts(x, sp) for x in vh]
    rt_y = [_parts(x, yp) for x in rt]
    a_ab = [jnp.where(strict, _mmp(_dot_nt, at_s[h], bt_s[h]), 0.0) for h in heads]
    a_ak = [jnp.where(strict, _mmp(_dot_nt, at_s[h], kt_s[h]), 0.0) for h in heads]
    a_rb = [jnp.where(incl, _mmp(_dot_nt, rt_y[h], bt_s[h][:yp]), 0.0) for h in heads]
    a_rk = [jnp.where(incl, _mmp(_dot_nt, rt_y[h], kt_s[h][:yp]), 0.0) for h in heads]
    x = [jnp.concatenate([at[h], _mmp(_dot, _parts(a_ak[h], sp), vh_s[h])], axis=1) for h in heads]
    p = a_ab
    for level in range(max(int(math.log2(L)), 1)):
        if level > 0:
            p = [_mmp(_dot, ps, ps) for ps in p_s]
        p_s = [_parts(q, sp) for q in p]
        x = [x[h] + _mmp(_dot, p_s[h], _parts(x[h], sp)) for h in heads]
    x_s = [_parts(q, sp) for q in x]
    qy = [_mmp(_dot, _parts(a_rb[h], yp), x_s[h][:yp]) for h in heads]
    y0 = [qy[h][:, RWKV_HEAD:] + _mmp(_dot, _parts(a_rk[h], yp), vh_s[h][:yp]) for h in heads]
    qh = [rt[h] + qy[h][:, :RWKV_HEAD] for h in heads]
    s0 = [wkv_ref[0, h] * p_mid[:, sls[h]] for h in heads]
    s0_s = [_parts(q, sp) for q in s0]
    y = [_mmp(_dot_nt, _parts(qh[h], yp), s0_s[h][:yp]) + y0[h] for h in heads]
    wtb = [_mmp(_dot_tn, [q[:, :RWKV_HEAD] for q in x_s[h]], bt_s[h]) for h in heads]
    uv_s = [[jnp.concatenate([x_s[h][i][:, RWKV_HEAD:], vh_s[h][i]], axis=0) for i in range(sp)] for h in heads]
    bk_s = [[jnp.concatenate([bt_s[h][i], kt_s[h][i]], axis=0) for i in range(sp)] for h in heads]
    for h in heads:
        s_new = s0[h] + _mmp(_dot, s0_s[h], _parts(wtb[h], sp)) + _mmp(_dot_tn, uv_s[h], bk_s[h])
        wkv_ref[0, h] = s_new * p_incl[L - 1:L, sls[h]]
    for h in heads:
        mu_y = jnp.mean(y[h], axis=-1, keepdims=True)
        yc = y[h] - mu_y
        var = jnp.mean(yc * yc, axis=-1, keepdims=True)
        yn = yc * lax.rsqrt(var + GN_EPS) * lnw_ref[:, sls[h]] + lnb_ref[:, sls[h]]
        bonus = jnp.sum(rk_all[:, sls[h]], axis=-1, keepdims=True) * vh[h]
        o_ref[0, :, sls[h]] = ((yn + bonus) * g[:, sls[h]]).astype(o_ref.dtype)


def _rwkv(zr, shift0, wkv0, wts, chunk, t_valid):
    batch, t, _ = zr.shape
    n_chunks = t // chunk
    assert t_valid == t or n_chunks == 1
    body = functools.partial(_rwkv_body, chunk=chunk, t_valid=min(t_valid, chunk))
    vec = lambda width: pl.BlockSpec((1, width), lambda b, c: (0, 0))
    mat = lambda rows: pl.BlockSpec((rows, RWKV_W), lambda b, c: (0, 0))
    return pl.pallas_call(
        body,
        grid=(batch, n_chunks),
        in_specs=[pl.BlockSpec((1, chunk, RWKV_COLS), lambda b, c: (b, c, 0)),
                  pl.BlockSpec((1, 1, RWKV_COLS), lambda b, c: (b, 0, 0)),
                  pl.BlockSpec((1, RWKV_HEADS, RWKV_HEAD, RWKV_HEAD), lambda b, c: (b, 0, 0, 0)),
                  vec(RWKV_COLS), vec(RWKV_W), mat(LORA_W), vec(RWKV_W), mat(LORA_W), mat(GATE_LORA),
                  vec(RWKV_W), vec(RWKV_W), vec(RWKV_W), vec(RWKV_W), vec(RWKV_W)],
        out_specs=[pl.BlockSpec((1, chunk, RWKV_W), lambda b, c: (b, c, 0)),
                   pl.BlockSpec((1, RWKV_HEADS, RWKV_HEAD, RWKV_HEAD), lambda b, c: (b, 0, 0, 0))],
        out_shape=[jax.ShapeDtypeStruct((batch, t, RWKV_W), BF16),
                   jax.ShapeDtypeStruct((batch, RWKV_HEADS, RWKV_HEAD, RWKV_HEAD), F32)],
        scratch_shapes=[pltpu.VMEM((1, RWKV_COLS), F32)],
        compiler_params=_cparams(("parallel", "arbitrary")),
        name="rwkv_scan",
    )(zr, shift0, wkv0, *wts)


def _mix_out_body(x_ref, o0_ref, o1_ref, o2_ref, l0_ref, l1_ref, l2_ref, orw_ref, gate_ref, woa_ref, wor_ref,
                  wo_ref, gffn_ref, rwt_ref, rb_ref, cnt0_ref, *rest, dilated, n_alias):
    x1_ref, h2_ref, te_ref, tg_ref, rk_ref, cnt_ref = rest[n_alias:n_alias + 6]
    stage = list(rest[n_alias + 6:])

    def token_major(ref, gi):
        if not dilated:
            return ref[...].astype(F32)
        dil = ATTN_GROUPS[gi][1]
        if dil == 1:
            return ref[0, 0].astype(F32)
        st_ref = stage.pop()
        for r in range(dil):
            sub = ref[0, r].astype(F32)
            for s in range(GROUP_W // LANE):
                st_ref[s, pl.ds(r, ref.shape[2], stride=dil), :] = sub[:, s * LANE:(s + 1) * LANE]
        return jnp.concatenate([st_ref[s] for s in range(GROUP_W // LANE)], axis=1)

    l0, l1, l2 = token_major(l0_ref, 0), token_major(l1_ref, 1), token_major(l2_ref, 2)
    m = jnp.maximum(jnp.maximum(l0, l1), l2)
    e0, e1, e2 = jnp.exp(l0 - m), jnp.exp(l1 - m), jnp.exp(l2 - m)
    o_att = (e0 * token_major(o0_ref, 0) + e1 * token_major(o1_ref, 1) + e2 * token_major(o2_ref, 2)) \
        * (1.0 / (e0 + e1 + e2))
    gates = gate_ref[...].astype(F32)
    merged = gates[:, :D_MODEL] * _dot(o_att.astype(BF16), woa_ref[...]) \
        + gates[:, D_MODEL:] * _dot(orw_ref[...], wor_ref[...])
    x1 = x_ref[...] + _dot(merged.astype(BF16), wo_ref[...])
    x1_ref[...] = x1
    h2 = _rms(x1, gffn_ref[...])
    h2_ref[...] = h2

    logits = _mm3(_dot_nt, rwt_ref[...], h2) + rb_ref[...]
    e_iota = lax.broadcasted_iota(jnp.int32, logits.shape, 0)
    vals, idxs = [], []
    for _ in range(TOP_K):
        top = jnp.max(logits, axis=0, keepdims=True)
        idx = jnp.min(jnp.where(logits == top, e_iota, N_EXPERTS), axis=0, keepdims=True)
        vals.append(top)
        idxs.append(idx)
        logits = jnp.where(e_iota == idx, -jnp.inf, logits)
    exps = [jnp.exp(t - vals[0]) for t in vals]
    inv = 1.0 / (exps[0] + exps[1] + exps[2] + exps[3])
    te_ref[...] = jnp.concatenate(idxs, axis=0)
    tg_ref[...] = jnp.concatenate([e * inv for e in exps], axis=0)

    @pl.when(pl.program_id(0) == 0)
    def _():
        cnt_ref[...] = cnt0_ref[...]

    hits = [e_iota == idx for idx in idxs]
    onehot = jnp.where(hits[0] | hits[1] | hits[2] | hits[3], 1.0, 0.0)
    tm = onehot.shape[1]
    earlier = lax.broadcasted_iota(jnp.int32, (tm, tm), 0) < lax.broadcasted_iota(jnp.int32, (tm, tm), 1)
    before = cnt_ref[...] + _dot(onehot.astype(BF16), jnp.where(earlier, 1.0, 0.0).astype(BF16))
    rk_ref[...] = jnp.concatenate([jnp.sum(jnp.where(h, before, 0.0), axis=0, keepdims=True) for h in hits],
                                  axis=0).astype(jnp.int32)
    cnt_ref[...] += jnp.sum(onehot, axis=1, keepdims=True)


def _mix_out(x, o_g, lse_g, o_rwkv, gates, wts, cnt0, tm, n_all, row0, bufs):
    n = x.shape[0]
    blk0 = row0 // tm
    row = lambda i: (i, 0)
    const = lambda i: (0, 0)
    out_row = lambda i: (i + blk0, 0)
    out_col = lambda i: (0, i + blk0)
    tok = lambda w: pl.BlockSpec((tm, w), row)
    dilated = o_g[0].ndim == 4
    if dilated:
        tiles = o_g[0].shape[1] * o_g[0].shape[2] // tm
        sub = lambda i: (i // tiles, 0, i % tiles, 0)
        att_specs = [pl.BlockSpec((1, d, tm // d, GROUP_W), sub) for _, d in ATTN_GROUPS] * 2
        stage = [pltpu.VMEM((GROUP_W // LANE, tm, LANE), F32) for _, d in ATTN_GROUPS if d > 1] * 2
    else:
        att_specs, stage = [tok(GROUP_W)] * 6, []
    in_specs = [tok(D_MODEL)] + att_specs + [tok(RWKV_W), tok(GATE_COLS),
                pl.BlockSpec((GROUP_W, D_MODEL), const), pl.BlockSpec((RWKV_W, D_MODEL), const),
                pl.BlockSpec((D_MODEL, D_MODEL), const), pl.BlockSpec((1, D_MODEL), const),
                pl.BlockSpec((N_EXPERTS, D_MODEL), const), pl.BlockSpec((N_EXPERTS, 1), const),
                pl.BlockSpec((N_EXPERTS, 1), const)]
    args = [x, *o_g, *lse_g, o_rwkv, gates, *wts, cnt0]
    aliases = {}
    if bufs is not None:
        in_specs += [pl.BlockSpec(memory_space=pl.ANY)] * 5
        aliases = {len(args) + j: j for j in range(5)}
        args += list(bufs)
    *new_bufs, cnt = pl.pallas_call(
        functools.partial(_mix_out_body, dilated=dilated, n_alias=len(aliases)),
        grid=(n // tm,),
        scratch_shapes=stage,
        in_specs=in_specs,
        out_specs=[pl.BlockSpec((tm, D_MODEL), out_row), pl.BlockSpec((tm, D_MODEL), out_row),
                   pl.BlockSpec((TOP_K, tm), out_col), pl.BlockSpec((TOP_K, tm), out_col),
                   pl.BlockSpec((TOP_K, tm), out_col), pl.BlockSpec((N_EXPERTS, 1), const)],
        out_shape=[jax.ShapeDtypeStruct((n_all, D_MODEL), F32), jax.ShapeDtypeStruct((n_all, D_MODEL), F32),
                   jax.ShapeDtypeStruct((TOP_K, n_all), jnp.int32), jax.ShapeDtypeStruct((TOP_K, n_all), F32),
                   jax.ShapeDtypeStruct((TOP_K, n_all), jnp.int32), jax.ShapeDtypeStruct((N_EXPERTS, 1), F32)],
        input_output_aliases=aliases,
        compiler_params=_cparams(("arbitrary",)),
        name="mix_out",
    )(*args)
    return new_bufs, cnt


def _moe_dispatch_body(tab_ref, idx_hbm, h2_hbm, x_hbm, idx_smem, hbuf, zrow, isem, fsem, dsem, zsem, *, n_tiles):
    i = pl.program_id(0)
    slot = i % 2
    hslot = i % 3

    def tile_fetch(tile, hs):
        return pltpu.make_async_copy(h2_hbm.at[pl.ds(pl.multiple_of(tile * MOE_TILE, MOE_TILE), MOE_TILE)],
                                     hbuf.at[hs], fsem.at[hs])

    def idx_copy(rec, s):
        return pltpu.make_async_copy(idx_hbm.at[pl.ds(pl.multiple_of(rec * IDX_REC, IDX_REC), IDX_REC)],
                                     idx_smem.at[pl.ds(pl.multiple_of(s * IDX_REC, IDX_REC), IDX_REC)],
                                     isem.at[s])

    def row_copies_start(s):
        def one(t, carry):
            for k in range(TOP_K):
                d = idx_smem[s * IDX_REC + k * MOE_TILE + t]
                pltpu.make_async_copy(hbuf.at[hslot, pl.ds(t, 1)], x_hbm.at[pl.ds(d, 1)], dsem.at[s]).start()
            return carry
        for t in range(MOE_TILE):
            one(t, 0)

    @pl.when(i == 0)
    def _():
        idx_copy(0, 0).start()
        tile_fetch(0, 0).start()

    @pl.when(i + 1 < n_tiles)
    def _():
        idx_copy(i + 1, 1 - slot).start()
        tile_fetch(i + 1, (i + 1) % 3).start()

    idx_copy(i, slot).wait()
    tile_fetch(i, hslot).wait()
    row_copies_start(slot)

    def wait_rows(s):
        for _ in range(TOP_K):
            pltpu.make_async_copy(hbuf.at[0], x_hbm.at[pl.ds(0, MOE_TILE)], dsem.at[s]).wait()

    @pl.when(i >= 1)
    def _():
        wait_rows(1 - slot)

    @pl.when(i == n_tiles - 1)
    def _():
        wait_rows(slot)
        zrow[...] = jnp.zeros_like(zrow)

        def zero_copy(dst_row):
            return pltpu.make_async_copy(zrow.at[pl.ds(0, 1)], x_hbm.at[pl.ds(dst_row, 1)], zsem)

        def per_expert(e, carry):
            first = tab_ref[e] + tab_ref[N_EXPERTS + e]
            last = tab_ref[e] + tab_ref[2 * N_EXPERTS + e]
            lax.fori_loop(first, last, lambda r, c: (zero_copy(r).start(), c)[1], 0)
            lax.fori_loop(first, last, lambda r, c: (zero_copy(r).wait(), c)[1], 0)
            return carry
        lax.fori_loop(0, N_EXPERTS, per_expert, 0)


def _moe_dispatch(h2, idx_rec, tables, n_rows):
    n_tiles = h2.shape[0] // MOE_TILE
    grid_spec = pltpu.PrefetchScalarGridSpec(
        num_scalar_prefetch=1,
        grid=(n_tiles,),
        in_specs=[pl.BlockSpec(memory_space=pl.ANY),
                  pl.BlockSpec(memory_space=pl.ANY)],
        out_specs=pl.BlockSpec(memory_space=pl.ANY),
        scratch_shapes=[pltpu.SMEM((2 * IDX_REC,), jnp.int32),
                        pltpu.VMEM((3, MOE_TILE, D_MODEL), F32),
                        pltpu.VMEM((8, D_MODEL), F32),
                        pltpu.SemaphoreType.DMA((2,)),
                        pltpu.SemaphoreType.DMA((3,)),
                        pltpu.SemaphoreType.DMA((2,)),
                        pltpu.SemaphoreType.DMA],
    )
    return pl.pallas_call(
        functools.partial(_moe_dispatch_body, n_tiles=n_tiles),
        grid_spec=grid_spec,
        out_shape=jax.ShapeDtypeStruct((n_rows, D_MODEL), F32),
        compiler_params=_cparams(("arbitrary",)),
        name="moe_dispatch",
    )(tables, idx_rec, h2)


def _moe_experts_body(be_ref, nused_ref, x_ref, w1_ref, b1_ref, w2_ref, b2_ref, y_ref, w1b, w2b):
    i = pl.program_id(0)

    @pl.when(i < nused_ref[0])
    def _():
        @pl.when((i == 0) | (be_ref[i] != be_ref[jnp.maximum(i - 1, 0)]))
        def _():
            w1b[...] = w1_ref[0].astype(BF16)
            w2b[...] = w2_ref[0].astype(BF16)

        u = _dot(x_ref[...].astype(BF16), w1b[...]) + b1_ref[0]
        glu = jnp.minimum(u[:, :D_MODEL], SWIGLU_LIMIT)
        lin = jnp.clip(u[:, D_MODEL:], -SWIGLU_LIMIT, SWIGLU_LIMIT)
        act = glu * _sigmoid(SWIGLU_ALPHA * glu) * (lin + 1.0)
        y_ref[...] = _dot(act.astype(BF16), w2b[...]) + b2_ref[0]

    @pl.when(i >= nused_ref[0])
    def _():
        y_ref[...] = jnp.zeros_like(y_ref)


def _moe_experts(x_rows, block_e, n_used, w1, b1, w2, b2):
    n_blocks = block_e.shape[0]
    by_expert = lambda i, be, nu: (be[i], 0, 0)
    grid_spec = pltpu.PrefetchScalarGridSpec(
        num_scalar_prefetch=2,
        grid=(n_blocks,),
        in_specs=[pl.BlockSpec((MOE_BLOCK, D_MODEL), lambda i, be, nu: (jnp.minimum(i, nu[0] - 1), 0)),
                  pl.BlockSpec((1, D_MODEL, 2 * D_MODEL), by_expert),
                  pl.BlockSpec((1, 1, 2 * D_MODEL), by_expert),
                  pl.BlockSpec((1, D_MODEL, D_MODEL), by_expert),
                  pl.BlockSpec((1, 1, D_MODEL), by_expert)],
        out_specs=pl.BlockSpec((MOE_BLOCK, D_MODEL), lambda i, be, nu: (i, 0)),
        scratch_shapes=[pltpu.VMEM((D_MODEL, 2 * D_MODEL), BF16),
                        pltpu.VMEM((D_MODEL, D_MODEL), BF16)],
    )
    return pl.pallas_call(
        _moe_experts_body,
        grid_spec=grid_spec,
        out_shape=jax.ShapeDtypeStruct((n_blocks * MOE_BLOCK, D_MODEL), F32),
        compiler_params=_cparams(("arbitrary",)),
        name="moe_experts",
    )(block_e, n_used, x_rows, w1, b1, w2, b2)


def _route(top_e, rank, counts):
    n_tok = top_e.shape[1]
    counts = counts.reshape(N_EXPERTS).astype(jnp.int32)
    padded = (counts + MOE_BLOCK - 1) // MOE_BLOCK * MOE_BLOCK
    pad_end = jnp.cumsum(padded)
    pad_start = pad_end - padded
    experts = jnp.arange(N_EXPERTS, dtype=jnp.int32)
    dest = rank + jnp.sum(jnp.where(top_e[..., None] == experts, pad_start, 0), axis=-1)
    n_blocks = -(-(n_tok * TOP_K + N_EXPERTS * (MOE_BLOCK - 1)) // MOE_BLOCK)
    blk_row0 = jnp.arange(n_blocks, dtype=jnp.int32) * MOE_BLOCK
    block_e = jnp.minimum(jnp.sum(blk_row0[:, None] >= pad_end[None, :], axis=1), N_EXPERTS - 1).astype(jnp.int32)
    n_used = (pad_end[-1] // MOE_BLOCK).astype(jnp.int32).reshape(1)
    n_tiles = n_tok // MOE_TILE
    rec = dest.reshape(TOP_K, n_tiles, MOE_TILE).transpose(1, 0, 2).reshape(n_tiles, TOP_K * MOE_TILE)
    idx_rec = jnp.concatenate([rec, jnp.zeros((n_tiles, IDX_REC - TOP_K * MOE_TILE), jnp.int32)], axis=1).reshape(-1)
    tables = jnp.concatenate([pad_start, counts, padded]).astype(jnp.int32)
    return idx_rec, tables, block_e, n_used, n_blocks * MOE_BLOCK


def _tail_body(idx_hbm, y_hbm, x1_ref, tg_ref, pe_ref, gple_ref, wpg_ref, wp_ref, gfin_ref, yp_ref, ys_ref,
               idx_smem, gbuf, isem, gsem, *, n_tiles, n_prompt_tiles):
    i = pl.program_id(0)
    slot = i % 2

    def idx_copy(rec, s):
        return pltpu.make_async_copy(idx_hbm.at[pl.ds(pl.multiple_of(rec * IDX_REC, IDX_REC), IDX_REC)],
                                     idx_smem.at[pl.ds(pl.multiple_of(s * IDX_REC, IDX_REC), IDX_REC)],
                                     isem.at[s])

    @pl.when(i == 0)
    def _():
        idx_copy(0, 0).start()

    @pl.when(i < n_tiles)
    def _():
        idx_copy(i, slot).wait()

        @pl.when(i + 1 < n_tiles)
        def _():
            idx_copy(i + 1, 1 - slot).start()

        for t in range(MOE_TILE):
            for k in range(TOP_K):
                d = idx_smem[slot * IDX_REC + k * MOE_TILE + t]
                pltpu.make_async_copy(y_hbm.at[pl.ds(d, 1)], gbuf.at[slot, k, pl.ds(t, 1)], gsem.at[slot]).start()

    @pl.when(i >= 1)
    def _():
        prev = 1 - slot
        for k in range(TOP_K):
            pltpu.make_async_copy(y_hbm.at[pl.ds(0, MOE_TILE)], gbuf.at[prev, k], gsem.at[prev]).wait()
        tg = tg_ref[...]
        moe = tg[:, 0:1] * gbuf[prev, 0]
        for k in range(1, TOP_K):
            moe = moe + tg[:, k:k + 1] * gbuf[prev, k]
        x2 = x1_ref[...] + moe
        gate = _sigmoid(_dot(_rms(x2, gple_ref[...]).astype(BF16), wpg_ref[...]))
        x3 = x2 + gate * _dot(pe_ref[...].astype(BF16), wp_ref[...])
        y = _rms(x3, gfin_ref[...])

        @pl.when(i - 1 < n_prompt_tiles)
        def _():
            yp_ref[...] = y

        @pl.when(i - 1 >= n_prompt_tiles)
        def _():
            ys_ref[...] = y


def _tail(x1_all, y_rows, idx_rec, tg_t, pe_all, wts, n_p):
    n_all = x1_all.shape[0]
    n_tiles = n_all // MOE_TILE
    n_prompt_tiles = n_p // MOE_TILE
    row = lambda i: (jnp.maximum(i - 1, 0), 0)
    const = lambda i: (0, 0)
    body = functools.partial(_tail_body, n_tiles=n_tiles, n_prompt_tiles=n_prompt_tiles)
    return pl.pallas_call(
        body,
        grid=(n_tiles + 1,),
        in_specs=[pl.BlockSpec(memory_space=pl.ANY), pl.BlockSpec(memory_space=pl.ANY),
                  pl.BlockSpec((MOE_TILE, D_MODEL), row), pl.BlockSpec((MOE_TILE, TOP_K), row),
                  pl.BlockSpec((MOE_TILE, PLE_DIM), row), pl.BlockSpec((1, D_MODEL), const),
                  pl.BlockSpec((D_MODEL, D_MODEL), const), pl.BlockSpec((PLE_DIM, D_MODEL), const),
                  pl.BlockSpec((1, D_MODEL), const)],
        out_specs=[pl.BlockSpec((MOE_TILE, D_MODEL),
                                lambda i: (jnp.clip(i - 1, 0, n_prompt_tiles - 1), 0)),
                   pl.BlockSpec((MOE_TILE, D_MODEL), lambda i: (jnp.maximum(i - 1 - n_prompt_tiles, 0), 0))],
        out_shape=[jax.ShapeDtypeStruct((n_p, D_MODEL), F32),
                   jax.ShapeDtypeStruct((n_all - n_p, D_MODEL), F32)],
        scratch_shapes=[pltpu.SMEM((2 * IDX_REC,), jnp.int32),
                        pltpu.VMEM((2, TOP_K, MOE_TILE, D_MODEL), F32),
                        pltpu.SemaphoreType.DMA((2,)),
                        pltpu.SemaphoreType.DMA((2,))],
        compiler_params=_cparams(("arbitrary",)),
        name="tail",
    )(idx_rec, y_rows, x1_all, tg_t, pe_all, *wts)


RWKV_CHUNK = 128
RWKV_CHUNK_SAMPLE = 8
STATE_PASSES = 1
Y_PASSES = 1
TM_PROMPT = 512


def kernel(x_prompt, x_sample, p_prompt, p_sample, cache_kv_w128, cache_kv_w512, cache_kv_w2048, state_rwkv_shift, state_rwkv_wkv, norm_mix_g, w_in, rwkv_mu, rwkv_w0, rwkv_w2, rwkv_a0, rwkv_a2, rwkv_g2, rwkv_k_k, rwkv_k_a, rwkv_r_k, rwkv_ln_w, rwkv_ln_b, w_out_attn, w_out_rwkv, w_out, norm_ffn_g, router_w, router_b, moe_w1, moe_b1, moe_w2, moe_b2, norm_ple_g, w_ple, w_ple_gate, norm_final_g):
    bp, seq, _ = x_prompt.shape
    bs, t_s, _ = x_sample.shape
    n_p, n_s = bp * seq, bs * t_s
    n_all = n_p + n_s
    assert w_in.shape[0] == 1, "single layer"
    caches = (cache_kv_w128, cache_kv_w512, cache_kv_w2048)

    row = lambda a: a.reshape(1, -1)
    w_in_b = w_in[0].astype(BF16)
    zeros64 = jnp.zeros((LORA_W // 2, RWKV_W), F32)
    rwkv_wts = (row(rwkv_mu[0]), row(rwkv_w0[0]), jnp.concatenate([rwkv_w2[0], zeros64], axis=0),
                row(rwkv_a0[0]), jnp.concatenate([zeros64, rwkv_a2[0]], axis=0), rwkv_g2[0],
                row(rwkv_k_k[0]), row(rwkv_k_a[0]), row(rwkv_r_k[0]), row(rwkv_ln_w[0]), row(rwkv_ln_b[0]))
    mix_wts = (w_out_attn[0].astype(BF16), w_out_rwkv[0].astype(BF16), w_out[0].astype(BF16),
               row(norm_ffn_g[0]), router_w[0].T, router_b[0].reshape(N_EXPERTS, 1))
    tail_wts = (row(norm_ple_g[0]), w_ple_gate[0].astype(BF16), w_ple[0].astype(BF16), row(norm_final_g))
    g_mix = row(norm_mix_g[0])

    xp = x_prompt.reshape(n_p, D_MODEL)
    qd_p, kvd_p, kv_p, zr_p, gate_p = _in_proj_dilated(xp, g_mix, w_in_b, TM_PROMPT, bp, seq)
    att_p = [_attn_prompt(qd_p[gi], kvd_p[gi], gi) for gi in range(N_GROUPS)]
    orw_p, wkv_p = _rwkv(zr_p.reshape(bp, seq, RWKV_COLS), jnp.zeros((bp, 1, RWKV_COLS), F32),
                         jnp.zeros((bp, RWKV_HEADS, RWKV_HEAD, RWKV_HEAD), F32), rwkv_wts, RWKV_CHUNK, seq)
    bufs, counts = _mix_out(xp, [a[0] for a in att_p], [a[1] for a in att_p], orw_p.reshape(n_p, RWKV_W), gate_p,
                            mix_wts, jnp.zeros((N_EXPERTS, 1), F32), TM_PROMPT, n_all, 0, None)

    xs = x_sample.reshape(n_s, D_MODEL)
    q_s, kv_s, zr_s, gate_s = _in_proj(xs, g_mix, w_in_b, n_s)
    t_pad = 8
    pad_t = lambda a: jnp.pad(a.reshape(bs, t_s, -1), ((0, 0), (0, t_pad - t_s), (0, 0)))
    q_s3, kv_s3 = pad_t(q_s), pad_t(kv_s)
    att_s, new_caches = [], []
    for gi in range(N_GROUPS):
        wc = caches[gi].shape[2]
        o, lse, newc = _attn_sample(q_s3, kv_s3, caches[gi].reshape(bs, wc, 2 * GROUP_W), gi, t_s)
        att_s.append((o[:, :t_s].reshape(n_s, GROUP_W), lse[:, :t_s].reshape(n_s, GROUP_W)))
        new_caches.append(newc.reshape(1, bs, wc, 2, HEADS_PER_GROUP, HEAD_DIM))
    zr_s3 = jnp.pad(zr_s.reshape(bs, t_s, RWKV_COLS), ((0, 0), (0, RWKV_CHUNK_SAMPLE - t_s), (0, 0)))
    orw_s, wkv_s = _rwkv(zr_s3, state_rwkv_shift[0].reshape(bs, 1, RWKV_COLS), state_rwkv_wkv[0], rwkv_wts,
                         RWKV_CHUNK_SAMPLE, t_s)
    bufs, counts = _mix_out(xs, [a[0] for a in att_s], [a[1] for a in att_s], orw_s[:, :t_s].reshape(n_s, RWKV_W),
                            gate_s, mix_wts, counts, n_s, n_all, n_p, bufs)
    x1_all, h2_all, top_e, top_g, rank = bufs

    idx_rec, tables, block_e, n_used, n_rows = _route(top_e, rank, counts)
    x_rows = _moe_dispatch(h2_all, idx_rec, tables, n_rows)
    y_rows = _moe_experts(x_rows, block_e, n_used, moe_w1[0], moe_b1[0].reshape(N_EXPERTS, 1, 2 * D_MODEL),
                          moe_w2[0], moe_b2[0].reshape(N_EXPERTS, 1, D_MODEL))
    pe_all = jnp.concatenate([p_prompt[0].reshape(n_p, PLE_DIM), p_sample[0].reshape(n_s, PLE_DIM)], axis=0)
    y_p, y_s = _tail(x1_all, y_rows, idx_rec, top_g.T, pe_all, tail_wts, n_p)

    kv_p4 = kv_p.reshape(bp, seq, N_GROUPS, 2, HEADS_PER_GROUP, HEAD_DIM)
    kv_out_p = [kv_p4[:, seq - min(w, seq):, gi][None] for gi, (w, _) in enumerate(ATTN_GROUPS)]
    shift_p = zr_p.reshape(bp, seq, RWKV_COLS)[:, -1][None]
    shift_s = zr_s.reshape(bs, t_s, RWKV_COLS)[:, -1][None]
    return (y_p.reshape(bp, seq, D_MODEL), y_s.reshape(bs, t_s, D_MODEL),
            kv_out_p[0], kv_out_p[1], kv_out_p[2], shift_p, wkv_p[None],
            new_caches[0], new_caches[1], new_caches[2], shift_s, wkv_s[None])
```

```python
import functools
import math

import numpy as np
import jax
import jax.numpy as jnp
from jax import lax
from jax.experimental import pallas as pl
from jax.experimental.pallas import tpu as pltpu

F32 = jnp.float32
BF16 = jnp.bfloat16

LANE = 128
D_MODEL = 1024
N_GROUPS = 3
HEADS_PER_GROUP = 4
HEAD_DIM = 64
ATTN_GROUPS = ((128, 1), (512, 4), (2048, 16))
GROUP_W = HEADS_PER_GROUP * HEAD_DIM
Q_COLS = N_GROUPS * GROUP_W
KV_COLS = 2 * Q_COLS
BAND = 128

RWKV_HEADS = 8
RWKV_HEAD = 64
RWKV_W = 512
LORA_W = 128
GATE_LORA = 128
RWKV_COLS = 3 * RWKV_W + LORA_W + GATE_LORA
GN_EPS = 64e-5
GATE_COLS = 2 * D_MODEL
Z_RWKV0 = 3 * Q_COLS
Z_GATE0 = Z_RWKV0 + RWKV_COLS
IN_COLS = Z_GATE0 + GATE_COLS

N_EXPERTS = 32
TOP_K = 4
SWIGLU_LIMIT = 7.0
SWIGLU_ALPHA = 1.702
MOE_BLOCK = 512
MOE_TILE = 128
IDX_REC = 1024
PLE_DIM = 256
RMS_EPS = 1e-6

NEG_BIG = -1e30
VMEM_LIMIT = 56 * 1024 * 1024


def _cparams(sem):
    return pltpu.CompilerParams(dimension_semantics=sem, vmem_limit_bytes=VMEM_LIMIT)


def _rms(x, g):
    return x * lax.rsqrt(jnp.mean(x * x, axis=-1, keepdims=True) + RMS_EPS) * g


def _sigmoid(x):
    return 1.0 / (1.0 + jnp.exp(-x))


def _dot(a, b):
    return jnp.dot(a, b, preferred_element_type=F32)


def _dot_nt(a, b):
    return lax.dot_general(a, b, (((1,), (1,)), ((), ())), preferred_element_type=F32)


def _dot_tn(a, b):
    return lax.dot_general(a, b, (((0,), (0,)), ((), ())), preferred_element_type=F32)


def _split(x):
    hi = x.astype(BF16)
    lo = (x - hi.astype(F32)).astype(BF16)
    return hi, lo


def _mm3(dot, a, b):
    ah, al = _split(a)
    bh, bl = _split(b)
    return dot(ah, bh) + (dot(ah, bl) + dot(al, bh))


def _parts(x, n):
    out = []
    for _ in range(n - 1):
        hi = x.astype(BF16)
        out.append(hi)
        x = x - hi.astype(F32)
    out.append(x.astype(BF16))
    return out


def _mmp(dot, ap, bp):
    order = max(len(ap), len(bp))
    acc = None
    for i, a in enumerate(ap):
        for j, b in enumerate(bp):
            if i + j < order:
                t = dot(a, b)
                acc = t if acc is None else acc + t
    return acc


def _in_proj_body(x_ref, g_ref, w_ref, q_ref, kv_ref, zr_ref, gate_ref):
    h = _rms(x_ref[...], g_ref[...]).astype(BF16)

    def proj(lo, width):
        return _dot(h, w_ref[:, lo:lo + width])

    q_ref[...] = (proj(0, Q_COLS) * (1.0 / math.sqrt(HEAD_DIM))).astype(BF16)
    for g in range(N_GROUPS):
        kv_ref[:, 2 * g * GROUP_W:(2 * g + 1) * GROUP_W] = proj(Q_COLS + g * GROUP_W, GROUP_W)
        kv_ref[:, (2 * g + 1) * GROUP_W:(2 * g + 2) * GROUP_W] = proj(2 * Q_COLS + g * GROUP_W, GROUP_W)
    zr_ref[...] = proj(Z_RWKV0, RWKV_COLS)
    gate_ref[...] = _sigmoid(proj(Z_GATE0, GATE_COLS)).astype(BF16)


def _in_proj_dilated_body(x_ref, g_ref, w_ref, *rest, tiles_per_seq):
    qd_refs, kvd_refs, kvt_refs = rest[0:N_GROUPS], rest[N_GROUPS:2 * N_GROUPS], rest[2 * N_GROUPS:3 * N_GROUPS]
    zr_ref, gate_ref, st_ref = rest[3 * N_GROUPS:]
    tm = x_ref.shape[0]
    tile_in_seq = pl.program_id(0) % tiles_per_seq
    h = _rms(x_ref[...], g_ref[...]).astype(BF16)

    def proj(lo, width):
        return _dot(h, w_ref[:, lo:lo + width])

    n_q = Q_COLS // LANE

    def stage(slab0, val):
        for s in range(val.shape[1] // LANE):
            st_ref[slab0 + s] = val[:, s * LANE:(s + 1) * LANE]

    stage(0, proj(0, Q_COLS) * (1.0 / math.sqrt(HEAD_DIM)))
    for g in range(N_GROUPS):
        for part, src in ((0, Q_COLS), (1, 2 * Q_COLS)):
            col = (2 * g + part) * GROUP_W
            stage(n_q + col // LANE, proj(src + g * GROUP_W, GROUP_W))
    for g, (window, _) in enumerate(ATTN_GROUPS):
        rows = min(window, tm)
        first_tile = tiles_per_seq - max(window // tm, 1)

        @pl.when(tile_in_seq >= first_tile)
        def _():
            for s in range(2 * GROUP_W // LANE):
                slab = st_ref[n_q + 2 * g * GROUP_W // LANE + s, tm - rows:tm, :]
                kvt_refs[g][0, s * LANE:(s + 1) * LANE, :] = slab.T
    for g, (_, dil) in enumerate(ATTN_GROUPS):
        for r in range(dil):
            rows = pl.ds(r, tm // dil, stride=dil) if dil > 1 else slice(None)
            for s in range(GROUP_W // LANE):
                qd_refs[g][0, r, :, s * LANE:(s + 1) * LANE] = st_ref[g * GROUP_W // LANE + s, rows, :].astype(BF16)
            for s in range(2 * GROUP_W // LANE):
                kvd_refs[g][0, r, :, s * LANE:(s + 1) * LANE] = \
                    st_ref[n_q + 2 * g * GROUP_W // LANE + s, rows, :].astype(BF16)
    zr_ref[...] = proj(Z_RWKV0, RWKV_COLS)
    gate_ref[...] = _sigmoid(proj(Z_GATE0, GATE_COLS)).astype(BF16)


def _in_proj_dilated(x, g, w_bf16, tm, batch, seq):
    n = x.shape[0]
    tiles = seq // tm
    row = lambda i: (i, 0)
    const = lambda i: (0, 0)
    sub = lambda i: (i // tiles, 0, i % tiles, 0)
    dils = [d for _, d in ATTN_GROUPS]
    assert all(tm % (16 * d) == 0 for d in dils)
    wins = [min(w, seq) for w, _ in ATTN_GROUPS]
    assert all(w % tm == 0 or tm % w == 0 for w in wins)

    def tail_spec(w):
        cols = min(w, tm)
        first = tiles - max(w // tm, 1)
        return pl.BlockSpec((1, 2 * GROUP_W, cols), lambda i: (i // tiles, 0, jnp.maximum(i % tiles - first, 0)))

    outs = pl.pallas_call(
        functools.partial(_in_proj_dilated_body, tiles_per_seq=tiles),
        grid=(n // tm,),
        in_specs=[pl.BlockSpec((tm, D_MODEL), row),
                  pl.BlockSpec((1, D_MODEL), const),
                  pl.BlockSpec((D_MODEL, IN_COLS), const, pipeline_mode=pl.Buffered(1))],
        out_specs=[pl.BlockSpec((1, d, tm // d, GROUP_W), sub) for d in dils]
                  + [pl.BlockSpec((1, d, tm // d, 2 * GROUP_W), sub) for d in dils]
                  + [tail_spec(w) for w in wins]
                  + [pl.BlockSpec((tm, RWKV_COLS), row),
                     pl.BlockSpec((tm, GATE_COLS), row)],
        out_shape=[jax.ShapeDtypeStruct((batch, d, seq // d, GROUP_W), BF16) for d in dils]
                  + [jax.ShapeDtypeStruct((batch, d, seq // d, 2 * GROUP_W), BF16) for d in dils]
                  + [jax.ShapeDtypeStruct((batch, 2 * GROUP_W, w), F32) for w in wins]
                  + [jax.ShapeDtypeStruct((n, RWKV_COLS), F32),
                     jax.ShapeDtypeStruct((n, GATE_COLS), BF16)],
        scratch_shapes=[pltpu.VMEM(((Q_COLS + KV_COLS) // LANE, tm, LANE), F32)],
        compiler_params=_cparams(("arbitrary",)),
        name="in_proj_dilated",
    )(x, g, w_bf16)
    g3 = N_GROUPS
    return outs[0:g3], outs[g3:2 * g3], outs[2 * g3:3 * g3], outs[3 * g3], outs[3 * g3 + 1]


def _in_proj(x, g, w_bf16, tm):
    n = x.shape[0]
    row = lambda i: (i, 0)
    const = lambda i: (0, 0)
    return pl.pallas_call(
        _in_proj_body,
        grid=(n // tm,),
        in_specs=[pl.BlockSpec((tm, D_MODEL), row),
                  pl.BlockSpec((1, D_MODEL), const),
                  pl.BlockSpec((D_MODEL, IN_COLS), const)],
        out_specs=[pl.BlockSpec((tm, Q_COLS), row),
                   pl.BlockSpec((tm, KV_COLS), row),
                   pl.BlockSpec((tm, RWKV_COLS), row),
                   pl.BlockSpec((tm, GATE_COLS), row)],
        out_shape=[jax.ShapeDtypeStruct((n, Q_COLS), BF16),
                   jax.ShapeDtypeStruct((n, KV_COLS), F32),
                   jax.ShapeDtypeStruct((n, RWKV_COLS), F32),
                   jax.ShapeDtypeStruct((n, GATE_COLS), BF16)],
        compiler_params=_cparams(("parallel",)),
        name="in_proj",
    )(x, g, w_bf16)


def _alibi_slopes(gi):
    return [2.0 ** (-8.0 * (gi * HEADS_PER_GROUP + h + 1) / (N_GROUPS * HEADS_PER_GROUP))
            for h in range(HEADS_PER_GROUP)]


def _head_of_lane(shape):
    return lax.broadcasted_iota(jnp.int32, shape, len(shape) - 1) // HEAD_DIM


def _stack_heads(q):
    hl = _head_of_lane(q.shape)
    return jnp.concatenate([jnp.where(hl == h, q, jnp.zeros_like(q)) for h in range(HEADS_PER_GROUP)], axis=0)


def _unstack_heads(x4, rows):
    hl = _head_of_lane((rows, GROUP_W))
    out = x4[0:rows]
    for h in range(1, HEADS_PER_GROUP):
        out = jnp.where(hl == h, x4[h * rows:(h + 1) * rows], out)
    return out


def _attn_prompt_body(q_ref, kvp_ref, kvc_ref, o_ref, lse_ref, kb_ref, bias_ref, *, dil, slopes, n_sub):
    c = pl.program_id(2)
    kb_ref[0:BAND, :] = kvp_ref[0, 0]
    kb_ref[BAND:, :] = kvc_ref[0, 0]

    qi = lax.broadcasted_iota(jnp.int32, (BAND, 2 * BAND), 0) + BAND
    ki = lax.broadcasted_iota(jnp.int32, (BAND, 2 * BAND), 1)
    rel = qi - ki
    dist = jnp.where((rel >= 0) & (rel <= BAND), (dil * rel).astype(F32), -NEG_BIG / slopes[-1])
    for h in range(HEADS_PER_GROUP):
        bias_ref[h * BAND:(h + 1) * BAND, :] = -slopes[h] * dist

    def sub_block(n, carry):
        r0 = pl.multiple_of(n * BAND, BAND)
        q4 = _stack_heads(q_ref[0, 0, pl.ds(r0, BAND), :])
        kv = kb_ref[pl.ds(r0, 2 * BAND), :]
        s = _dot_nt(q4, kv[:, :GROUP_W]) + bias_ref[...]
        n_before_start = jnp.where((c == 0) & (n == 0), BAND, 0)
        kcol = lax.broadcasted_iota(jnp.int32, s.shape, 1)
        s = jnp.where(kcol < n_before_start, NEG_BIG, s)
        m = jnp.max(s, axis=-1, keepdims=True)
        e = jnp.exp(s - m)
        den = jnp.sum(e, axis=-1, keepdims=True)
        o4 = _dot(e.astype(BF16), kv[:, GROUP_W:]) * (1.0 / den)
        o_ref[0, 0, pl.ds(r0, BAND), :] = _unstack_heads(o4, BAND).astype(o_ref.dtype)
        lse4 = jnp.broadcast_to(m + jnp.log(den), (HEADS_PER_GROUP * BAND, GROUP_W))
        lse_ref[0, 0, pl.ds(r0, BAND), :] = _unstack_heads(lse4, BAND)
        return carry

    lax.fori_loop(0, n_sub, sub_block, 0, unroll=2)


def _attn_prompt(qd, kvd, gi):
    window, dil = ATTN_GROUPS[gi]
    assert window // dil == BAND
    batch, _, sub_len, _ = qd.shape
    chunk = min(sub_len, 1024)
    n_chunks = sub_len // chunk
    sub_per_chunk = chunk // BAND
    body = functools.partial(_attn_prompt_body, dil=dil, slopes=_alibi_slopes(gi), n_sub=sub_per_chunk)
    cur = lambda b, r, c: (b, r, c, 0)
    return pl.pallas_call(
        body,
        grid=(batch, dil, n_chunks),
        in_specs=[
            pl.BlockSpec((1, 1, chunk, GROUP_W), cur),
            pl.BlockSpec((1, 1, BAND, 2 * GROUP_W),
                         lambda b, r, c: (b, r, jnp.maximum(c * sub_per_chunk - 1, 0), 0)),
            pl.BlockSpec((1, 1, chunk, 2 * GROUP_W), cur),
        ],
        out_specs=[pl.BlockSpec((1, 1, chunk, GROUP_W), cur), pl.BlockSpec((1, 1, chunk, GROUP_W), cur)],
        out_shape=[jax.ShapeDtypeStruct(qd.shape, BF16), jax.ShapeDtypeStruct(qd.shape, F32)],
        scratch_shapes=[pltpu.VMEM((chunk + BAND, 2 * GROUP_W), BF16),
                        pltpu.VMEM((HEADS_PER_GROUP * BAND, 2 * BAND), F32)],
        compiler_params=_cparams(("parallel", "parallel", "arbitrary")),
        name=f"attn_prompt_g{gi}",
    )(qd, kvd, kvd)


def _attn_sample_body(q_ref, tail_ref, cache_ref, o_ref, lse_ref, newc_ref, *, dil, slopes, wc, t_new):
    tp = q_ref.shape[1]
    cache = cache_ref[0]
    tail = tail_ref[0]
    newc_ref[0] = pltpu.roll(cache, wc - t_new, 1)
    lane = lax.broadcasted_iota(jnp.int32, tail.shape, 1)
    newc_ref[0, :, wc - LANE:wc] = jnp.where(lane >= LANE - t_new, tail, newc_ref[0, :, wc - LANE:wc])

    q4 = _stack_heads(q_ref[0])
    cb = cache.astype(BF16)
    nb = tail.astype(BF16)
    t_row = lax.broadcasted_iota(jnp.int32, (tp, 1), 0)
    slope_col = jnp.concatenate([jnp.full((tp, 1), s, F32) for s in slopes], axis=0)
    t4 = jnp.concatenate([t_row] * HEADS_PER_GROUP, axis=0)

    def scores(keys_t_bf16, dist):
        ok = (dist >= 0) & ((dist & (dil - 1)) == 0) & (dist <= BAND * dil)
        s = _dot(q4, keys_t_bf16[:GROUP_W, :])
        return jnp.where(ok, s - slope_col * dist.astype(F32), NEG_BIG)

    pc = lax.broadcasted_iota(jnp.int32, (HEADS_PER_GROUP * tp, wc), 1)
    pn = lax.broadcasted_iota(jnp.int32, (HEADS_PER_GROUP * tp, LANE), 1) - (LANE - t_new)
    s_c = scores(cb, wc + t4 - pc)
    s_n = scores(nb, jnp.where(pn >= 0, t4 - pn, -1))
    m = jnp.maximum(jnp.max(s_c, axis=-1, keepdims=True), jnp.max(s_n, axis=-1, keepdims=True))
    e_c = jnp.exp(s_c - m)
    e_n = jnp.exp(s_n - m)
    den = jnp.sum(e_c, axis=-1, keepdims=True) + jnp.sum(e_n, axis=-1, keepdims=True)
    o4 = (_dot_nt(e_c.astype(BF16), cb[GROUP_W:, :]) + _dot_nt(e_n.astype(BF16), nb[GROUP_W:, :])) * (1.0 / den)
    o_ref[0] = _unstack_heads(o4, tp).astype(o_ref.dtype)
    lse4 = jnp.broadcast_to(m + jnp.log(den), (HEADS_PER_GROUP * tp, GROUP_W))
    lse_ref[0] = _unstack_heads(lse4, tp)


def _attn_sample(q, tail_t, cache_t, gi, t_new):
    window, dil = ATTN_GROUPS[gi]
    batch, tp, _ = q.shape
    wc = cache_t.shape[2]
    assert wc + 0 - dil * BAND >= 0
    body = functools.partial(_attn_sample_body, dil=dil, slopes=_alibi_slopes(gi), wc=wc, t_new=t_new)
    return pl.pallas_call(
        body,
        grid=(batch,),
        in_specs=[pl.BlockSpec((1, tp, GROUP_W), lambda b: (b, 0, gi)),
                  pl.BlockSpec((1, 2 * GROUP_W, LANE), lambda b: (b, gi, 0)),
                  pl.BlockSpec((1, 2 * GROUP_W, wc), lambda b: (b, 0, 0))],
        out_specs=[pl.BlockSpec((1, tp, GROUP_W), lambda b: (b, 0, 0)),
                   pl.BlockSpec((1, tp, GROUP_W), lambda b: (b, 0, 0)),
                   pl.BlockSpec((1, 2 * GROUP_W, wc), lambda b: (b, 0, 0))],
        out_shape=[jax.ShapeDtypeStruct((batch, tp, GROUP_W), BF16),
                   jax.ShapeDtypeStruct((batch, tp, GROUP_W), F32),
                   jax.ShapeDtypeStruct((batch, 2 * GROUP_W, wc), F32)],
        compiler_params=_cparams(("parallel",)),
        name=f"attn_sample_g{gi}",
    )(q, tail_t, cache_t)


def _rwkv_body(zr_ref, shift0_ref, wkv0_ref, mu_ref, w0_ref, w2_ref, a0_ref, a2_ref, g2_ref, kk_ref, ka_ref,
               rk_ref, lnw_ref, lnb_ref, o_ref, wkv_ref, prev_ref, *, chunk, t_valid):
    c = pl.program_id(1)
    L = chunk
    mm = functools.partial(_mm3, _dot)
    heads = range(RWKV_HEADS)
    sls = [slice(h * RWKV_HEAD, (h + 1) * RWKV_HEAD) for h in heads]

    @pl.when(c == 0)
    def _():
        prev_ref[...] = shift0_ref[0]
        wkv_ref[0] = wkv0_ref[0]

    zr = zr_ref[0]
    row = lax.broadcasted_iota(jnp.int32, (L, 1), 0)
    prev = jnp.where(row == 0, prev_ref[...], pltpu.roll(zr, 1, 0))
    prev_ref[...] = zr[L - 1:L, :]
    zm = zr + (prev - zr) * mu_ref[...]

    r = zm[:, 0:RWKV_W]
    k = zm[:, RWKV_W:2 * RWKV_W]
    v = zm[:, 2 * RWKV_W:3 * RWKV_W]
    xwa = zm[:, 3 * RWKV_W:3 * RWKV_W + LORA_W]
    xg = zm[:, 3 * RWKV_W + LORA_W:]
    w_pre = w0_ref[...] + mm(jnp.tanh(xwa), w2_ref[...])
    softplus = jnp.maximum(-w_pre, 0.0) + jnp.log(1.0 + jnp.exp(-jnp.abs(w_pre)))
    w_log = -softplus - 0.5
    lw = -jnp.exp(w_log)
    a = _sigmoid(a0_ref[...] + mm(xwa, a2_ref[...]))
    g = mm(_sigmoid(xg), g2_ref[...])
    kk_raw = k * kk_ref[...]
    k2 = k * (1.0 + (a - 1.0) * ka_ref[...])
    if t_valid < L:
        live = row < t_valid
        lw = jnp.where(live, lw, 0.0)
        kk_raw = jnp.where(live, kk_raw, 0.0)
        k2 = jnp.where(live, k2, 0.0)
        v = jnp.where(live, v, 0.0)

    ti = lax.broadcasted_iota(jnp.int32, (L, L), 0)
    si = lax.broadcasted_iota(jnp.int32, (L, L), 1)
    incl = ti >= si
    strict = ti > si
    clw = _mmp(_dot, [jnp.where(incl, 1.0, 0.0).astype(BF16)], _parts(lw, 3))
    mid = max(L // 2 - 1, 0)
    clw_mid = clw[mid:mid + 1, :]
    rel = clw - clw_mid
    p_incl = jnp.exp(rel)
    p_inv = jnp.exp(-rel)
    p_prev = jnp.exp(rel - lw)
    p_mid = jnp.exp(clw_mid)
    rt_all = r * p_incl
    kt_all = k2 * p_inv
    rk_all = r * k2 * rk_ref[...]

    sp, yp = STATE_PASSES, Y_PASSES
    kkh = [kk_raw[:, s] for s in sls]
    kkn = [x / jnp.maximum(jnp.sqrt(jnp.sum(x * x, axis=-1, keepdims=True)), 1e-12) for x in kkh]
    at = [-kkn[h] * p_prev[:, sls[h]] for h in heads]
    bt = [kkn[h] * a[:, sls[h]] * p_inv[:, sls[h]] for h in heads]
    rt = [rt_all[:, s] for s in sls]
    kt = [kt_all[:, s] for s in sls]
    vh = [v[:, s] for s in sls]
    at_s = [_parts(x, sp) for x in at]
    bt_s = [_parts(x, sp) for x in bt]
    kt_s = [_parts(x, sp) for x in kt]
    vh_s = [_parts(x, sp) for x in vh]
    rt_y = [_parts(x, yp) for x in rt]
    a_ab = [jnp.where(strict, _mmp(_dot_nt, at_s[h], bt_s[h]), 0.0) for h in heads]
    a_ak = [jnp.where(strict, _mmp(_dot_nt, at_s[h], kt_s[h]), 0.0) for h in heads]
    a_rb = [jnp.where(incl, _mmp(_dot_nt, rt_y[h], bt_s[h][:yp]), 0.0) for h in heads]
    a_rk = [jnp.where(incl, _mmp(_dot_nt, rt_y[h], kt_s[h][:yp]), 0.0) for h in heads]
    x = [jnp.concatenate([at[h], _mmp(_dot, _parts(a_ak[h], sp), vh_s[h])], axis=1) for h in heads]
    p = a_ab
    for level in range(max(int(math.log2(L)), 1)):
        if level > 0:
            p = [_mmp(_dot, ps, ps) for ps in p_s]
        p_s = [_parts(q, sp) for q in p]
        x = [x[h] + _mmp(_dot, p_s[h], _parts(x[h], sp)) for h in heads]
    x_s = [_parts(q, sp) for q in x]
    qy = [_mmp(_dot, _parts(a_rb[h], yp), x_s[h][:yp]) for h in heads]
    y0 = [qy[h][:, RWKV_HEAD:] + _mmp(_dot, _parts(a_rk[h], yp), vh_s[h][:yp]) for h in heads]
    qh = [rt[h] + qy[h][:, :RWKV_HEAD] for h in heads]
    s0 = [wkv_ref[0, h] * p_mid[:, sls[h]] for h in heads]
    s0_s = [_parts(q, sp) for q in s0]
    y = [_mmp(_dot_nt, _parts(qh[h], yp), s0_s[h][:yp]) + y0[h] for h in heads]
    wtb = [_mmp(_dot_tn, [q[:, :RWKV_HEAD] for q in x_s[h]], bt_s[h]) for h in heads]
    uv_s = [[jnp.concatenate([x_s[h][i][:, RWKV_HEAD:], vh_s[h][i]], axis=0) for i in range(sp)] for h in heads]
    bk_s = [[jnp.concatenate([bt_s[h][i], kt_s[h][i]], axis=0) for i in range(sp)] for h in heads]
    for h in heads:
        s_new = s0[h] + _mmp(_dot, s0_s[h], _parts(wtb[h], sp)) + _mmp(_dot_tn, uv_s[h], bk_s[h])
        wkv_ref[0, h] = s_new * p_incl[L - 1:L, sls[h]]
    for h in heads:
        mu_y = jnp.mean(y[h], axis=-1, keepdims=True)
        yc = y[h] - mu_y
        var = jnp.mean(yc * yc, axis=-1, keepdims=True)
        yn = yc * lax.rsqrt(var + GN_EPS) * lnw_ref[:, sls[h]] + lnb_ref[:, sls[h]]
        bonus = jnp.sum(rk_all[:, sls[h]], axis=-1, keepdims=True) * vh[h]
        o_ref[0, :, sls[h]] = ((yn + bonus) * g[:, sls[h]]).astype(o_ref.dtype)


def _rwkv(zr, shift0, wkv0, wts, chunk, t_valid):
    batch, t, _ = zr.shape
    n_chunks = t // chunk
    assert t_valid == t or n_chunks == 1
    body = functools.partial(_rwkv_body, chunk=chunk, t_valid=min(t_valid, chunk))
    vec = lambda width: pl.BlockSpec((1, width), lambda b, c: (0, 0))
    mat = lambda rows: pl.BlockSpec((rows, RWKV_W), lambda b, c: (0, 0))
    return pl.pallas_call(
        body,
        grid=(batch, n_chunks),
        in_specs=[pl.BlockSpec((1, chunk, RWKV_COLS), lambda b, c: (b, c, 0)),
                  pl.BlockSpec((1, 1, RWKV_COLS), lambda b, c: (b, 0, 0)),
                  pl.BlockSpec((1, RWKV_HEADS, RWKV_HEAD, RWKV_HEAD), lambda b, c: (b, 0, 0, 0)),
                  vec(RWKV_COLS), vec(RWKV_W), mat(LORA_W), vec(RWKV_W), mat(LORA_W), mat(GATE_LORA),
                  vec(RWKV_W), vec(RWKV_W), vec(RWKV_W), vec(RWKV_W), vec(RWKV_W)],
        out_specs=[pl.BlockSpec((1, chunk, RWKV_W), lambda b, c: (b, c, 0)),
                   pl.BlockSpec((1, RWKV_HEADS, RWKV_HEAD, RWKV_HEAD), lambda b, c: (b, 0, 0, 0))],
        out_shape=[jax.ShapeDtypeStruct((batch, t, RWKV_W), BF16),
                   jax.ShapeDtypeStruct((batch, RWKV_HEADS, RWKV_HEAD, RWKV_HEAD), F32)],
        scratch_shapes=[pltpu.VMEM((1, RWKV_COLS), F32)],
        compiler_params=_cparams(("parallel", "arbitrary")),
        name="rwkv_scan",
    )(zr, shift0, wkv0, *wts)


def _mix_out_body(x_ref, o0_ref, o1_ref, o2_ref, l0_ref, l1_ref, l2_ref, orw_ref, gate_ref, woa_ref, wor_ref,
                  wo_ref, gffn_ref, rwt_ref, rb_ref, cnt0_ref, *rest, dilated, n_alias):
    x1_ref, h2_ref, te_ref, tg_ref, rk_ref, cnt_ref = rest[n_alias:n_alias + 6]
    stage = list(rest[n_alias + 6:])

    def token_major(ref, gi):
        if not dilated:
            return ref[...].astype(F32)
        dil = ATTN_GROUPS[gi][1]
        if dil == 1:
            return ref[0, 0].astype(F32)
        st_ref = stage.pop()
        for r in range(dil):
            sub = ref[0, r].astype(F32)
            for s in range(GROUP_W // LANE):
                st_ref[s, pl.ds(r, ref.shape[2], stride=dil), :] = sub[:, s * LANE:(s + 1) * LANE]
        return jnp.concatenate([st_ref[s] for s in range(GROUP_W // LANE)], axis=1)

    l0, l1, l2 = token_major(l0_ref, 0), token_major(l1_ref, 1), token_major(l2_ref, 2)
    m = jnp.maximum(jnp.maximum(l0, l1), l2)
    e0, e1, e2 = jnp.exp(l0 - m), jnp.exp(l1 - m), jnp.exp(l2 - m)
    o_att = (e0 * token_major(o0_ref, 0) + e1 * token_major(o1_ref, 1) + e2 * token_major(o2_ref, 2)) \
        * (1.0 / (e0 + e1 + e2))
    gates = gate_ref[...].astype(F32)
    merged = gates[:, :D_MODEL] * _dot(o_att.astype(BF16), woa_ref[...]) \
        + gates[:, D_MODEL:] * _dot(orw_ref[...], wor_ref[...])
    x1 = x_ref[...] + _dot(merged.astype(BF16), wo_ref[...])
    x1_ref[...] = x1
    h2 = _rms(x1, gffn_ref[...])
    h2_ref[...] = h2

    logits = _mm3(_dot_nt, rwt_ref[...], h2) + rb_ref[...]
    e_iota = lax.broadcasted_iota(jnp.int32, logits.shape, 0)
    vals, idxs = [], []
    for _ in range(TOP_K):
        top = jnp.max(logits, axis=0, keepdims=True)
        idx = jnp.min(jnp.where(logits == top, e_iota, N_EXPERTS), axis=0, keepdims=True)
        vals.append(top)
        idxs.append(idx)
        logits = jnp.where(e_iota == idx, -jnp.inf, logits)
    exps = [jnp.exp(t - vals[0]) for t in vals]
    inv = 1.0 / (exps[0] + exps[1] + exps[2] + exps[3])
    te_ref[...] = jnp.concatenate(idxs, axis=0)
    tg_ref[...] = jnp.concatenate([e * inv for e in exps], axis=0)

    @pl.when(pl.program_id(0) == 0)
    def _():
        cnt_ref[...] = cnt0_ref[...]

    hits = [e_iota == idx for idx in idxs]
    onehot = jnp.where(hits[0] | hits[1] | hits[2] | hits[3], 1.0, 0.0)
    tm = onehot.shape[1]
    earlier = lax.broadcasted_iota(jnp.int32, (tm, tm), 0) < lax.broadcasted_iota(jnp.int32, (tm, tm), 1)
    before = cnt_ref[...] + _dot(onehot.astype(BF16), jnp.where(earlier, 1.0, 0.0).astype(BF16))
    rk_ref[...] = jnp.concatenate([jnp.sum(jnp.where(h, before, 0.0), axis=0, keepdims=True) for h in hits],
                                  axis=0).astype(jnp.int32)
    cnt_ref[...] += jnp.sum(onehot, axis=1, keepdims=True)


def _mix_out(x, o_g, lse_g, o_rwkv, gates, wts, cnt0, tm, n_all, row0, bufs):
    n = x.shape[0]
    blk0 = row0 // tm
    row = lambda i: (i, 0)
    const = lambda i: (0, 0)
    out_row = lambda i: (i + blk0, 0)
    out_col = lambda i: (0, i + blk0)
    tok = lambda w: pl.BlockSpec((tm, w), row)
    dilated = o_g[0].ndim == 4
    if dilated:
        tiles = o_g[0].shape[1] * o_g[0].shape[2] // tm
        sub = lambda i: (i // tiles, 0, i % tiles, 0)
        att_specs = [pl.BlockSpec((1, d, tm // d, GROUP_W), sub) for _, d in ATTN_GROUPS] * 2
        stage = [pltpu.VMEM((GROUP_W // LANE, tm, LANE), F32) for _, d in ATTN_GROUPS if d > 1] * 2
    else:
        att_specs, stage = [tok(GROUP_W)] * 6, []
    in_specs = [tok(D_MODEL)] + att_specs + [tok(RWKV_W), tok(GATE_COLS),
                pl.BlockSpec((GROUP_W, D_MODEL), const), pl.BlockSpec((RWKV_W, D_MODEL), const),
                pl.BlockSpec((D_MODEL, D_MODEL), const), pl.BlockSpec((1, D_MODEL), const),
                pl.BlockSpec((N_EXPERTS, D_MODEL), const), pl.BlockSpec((N_EXPERTS, 1), const),
                pl.BlockSpec((N_EXPERTS, 1), const)]
    args = [x, *o_g, *lse_g, o_rwkv, gates, *wts, cnt0]
    aliases = {}
    if bufs is not None:
        in_specs += [pl.BlockSpec(memory_space=pl.ANY)] * 5
        aliases = {len(args) + j: j for j in range(5)}
        args += list(bufs)
    *new_bufs, cnt = pl.pallas_call(
        functools.partial(_mix_out_body, dilated=dilated, n_alias=len(aliases)),
        grid=(n // tm,),
        scratch_shapes=stage,
        in_specs=in_specs,
        out_specs=[pl.BlockSpec((tm, D_MODEL), out_row), pl.BlockSpec((tm, D_MODEL), out_row),
                   pl.BlockSpec((TOP_K, tm), out_col), pl.BlockSpec((TOP_K, tm), out_col),
                   pl.BlockSpec((TOP_K, tm), out_col), pl.BlockSpec((N_EXPERTS, 1), const)],
        out_shape=[jax.ShapeDtypeStruct((n_all, D_MODEL), F32), jax.ShapeDtypeStruct((n_all, D_MODEL), F32),
                   jax.ShapeDtypeStruct((TOP_K, n_all), jnp.int32), jax.ShapeDtypeStruct((TOP_K, n_all), F32),
                   jax.ShapeDtypeStruct((TOP_K, n_all), jnp.int32), jax.ShapeDtypeStruct((N_EXPERTS, 1), F32)],
        input_output_aliases=aliases,
        compiler_params=_cparams(("arbitrary",)),
        name="mix_out",
    )(*args)
    return new_bufs, cnt


def _moe_dispatch_body(tab_ref, idx_hbm, h2_hbm, x_hbm, idx_smem, hbuf, zrow, isem, fsem, dsem, zsem, *, n_tiles):
    i = pl.program_id(0)
    slot = i % 2
    hslot = i % 3

    def tile_fetch(tile, hs):
        return pltpu.make_async_copy(h2_hbm.at[pl.ds(pl.multiple_of(tile * MOE_TILE, MOE_TILE), MOE_TILE)],
                                     hbuf.at[hs], fsem.at[hs])

    def idx_copy(rec, s):
        return pltpu.make_async_copy(idx_hbm.at[pl.ds(pl.multiple_of(rec * IDX_REC, IDX_REC), IDX_REC)],
                                     idx_smem.at[pl.ds(pl.multiple_of(s * IDX_REC, IDX_REC), IDX_REC)],
                                     isem.at[s])

    def row_copies_start(s):
        def one(t, carry):
            for k in range(TOP_K):
                d = idx_smem[s * IDX_REC + k * MOE_TILE + t]
                pltpu.make_async_copy(hbuf.at[hslot, pl.ds(t, 1)], x_hbm.at[pl.ds(d, 1)], dsem.at[s]).start()
            return carry
        for t in range(MOE_TILE):
            one(t, 0)

    @pl.when(i == 0)
    def _():
        idx_copy(0, 0).start()
        tile_fetch(0, 0).start()

    @pl.when(i + 1 < n_tiles)
    def _():
        idx_copy(i + 1, 1 - slot).start()
        tile_fetch(i + 1, (i + 1) % 3).start()

    idx_copy(i, slot).wait()
    tile_fetch(i, hslot).wait()
    row_copies_start(slot)

    def wait_rows(s):
        for _ in range(TOP_K):
            pltpu.make_async_copy(hbuf.at[0], x_hbm.at[pl.ds(0, MOE_TILE)], dsem.at[s]).wait()

    @pl.when(i >= 1)
    def _():
        wait_rows(1 - slot)

    @pl.when(i == n_tiles - 1)
    def _():
        wait_rows(slot)
        zrow[...] = jnp.zeros_like(zrow)

        def zero_copy(dst_row):
            return pltpu.make_async_copy(zrow.at[pl.ds(0, 1)], x_hbm.at[pl.ds(dst_row, 1)], zsem)

        def per_expert(e, carry):
            first = tab_ref[e] + tab_ref[N_EXPERTS + e]
            last = tab_ref[e] + tab_ref[2 * N_EXPERTS + e]
            lax.fori_loop(first, last, lambda r, c: (zero_copy(r).start(), c)[1], 0)
            lax.fori_loop(first, last, lambda r, c: (zero_copy(r).wait(), c)[1], 0)
            return carry
        lax.fori_loop(0, N_EXPERTS, per_expert, 0)


def _moe_dispatch(h2, idx_rec, tables, n_rows):
    n_tiles = h2.shape[0] // MOE_TILE
    grid_spec = pltpu.PrefetchScalarGridSpec(
        num_scalar_prefetch=1,
        grid=(n_tiles,),
        in_specs=[pl.BlockSpec(memory_space=pl.ANY),
                  pl.BlockSpec(memory_space=pl.ANY)],
        out_specs=pl.BlockSpec(memory_space=pl.ANY),
        scratch_shapes=[pltpu.SMEM((2 * IDX_REC,), jnp.int32),
                        pltpu.VMEM((3, MOE_TILE, D_MODEL), F32),
                        pltpu.VMEM((8, D_MODEL), F32),
                        pltpu.SemaphoreType.DMA((2,)),
                        pltpu.SemaphoreType.DMA((3,)),
                        pltpu.SemaphoreType.DMA((2,)),
                        pltpu.SemaphoreType.DMA],
    )
    return pl.pallas_call(
        functools.partial(_moe_dispatch_body, n_tiles=n_tiles),
        grid_spec=grid_spec,
        out_shape=jax.ShapeDtypeStruct((n_rows, D_MODEL), F32),
        compiler_params=_cparams(("arbitrary",)),
        name="moe_dispatch",
    )(tables, idx_rec, h2)


def _moe_experts_body(be_ref, nused_ref, x_ref, w1_ref, b1_ref, w2_ref, b2_ref, y_ref, w1b, w2b):
    i = pl.program_id(0)

    @pl.when(i < nused_ref[0])
    def _():
        @pl.when((i == 0) | (be_ref[i] != be_ref[jnp.maximum(i - 1, 0)]))
        def _():
            w1b[...] = w1_ref[0].astype(BF16)
            w2b[...] = w2_ref[0].astype(BF16)

        u = _dot(x_ref[...].astype(BF16), w1b[...]) + b1_ref[0]
        glu = jnp.minimum(u[:, :D_MODEL], SWIGLU_LIMIT)
        lin = jnp.clip(u[:, D_MODEL:], -SWIGLU_LIMIT, SWIGLU_LIMIT)
        act = glu * _sigmoid(SWIGLU_ALPHA * glu) * (lin + 1.0)
        y_ref[...] = _dot(act.astype(BF16), w2b[...]) + b2_ref[0]

    @pl.when(i >= nused_ref[0])
    def _():
        y_ref[...] = jnp.zeros_like(y_ref)


def _moe_experts(x_rows, block_e, n_used, w1, b1, w2, b2):
    n_blocks = block_e.shape[0]
    by_expert = lambda i, be, nu: (be[i], 0, 0)
    grid_spec = pltpu.PrefetchScalarGridSpec(
        num_scalar_prefetch=2,
        grid=(n_blocks,),
        in_specs=[pl.BlockSpec((MOE_BLOCK, D_MODEL), lambda i, be, nu: (jnp.minimum(i, nu[0] - 1), 0)),
                  pl.BlockSpec((1, D_MODEL, 2 * D_MODEL), by_expert),
                  pl.BlockSpec((1, 1, 2 * D_MODEL), by_expert),
                  pl.BlockSpec((1, D_MODEL, D_MODEL), by_expert),
                  pl.BlockSpec((1, 1, D_MODEL), by_expert)],
        out_specs=pl.BlockSpec((MOE_BLOCK, D_MODEL), lambda i, be, nu: (i, 0)),
        scratch_shapes=[pltpu.VMEM((D_MODEL, 2 * D_MODEL), BF16),
                        pltpu.VMEM((D_MODEL, D_MODEL), BF16)],
    )
    return pl.pallas_call(
        _moe_experts_body,
        grid_spec=grid_spec,
        out_shape=jax.ShapeDtypeStruct((n_blocks * MOE_BLOCK, D_MODEL), F32),
        compiler_params=_cparams(("arbitrary",)),
        name="moe_experts",
    )(block_e, n_used, x_rows, w1, b1, w2, b2)


def _route(top_e, rank, counts):
    n_tok = top_e.shape[1]
    counts = counts.reshape(N_EXPERTS).astype(jnp.int32)
    padded = (counts + MOE_BLOCK - 1) // MOE_BLOCK * MOE_BLOCK
    pad_end = jnp.cumsum(padded)
    pad_start = pad_end - padded
    experts = jnp.arange(N_EXPERTS, dtype=jnp.int32)
    dest = rank + jnp.sum(jnp.where(top_e[..., None] == experts, pad_start, 0), axis=-1)
    n_blocks = -(-(n_tok * TOP_K + N_EXPERTS * (MOE_BLOCK - 1)) // MOE_BLOCK)
    blk_row0 = jnp.arange(n_blocks, dtype=jnp.int32) * MOE_BLOCK
    block_e = jnp.minimum(jnp.sum(blk_row0[:, None] >= pad_end[None, :], axis=1), N_EXPERTS - 1).astype(jnp.int32)
    n_used = (pad_end[-1] // MOE_BLOCK).astype(jnp.int32).reshape(1)
    n_tiles = n_tok // MOE_TILE
    rec = dest.reshape(TOP_K, n_tiles, MOE_TILE).transpose(1, 0, 2).reshape(n_tiles, TOP_K * MOE_TILE)
    idx_rec = jnp.concatenate([rec, jnp.zeros((n_tiles, IDX_REC - TOP_K * MOE_TILE), jnp.int32)], axis=1).reshape(-1)
    tables = jnp.concatenate([pad_start, counts, padded]).astype(jnp.int32)
    return idx_rec, tables, block_e, n_used, n_blocks * MOE_BLOCK


def _tail_body(idx_hbm, y_hbm, x1_ref, tg_ref, pe_ref, gple_ref, wpg_ref, wp_ref, gfin_ref, yp_ref, ys_ref,
               idx_smem, gbuf, isem, gsem, *, n_tiles, n_prompt_tiles):
    i = pl.program_id(0)
    slot = i % 2

    def idx_copy(rec, s):
        return pltpu.make_async_copy(idx_hbm.at[pl.ds(pl.multiple_of(rec * IDX_REC, IDX_REC), IDX_REC)],
                                     idx_smem.at[pl.ds(pl.multiple_of(s * IDX_REC, IDX_REC), IDX_REC)],
                                     isem.at[s])

    @pl.when(i == 0)
    def _():
        idx_copy(0, 0).start()

    @pl.when(i < n_tiles)
    def _():
        idx_copy(i, slot).wait()

        @pl.when(i + 1 < n_tiles)
        def _():
            idx_copy(i + 1, 1 - slot).start()

        for t in range(MOE_TILE):
            for k in range(TOP_K):
                d = idx_smem[slot * IDX_REC + k * MOE_TILE + t]
                pltpu.make_async_copy(y_hbm.at[pl.ds(d, 1)], gbuf.at[slot, k, pl.ds(t, 1)], gsem.at[slot]).start()

    @pl.when(i >= 1)
    def _():
        prev = 1 - slot
        for k in range(TOP_K):
            pltpu.make_async_copy(y_hbm.at[pl.ds(0, MOE_TILE)], gbuf.at[prev, k], gsem.at[prev]).wait()
        tg = tg_ref[...]
        moe = tg[:, 0:1] * gbuf[prev, 0]
        for k in range(1, TOP_K):
            moe = moe + tg[:, k:k + 1] * gbuf[prev, k]
        x2 = x1_ref[...] + moe
        gate = _sigmoid(_dot(_rms(x2, gple_ref[...]).astype(BF16), wpg_ref[...]))
        x3 = x2 + gate * _dot(pe_ref[...].astype(BF16), wp_ref[...])
        y = _rms(x3, gfin_ref[...])

        @pl.when(i - 1 < n_prompt_tiles)
        def _():
            yp_ref[...] = y

        @pl.when(i - 1 >= n_prompt_tiles)
        def _():
            ys_ref[...] = y


def _tail(x1_all, y_rows, idx_rec, tg_t, pe_all, wts, n_p):
    n_all = x1_all.shape[0]
    n_tiles = n_all // MOE_TILE
    n_prompt_tiles = n_p // MOE_TILE
    row = lambda i: (jnp.maximum(i - 1, 0), 0)
    const = lambda i: (0, 0)
    body = functools.partial(_tail_body, n_tiles=n_tiles, n_prompt_tiles=n_prompt_tiles)
    return pl.pallas_call(
        body,
        grid=(n_tiles + 1,),
        in_specs=[pl.BlockSpec(memory_space=pl.ANY), pl.BlockSpec(memory_space=pl.ANY),
                  pl.BlockSpec((MOE_TILE, D_MODEL), row), pl.BlockSpec((MOE_TILE, TOP_K), row),
                  pl.BlockSpec((MOE_TILE, PLE_DIM), row), pl.BlockSpec((1, D_MODEL), const),
                  pl.BlockSpec((D_MODEL, D_MODEL), const), pl.BlockSpec((PLE_DIM, D_MODEL), const),
                  pl.BlockSpec((1, D_MODEL), const)],
        out_specs=[pl.BlockSpec((MOE_TILE, D_MODEL),
                                lambda i: (jnp.clip(i - 1, 0, n_prompt_tiles - 1), 0)),
                   pl.BlockSpec((MOE_TILE, D_MODEL), lambda i: (jnp.maximum(i - 1 - n_prompt_tiles, 0), 0))],
        out_shape=[jax.ShapeDtypeStruct((n_p, D_MODEL), F32),
                   jax.ShapeDtypeStruct((n_all - n_p, D_MODEL), F32)],
        scratch_shapes=[pltpu.SMEM((2 * IDX_REC,), jnp.int32),
                        pltpu.VMEM((2, TOP_K, MOE_TILE, D_MODEL), F32),
                        pltpu.SemaphoreType.DMA((2,)),
                        pltpu.SemaphoreType.DMA((2,))],
        compiler_params=_cparams(("arbitrary",)),
        name="tail",
    )(idx_rec, y_rows, x1_all, tg_t, pe_all, *wts)


RWKV_CHUNK = 128
RWKV_CHUNK_SAMPLE = 8
STATE_PASSES = 1
Y_PASSES = 1
TM_PROMPT = 512


def kernel(x_prompt, x_sample, p_prompt, p_sample, cache_kv_w128, cache_kv_w512, cache_kv_w2048, state_rwkv_shift, state_rwkv_wkv, norm_mix_g, w_in, rwkv_mu, rwkv_w0, rwkv_w2, rwkv_a0, rwkv_a2, rwkv_g2, rwkv_k_k, rwkv_k_a, rwkv_r_k, rwkv_ln_w, rwkv_ln_b, w_out_attn, w_out_rwkv, w_out, norm_ffn_g, router_w, router_b, moe_w1, moe_b1, moe_w2, moe_b2, norm_ple_g, w_ple, w_ple_gate, norm_final_g):
    bp, seq, _ = x_prompt.shape
    bs, t_s, _ = x_sample.shape
    n_p, n_s = bp * seq, bs * t_s
    n_all = n_p + n_s
    assert w_in.shape[0] == 1, "single layer"
    caches = (cache_kv_w128, cache_kv_w512, cache_kv_w2048)

    row = lambda a: a.reshape(1, -1)
    w_in_b = w_in[0].astype(BF16)
    zeros64 = jnp.zeros((LORA_W // 2, RWKV_W), F32)
    rwkv_wts = (row(rwkv_mu[0]), row(rwkv_w0[0]), jnp.concatenate([rwkv_w2[0], zeros64], axis=0),
                row(rwkv_a0[0]), jnp.concatenate([zeros64, rwkv_a2[0]], axis=0), rwkv_g2[0],
                row(rwkv_k_k[0]), row(rwkv_k_a[0]), row(rwkv_r_k[0]), row(rwkv_ln_w[0]), row(rwkv_ln_b[0]))
    mix_wts = (w_out_attn[0].astype(BF16), w_out_rwkv[0].astype(BF16), w_out[0].astype(BF16),
               row(norm_ffn_g[0]), router_w[0].T, router_b[0].reshape(N_EXPERTS, 1))
    tail_wts = (row(norm_ple_g[0]), w_ple_gate[0].astype(BF16), w_ple[0].astype(BF16), row(norm_final_g))
    g_mix = row(norm_mix_g[0])

    xp = x_prompt.reshape(n_p, D_MODEL)
    qd_p, kvd_p, kvt_p, zr_p, gate_p = _in_proj_dilated(xp, g_mix, w_in_b, TM_PROMPT, bp, seq)
    att_p = [_attn_prompt(qd_p[gi], kvd_p[gi], gi) for gi in range(N_GROUPS)]
    orw_p, wkv_p = _rwkv(zr_p.reshape(bp, seq, RWKV_COLS), jnp.zeros((bp, 1, RWKV_COLS), F32),
                         jnp.zeros((bp, RWKV_HEADS, RWKV_HEAD, RWKV_HEAD), F32), rwkv_wts, RWKV_CHUNK, seq)
    bufs, counts = _mix_out(xp, [a[0] for a in att_p], [a[1] for a in att_p], orw_p.reshape(n_p, RWKV_W), gate_p,
                            mix_wts, jnp.zeros((N_EXPERTS, 1), F32), TM_PROMPT, n_all, 0, None)

    xs = x_sample.reshape(n_s, D_MODEL)
    q_s, kv_s, zr_s, gate_s = _in_proj(xs, g_mix, w_in_b, n_s)
    t_pad = 8
    q_s3 = jnp.pad(q_s.reshape(bs, t_s, Q_COLS), ((0, 0), (0, t_pad - t_s), (0, 0)))
    feat_major = lambda a: jnp.swapaxes(a, 1, 2)
    tail_t = jnp.pad(feat_major(kv_s.reshape(bs, t_s, KV_COLS)), ((0, 0), (0, 0), (LANE - t_s, 0)))
    att_s, new_caches = [], []
    for gi in range(N_GROUPS):
        wc = caches[gi].shape[2]
        o, lse, newc = _attn_sample(q_s3, tail_t, feat_major(caches[gi].reshape(bs, wc, 2 * GROUP_W)), gi, t_s)
        att_s.append((o[:, :t_s].reshape(n_s, GROUP_W), lse[:, :t_s].reshape(n_s, GROUP_W)))
        new_caches.append(feat_major(newc).reshape(1, bs, wc, 2, HEADS_PER_GROUP, HEAD_DIM))
    zr_s3 = jnp.pad(zr_s.reshape(bs, t_s, RWKV_COLS), ((0, 0), (0, RWKV_CHUNK_SAMPLE - t_s), (0, 0)))
    orw_s, wkv_s = _rwkv(zr_s3, state_rwkv_shift[0].reshape(bs, 1, RWKV_COLS), state_rwkv_wkv[0], rwkv_wts,
                         RWKV_CHUNK_SAMPLE, t_s)
    bufs, counts = _mix_out(xs, [a[0] for a in att_s], [a[1] for a in att_s], orw_s[:, :t_s].reshape(n_s, RWKV_W),
                            gate_s, mix_wts, counts, n_s, n_all, n_p, bufs)
    x1_all, h2_all, top_e, top_g, rank = bufs

    idx_rec, tables, block_e, n_used, n_rows = _route(top_e, rank, counts)
    x_rows = _moe_dispatch(h2_all, idx_rec, tables, n_rows)
    y_rows = _moe_experts(x_rows, block_e, n_used, moe_w1[0], moe_b1[0].reshape(N_EXPERTS, 1, 2 * D_MODEL),
                          moe_w2[0], moe_b2[0].reshape(N_EXPERTS, 1, D_MODEL))
    pe_all = jnp.concatenate([p_prompt[0].reshape(n_p, PLE_DIM), p_sample[0].reshape(n_s, PLE_DIM)], axis=0)
    y_p, y_s = _tail(x1_all, y_rows, idx_rec, top_g.T, pe_all, tail_wts, n_p)

    kv_out_p = [feat_major(t).reshape(1, bp, t.shape[2], 2, HEADS_PER_GROUP, HEAD_DIM) for t in kvt_p]
    shift_p = zr_p.reshape(bp, seq, RWKV_COLS)[:, -1][None]
    shift_s = zr_s.reshape(bs, t_s, RWKV_COLS)[:, -1][None]
    return (y_p.reshape(bp, seq, D_MODEL), y_s.reshape(bs, t_s, D_MODEL),
            kv_out_p[0], kv_out_p[1], kv_out_p[2], shift_p, wkv_p[None],
            new_caches[0], new_caches[1], new_caches[2], shift_s, wkv_s[None])
```

```python
import functools
import math

import numpy as np
import jax
import jax.numpy as jnp
from jax import lax
from jax.experimental import pallas as pl
from jax.experimental.pallas import tpu as pltpu

F32 = jnp.float32
BF16 = jnp.bfloat16

LANE = 128
D_MODEL = 1024
N_GROUPS = 3
HEADS_PER_GROUP = 4
HEAD_DIM = 64
ATTN_GROUPS = ((128, 1), (512, 4), (2048, 16))
GROUP_W = HEADS_PER_GROUP * HEAD_DIM
Q_COLS = N_GROUPS * GROUP_W
KV_COLS = 2 * Q_COLS
BAND = 128

RWKV_HEADS = 8
RWKV_HEAD = 64
RWKV_W = 512
LORA_W = 128
GATE_LORA = 128
RWKV_COLS = 3 * RWKV_W + LORA_W + GATE_LORA
GN_EPS = 64e-5
GATE_COLS = 2 * D_MODEL
Z_RWKV0 = 3 * Q_COLS
Z_GATE0 = Z_RWKV0 + RWKV_COLS
IN_COLS = Z_GATE0 + GATE_COLS

N_EXPERTS = 32
TOP_K = 4
SWIGLU_LIMIT = 7.0
SWIGLU_ALPHA = 1.702
MOE_BLOCK = 512
MOE_TILE = 128
DISPATCH_LAG = 2
IDX_REC = 1024
PLE_DIM = 256
RMS_EPS = 1e-6

NEG_BIG = -1e30
VMEM_LIMIT = 56 * 1024 * 1024


def _cparams(sem):
    return pltpu.CompilerParams(dimension_semantics=sem, vmem_limit_bytes=VMEM_LIMIT)


def _rms(x, g):
    return x * lax.rsqrt(jnp.mean(x * x, axis=-1, keepdims=True) + RMS_EPS) * g


def _sigmoid(x):
    return 1.0 / (1.0 + jnp.exp(-x))


def _dot(a, b):
    return jnp.dot(a, b, preferred_element_type=F32)


def _dot_nt(a, b):
    return lax.dot_general(a, b, (((1,), (1,)), ((), ())), preferred_element_type=F32)


def _dot_tn(a, b):
    return lax.dot_general(a, b, (((0,), (0,)), ((), ())), preferred_element_type=F32)


def _split(x):
    hi = x.astype(BF16)
    lo = (x - hi.astype(F32)).astype(BF16)
    return hi, lo


def _mm3(dot, a, b):
    ah, al = _split(a)
    bh, bl = _split(b)
    return dot(ah, bh) + (dot(ah, bl) + dot(al, bh))


def _parts(x, n):
    out = []
    for _ in range(n - 1):
        hi = x.astype(BF16)
        out.append(hi)
        x = x - hi.astype(F32)
    out.append(x.astype(BF16))
    return out


def _mmp(dot, ap, bp):
    order = max(len(ap), len(bp))
    acc = None
    for i, a in enumerate(ap):
        for j, b in enumerate(bp):
            if i + j < order:
                t = dot(a, b)
                acc = t if acc is None else acc + t
    return acc


def _in_proj_body(x_ref, g_ref, w_ref, q_ref, kv_ref, zr_ref, gate_ref):
    h = _rms(x_ref[...], g_ref[...]).astype(BF16)

    def proj(lo, width):
        return _dot(h, w_ref[:, lo:lo + width])

    q_ref[...] = (proj(0, Q_COLS) * (1.0 / math.sqrt(HEAD_DIM))).astype(BF16)
    for g in range(N_GROUPS):
        kv_ref[:, 2 * g * GROUP_W:(2 * g + 1) * GROUP_W] = proj(Q_COLS + g * GROUP_W, GROUP_W)
        kv_ref[:, (2 * g + 1) * GROUP_W:(2 * g + 2) * GROUP_W] = proj(2 * Q_COLS + g * GROUP_W, GROUP_W)
    zr_ref[...] = proj(Z_RWKV0, RWKV_COLS)
    gate_ref[...] = _sigmoid(proj(Z_GATE0, GATE_COLS)).astype(BF16)


def _in_proj_dilated_body(x_ref, g_ref, w_ref, *rest, tiles_per_seq):
    qd_refs, kvd_refs, kvt_refs = rest[0:N_GROUPS], rest[N_GROUPS:2 * N_GROUPS], rest[2 * N_GROUPS:3 * N_GROUPS]
    zr_ref, gate_ref, st_ref = rest[3 * N_GROUPS:]
    tm = x_ref.shape[0]
    tile_in_seq = pl.program_id(0) % tiles_per_seq
    h = _rms(x_ref[...], g_ref[...]).astype(BF16)

    def proj(lo, width):
        return _dot(h, w_ref[:, lo:lo + width])

    n_q = Q_COLS // LANE

    def stage(slab0, val):
        for s in range(val.shape[1] // LANE):
            st_ref[slab0 + s] = val[:, s * LANE:(s + 1) * LANE]

    stage(0, proj(0, Q_COLS) * (1.0 / math.sqrt(HEAD_DIM)))
    for g in range(N_GROUPS):
        for part, src in ((0, Q_COLS), (1, 2 * Q_COLS)):
            col = (2 * g + part) * GROUP_W
            stage(n_q + col // LANE, proj(src + g * GROUP_W, GROUP_W))
    for g, (window, _) in enumerate(ATTN_GROUPS):
        rows = min(window, tm)
        first_tile = tiles_per_seq - max(window // tm, 1)

        @pl.when(tile_in_seq >= first_tile)
        def _():
            for s in range(2 * GROUP_W // LANE):
                slab = st_ref[n_q + 2 * g * GROUP_W // LANE + s, tm - rows:tm, :]
                kvt_refs[g][0, s * LANE:(s + 1) * LANE, :] = slab.T
    for g, (_, dil) in enumerate(ATTN_GROUPS):
        for r in range(dil):
            rows = pl.ds(r, tm // dil, stride=dil) if dil > 1 else slice(None)
            for s in range(GROUP_W // LANE):
                qd_refs[g][0, r, :, s * LANE:(s + 1) * LANE] = st_ref[g * GROUP_W // LANE + s, rows, :].astype(BF16)
            for s in range(2 * GROUP_W // LANE):
                kvd_refs[g][0, r, :, s * LANE:(s + 1) * LANE] = \
                    st_ref[n_q + 2 * g * GROUP_W // LANE + s, rows, :].astype(BF16)
    zr_ref[...] = proj(Z_RWKV0, RWKV_COLS)
    gate_ref[...] = _sigmoid(proj(Z_GATE0, GATE_COLS)).astype(BF16)


def _in_proj_dilated(x, g, w_bf16, tm, batch, seq):
    n = x.shape[0]
    tiles = seq // tm
    row = lambda i: (i, 0)
    const = lambda i: (0, 0)
    sub = lambda i: (i // tiles, 0, i % tiles, 0)
    dils = [d for _, d in ATTN_GROUPS]
    assert all(tm % (16 * d) == 0 for d in dils)
    wins = [min(w, seq) for w, _ in ATTN_GROUPS]
    assert all(w % tm == 0 or tm % w == 0 for w in wins)

    def tail_spec(w):
        cols = min(w, tm)
        first = tiles - max(w // tm, 1)
        return pl.BlockSpec((1, 2 * GROUP_W, cols), lambda i: (i // tiles, 0, jnp.maximum(i % tiles - first, 0)))

    outs = pl.pallas_call(
        functools.partial(_in_proj_dilated_body, tiles_per_seq=tiles),
        grid=(n // tm,),
        in_specs=[pl.BlockSpec((tm, D_MODEL), row),
                  pl.BlockSpec((1, D_MODEL), const),
                  pl.BlockSpec((D_MODEL, IN_COLS), const, pipeline_mode=pl.Buffered(1))],
        out_specs=[pl.BlockSpec((1, d, tm // d, GROUP_W), sub) for d in dils]
                  + [pl.BlockSpec((1, d, tm // d, 2 * GROUP_W), sub) for d in dils]
                  + [tail_spec(w) for w in wins]
                  + [pl.BlockSpec((tm, RWKV_COLS), row),
                     pl.BlockSpec((tm, GATE_COLS), row)],
        out_shape=[jax.ShapeDtypeStruct((batch, d, seq // d, GROUP_W), BF16) for d in dils]
                  + [jax.ShapeDtypeStruct((batch, d, seq // d, 2 * GROUP_W), BF16) for d in dils]
                  + [jax.ShapeDtypeStruct((batch, 2 * GROUP_W, w), F32) for w in wins]
                  + [jax.ShapeDtypeStruct((n, RWKV_COLS), F32),
                     jax.ShapeDtypeStruct((n, GATE_COLS), BF16)],
        scratch_shapes=[pltpu.VMEM(((Q_COLS + KV_COLS) // LANE, tm, LANE), F32)],
        compiler_params=_cparams(("arbitrary",)),
        name="in_proj_dilated",
    )(x, g, w_bf16)
    g3 = N_GROUPS
    return outs[0:g3], outs[g3:2 * g3], outs[2 * g3:3 * g3], outs[3 * g3], outs[3 * g3 + 1]


def _in_proj(x, g, w_bf16, tm):
    n = x.shape[0]
    row = lambda i: (i, 0)
    const = lambda i: (0, 0)
    return pl.pallas_call(
        _in_proj_body,
        grid=(n // tm,),
        in_specs=[pl.BlockSpec((tm, D_MODEL), row),
                  pl.BlockSpec((1, D_MODEL), const),
                  pl.BlockSpec((D_MODEL, IN_COLS), const)],
        out_specs=[pl.BlockSpec((tm, Q_COLS), row),
                   pl.BlockSpec((tm, KV_COLS), row),
                   pl.BlockSpec((tm, RWKV_COLS), row),
                   pl.BlockSpec((tm, GATE_COLS), row)],
        out_shape=[jax.ShapeDtypeStruct((n, Q_COLS), BF16),
                   jax.ShapeDtypeStruct((n, KV_COLS), F32),
                   jax.ShapeDtypeStruct((n, RWKV_COLS), F32),
                   jax.ShapeDtypeStruct((n, GATE_COLS), BF16)],
        compiler_params=_cparams(("parallel",)),
        name="in_proj",
    )(x, g, w_bf16)


def _alibi_slopes(gi):
    return [2.0 ** (-8.0 * (gi * HEADS_PER_GROUP + h + 1) / (N_GROUPS * HEADS_PER_GROUP))
            for h in range(HEADS_PER_GROUP)]


def _head_of_lane(shape):
    return lax.broadcasted_iota(jnp.int32, shape, len(shape) - 1) // HEAD_DIM


def _stack_heads(q):
    hl = _head_of_lane(q.shape)
    return jnp.concatenate([jnp.where(hl == h, q, jnp.zeros_like(q)) for h in range(HEADS_PER_GROUP)], axis=0)


def _unstack_heads(x4, rows):
    hl = _head_of_lane((rows, GROUP_W))
    out = x4[0:rows]
    for h in range(1, HEADS_PER_GROUP):
        out = jnp.where(hl == h, x4[h * rows:(h + 1) * rows], out)
    return out


def _attn_prompt_body(q_ref, kvp_ref, kvc_ref, o_ref, lse_ref, kb_ref, bias_ref, *, dil, slopes, n_sub):
    c = pl.program_id(2)
    kb_ref[0:BAND, :] = kvp_ref[0, 0]
    kb_ref[BAND:, :] = kvc_ref[0, 0]

    qi = lax.broadcasted_iota(jnp.int32, (BAND, 2 * BAND), 0) + BAND
    ki = lax.broadcasted_iota(jnp.int32, (BAND, 2 * BAND), 1)
    rel = qi - ki
    dist = jnp.where((rel >= 0) & (rel <= BAND), (dil * rel).astype(F32), -NEG_BIG / slopes[-1])
    for h in range(HEADS_PER_GROUP):
        bias_ref[h * BAND:(h + 1) * BAND, :] = -slopes[h] * dist

    def sub_block(n, carry):
        r0 = pl.multiple_of(n * BAND, BAND)
        q4 = _stack_heads(q_ref[0, 0, pl.ds(r0, BAND), :])
        kv = kb_ref[pl.ds(r0, 2 * BAND), :]
        s = _dot_nt(q4, kv[:, :GROUP_W]) + bias_ref[...]
        n_before_start = jnp.where((c == 0) & (n == 0), BAND, 0)
        kcol = lax.broadcasted_iota(jnp.int32, s.shape, 1)
        s = jnp.where(kcol < n_before_start, NEG_BIG, s)
        m = jnp.max(s, axis=-1, keepdims=True)
        e = jnp.exp(s - m)
        den = jnp.sum(e, axis=-1, keepdims=True)
        o4 = _dot(e.astype(BF16), kv[:, GROUP_W:]) * (1.0 / den)
        o_ref[0, 0, pl.ds(r0, BAND), :] = _unstack_heads(o4, BAND).astype(o_ref.dtype)
        lse4 = jnp.broadcast_to(m + jnp.log(den), (HEADS_PER_GROUP * BAND, GROUP_W))
        lse_ref[0, 0, pl.ds(r0, BAND), :] = _unstack_heads(lse4, BAND)
        return carry

    lax.fori_loop(0, n_sub, sub_block, 0, unroll=min(n_sub, 4))


def _attn_prompt(qd, kvd, gi):
    window, dil = ATTN_GROUPS[gi]
    assert window // dil == BAND
    batch, _, sub_len, _ = qd.shape
    chunk = min(sub_len, 1024)
    n_chunks = sub_len // chunk
    sub_per_chunk = chunk // BAND
    body = functools.partial(_attn_prompt_body, dil=dil, slopes=_alibi_slopes(gi), n_sub=sub_per_chunk)
    cur = lambda b, r, c: (b, r, c, 0)
    return pl.pallas_call(
        body,
        grid=(batch, dil, n_chunks),
        in_specs=[
            pl.BlockSpec((1, 1, chunk, GROUP_W), cur),
            pl.BlockSpec((1, 1, BAND, 2 * GROUP_W),
                         lambda b, r, c: (b, r, jnp.maximum(c * sub_per_chunk - 1, 0), 0)),
            pl.BlockSpec((1, 1, chunk, 2 * GROUP_W), cur),
        ],
        out_specs=[pl.BlockSpec((1, 1, chunk, GROUP_W), cur), pl.BlockSpec((1, 1, chunk, GROUP_W), cur)],
        out_shape=[jax.ShapeDtypeStruct(qd.shape, BF16), jax.ShapeDtypeStruct(qd.shape, F32)],
        scratch_shapes=[pltpu.VMEM((chunk + BAND, 2 * GROUP_W), BF16),
                        pltpu.VMEM((HEADS_PER_GROUP * BAND, 2 * BAND), F32)],
        compiler_params=_cparams(("parallel", "parallel", "arbitrary")),
        name=f"attn_prompt_g{gi}",
    )(qd, kvd, kvd)


def _attn_sample_body(q_ref, tail_ref, cache_ref, o_ref, lse_ref, newc_ref, *, dil, slopes, wc, t_new):
    tp = q_ref.shape[1]
    cache = cache_ref[0]
    tail = tail_ref[0]
    newc_ref[0] = pltpu.roll(cache, wc - t_new, 1)
    lane = lax.broadcasted_iota(jnp.int32, tail.shape, 1)
    newc_ref[0, :, wc - LANE:wc] = jnp.where(lane >= LANE - t_new, tail, newc_ref[0, :, wc - LANE:wc])

    q4 = _stack_heads(q_ref[0])
    cb = cache.astype(BF16)
    nb = tail.astype(BF16)
    t_row = lax.broadcasted_iota(jnp.int32, (tp, 1), 0)
    slope_col = jnp.concatenate([jnp.full((tp, 1), s, F32) for s in slopes], axis=0)
    t4 = jnp.concatenate([t_row] * HEADS_PER_GROUP, axis=0)

    def scores(keys_t_bf16, dist):
        ok = (dist >= 0) & ((dist & (dil - 1)) == 0) & (dist <= BAND * dil)
        s = _dot(q4, keys_t_bf16[:GROUP_W, :])
        return jnp.where(ok, s - slope_col * dist.astype(F32), NEG_BIG)

    pc = lax.broadcasted_iota(jnp.int32, (HEADS_PER_GROUP * tp, wc), 1)
    pn = lax.broadcasted_iota(jnp.int32, (HEADS_PER_GROUP * tp, LANE), 1) - (LANE - t_new)
    s_c = scores(cb, wc + t4 - pc)
    s_n = scores(nb, jnp.where(pn >= 0, t4 - pn, -1))
    m = jnp.maximum(jnp.max(s_c, axis=-1, keepdims=True), jnp.max(s_n, axis=-1, keepdims=True))
    e_c = jnp.exp(s_c - m)
    e_n = jnp.exp(s_n - m)
    den = jnp.sum(e_c, axis=-1, keepdims=True) + jnp.sum(e_n, axis=-1, keepdims=True)
    o4 = (_dot_nt(e_c.astype(BF16), cb[GROUP_W:, :]) + _dot_nt(e_n.astype(BF16), nb[GROUP_W:, :])) * (1.0 / den)
    o_ref[0] = _unstack_heads(o4, tp).astype(o_ref.dtype)
    lse4 = jnp.broadcast_to(m + jnp.log(den), (HEADS_PER_GROUP * tp, GROUP_W))
    lse_ref[0] = _unstack_heads(lse4, tp)


def _attn_sample(q, tail_t, cache_t, gi, t_new):
    window, dil = ATTN_GROUPS[gi]
    batch, tp, _ = q.shape
    wc = cache_t.shape[2]
    assert wc + 0 - dil * BAND >= 0
    body = functools.partial(_attn_sample_body, dil=dil, slopes=_alibi_slopes(gi), wc=wc, t_new=t_new)
    return pl.pallas_call(
        body,
        grid=(batch,),
        in_specs=[pl.BlockSpec((1, tp, GROUP_W), lambda b: (b, 0, gi)),
                  pl.BlockSpec((1, 2 * GROUP_W, LANE), lambda b: (b, gi, 0)),
                  pl.BlockSpec((1, 2 * GROUP_W, wc), lambda b: (b, 0, 0))],
        out_specs=[pl.BlockSpec((1, tp, GROUP_W), lambda b: (b, 0, 0)),
                   pl.BlockSpec((1, tp, GROUP_W), lambda b: (b, 0, 0)),
                   pl.BlockSpec((1, 2 * GROUP_W, wc), lambda b: (b, 0, 0))],
        out_shape=[jax.ShapeDtypeStruct((batch, tp, GROUP_W), BF16),
                   jax.ShapeDtypeStruct((batch, tp, GROUP_W), F32),
                   jax.ShapeDtypeStruct((batch, 2 * GROUP_W, wc), F32)],
        compiler_params=_cparams(("parallel",)),
        name=f"attn_sample_g{gi}",
    )(q, tail_t, cache_t)


def _rwkv_body(zr_ref, shift0_ref, wkv0_ref, mu_ref, w0_ref, w2_ref, a0_ref, a2_ref, g2_ref, kk_ref, ka_ref,
               rk_ref, lnw_ref, lnb_ref, o_ref, wkv_ref, prev_ref, *, chunk, t_valid):
    c = pl.program_id(1)
    L = chunk
    mm = functools.partial(_mm3, _dot)
    heads = range(RWKV_HEADS)
    sls = [slice(h * RWKV_HEAD, (h + 1) * RWKV_HEAD) for h in heads]

    @pl.when(c == 0)
    def _():
        prev_ref[...] = shift0_ref[0]
        wkv_ref[0] = wkv0_ref[0]

    zr = zr_ref[0]
    row = lax.broadcasted_iota(jnp.int32, (L, 1), 0)
    prev = jnp.where(row == 0, prev_ref[...], pltpu.roll(zr, 1, 0))
    prev_ref[...] = zr[L - 1:L, :]
    zm = zr + (prev - zr) * mu_ref[...]

    r = zm[:, 0:RWKV_W]
    k = zm[:, RWKV_W:2 * RWKV_W]
    v = zm[:, 2 * RWKV_W:3 * RWKV_W]
    xwa = zm[:, 3 * RWKV_W:3 * RWKV_W + LORA_W]
    xg = zm[:, 3 * RWKV_W + LORA_W:]
    w_pre = w0_ref[...] + mm(jnp.tanh(xwa), w2_ref[...])
    softplus = jnp.maximum(-w_pre, 0.0) + jnp.log(1.0 + jnp.exp(-jnp.abs(w_pre)))
    w_log = -softplus - 0.5
    lw = -jnp.exp(w_log)
    a = _sigmoid(a0_ref[...] + mm(xwa, a2_ref[...]))
    g = mm(_sigmoid(xg), g2_ref[...])
    kk_raw = k * kk_ref[...]
    k2 = k * (1.0 + (a - 1.0) * ka_ref[...])
    if t_valid < L:
        live = row < t_valid
        lw = jnp.where(live, lw, 0.0)
        kk_raw = jnp.where(live, kk_raw, 0.0)
        k2 = jnp.where(live, k2, 0.0)
        v = jnp.where(live, v, 0.0)

    ti = lax.broadcasted_iota(jnp.int32, (L, L), 0)
    si = lax.broadcasted_iota(jnp.int32, (L, L), 1)
    incl = ti >= si
    strict = ti > si
    clw = _mmp(_dot, [jnp.where(incl, 1.0, 0.0).astype(BF16)], _parts(lw, 3))
    mid = max(L // 2 - 1, 0)
    clw_mid = clw[mid:mid + 1, :]
    rel = clw - clw_mid
    p_incl = jnp.exp(rel)
    p_inv = jnp.exp(-rel)
    p_prev = jnp.exp(rel - lw)
    p_mid = jnp.exp(clw_mid)
    rt_all = r * p_incl
    kt_all = k2 * p_inv
    rk_all = r * k2 * rk_ref[...]

    sp, yp = STATE_PASSES, Y_PASSES
    kkh = [kk_raw[:, s] for s in sls]
    kkn = [x / jnp.maximum(jnp.sqrt(jnp.sum(x * x, axis=-1, keepdims=True)), 1e-12) for x in kkh]
    at = [-kkn[h] * p_prev[:, sls[h]] for h in heads]
    bt = [kkn[h] * a[:, sls[h]] * p_inv[:, sls[h]] for h in heads]
    rt = [rt_all[:, s] for s in sls]
    kt = [kt_all[:, s] for s in sls]
    vh = [v[:, s] for s in sls]
    at_s = [_parts(x, sp) for x in at]
    bt_s = [_parts(x, sp) for x in bt]
    kt_s = [_parts(x, sp) for x in kt]
    vh_s = [_parts(x, sp) for x in vh]
    rt_y = [_parts(x, yp) for x in rt]
    a_ab = [jnp.where(strict, _mmp(_dot_nt, at_s[h], bt_s[h]), 0.0) for h in heads]
    a_ak = [jnp.where(strict, _mmp(_dot_nt, at_s[h], kt_s[h]), 0.0) for h in heads]
    a_rb = [jnp.where(incl, _mmp(_dot_nt, rt_y[h], bt_s[h][:yp]), 0.0) for h in heads]
    a_rk = [jnp.where(incl, _mmp(_dot_nt, rt_y[h], kt_s[h][:yp]), 0.0) for h in heads]
    x = [jnp.concatenate([at[h], _mmp(_dot, _parts(a_ak[h], sp), vh_s[h])], axis=1) for h in heads]
    p = a_ab
    for level in range(max(int(math.log2(L)), 1)):
        if level > 0:
            p = [_mmp(_dot, ps, ps) for ps in p_s]
        p_s = [_parts(q, sp) for q in p]
        x = [x[h] + _mmp(_dot, p_s[h], _parts(x[h], sp)) for h in heads]
    x_s = [_parts(q, sp) for q in x]
    qy = [_mmp(_dot, _parts(a_rb[h], yp), x_s[h][:yp]) for h in heads]
    y0 = [qy[h][:, RWKV_HEAD:] + _mmp(_dot, _parts(a_rk[h], yp), vh_s[h][:yp]) for h in heads]
    qh = [rt[h] + qy[h][:, :RWKV_HEAD] for h in heads]
    s0 = [wkv_ref[0, h] * p_mid[:, sls[h]] for h in heads]
    s0_s = [_parts(q, sp) for q in s0]
    y = [_mmp(_dot_nt, _parts(qh[h], yp), s0_s[h][:yp]) + y0[h] for h in heads]
    wtb = [_mmp(_dot_tn, [q[:, :RWKV_HEAD] for q in x_s[h]], bt_s[h]) for h in heads]
    uv_s = [[jnp.concatenate([x_s[h][i][:, RWKV_HEAD:], vh_s[h][i]], axis=0) for i in range(sp)] for h in heads]
    bk_s = [[jnp.concatenate([bt_s[h][i], kt_s[h][i]], axis=0) for i in range(sp)] for h in heads]
    for h in heads:
        s_new = s0[h] + _mmp(_dot, s0_s[h], _parts(wtb[h], sp)) + _mmp(_dot_tn, uv_s[h], bk_s[h])
        wkv_ref[0, h] = s_new * p_incl[L - 1:L, sls[h]]
    for h in heads:
        mu_y = jnp.mean(y[h], axis=-1, keepdims=True)
        yc = y[h] - mu_y
        var = jnp.mean(yc * yc, axis=-1, keepdims=True)
        yn = yc * lax.rsqrt(var + GN_EPS) * lnw_ref[:, sls[h]] + lnb_ref[:, sls[h]]
        bonus = jnp.sum(rk_all[:, sls[h]], axis=-1, keepdims=True) * vh[h]
        o_ref[0, :, sls[h]] = ((yn + bonus) * g[:, sls[h]]).astype(o_ref.dtype)


def _rwkv(zr, shift0, wkv0, wts, chunk, t_valid):
    batch, t, _ = zr.shape
    n_chunks = t // chunk
    assert t_valid == t or n_chunks == 1
    body = functools.partial(_rwkv_body, chunk=chunk, t_valid=min(t_valid, chunk))
    vec = lambda width: pl.BlockSpec((1, width), lambda b, c: (0, 0))
    mat = lambda rows: pl.BlockSpec((rows, RWKV_W), lambda b, c: (0, 0))
    return pl.pallas_call(
        body,
        grid=(batch, n_chunks),
        in_specs=[pl.BlockSpec((1, chunk, RWKV_COLS), lambda b, c: (b, c, 0)),
                  pl.BlockSpec((1, 1, RWKV_COLS), lambda b, c: (b, 0, 0)),
                  pl.BlockSpec((1, RWKV_HEADS, RWKV_HEAD, RWKV_HEAD), lambda b, c: (b, 0, 0, 0)),
                  vec(RWKV_COLS), vec(RWKV_W), mat(LORA_W), vec(RWKV_W), mat(LORA_W), mat(GATE_LORA),
                  vec(RWKV_W), vec(RWKV_W), vec(RWKV_W), vec(RWKV_W), vec(RWKV_W)],
        out_specs=[pl.BlockSpec((1, chunk, RWKV_W), lambda b, c: (b, c, 0)),
                   pl.BlockSpec((1, RWKV_HEADS, RWKV_HEAD, RWKV_HEAD), lambda b, c: (b, 0, 0, 0))],
        out_shape=[jax.ShapeDtypeStruct((batch, t, RWKV_W), BF16),
                   jax.ShapeDtypeStruct((batch, RWKV_HEADS, RWKV_HEAD, RWKV_HEAD), F32)],
        scratch_shapes=[pltpu.VMEM((1, RWKV_COLS), F32)],
        compiler_params=_cparams(("parallel", "arbitrary")),
        name="rwkv_scan",
    )(zr, shift0, wkv0, *wts)


def _mix_out_body(x_ref, o0_ref, o1_ref, o2_ref, l0_ref, l1_ref, l2_ref, orw_ref, gate_ref, woa_ref, wor_ref,
                  wo_ref, gffn_ref, rwt_ref, rb_ref, cnt0_ref, *rest, dilated, n_alias):
    x1_ref, h2_ref, te_ref, tg_ref, rk_ref, cnt_ref = rest[n_alias:n_alias + 6]
    stage = list(rest[n_alias + 6:])

    def token_major(ref, gi):
        if not dilated:
            return ref[...].astype(F32)
        dil = ATTN_GROUPS[gi][1]
        if dil == 1:
            return ref[0, 0].astype(F32)
        st_ref = stage.pop()
        for r in range(dil):
            sub = ref[0, r].astype(F32)
            for s in range(GROUP_W // LANE):
                st_ref[s, pl.ds(r, ref.shape[2], stride=dil), :] = sub[:, s * LANE:(s + 1) * LANE]
        return jnp.concatenate([st_ref[s] for s in range(GROUP_W // LANE)], axis=1)

    l0, l1, l2 = token_major(l0_ref, 0), token_major(l1_ref, 1), token_major(l2_ref, 2)
    m = jnp.maximum(jnp.maximum(l0, l1), l2)
    e0, e1, e2 = jnp.exp(l0 - m), jnp.exp(l1 - m), jnp.exp(l2 - m)
    o_att = (e0 * token_major(o0_ref, 0) + e1 * token_major(o1_ref, 1) + e2 * token_major(o2_ref, 2)) \
        * (1.0 / (e0 + e1 + e2))
    gates = gate_ref[...].astype(F32)
    merged = gates[:, :D_MODEL] * _dot(o_att.astype(BF16), woa_ref[...]) \
        + gates[:, D_MODEL:] * _dot(orw_ref[...], wor_ref[...])
    x1 = x_ref[...] + _dot(merged.astype(BF16), wo_ref[...])
    x1_ref[...] = x1
    h2 = _rms(x1, gffn_ref[...])
    h2_ref[...] = h2

    logits = _mm3(_dot_nt, rwt_ref[...], h2) + rb_ref[...]
    e_iota = lax.broadcasted_iota(jnp.int32, logits.shape, 0)
    vals, idxs = [], []
    for _ in range(TOP_K):
        top = jnp.max(logits, axis=0, keepdims=True)
        idx = jnp.min(jnp.where(logits == top, e_iota, N_EXPERTS), axis=0, keepdims=True)
        vals.append(top)
        idxs.append(idx)
        logits = jnp.where(e_iota == idx, -jnp.inf, logits)
    exps = [jnp.exp(t - vals[0]) for t in vals]
    inv = 1.0 / (exps[0] + exps[1] + exps[2] + exps[3])
    te_ref[...] = jnp.concatenate(idxs, axis=0)
    tg_ref[...] = jnp.concatenate([e * inv for e in exps], axis=0)

    @pl.when(pl.program_id(0) == 0)
    def _():
        cnt_ref[...] = cnt0_ref[...]

    hits = [e_iota == idx for idx in idxs]
    onehot = jnp.where(hits[0] | hits[1] | hits[2] | hits[3], 1.0, 0.0)
    tm = onehot.shape[1]
    earlier = lax.broadcasted_iota(jnp.int32, (tm, tm), 0) < lax.broadcasted_iota(jnp.int32, (tm, tm), 1)
    before = cnt_ref[...] + _dot(onehot.astype(BF16), jnp.where(earlier, 1.0, 0.0).astype(BF16))
    rk_ref[...] = jnp.concatenate([jnp.sum(jnp.where(h, before, 0.0), axis=0, keepdims=True) for h in hits],
                                  axis=0).astype(jnp.int32)
    cnt_ref[...] += jnp.sum(onehot, axis=1, keepdims=True)


def _mix_out(x, o_g, lse_g, o_rwkv, gates, wts, cnt0, tm, n_all, row0, bufs):
    n = x.shape[0]
    blk0 = row0 // tm
    row = lambda i: (i, 0)
    const = lambda i: (0, 0)
    out_row = lambda i: (i + blk0, 0)
    out_col = lambda i: (0, i + blk0)
    tok = lambda w: pl.BlockSpec((tm, w), row)
    dilated = o_g[0].ndim == 4
    if dilated:
        tiles = o_g[0].shape[1] * o_g[0].shape[2] // tm
        sub = lambda i: (i // tiles, 0, i % tiles, 0)
        att_specs = [pl.BlockSpec((1, d, tm // d, GROUP_W), sub) for _, d in ATTN_GROUPS] * 2
        stage = [pltpu.VMEM((GROUP_W // LANE, tm, LANE), F32) for _, d in ATTN_GROUPS if d > 1] * 2
    else:
        att_specs, stage = [tok(GROUP_W)] * 6, []
    in_specs = [tok(D_MODEL)] + att_specs + [tok(RWKV_W), tok(GATE_COLS),
                pl.BlockSpec((GROUP_W, D_MODEL), const), pl.BlockSpec((RWKV_W, D_MODEL), const),
                pl.BlockSpec((D_MODEL, D_MODEL), const), pl.BlockSpec((1, D_MODEL), const),
                pl.BlockSpec((N_EXPERTS, D_MODEL), const), pl.BlockSpec((N_EXPERTS, 1), const),
                pl.BlockSpec((N_EXPERTS, 1), const)]
    args = [x, *o_g, *lse_g, o_rwkv, gates, *wts, cnt0]
    aliases = {}
    if bufs is not None:
        in_specs += [pl.BlockSpec(memory_space=pl.ANY)] * 5
        aliases = {len(args) + j: j for j in range(5)}
        args += list(bufs)
    *new_bufs, cnt = pl.pallas_call(
        functools.partial(_mix_out_body, dilated=dilated, n_alias=len(aliases)),
        grid=(n // tm,),
        scratch_shapes=stage,
        in_specs=in_specs,
        out_specs=[pl.BlockSpec((tm, D_MODEL), out_row), pl.BlockSpec((tm, D_MODEL), out_row),
                   pl.BlockSpec((TOP_K, tm), out_col), pl.BlockSpec((TOP_K, tm), out_col),
                   pl.BlockSpec((TOP_K, tm), out_col), pl.BlockSpec((N_EXPERTS, 1), const)],
        out_shape=[jax.ShapeDtypeStruct((n_all, D_MODEL), F32), jax.ShapeDtypeStruct((n_all, D_MODEL), F32),
                   jax.ShapeDtypeStruct((TOP_K, n_all), jnp.int32), jax.ShapeDtypeStruct((TOP_K, n_all), F32),
                   jax.ShapeDtypeStruct((TOP_K, n_all), jnp.int32), jax.ShapeDtypeStruct((N_EXPERTS, 1), F32)],
        input_output_aliases=aliases,
        compiler_params=_cparams(("arbitrary",)),
        name="mix_out",
    )(*args)
    return new_bufs, cnt


def _moe_dispatch_body(tab_ref, idx_hbm, h2_hbm, x_hbm, idx_smem, hbuf, zrow, isem, fsem, dsem, zsem, *, n_tiles):
    i = pl.program_id(0)
    slot = i % 2
    n_buf, n_sem = DISPATCH_LAG + 2, DISPATCH_LAG + 1
    hslot = i % n_buf
    dslot = i % n_sem

    def tile_fetch(tile, hs):
        return pltpu.make_async_copy(h2_hbm.at[pl.ds(pl.multiple_of(tile * MOE_TILE, MOE_TILE), MOE_TILE)],
                                     hbuf.at[hs], fsem.at[hs])

    def idx_copy(rec, s):
        return pltpu.make_async_copy(idx_hbm.at[pl.ds(pl.multiple_of(rec * IDX_REC, IDX_REC), IDX_REC)],
                                     idx_smem.at[pl.ds(pl.multiple_of(s * IDX_REC, IDX_REC), IDX_REC)],
                                     isem.at[s])

    def row_copies_start(s):
        def one(t, carry):
            for k in range(TOP_K):
                d = idx_smem[s * IDX_REC + k * MOE_TILE + t]
                pltpu.make_async_copy(hbuf.at[hslot, pl.ds(t, 1)], x_hbm.at[pl.ds(d, 1)], dsem.at[dslot]).start()
            return carry
        for t in range(MOE_TILE):
            one(t, 0)

    @pl.when(i == 0)
    def _():
        idx_copy(0, 0).start()
        tile_fetch(0, 0).start()

    @pl.when(i + 1 < n_tiles)
    def _():
        idx_copy(i + 1, 1 - slot).start()
        tile_fetch(i + 1, (i + 1) % n_buf).start()

    idx_copy(i, slot).wait()
    tile_fetch(i, hslot).wait()
    row_copies_start(slot)

    def wait_rows(tile):
        for _ in range(TOP_K):
            pltpu.make_async_copy(hbuf.at[0], x_hbm.at[pl.ds(0, MOE_TILE)], dsem.at[tile % n_sem]).wait()

    @pl.when(i >= DISPATCH_LAG)
    def _():
        wait_rows(i - DISPATCH_LAG)

    @pl.when(i == n_tiles - 1)
    def _():
        for back in range(DISPATCH_LAG - 1, -1, -1):
            wait_rows(i - back)
        zrow[...] = jnp.zeros_like(zrow)

        def zero_copy(dst_row):
            return pltpu.make_async_copy(zrow.at[pl.ds(0, 1)], x_hbm.at[pl.ds(dst_row, 1)], zsem)

        def per_expert(e, carry):
            first = tab_ref[e] + tab_ref[N_EXPERTS + e]
            last = tab_ref[e] + tab_ref[2 * N_EXPERTS + e]
            lax.fori_loop(first, last, lambda r, c: (zero_copy(r).start(), c)[1], 0)
            lax.fori_loop(first, last, lambda r, c: (zero_copy(r).wait(), c)[1], 0)
            return carry
        lax.fori_loop(0, N_EXPERTS, per_expert, 0)


def _moe_dispatch(h2, idx_rec, tables, n_rows):
    n_tiles = h2.shape[0] // MOE_TILE
    grid_spec = pltpu.PrefetchScalarGridSpec(
        num_scalar_prefetch=1,
        grid=(n_tiles,),
        in_specs=[pl.BlockSpec(memory_space=pl.ANY),
                  pl.BlockSpec(memory_space=pl.ANY)],
        out_specs=pl.BlockSpec(memory_space=pl.ANY),
        scratch_shapes=[pltpu.SMEM((2 * IDX_REC,), jnp.int32),
                        pltpu.VMEM((DISPATCH_LAG + 2, MOE_TILE, D_MODEL), F32),
                        pltpu.VMEM((8, D_MODEL), F32),
                        pltpu.SemaphoreType.DMA((2,)),
                        pltpu.SemaphoreType.DMA((DISPATCH_LAG + 2,)),
                        pltpu.SemaphoreType.DMA((DISPATCH_LAG + 1,)),
                        pltpu.SemaphoreType.DMA],
    )
    return pl.pallas_call(
        functools.partial(_moe_dispatch_body, n_tiles=n_tiles),
        grid_spec=grid_spec,
        out_shape=jax.ShapeDtypeStruct((n_rows, D_MODEL), F32),
        compiler_params=_cparams(("arbitrary",)),
        name="moe_dispatch",
    )(tables, idx_rec, h2)


def _moe_experts_body(be_ref, nused_ref, x_ref, w1_ref, b1_ref, w2_ref, b2_ref, y_ref, w1b, w2b):
    i = pl.program_id(0)

    @pl.when(i < nused_ref[0])
    def _():
        @pl.when((i == 0) | (be_ref[i] != be_ref[jnp.maximum(i - 1, 0)]))
        def _():
            w1b[...] = w1_ref[0].astype(BF16)
            w2b[...] = w2_ref[0].astype(BF16)

        xb = x_ref[...].astype(BF16)
        glu = jnp.minimum(_dot(xb, w1b[:, :D_MODEL]) + b1_ref[0, :, :D_MODEL], SWIGLU_LIMIT)
        lin = jnp.clip(_dot(xb, w1b[:, D_MODEL:]) + b1_ref[0, :, D_MODEL:], -SWIGLU_LIMIT, SWIGLU_LIMIT)
        act = glu * _sigmoid(SWIGLU_ALPHA * glu) * (lin + 1.0)
        y_ref[...] = _dot(act.astype(BF16), w2b[...]) + b2_ref[0]

    @pl.when(i >= nused_ref[0])
    def _():
        y_ref[...] = jnp.zeros_like(y_ref)


def _moe_experts(x_rows, block_e, n_used, w1, b1, w2, b2):
    n_blocks = block_e.shape[0]
    by_expert = lambda i, be, nu: (be[i], 0, 0)
    grid_spec = pltpu.PrefetchScalarGridSpec(
        num_scalar_prefetch=2,
        grid=(n_blocks,),
        in_specs=[pl.BlockSpec((MOE_BLOCK, D_MODEL), lambda i, be, nu: (jnp.minimum(i, nu[0] - 1), 0)),
                  pl.BlockSpec((1, D_MODEL, 2 * D_MODEL), by_expert),
                  pl.BlockSpec((1, 1, 2 * D_MODEL), by_expert),
                  pl.BlockSpec((1, D_MODEL, D_MODEL), by_expert),
                  pl.BlockSpec((1, 1, D_MODEL), by_expert)],
        out_specs=pl.BlockSpec((MOE_BLOCK, D_MODEL), lambda i, be, nu: (i, 0)),
        scratch_shapes=[pltpu.VMEM((D_MODEL, 2 * D_MODEL), BF16),
                        pltpu.VMEM((D_MODEL, D_MODEL), BF16)],
    )
    return pl.pallas_call(
        _moe_experts_body,
        grid_spec=grid_spec,
        out_shape=jax.ShapeDtypeStruct((n_blocks * MOE_BLOCK, D_MODEL), F32),
        compiler_params=_cparams(("arbitrary",)),
        name="moe_experts",
    )(block_e, n_used, x_rows, w1, b1, w2, b2)


def _route(top_e, rank, counts):
    n_tok = top_e.shape[1]
    counts = counts.reshape(N_EXPERTS).astype(jnp.int32)
    padded = (counts + MOE_BLOCK - 1) // MOE_BLOCK * MOE_BLOCK
    pad_end = jnp.cumsum(padded)
    pad_start = pad_end - padded
    experts = jnp.arange(N_EXPERTS, dtype=jnp.int32)
    dest = rank + jnp.sum(jnp.where(top_e[..., None] == experts, pad_start, 0), axis=-1)
    n_blocks = -(-(n_tok * TOP_K + N_EXPERTS * (MOE_BLOCK - 1)) // MOE_BLOCK)
    blk_row0 = jnp.arange(n_blocks, dtype=jnp.int32) * MOE_BLOCK
    block_e = jnp.minimum(jnp.sum(blk_row0[:, None] >= pad_end[None, :], axis=1), N_EXPERTS - 1).astype(jnp.int32)
    n_used = (pad_end[-1] // MOE_BLOCK).astype(jnp.int32).reshape(1)
    n_tiles = n_tok // MOE_TILE
    rec = dest.reshape(TOP_K, n_tiles, MOE_TILE).transpose(1, 0, 2).reshape(n_tiles, TOP_K * MOE_TILE)
    rec = jnp.concatenate([rec, jnp.zeros((2, TOP_K * MOE_TILE), jnp.int32)], axis=0)
    idx_rec = jnp.concatenate([rec, jnp.zeros((n_tiles + 2, IDX_REC - TOP_K * MOE_TILE), jnp.int32)],
                              axis=1).reshape(-1)
    tables = jnp.concatenate([pad_start, counts, padded]).astype(jnp.int32)
    return idx_rec, tables, block_e, n_used, n_blocks * MOE_BLOCK


def _tail_body(idx_hbm, y_hbm, x1_ref, tg_ref, pe_ref, gple_ref, wpg_ref, wp_ref, gfin_ref, yp_ref, ys_ref,
               idx_smem, gbuf, isem, gsem, *, n_tiles, n_prompt_tiles):
    i = pl.program_id(0)
    slot = i % 2
    prev = 1 - slot

    def idx_copy(rec, s):
        return pltpu.make_async_copy(idx_hbm.at[pl.ds(pl.multiple_of(rec * IDX_REC, IDX_REC), IDX_REC)],
                                     idx_smem.at[pl.ds(pl.multiple_of(s * IDX_REC, IDX_REC), IDX_REC)],
                                     isem.at[s])

    def gather_wait(s):
        for k in range(TOP_K):
            pltpu.make_async_copy(y_hbm.at[pl.ds(0, MOE_TILE)], gbuf.at[s, k], gsem.at[s]).wait()

    @pl.when(i == 0)
    def _():
        idx_copy(0, 0).start()
        gbuf[1] = jnp.zeros(gbuf.shape[1:], gbuf.dtype)

    @pl.when(i >= 1)
    def _():
        gather_wait(prev)

    idx_copy(i, slot).wait()
    idx_copy(i + 1, prev).start()
    tg = tg_ref[...]
    moe = tg[:, 0:1] * gbuf[prev, 0]
    for k in range(1, TOP_K):
        moe = moe + tg[:, k:k + 1] * gbuf[prev, k]
    x2 = x1_ref[...] + moe
    gate = _sigmoid(_dot(_rms(x2, gple_ref[...]).astype(BF16), wpg_ref[...]))
    x3 = x2 + gate * _dot(pe_ref[...].astype(BF16), wp_ref[...])
    y = _rms(x3, gfin_ref[...])
    for t in range(MOE_TILE):
        for k in range(TOP_K):
            d = idx_smem[slot * IDX_REC + k * MOE_TILE + t]
            pltpu.make_async_copy(y_hbm.at[pl.ds(d, 1)], gbuf.at[slot, k, pl.ds(t, 1)], gsem.at[slot]).start()

    @pl.when(i - 1 < n_prompt_tiles)
    def _():
        yp_ref[...] = y

    @pl.when(i - 1 >= n_prompt_tiles)
    def _():
        ys_ref[...] = y

    @pl.when(i == n_tiles)
    def _():
        idx_copy(i + 1, prev).wait()
        gather_wait(slot)


def _tail(x1_all, y_rows, idx_rec, tg_t, pe_all, wts, n_p):
    n_all = x1_all.shape[0]
    n_tiles = n_all // MOE_TILE
    n_prompt_tiles = n_p // MOE_TILE
    row = lambda i: (jnp.maximum(i - 1, 0), 0)
    const = lambda i: (0, 0)
    body = functools.partial(_tail_body, n_tiles=n_tiles, n_prompt_tiles=n_prompt_tiles)
    return pl.pallas_call(
        body,
        grid=(n_tiles + 1,),
        in_specs=[pl.BlockSpec(memory_space=pl.ANY), pl.BlockSpec(memory_space=pl.ANY),
                  pl.BlockSpec((MOE_TILE, D_MODEL), row), pl.BlockSpec((MOE_TILE, TOP_K), row),
                  pl.BlockSpec((MOE_TILE, PLE_DIM), row), pl.BlockSpec((1, D_MODEL), const),
                  pl.BlockSpec((D_MODEL, D_MODEL), const), pl.BlockSpec((PLE_DIM, D_MODEL), const),
                  pl.BlockSpec((1, D_MODEL), const)],
        out_specs=[pl.BlockSpec((MOE_TILE, D_MODEL),
                                lambda i: (jnp.clip(i - 1, 0, n_prompt_tiles - 1), 0)),
                   pl.BlockSpec((MOE_TILE, D_MODEL), lambda i: (jnp.maximum(i - 1 - n_prompt_tiles, 0), 0))],
        out_shape=[jax.ShapeDtypeStruct((n_p, D_MODEL), F32),
                   jax.ShapeDtypeStruct((n_all - n_p, D_MODEL), F32)],
        scratch_shapes=[pltpu.SMEM((2 * IDX_REC,), jnp.int32),
                        pltpu.VMEM((2, TOP_K, MOE_TILE, D_MODEL), F32),
                        pltpu.SemaphoreType.DMA((2,)),
                        pltpu.SemaphoreType.DMA((2,))],
        compiler_params=_cparams(("arbitrary",)),
        name="tail",
    )(idx_rec, y_rows, x1_all, tg_t, pe_all, *wts)


RWKV_CHUNK = 128
RWKV_CHUNK_SAMPLE = 8
STATE_PASSES = 1
Y_PASSES = 1
TM_PROMPT = 512


def kernel(x_prompt, x_sample, p_prompt, p_sample, cache_kv_w128, cache_kv_w512, cache_kv_w2048, state_rwkv_shift, state_rwkv_wkv, norm_mix_g, w_in, rwkv_mu, rwkv_w0, rwkv_w2, rwkv_a0, rwkv_a2, rwkv_g2, rwkv_k_k, rwkv_k_a, rwkv_r_k, rwkv_ln_w, rwkv_ln_b, w_out_attn, w_out_rwkv, w_out, norm_ffn_g, router_w, router_b, moe_w1, moe_b1, moe_w2, moe_b2, norm_ple_g, w_ple, w_ple_gate, norm_final_g):
    bp, seq, _ = x_prompt.shape
    bs, t_s, _ = x_sample.shape
    n_p, n_s = bp * seq, bs * t_s
    n_all = n_p + n_s
    assert w_in.shape[0] == 1, "single layer"
    caches = (cache_kv_w128, cache_kv_w512, cache_kv_w2048)

    row = lambda a: a.reshape(1, -1)
    w_in_b = w_in[0].astype(BF16)
    zeros64 = jnp.zeros((LORA_W // 2, RWKV_W), F32)
    rwkv_wts = (row(rwkv_mu[0]), row(rwkv_w0[0]), jnp.concatenate([rwkv_w2[0], zeros64], axis=0),
                row(rwkv_a0[0]), jnp.concatenate([zeros64, rwkv_a2[0]], axis=0), rwkv_g2[0],
                row(rwkv_k_k[0]), row(rwkv_k_a[0]), row(rwkv_r_k[0]), row(rwkv_ln_w[0]), row(rwkv_ln_b[0]))
    mix_wts = (w_out_attn[0].astype(BF16), w_out_rwkv[0].astype(BF16), w_out[0].astype(BF16),
               row(norm_ffn_g[0]), router_w[0].T, router_b[0].reshape(N_EXPERTS, 1))
    tail_wts = (row(norm_ple_g[0]), w_ple_gate[0].astype(BF16), w_ple[0].astype(BF16), row(norm_final_g))
    g_mix = row(norm_mix_g[0])

    xp = x_prompt.reshape(n_p, D_MODEL)
    qd_p, kvd_p, kvt_p, zr_p, gate_p = _in_proj_dilated(xp, g_mix, w_in_b, TM_PROMPT, bp, seq)
    att_p = [_attn_prompt(qd_p[gi], kvd_p[gi], gi) for gi in range(N_GROUPS)]
    orw_p, wkv_p = _rwkv(zr_p.reshape(bp, seq, RWKV_COLS), jnp.zeros((bp, 1, RWKV_COLS), F32),
                         jnp.zeros((bp, RWKV_HEADS, RWKV_HEAD, RWKV_HEAD), F32), rwkv_wts, RWKV_CHUNK, seq)
    bufs, counts = _mix_out(xp, [a[0] for a in att_p], [a[1] for a in att_p], orw_p.reshape(n_p, RWKV_W), gate_p,
                            mix_wts, jnp.zeros((N_EXPERTS, 1), F32), TM_PROMPT, n_all, 0, None)

    xs = x_sample.reshape(n_s, D_MODEL)
    q_s, kv_s, zr_s, gate_s = _in_proj(xs, g_mix, w_in_b, n_s)
    t_pad = 8
    q_s3 = jnp.pad(q_s.reshape(bs, t_s, Q_COLS), ((0, 0), (0, t_pad - t_s), (0, 0)))
    feat_major = lambda a: jnp.swapaxes(a, 1, 2)
    tail_t = jnp.pad(feat_major(kv_s.reshape(bs, t_s, KV_COLS)), ((0, 0), (0, 0), (LANE - t_s, 0)))
    att_s, new_caches = [], []
    for gi in range(N_GROUPS):
        wc = caches[gi].shape[2]
        o, lse, newc = _attn_sample(q_s3, tail_t, feat_major(caches[gi].reshape(bs, wc, 2 * GROUP_W)), gi, t_s)
        att_s.append((o[:, :t_s].reshape(n_s, GROUP_W), lse[:, :t_s].reshape(n_s, GROUP_W)))
        new_caches.append(feat_major(newc).reshape(1, bs, wc, 2, HEADS_PER_GROUP, HEAD_DIM))
    zr_s3 = jnp.pad(zr_s.reshape(bs, t_s, RWKV_COLS), ((0, 0), (0, RWKV_CHUNK_SAMPLE - t_s), (0, 0)))
    orw_s, wkv_s = _rwkv(zr_s3, state_rwkv_shift[0].reshape(bs, 1, RWKV_COLS), state_rwkv_wkv[0], rwkv_wts,
                         RWKV_CHUNK_SAMPLE, t_s)
    bufs, counts = _mix_out(xs, [a[0] for a in att_s], [a[1] for a in att_s], orw_s[:, :t_s].reshape(n_s, RWKV_W),
                            gate_s, mix_wts, counts, n_s, n_all, n_p, bufs)
    x1_all, h2_all, top_e, top_g, rank = bufs

    idx_rec, tables, block_e, n_used, n_rows = _route(top_e, rank, counts)
    x_rows = _moe_dispatch(h2_all, idx_rec, tables, n_rows)
    y_rows = _moe_experts(x_rows, block_e, n_used, moe_w1[0], moe_b1[0].reshape(N_EXPERTS, 1, 2 * D_MODEL),
                          moe_w2[0], moe_b2[0].reshape(N_EXPERTS, 1, D_MODEL))
    pe_all = jnp.concatenate([p_prompt[0].reshape(n_p, PLE_DIM), p_sample[0].reshape(n_s, PLE_DIM)], axis=0)
    y_p, y_s = _tail(x1_all, y_rows, idx_rec, top_g.T, pe_all, tail_wts, n_p)

    kv_out_p = [feat_major(t).reshape(1, bp, t.shape[2], 2, HEADS_PER_GROUP, HEAD_DIM) for t in kvt_p]
    shift_p = zr_p.reshape(bp, seq, RWKV_COLS)[:, -1][None]
    shift_s = zr_s.reshape(bs, t_s, RWKV_COLS)[:, -1][None]
    return (y_p.reshape(bp, seq, D_MODEL), y_s.reshape(bs, t_s, D_MODEL),
            kv_out_p[0], kv_out_p[1], kv_out_p[2], shift_p, wkv_p[None],
            new_caches[0], new_caches[1], new_caches[2], shift_s, wkv_s[None])
```

```python
import functools
import math

import numpy as np
import jax
import jax.numpy as jnp
from jax import lax
from jax.experimental import pallas as pl
from jax.experimental.pallas import tpu as pltpu

F32 = jnp.float32
BF16 = jnp.bfloat16

LANE = 128
D_MODEL = 1024
N_GROUPS = 3
HEADS_PER_GROUP = 4
HEAD_DIM = 64
ATTN_GROUPS = ((128, 1), (512, 4), (2048, 16))
GROUP_W = HEADS_PER_GROUP * HEAD_DIM
Q_COLS = N_GROUPS * GROUP_W
KV_COLS = 2 * Q_COLS
BAND = 128

RWKV_HEADS = 8
RWKV_HEAD = 64
RWKV_W = 512
LORA_W = 128
GATE_LORA = 128
RWKV_COLS = 3 * RWKV_W + LORA_W + GATE_LORA
GN_EPS = 64e-5
GATE_COLS = 2 * D_MODEL
Z_RWKV0 = 3 * Q_COLS
Z_GATE0 = Z_RWKV0 + RWKV_COLS
IN_COLS = Z_GATE0 + GATE_COLS

N_EXPERTS = 32
TOP_K = 4
SWIGLU_LIMIT = 7.0
SWIGLU_ALPHA = 1.702
MOE_BLOCK = 512
MOE_TILE = 128
DISPATCH_LAG = 1
IDX_REC = 1024
PLE_DIM = 256
RMS_EPS = 1e-6

NEG_BIG = -1e30
VMEM_LIMIT = 56 * 1024 * 1024


def _cparams(sem):
    return pltpu.CompilerParams(dimension_semantics=sem, vmem_limit_bytes=VMEM_LIMIT)


def _rms(x, g):
    return x * lax.rsqrt(jnp.mean(x * x, axis=-1, keepdims=True) + RMS_EPS) * g


def _sigmoid(x):
    return 1.0 / (1.0 + jnp.exp(-x))


def _dot(a, b):
    return jnp.dot(a, b, preferred_element_type=F32)


def _dot_nt(a, b):
    return lax.dot_general(a, b, (((1,), (1,)), ((), ())), preferred_element_type=F32)


def _dot_tn(a, b):
    return lax.dot_general(a, b, (((0,), (0,)), ((), ())), preferred_element_type=F32)


def _split(x):
    hi = x.astype(BF16)
    lo = (x - hi.astype(F32)).astype(BF16)
    return hi, lo


def _mm3(dot, a, b):
    ah, al = _split(a)
    bh, bl = _split(b)
    return dot(ah, bh) + (dot(ah, bl) + dot(al, bh))


def _parts(x, n):
    out = []
    for _ in range(n - 1):
        hi = x.astype(BF16)
        out.append(hi)
        x = x - hi.astype(F32)
    out.append(x.astype(BF16))
    return out


def _mmp(dot, ap, bp):
    order = max(len(ap), len(bp))
    acc = None
    for i, a in enumerate(ap):
        for j, b in enumerate(bp):
            if i + j < order:
                t = dot(a, b)
                acc = t if acc is None else acc + t
    return acc


def _in_proj_body(x_ref, g_ref, w_ref, q_ref, kv_ref, zr_ref, gate_ref):
    h = _rms(x_ref[...], g_ref[...]).astype(BF16)

    def proj(lo, width):
        return _dot(h, w_ref[:, lo:lo + width])

    q_ref[...] = (proj(0, Q_COLS) * (1.0 / math.sqrt(HEAD_DIM))).astype(BF16)
    for g in range(N_GROUPS):
        kv_ref[:, 2 * g * GROUP_W:(2 * g + 1) * GROUP_W] = proj(Q_COLS + g * GROUP_W, GROUP_W)
        kv_ref[:, (2 * g + 1) * GROUP_W:(2 * g + 2) * GROUP_W] = proj(2 * Q_COLS + g * GROUP_W, GROUP_W)
    zr_ref[...] = proj(Z_RWKV0, RWKV_COLS)
    gate_ref[...] = _sigmoid(proj(Z_GATE0, GATE_COLS)).astype(BF16)


def _in_proj_dilated_body(x_ref, g_ref, w_ref, *rest, tiles_per_seq):
    qd_refs, kvd_refs, kvt_refs = rest[0:N_GROUPS], rest[N_GROUPS:2 * N_GROUPS], rest[2 * N_GROUPS:3 * N_GROUPS]
    zr_ref, gate_ref, st_ref = rest[3 * N_GROUPS:]
    tm = x_ref.shape[0]
    tile_in_seq = pl.program_id(0) % tiles_per_seq
    h = _rms(x_ref[...], g_ref[...]).astype(BF16)

    def proj(lo, width):
        return _dot(h, w_ref[:, lo:lo + width])

    n_q = Q_COLS // LANE

    def stage(slab0, val):
        for s in range(val.shape[1] // LANE):
            st_ref[slab0 + s] = val[:, s * LANE:(s + 1) * LANE]

    stage(0, proj(0, Q_COLS) * (1.0 / math.sqrt(HEAD_DIM)))
    for g in range(N_GROUPS):
        for part, src in ((0, Q_COLS), (1, 2 * Q_COLS)):
            col = (2 * g + part) * GROUP_W
            stage(n_q + col // LANE, proj(src + g * GROUP_W, GROUP_W))
    for g, (window, _) in enumerate(ATTN_GROUPS):
        rows = min(window, tm)
        first_tile = tiles_per_seq - max(window // tm, 1)

        @pl.when(tile_in_seq >= first_tile)
        def _():
            for s in range(2 * GROUP_W // LANE):
                slab = st_ref[n_q + 2 * g * GROUP_W // LANE + s, tm - rows:tm, :]
                kvt_refs[g][0, s * LANE:(s + 1) * LANE, :] = slab.T
    for g, (_, dil) in enumerate(ATTN_GROUPS):
        for r in range(dil):
            rows = pl.ds(r, tm // dil, stride=dil) if dil > 1 else slice(None)
            for s in range(GROUP_W // LANE):
                qd_refs[g][0, r, :, s * LANE:(s + 1) * LANE] = st_ref[g * GROUP_W // LANE + s, rows, :].astype(BF16)
            for s in range(2 * GROUP_W // LANE):
                kvd_refs[g][0, r, :, s * LANE:(s + 1) * LANE] = \
                    st_ref[n_q + 2 * g * GROUP_W // LANE + s, rows, :].astype(BF16)
    zr_ref[...] = proj(Z_RWKV0, RWKV_COLS)
    gate_ref[...] = _sigmoid(proj(Z_GATE0, GATE_COLS)).astype(BF16)


def _in_proj_dilated(x, g, w_bf16, tm, batch, seq):
    n = x.shape[0]
    tiles = seq // tm
    row = lambda i: (i, 0)
    const = lambda i: (0, 0)
    sub = lambda i: (i // tiles, 0, i % tiles, 0)
    dils = [d for _, d in ATTN_GROUPS]
    assert all(tm % (16 * d) == 0 for d in dils)
    wins = [min(w, seq) for w, _ in ATTN_GROUPS]
    assert all(w % tm == 0 or tm % w == 0 for w in wins)

    def tail_spec(w):
        cols = min(w, tm)
        first = tiles - max(w // tm, 1)
        return pl.BlockSpec((1, 2 * GROUP_W, cols), lambda i: (i // tiles, 0, jnp.maximum(i % tiles - first, 0)))

    outs = pl.pallas_call(
        functools.partial(_in_proj_dilated_body, tiles_per_seq=tiles),
        grid=(n // tm,),
        in_specs=[pl.BlockSpec((tm, D_MODEL), row),
                  pl.BlockSpec((1, D_MODEL), const),
                  pl.BlockSpec((D_MODEL, IN_COLS), const, pipeline_mode=pl.Buffered(1))],
        out_specs=[pl.BlockSpec((1, d, tm // d, GROUP_W), sub) for d in dils]
                  + [pl.BlockSpec((1, d, tm // d, 2 * GROUP_W), sub) for d in dils]
                  + [tail_spec(w) for w in wins]
                  + [pl.BlockSpec((tm, RWKV_COLS), row),
                     pl.BlockSpec((tm, GATE_COLS), row)],
        out_shape=[jax.ShapeDtypeStruct((batch, d, seq // d, GROUP_W), BF16) for d in dils]
                  + [jax.ShapeDtypeStruct((batch, d, seq // d, 2 * GROUP_W), BF16) for d in dils]
                  + [jax.ShapeDtypeStruct((batch, 2 * GROUP_W, w), F32) for w in wins]
                  + [jax.ShapeDtypeStruct((n, RWKV_COLS), F32),
                     jax.ShapeDtypeStruct((n, GATE_COLS), BF16)],
        scratch_shapes=[pltpu.VMEM(((Q_COLS + KV_COLS) // LANE, tm, LANE), F32)],
        compiler_params=_cparams(("arbitrary",)),
        name="in_proj_dilated",
    )(x, g, w_bf16)
    g3 = N_GROUPS
    return outs[0:g3], outs[g3:2 * g3], outs[2 * g3:3 * g3], outs[3 * g3], outs[3 * g3 + 1]


def _in_proj(x, g, w_bf16, tm):
    n = x.shape[0]
    row = lambda i: (i, 0)
    const = lambda i: (0, 0)
    return pl.pallas_call(
        _in_proj_body,
        grid=(n // tm,),
        in_specs=[pl.BlockSpec((tm, D_MODEL), row),
                  pl.BlockSpec((1, D_MODEL), const),
                  pl.BlockSpec((D_MODEL, IN_COLS), const)],
        out_specs=[pl.BlockSpec((tm, Q_COLS), row),
                   pl.BlockSpec((tm, KV_COLS), row),
                   pl.BlockSpec((tm, RWKV_COLS), row),
                   pl.BlockSpec((tm, GATE_COLS), row)],
        out_shape=[jax.ShapeDtypeStruct((n, Q_COLS), BF16),
                   jax.ShapeDtypeStruct((n, KV_COLS), F32),
                   jax.ShapeDtypeStruct((n, RWKV_COLS), F32),
                   jax.ShapeDtypeStruct((n, GATE_COLS), BF16)],
        compiler_params=_cparams(("parallel",)),
        name="in_proj",
    )(x, g, w_bf16)


def _alibi_slopes(gi):
    return [2.0 ** (-8.0 * (gi * HEADS_PER_GROUP + h + 1) / (N_GROUPS * HEADS_PER_GROUP))
            for h in range(HEADS_PER_GROUP)]


def _head_of_lane(shape):
    return lax.broadcasted_iota(jnp.int32, shape, len(shape) - 1) // HEAD_DIM


def _stack_heads(q):
    hl = _head_of_lane(q.shape)
    return jnp.concatenate([jnp.where(hl == h, q, jnp.zeros_like(q)) for h in range(HEADS_PER_GROUP)], axis=0)


def _unstack_heads(x4, rows):
    hl = _head_of_lane((rows, GROUP_W))
    out = x4[0:rows]
    for h in range(1, HEADS_PER_GROUP):
        out = jnp.where(hl == h, x4[h * rows:(h + 1) * rows], out)
    return out


def _attn_prompt_body(q_ref, kvp_ref, kvc_ref, o_ref, lse_ref, kb_ref, bias_ref, *, dil, slopes, n_sub):
    c = pl.program_id(2)
    kb_ref[0:BAND, :] = kvp_ref[0, 0]
    kb_ref[BAND:, :] = kvc_ref[0, 0]

    qi = lax.broadcasted_iota(jnp.int32, (BAND, 2 * BAND), 0) + BAND
    ki = lax.broadcasted_iota(jnp.int32, (BAND, 2 * BAND), 1)
    rel = qi - ki
    dist = jnp.where((rel >= 0) & (rel <= BAND), (dil * rel).astype(F32), -NEG_BIG / slopes[-1])
    for h in range(HEADS_PER_GROUP):
        bias_ref[h * BAND:(h + 1) * BAND, :] = -slopes[h] * dist

    def sub_block(n, carry):
        r0 = pl.multiple_of(n * BAND, BAND)
        q4 = _stack_heads(q_ref[0, 0, pl.ds(r0, BAND), :])
        kv = kb_ref[pl.ds(r0, 2 * BAND), :]
        s = _dot_nt(q4, kv[:, :GROUP_W]) + bias_ref[...]
        n_before_start = jnp.where((c == 0) & (n == 0), BAND, 0)
        kcol = lax.broadcasted_iota(jnp.int32, s.shape, 1)
        s = jnp.where(kcol < n_before_start, NEG_BIG, s)
        m = jnp.max(s, axis=-1, keepdims=True)
        e = jnp.exp(s - m)
        den = jnp.sum(e, axis=-1, keepdims=True)
        o4 = _dot(e.astype(BF16), kv[:, GROUP_W:]) * (1.0 / den)
        o_ref[0, 0, pl.ds(r0, BAND), :] = _unstack_heads(o4, BAND).astype(o_ref.dtype)
        lse4 = jnp.broadcast_to(m + jnp.log(den), (HEADS_PER_GROUP * BAND, GROUP_W))
        lse_ref[0, 0, pl.ds(r0, BAND), :] = _unstack_heads(lse4, BAND)
        return carry

    lax.fori_loop(0, n_sub, sub_block, 0, unroll=min(n_sub, 4))


def _attn_prompt(qd, kvd, gi):
    window, dil = ATTN_GROUPS[gi]
    assert window // dil == BAND
    batch, _, sub_len, _ = qd.shape
    chunk = min(sub_len, 1024)
    n_chunks = sub_len // chunk
    sub_per_chunk = chunk // BAND
    body = functools.partial(_attn_prompt_body, dil=dil, slopes=_alibi_slopes(gi), n_sub=sub_per_chunk)
    cur = lambda b, r, c: (b, r, c, 0)
    return pl.pallas_call(
        body,
        grid=(batch, dil, n_chunks),
        in_specs=[
            pl.BlockSpec((1, 1, chunk, GROUP_W), cur),
            pl.BlockSpec((1, 1, BAND, 2 * GROUP_W),
                         lambda b, r, c: (b, r, jnp.maximum(c * sub_per_chunk - 1, 0), 0)),
            pl.BlockSpec((1, 1, chunk, 2 * GROUP_W), cur),
        ],
        out_specs=[pl.BlockSpec((1, 1, chunk, GROUP_W), cur), pl.BlockSpec((1, 1, chunk, GROUP_W), cur)],
        out_shape=[jax.ShapeDtypeStruct(qd.shape, BF16), jax.ShapeDtypeStruct(qd.shape, F32)],
        scratch_shapes=[pltpu.VMEM((chunk + BAND, 2 * GROUP_W), BF16),
                        pltpu.VMEM((HEADS_PER_GROUP * BAND, 2 * BAND), F32)],
        compiler_params=_cparams(("parallel", "parallel", "arbitrary")),
        name=f"attn_prompt_g{gi}",
    )(qd, kvd, kvd)


def _attn_sample_body(q_ref, tail_ref, cache_ref, o_ref, lse_ref, newc_ref, *, dil, slopes, wc, t_new):
    tp = q_ref.shape[1]
    cache = cache_ref[0]
    tail = tail_ref[0]
    newc_ref[0] = pltpu.roll(cache, wc - t_new, 1)
    lane = lax.broadcasted_iota(jnp.int32, tail.shape, 1)
    newc_ref[0, :, wc - LANE:wc] = jnp.where(lane >= LANE - t_new, tail, newc_ref[0, :, wc - LANE:wc])

    q4 = _stack_heads(q_ref[0])
    cb = cache.astype(BF16)
    nb = tail.astype(BF16)
    t_row = lax.broadcasted_iota(jnp.int32, (tp, 1), 0)
    slope_col = jnp.concatenate([jnp.full((tp, 1), s, F32) for s in slopes], axis=0)
    t4 = jnp.concatenate([t_row] * HEADS_PER_GROUP, axis=0)

    def scores(keys_t_bf16, dist):
        ok = (dist >= 0) & ((dist & (dil - 1)) == 0) & (dist <= BAND * dil)
        s = _dot(q4, keys_t_bf16[:GROUP_W, :])
        return jnp.where(ok, s - slope_col * dist.astype(F32), NEG_BIG)

    pc = lax.broadcasted_iota(jnp.int32, (HEADS_PER_GROUP * tp, wc), 1)
    pn = lax.broadcasted_iota(jnp.int32, (HEADS_PER_GROUP * tp, LANE), 1) - (LANE - t_new)
    s_c = scores(cb, wc + t4 - pc)
    s_n = scores(nb, jnp.where(pn >= 0, t4 - pn, -1))
    m = jnp.maximum(jnp.max(s_c, axis=-1, keepdims=True), jnp.max(s_n, axis=-1, keepdims=True))
    e_c = jnp.exp(s_c - m)
    e_n = jnp.exp(s_n - m)
    den = jnp.sum(e_c, axis=-1, keepdims=True) + jnp.sum(e_n, axis=-1, keepdims=True)
    o4 = (_dot_nt(e_c.astype(BF16), cb[GROUP_W:, :]) + _dot_nt(e_n.astype(BF16), nb[GROUP_W:, :])) * (1.0 / den)
    o_ref[0] = _unstack_heads(o4, tp).astype(o_ref.dtype)
    lse4 = jnp.broadcast_to(m + jnp.log(den), (HEADS_PER_GROUP * tp, GROUP_W))
    lse_ref[0] = _unstack_heads(lse4, tp)


def _attn_sample(q, tail_t, cache_t, gi, t_new):
    window, dil = ATTN_GROUPS[gi]
    batch, tp, _ = q.shape
    wc = cache_t.shape[2]
    assert wc + 0 - dil * BAND >= 0
    body = functools.partial(_attn_sample_body, dil=dil, slopes=_alibi_slopes(gi), wc=wc, t_new=t_new)
    return pl.pallas_call(
        body,
        grid=(batch,),
        in_specs=[pl.BlockSpec((1, tp, GROUP_W), lambda b: (b, 0, gi)),
                  pl.BlockSpec((1, 2 * GROUP_W, LANE), lambda b: (b, gi, 0)),
                  pl.BlockSpec((1, 2 * GROUP_W, wc), lambda b: (b, 0, 0))],
        out_specs=[pl.BlockSpec((1, tp, GROUP_W), lambda b: (b, 0, 0)),
                   pl.BlockSpec((1, tp, GROUP_W), lambda b: (b, 0, 0)),
                   pl.BlockSpec((1, 2 * GROUP_W, wc), lambda b: (b, 0, 0))],
        out_shape=[jax.ShapeDtypeStruct((batch, tp, GROUP_W), BF16),
                   jax.ShapeDtypeStruct((batch, tp, GROUP_W), F32),
                   jax.ShapeDtypeStruct((batch, 2 * GROUP_W, wc), F32)],
        compiler_params=_cparams(("parallel",)),
        name=f"attn_sample_g{gi}",
    )(q, tail_t, cache_t)


def _rwkv_body(zr_ref, shift0_ref, wkv0_ref, mu_ref, w0_ref, w2_ref, a0_ref, a2_ref, g2_ref, kk_ref, ka_ref,
               rk_ref, lnw_ref, lnb_ref, o_ref, wkv_ref, prev_ref, *, chunk, t_valid):
    c = pl.program_id(1)
    L = chunk
    mm = functools.partial(_mm3, _dot)
    heads = range(RWKV_HEADS)
    sls = [slice(h * RWKV_HEAD, (h + 1) * RWKV_HEAD) for h in heads]

    @pl.when(c == 0)
    def _():
        prev_ref[...] = shift0_ref[0]
        wkv_ref[0] = wkv0_ref[0]

    zr = zr_ref[0]
    row = lax.broadcasted_iota(jnp.int32, (L, 1), 0)
    prev = jnp.where(row == 0, prev_ref[...], pltpu.roll(zr, 1, 0))
    prev_ref[...] = zr[L - 1:L, :]
    zm = zr + (prev - zr) * mu_ref[...]

    r = zm[:, 0:RWKV_W]
    k = zm[:, RWKV_W:2 * RWKV_W]
    v = zm[:, 2 * RWKV_W:3 * RWKV_W]
    xwa = zm[:, 3 * RWKV_W:3 * RWKV_W + LORA_W]
    xg = zm[:, 3 * RWKV_W + LORA_W:]
    w_pre = w0_ref[...] + mm(jnp.tanh(xwa), w2_ref[...])
    softplus = jnp.maximum(-w_pre, 0.0) + jnp.log(1.0 + jnp.exp(-jnp.abs(w_pre)))
    w_log = -softplus - 0.5
    lw = -jnp.exp(w_log)
    a = _sigmoid(a0_ref[...] + mm(xwa, a2_ref[...]))
    g = mm(_sigmoid(xg), g2_ref[...])
    kk_raw = k * kk_ref[...]
    k2 = k * (1.0 + (a - 1.0) * ka_ref[...])
    if t_valid < L:
        live = row < t_valid
        lw = jnp.where(live, lw, 0.0)
        kk_raw = jnp.where(live, kk_raw, 0.0)
        k2 = jnp.where(live, k2, 0.0)
        v = jnp.where(live, v, 0.0)

    ti = lax.broadcasted_iota(jnp.int32, (L, L), 0)
    si = lax.broadcasted_iota(jnp.int32, (L, L), 1)
    incl = ti >= si
    strict = ti > si
    clw = _mmp(_dot, [jnp.where(incl, 1.0, 0.0).astype(BF16)], _parts(lw, 3))
    mid = max(L // 2 - 1, 0)
    clw_mid = clw[mid:mid + 1, :]
    rel = clw - clw_mid
    p_incl = jnp.exp(rel)
    p_inv = jnp.exp(-rel)
    p_prev = jnp.exp(rel - lw)
    p_mid = jnp.exp(clw_mid)
    rt_all = r * p_incl
    kt_all = k2 * p_inv
    rk_all = r * k2 * rk_ref[...]

    sp, yp = STATE_PASSES, Y_PASSES
    kkh = [kk_raw[:, s] for s in sls]
    kkn = [x / jnp.maximum(jnp.sqrt(jnp.sum(x * x, axis=-1, keepdims=True)), 1e-12) for x in kkh]
    at = [-kkn[h] * p_prev[:, sls[h]] for h in heads]
    bt = [kkn[h] * a[:, sls[h]] * p_inv[:, sls[h]] for h in heads]
    rt = [rt_all[:, s] for s in sls]
    kt = [kt_all[:, s] for s in sls]
    vh = [v[:, s] for s in sls]
    at_s = [_parts(x, sp) for x in at]
    bt_s = [_parts(x, sp) for x in bt]
    kt_s = [_parts(x, sp) for x in kt]
    vh_s = [_parts(x, sp) for x in vh]
    rt_y = [_parts(x, yp) for x in rt]
    a_ab = [jnp.where(strict, _mmp(_dot_nt, at_s[h], bt_s[h]), 0.0) for h in heads]
    a_ak = [jnp.where(strict, _mmp(_dot_nt, at_s[h], kt_s[h]), 0.0) for h in heads]
    a_rb = [jnp.where(incl, _mmp(_dot_nt, rt_y[h], bt_s[h][:yp]), 0.0) for h in heads]
    a_rk = [jnp.where(incl, _mmp(_dot_nt, rt_y[h], kt_s[h][:yp]), 0.0) for h in heads]
    x = [jnp.concatenate([at[h], _mmp(_dot, _parts(a_ak[h], sp), vh_s[h])], axis=1) for h in heads]
    p = a_ab
    for level in range(max(int(math.log2(L)), 1)):
        if level > 0:
            p = [_mmp(_dot, ps, ps) for ps in p_s]
        p_s = [_parts(q, sp) for q in p]
        x = [x[h] + _mmp(_dot, p_s[h], _parts(x[h], sp)) for h in heads]
    x_s = [_parts(q, sp) for q in x]
    qy = [_mmp(_dot, _parts(a_rb[h], yp), x_s[h][:yp]) for h in heads]
    y0 = [qy[h][:, RWKV_HEAD:] + _mmp(_dot, _parts(a_rk[h], yp), vh_s[h][:yp]) for h in heads]
    qh = [rt[h] + qy[h][:, :RWKV_HEAD] for h in heads]
    s0 = [wkv_ref[0, h] * p_mid[:, sls[h]] for h in heads]
    s0_s = [_parts(q, sp) for q in s0]
    y = [_mmp(_dot_nt, _parts(qh[h], yp), s0_s[h][:yp]) + y0[h] for h in heads]
    wtb = [_mmp(_dot_tn, [q[:, :RWKV_HEAD] for q in x_s[h]], bt_s[h]) for h in heads]
    uv_s = [[jnp.concatenate([x_s[h][i][:, RWKV_HEAD:], vh_s[h][i]], axis=0) for i in range(sp)] for h in heads]
    bk_s = [[jnp.concatenate([bt_s[h][i], kt_s[h][i]], axis=0) for i in range(sp)] for h in heads]
    for h in heads:
        s_new = s0[h] + _mmp(_dot, s0_s[h], _parts(wtb[h], sp)) + _mmp(_dot_tn, uv_s[h], bk_s[h])
        wkv_ref[0, h] = s_new * p_incl[L - 1:L, sls[h]]
    for h in heads:
        mu_y = jnp.mean(y[h], axis=-1, keepdims=True)
        yc = y[h] - mu_y
        var = jnp.mean(yc * yc, axis=-1, keepdims=True)
        yn = yc * lax.rsqrt(var + GN_EPS) * lnw_ref[:, sls[h]] + lnb_ref[:, sls[h]]
        bonus = jnp.sum(rk_all[:, sls[h]], axis=-1, keepdims=True) * vh[h]
        o_ref[0, :, sls[h]] = ((yn + bonus) * g[:, sls[h]]).astype(o_ref.dtype)


def _rwkv(zr, shift0, wkv0, wts, chunk, t_valid):
    batch, t, _ = zr.shape
    n_chunks = t // chunk
    assert t_valid == t or n_chunks == 1
    body = functools.partial(_rwkv_body, chunk=chunk, t_valid=min(t_valid, chunk))
    vec = lambda width: pl.BlockSpec((1, width), lambda b, c: (0, 0))
    mat = lambda rows: pl.BlockSpec((rows, RWKV_W), lambda b, c: (0, 0))
    return pl.pallas_call(
        body,
        grid=(batch, n_chunks),
        in_specs=[pl.BlockSpec((1, chunk, RWKV_COLS), lambda b, c: (b, c, 0)),
                  pl.BlockSpec((1, 1, RWKV_COLS), lambda b, c: (b, 0, 0)),
                  pl.BlockSpec((1, RWKV_HEADS, RWKV_HEAD, RWKV_HEAD), lambda b, c: (b, 0, 0, 0)),
                  vec(RWKV_COLS), vec(RWKV_W), mat(LORA_W), vec(RWKV_W), mat(LORA_W), mat(GATE_LORA),
                  vec(RWKV_W), vec(RWKV_W), vec(RWKV_W), vec(RWKV_W), vec(RWKV_W)],
        out_specs=[pl.BlockSpec((1, chunk, RWKV_W), lambda b, c: (b, c, 0)),
                   pl.BlockSpec((1, RWKV_HEADS, RWKV_HEAD, RWKV_HEAD), lambda b, c: (b, 0, 0, 0))],
        out_shape=[jax.ShapeDtypeStruct((batch, t, RWKV_W), BF16),
                   jax.ShapeDtypeStruct((batch, RWKV_HEADS, RWKV_HEAD, RWKV_HEAD), F32)],
        scratch_shapes=[pltpu.VMEM((1, RWKV_COLS), F32)],
        compiler_params=_cparams(("parallel", "arbitrary")),
        name="rwkv_scan",
    )(zr, shift0, wkv0, *wts)


def _mix_out_body(x_ref, o0_ref, o1_ref, o2_ref, l0_ref, l1_ref, l2_ref, orw_ref, gate_ref, woa_ref, wor_ref,
                  wo_ref, gffn_ref, rwt_ref, rb_ref, cnt0_ref, *rest, dilated, n_alias):
    x1_ref, h2_ref, te_ref, tg_ref, rk_ref, cnt_ref = rest[n_alias:n_alias + 6]
    stage = list(rest[n_alias + 6:])

    def token_major(ref, gi):
        if not dilated:
            return ref[...].astype(F32)
        dil = ATTN_GROUPS[gi][1]
        if dil == 1:
            return ref[0, 0].astype(F32)
        st_ref = stage.pop()
        for r in range(dil):
            sub = ref[0, r].astype(F32)
            for s in range(GROUP_W // LANE):
                st_ref[s, pl.ds(r, ref.shape[2], stride=dil), :] = sub[:, s * LANE:(s + 1) * LANE]
        return jnp.concatenate([st_ref[s] for s in range(GROUP_W // LANE)], axis=1)

    l0, l1, l2 = token_major(l0_ref, 0), token_major(l1_ref, 1), token_major(l2_ref, 2)
    m = jnp.maximum(jnp.maximum(l0, l1), l2)
    e0, e1, e2 = jnp.exp(l0 - m), jnp.exp(l1 - m), jnp.exp(l2 - m)
    o_att = (e0 * token_major(o0_ref, 0) + e1 * token_major(o1_ref, 1) + e2 * token_major(o2_ref, 2)) \
        * (1.0 / (e0 + e1 + e2))
    gates = gate_ref[...].astype(F32)
    merged = gates[:, :D_MODEL] * _dot(o_att.astype(BF16), woa_ref[...]) \
        + gates[:, D_MODEL:] * _dot(orw_ref[...], wor_ref[...])
    x1 = x_ref[...] + _dot(merged.astype(BF16), wo_ref[...])
    x1_ref[...] = x1
    h2 = _rms(x1, gffn_ref[...])
    h2_ref[...] = h2

    logits = _mm3(_dot_nt, rwt_ref[...], h2) + rb_ref[...]
    e_iota = lax.broadcasted_iota(jnp.int32, logits.shape, 0)
    vals, idxs = [], []
    for _ in range(TOP_K):
        top = jnp.max(logits, axis=0, keepdims=True)
        idx = jnp.min(jnp.where(logits == top, e_iota, N_EXPERTS), axis=0, keepdims=True)
        vals.append(top)
        idxs.append(idx)
        logits = jnp.where(e_iota == idx, -jnp.inf, logits)
    exps = [jnp.exp(t - vals[0]) for t in vals]
    inv = 1.0 / (exps[0] + exps[1] + exps[2] + exps[3])
    te_ref[...] = jnp.concatenate(idxs, axis=0)
    tg_ref[...] = jnp.concatenate([e * inv for e in exps], axis=0)

    @pl.when(pl.program_id(0) == 0)
    def _():
        cnt_ref[...] = cnt0_ref[...]

    hits = [e_iota == idx for idx in idxs]
    onehot = jnp.where(hits[0] | hits[1] | hits[2] | hits[3], 1.0, 0.0)
    tm = onehot.shape[1]
    earlier = lax.broadcasted_iota(jnp.int32, (tm, tm), 0) < lax.broadcasted_iota(jnp.int32, (tm, tm), 1)
    before = cnt_ref[...] + _dot(onehot.astype(BF16), jnp.where(earlier, 1.0, 0.0).astype(BF16))
    rk_ref[...] = jnp.concatenate([jnp.sum(jnp.where(h, before, 0.0), axis=0, keepdims=True) for h in hits],
                                  axis=0).astype(jnp.int32)
    cnt_ref[...] += jnp.sum(onehot, axis=1, keepdims=True)


def _mix_out(x, o_g, lse_g, o_rwkv, gates, wts, cnt0, tm, n_all, row0, bufs):
    n = x.shape[0]
    blk0 = row0 // tm
    row = lambda i: (i, 0)
    const = lambda i: (0, 0)
    out_row = lambda i: (i + blk0, 0)
    out_col = lambda i: (0, i + blk0)
    tok = lambda w: pl.BlockSpec((tm, w), row)
    dilated = o_g[0].ndim == 4
    if dilated:
        tiles = o_g[0].shape[1] * o_g[0].shape[2] // tm
        sub = lambda i: (i // tiles, 0, i % tiles, 0)
        att_specs = [pl.BlockSpec((1, d, tm // d, GROUP_W), sub) for _, d in ATTN_GROUPS] * 2
        stage = [pltpu.VMEM((GROUP_W // LANE, tm, LANE), F32) for _, d in ATTN_GROUPS if d > 1] * 2
    else:
        att_specs, stage = [tok(GROUP_W)] * 6, []
    in_specs = [tok(D_MODEL)] + att_specs + [tok(RWKV_W), tok(GATE_COLS),
                pl.BlockSpec((GROUP_W, D_MODEL), const), pl.BlockSpec((RWKV_W, D_MODEL), const),
                pl.BlockSpec((D_MODEL, D_MODEL), const), pl.BlockSpec((1, D_MODEL), const),
                pl.BlockSpec((N_EXPERTS, D_MODEL), const), pl.BlockSpec((N_EXPERTS, 1), const),
                pl.BlockSpec((N_EXPERTS, 1), const)]
    args = [x, *o_g, *lse_g, o_rwkv, gates, *wts, cnt0]
    aliases = {}
    if bufs is not None:
        in_specs += [pl.BlockSpec(memory_space=pl.ANY)] * 5
        aliases = {len(args) + j: j for j in range(5)}
        args += list(bufs)
    *new_bufs, cnt = pl.pallas_call(
        functools.partial(_mix_out_body, dilated=dilated, n_alias=len(aliases)),
        grid=(n // tm,),
        scratch_shapes=stage,
        in_specs=in_specs,
        out_specs=[pl.BlockSpec((tm, D_MODEL), out_row), pl.BlockSpec((tm, D_MODEL), out_row),
                   pl.BlockSpec((TOP_K, tm), out_col), pl.BlockSpec((TOP_K, tm), out_col),
                   pl.BlockSpec((TOP_K, tm), out_col), pl.BlockSpec((N_EXPERTS, 1), const)],
        out_shape=[jax.ShapeDtypeStruct((n_all, D_MODEL), F32), jax.ShapeDtypeStruct((n_all, D_MODEL), F32),
                   jax.ShapeDtypeStruct((TOP_K, n_all), jnp.int32), jax.ShapeDtypeStruct((TOP_K, n_all), F32),
                   jax.ShapeDtypeStruct((TOP_K, n_all), jnp.int32), jax.ShapeDtypeStruct((N_EXPERTS, 1), F32)],
        input_output_aliases=aliases,
        compiler_params=_cparams(("arbitrary",)),
        name="mix_out",
    )(*args)
    return new_bufs, cnt


def _moe_dispatch_body(tab_ref, idx_hbm, h2_hbm, x_hbm, idx_smem, hbuf, zrow, isem, fsem, dsem, zsem, *, n_tiles):
    i = pl.program_id(0)
    slot = i % 2
    n_buf, n_sem = DISPATCH_LAG + 2, DISPATCH_LAG + 1
    hslot = i % n_buf
    dslot = i % n_sem

    def tile_fetch(tile, hs):
        return pltpu.make_async_copy(h2_hbm.at[pl.ds(pl.multiple_of(tile * MOE_TILE, MOE_TILE), MOE_TILE)],
                                     hbuf.at[hs], fsem.at[hs])

    def idx_copy(rec, s):
        return pltpu.make_async_copy(idx_hbm.at[pl.ds(pl.multiple_of(rec * IDX_REC, IDX_REC), IDX_REC)],
                                     idx_smem.at[pl.ds(pl.multiple_of(s * IDX_REC, IDX_REC), IDX_REC)],
                                     isem.at[s])

    def row_copies_start(s):
        def one(t, carry):
            for k in range(TOP_K):
                d = idx_smem[s * IDX_REC + k * MOE_TILE + t]
                pltpu.make_async_copy(hbuf.at[hslot, pl.ds(t, 1)], x_hbm.at[pl.ds(d, 1)], dsem.at[dslot]).start()
            return carry
        for t in range(MOE_TILE):
            one(t, 0)

    @pl.when(i == 0)
    def _():
        idx_copy(0, 0).start()
        tile_fetch(0, 0).start()

    @pl.when(i + 1 < n_tiles)
    def _():
        idx_copy(i + 1, 1 - slot).start()
        tile_fetch(i + 1, (i + 1) % n_buf).start()

    idx_copy(i, slot).wait()
    tile_fetch(i, hslot).wait()
    row_copies_start(slot)

    def wait_rows(tile):
        for _ in range(TOP_K):
            pltpu.make_async_copy(hbuf.at[0], x_hbm.at[pl.ds(0, MOE_TILE)], dsem.at[tile % n_sem]).wait()

    @pl.when(i >= DISPATCH_LAG)
    def _():
        wait_rows(i - DISPATCH_LAG)

    @pl.when(i == n_tiles - 1)
    def _():
        for back in range(DISPATCH_LAG - 1, -1, -1):
            wait_rows(i - back)
        zrow[...] = jnp.zeros_like(zrow)

        def zero_copy(dst_row):
            return pltpu.make_async_copy(zrow.at[pl.ds(0, 1)], x_hbm.at[pl.ds(dst_row, 1)], zsem)

        def per_expert(e, carry):
            first = tab_ref[e] + tab_ref[N_EXPERTS + e]
            last = tab_ref[e] + tab_ref[2 * N_EXPERTS + e]
            lax.fori_loop(first, last, lambda r, c: (zero_copy(r).start(), c)[1], 0)
            lax.fori_loop(first, last, lambda r, c: (zero_copy(r).wait(), c)[1], 0)
            return carry
        lax.fori_loop(0, N_EXPERTS, per_expert, 0)


def _moe_dispatch(h2, idx_rec, tables, n_rows):
    n_tiles = h2.shape[0] // MOE_TILE
    grid_spec = pltpu.PrefetchScalarGridSpec(
        num_scalar_prefetch=1,
        grid=(n_tiles,),
        in_specs=[pl.BlockSpec(memory_space=pl.ANY),
                  pl.BlockSpec(memory_space=pl.ANY)],
        out_specs=pl.BlockSpec(memory_space=pl.ANY),
        scratch_shapes=[pltpu.SMEM((2 * IDX_REC,), jnp.int32),
                        pltpu.VMEM((DISPATCH_LAG + 2, MOE_TILE, D_MODEL), F32),
                        pltpu.VMEM((8, D_MODEL), F32),
                        pltpu.SemaphoreType.DMA((2,)),
                        pltpu.SemaphoreType.DMA((DISPATCH_LAG + 2,)),
                        pltpu.SemaphoreType.DMA((DISPATCH_LAG + 1,)),
                        pltpu.SemaphoreType.DMA],
    )
    return pl.pallas_call(
        functools.partial(_moe_dispatch_body, n_tiles=n_tiles),
        grid_spec=grid_spec,
        out_shape=jax.ShapeDtypeStruct((n_rows, D_MODEL), F32),
        compiler_params=_cparams(("arbitrary",)),
        name="moe_dispatch",
    )(tables, idx_rec, h2)


def _moe_experts_body(be_ref, nused_ref, x_ref, w1_ref, b1_ref, w2_ref, b2_ref, y_ref, w1b, w2b):
    i = pl.program_id(0)

    @pl.when(i < nused_ref[0])
    def _():
        @pl.when((i == 0) | (be_ref[i] != be_ref[jnp.maximum(i - 1, 0)]))
        def _():
            w1b[...] = w1_ref[0].astype(BF16)
            w2b[...] = w2_ref[0].astype(BF16)

        xb = x_ref[...].astype(BF16)
        glu = jnp.minimum(_dot(xb, w1b[:, :D_MODEL]) + b1_ref[0, :, :D_MODEL], SWIGLU_LIMIT)
        lin = jnp.clip(_dot(xb, w1b[:, D_MODEL:]) + b1_ref[0, :, D_MODEL:], -SWIGLU_LIMIT, SWIGLU_LIMIT)
        act = glu * _sigmoid(SWIGLU_ALPHA * glu) * (lin + 1.0)
        y_ref[...] = _dot(act.astype(BF16), w2b[...]) + b2_ref[0]

    @pl.when(i >= nused_ref[0])
    def _():
        y_ref[...] = jnp.zeros_like(y_ref)


def _moe_experts(x_rows, block_e, n_used, w1, b1, w2, b2):
    n_blocks = block_e.shape[0]
    by_expert = lambda i, be, nu: (be[i], 0, 0)
    grid_spec = pltpu.PrefetchScalarGridSpec(
        num_scalar_prefetch=2,
        grid=(n_blocks,),
        in_specs=[pl.BlockSpec((MOE_BLOCK, D_MODEL), lambda i, be, nu: (jnp.minimum(i, nu[0] - 1), 0)),
                  pl.BlockSpec((1, D_MODEL, 2 * D_MODEL), by_expert),
                  pl.BlockSpec((1, 1, 2 * D_MODEL), by_expert),
                  pl.BlockSpec((1, D_MODEL, D_MODEL), by_expert),
                  pl.BlockSpec((1, 1, D_MODEL), by_expert)],
        out_specs=pl.BlockSpec((MOE_BLOCK, D_MODEL), lambda i, be, nu: (i, 0)),
        scratch_shapes=[pltpu.VMEM((D_MODEL, 2 * D_MODEL), BF16),
                        pltpu.VMEM((D_MODEL, D_MODEL), BF16)],
    )
    return pl.pallas_call(
        _moe_experts_body,
        grid_spec=grid_spec,
        out_shape=jax.ShapeDtypeStruct((n_blocks * MOE_BLOCK, D_MODEL), F32),
        compiler_params=_cparams(("arbitrary",)),
        name="moe_experts",
    )(block_e, n_used, x_rows, w1, b1, w2, b2)


def _route(top_e, rank, counts):
    n_tok = top_e.shape[1]
    counts = counts.reshape(N_EXPERTS).astype(jnp.int32)
    padded = (counts + MOE_BLOCK - 1) // MOE_BLOCK * MOE_BLOCK
    pad_end = jnp.cumsum(padded)
    pad_start = pad_end - padded
    experts = jnp.arange(N_EXPERTS, dtype=jnp.int32)
    dest = rank + jnp.sum(jnp.where(top_e[..., None] == experts, pad_start, 0), axis=-1)
    n_blocks = -(-(n_tok * TOP_K + N_EXPERTS * (MOE_BLOCK - 1)) // MOE_BLOCK)
    blk_row0 = jnp.arange(n_blocks, dtype=jnp.int32) * MOE_BLOCK
    block_e = jnp.minimum(jnp.sum(blk_row0[:, None] >= pad_end[None, :], axis=1), N_EXPERTS - 1).astype(jnp.int32)
    n_used = (pad_end[-1] // MOE_BLOCK).astype(jnp.int32).reshape(1)
    n_tiles = n_tok // MOE_TILE
    rec = dest.reshape(TOP_K, n_tiles, MOE_TILE).transpose(1, 0, 2).reshape(n_tiles, TOP_K * MOE_TILE)
    idx_rec = jnp.concatenate([rec, jnp.zeros((n_tiles, IDX_REC - TOP_K * MOE_TILE), jnp.int32)], axis=1).reshape(-1)
    tables = jnp.concatenate([pad_start, counts, padded]).astype(jnp.int32)
    return idx_rec, tables, block_e, n_used, n_blocks * MOE_BLOCK


def _tail_body(idx_hbm, y_hbm, x1_ref, tg_ref, pe_ref, gple_ref, wpg_ref, wp_ref, gfin_ref, yp_ref, ys_ref,
               idx_smem, gbuf, isem, gsem, *, n_tiles, n_prompt_tiles):
    i = pl.program_id(0)
    slot = i % 2

    def idx_copy(rec, s):
        return pltpu.make_async_copy(idx_hbm.at[pl.ds(pl.multiple_of(rec * IDX_REC, IDX_REC), IDX_REC)],
                                     idx_smem.at[pl.ds(pl.multiple_of(s * IDX_REC, IDX_REC), IDX_REC)],
                                     isem.at[s])

    @pl.when(i == 0)
    def _():
        idx_copy(0, 0).start()

    @pl.when(i < n_tiles)
    def _():
        idx_copy(i, slot).wait()

        @pl.when(i + 1 < n_tiles)
        def _():
            idx_copy(i + 1, 1 - slot).start()

        for t in range(MOE_TILE):
            for k in range(TOP_K):
                d = idx_smem[slot * IDX_REC + k * MOE_TILE + t]
                pltpu.make_async_copy(y_hbm.at[pl.ds(d, 1)], gbuf.at[slot, k, pl.ds(t, 1)], gsem.at[slot]).start()

    @pl.when(i >= 1)
    def _():
        prev = 1 - slot
        for k in range(TOP_K):
            pltpu.make_async_copy(y_hbm.at[pl.ds(0, MOE_TILE)], gbuf.at[prev, k], gsem.at[prev]).wait()
        tg = tg_ref[...]
        moe = tg[:, 0:1] * gbuf[prev, 0]
        for k in range(1, TOP_K):
            moe = moe + tg[:, k:k + 1] * gbuf[prev, k]
        x2 = x1_ref[...] + moe
        gate = _sigmoid(_dot(_rms(x2, gple_ref[...]).astype(BF16), wpg_ref[...]))
        x3 = x2 + gate * _dot(pe_ref[...].astype(BF16), wp_ref[...])
        y = _rms(x3, gfin_ref[...])

        @pl.when(i - 1 < n_prompt_tiles)
        def _():
            yp_ref[...] = y

        @pl.when(i - 1 >= n_prompt_tiles)
        def _():
            ys_ref[...] = y


def _tail(x1_all, y_rows, idx_rec, tg_t, pe_all, wts, n_p):
    n_all = x1_all.shape[0]
    n_tiles = n_all // MOE_TILE
    n_prompt_tiles = n_p // MOE_TILE
    row = lambda i: (jnp.maximum(i - 1, 0), 0)
    const = lambda i: (0, 0)
    body = functools.partial(_tail_body, n_tiles=n_tiles, n_prompt_tiles=n_prompt_tiles)
    return pl.pallas_call(
        body,
        grid=(n_tiles + 1,),
        in_specs=[pl.BlockSpec(memory_space=pl.ANY), pl.BlockSpec(memory_space=pl.ANY),
                  pl.BlockSpec((MOE_TILE, D_MODEL), row), pl.BlockSpec((MOE_TILE, TOP_K), row),
                  pl.BlockSpec((MOE_TILE, PLE_DIM), row), pl.BlockSpec((1, D_MODEL), const),
                  pl.BlockSpec((D_MODEL, D_MODEL), const), pl.BlockSpec((PLE_DIM, D_MODEL), const),
                  pl.BlockSpec((1, D_MODEL), const)],
        out_specs=[pl.BlockSpec((MOE_TILE, D_MODEL),
                                lambda i: (jnp.clip(i - 1, 0, n_prompt_tiles - 1), 0)),
                   pl.BlockSpec((MOE_TILE, D_MODEL), lambda i: (jnp.maximum(i - 1 - n_prompt_tiles, 0), 0))],
        out_shape=[jax.ShapeDtypeStruct((n_p, D_MODEL), F32),
                   jax.ShapeDtypeStruct((n_all - n_p, D_MODEL), F32)],
        scratch_shapes=[pltpu.SMEM((2 * IDX_REC,), jnp.int32),
                        pltpu.VMEM((2, TOP_K, MOE_TILE, D_MODEL), F32),
                        pltpu.SemaphoreType.DMA((2,)),
                        pltpu.SemaphoreType.DMA((2,))],
        compiler_params=_cparams(("arbitrary",)),
        name="tail",
    )(idx_rec, y_rows, x1_all, tg_t, pe_all, *wts)


RWKV_CHUNK = 128
RWKV_CHUNK_SAMPLE = 8
STATE_PASSES = 1
Y_PASSES = 1
TM_PROMPT = 512


def kernel(x_prompt, x_sample, p_prompt, p_sample, cache_kv_w128, cache_kv_w512, cache_kv_w2048, state_rwkv_shift, state_rwkv_wkv, norm_mix_g, w_in, rwkv_mu, rwkv_w0, rwkv_w2, rwkv_a0, rwkv_a2, rwkv_g2, rwkv_k_k, rwkv_k_a, rwkv_r_k, rwkv_ln_w, rwkv_ln_b, w_out_attn, w_out_rwkv, w_out, norm_ffn_g, router_w, router_b, moe_w1, moe_b1, moe_w2, moe_b2, norm_ple_g, w_ple, w_ple_gate, norm_final_g):
    bp, seq, _ = x_prompt.shape
    bs, t_s, _ = x_sample.shape
    n_p, n_s = bp * seq, bs * t_s
    n_all = n_p + n_s
    assert w_in.shape[0] == 1, "single layer"
    caches = (cache_kv_w128, cache_kv_w512, cache_kv_w2048)

    row = lambda a: a.reshape(1, -1)
    w_in_b = w_in[0].astype(BF16)
    zeros64 = jnp.zeros((LORA_W // 2, RWKV_W), F32)
    rwkv_wts = (row(rwkv_mu[0]), row(rwkv_w0[0]), jnp.concatenate([rwkv_w2[0], zeros64], axis=0),
                row(rwkv_a0[0]), jnp.concatenate([zeros64, rwkv_a2[0]], axis=0), rwkv_g2[0],
                row(rwkv_k_k[0]), row(rwkv_k_a[0]), row(rwkv_r_k[0]), row(rwkv_ln_w[0]), row(rwkv_ln_b[0]))
    mix_wts = (w_out_attn[0].astype(BF16), w_out_rwkv[0].astype(BF16), w_out[0].astype(BF16),
               row(norm_ffn_g[0]), router_w[0].T, router_b[0].reshape(N_EXPERTS, 1))
    tail_wts = (row(norm_ple_g[0]), w_ple_gate[0].astype(BF16), w_ple[0].astype(BF16), row(norm_final_g))
    g_mix = row(norm_mix_g[0])

    xp = x_prompt.reshape(n_p, D_MODEL)
    qd_p, kvd_p, kvt_p, zr_p, gate_p = _in_proj_dilated(xp, g_mix, w_in_b, TM_PROMPT, bp, seq)
    att_p = [_attn_prompt(qd_p[gi], kvd_p[gi], gi) for gi in range(N_GROUPS)]
    orw_p, wkv_p = _rwkv(zr_p.reshape(bp, seq, RWKV_COLS), jnp.zeros((bp, 1, RWKV_COLS), F32),
                         jnp.zeros((bp, RWKV_HEADS, RWKV_HEAD, RWKV_HEAD), F32), rwkv_wts, RWKV_CHUNK, seq)
    bufs, counts = _mix_out(xp, [a[0] for a in att_p], [a[1] for a in att_p], orw_p.reshape(n_p, RWKV_W), gate_p,
                            mix_wts, jnp.zeros((N_EXPERTS, 1), F32), TM_PROMPT, n_all, 0, None)

    xs = x_sample.reshape(n_s, D_MODEL)
    q_s, kv_s, zr_s, gate_s = _in_proj(xs, g_mix, w_in_b, n_s)
    t_pad = 8
    q_s3 = jnp.pad(q_s.reshape(bs, t_s, Q_COLS), ((0, 0), (0, t_pad - t_s), (0, 0)))
    feat_major = lambda a: jnp.swapaxes(a, 1, 2)
    tail_t = jnp.pad(feat_major(kv_s.reshape(bs, t_s, KV_COLS)), ((0, 0), (0, 0), (LANE - t_s, 0)))
    att_s, new_caches = [], []
    for gi in range(N_GROUPS):
        wc = caches[gi].shape[2]
        o, lse, newc = _attn_sample(q_s3, tail_t, feat_major(caches[gi].reshape(bs, wc, 2 * GROUP_W)), gi, t_s)
        att_s.append((o[:, :t_s].reshape(n_s, GROUP_W), lse[:, :t_s].reshape(n_s, GROUP_W)))
        new_caches.append(feat_major(newc).reshape(1, bs, wc, 2, HEADS_PER_GROUP, HEAD_DIM))
    zr_s3 = jnp.pad(zr_s.reshape(bs, t_s, RWKV_COLS), ((0, 0), (0, RWKV_CHUNK_SAMPLE - t_s), (0, 0)))
    orw_s, wkv_s = _rwkv(zr_s3, state_rwkv_shift[0].reshape(bs, 1, RWKV_COLS), state_rwkv_wkv[0], rwkv_wts,
                         RWKV_CHUNK_SAMPLE, t_s)
    bufs, counts = _mix_out(xs, [a[0] for a in att_s], [a[1] for a in att_s], orw_s[:, :t_s].reshape(n_s, RWKV_W),
                            gate_s, mix_wts, counts, n_s, n_all, n_p, bufs)
    x1_all, h2_all, top_e, top_g, rank = bufs

    idx_rec, tables, block_e, n_used, n_rows = _route(top_e, rank, counts)
    x_rows = _moe_dispatch(h2_all, idx_rec, tables, n_rows)
    y_rows = _moe_experts(x_rows, block_e, n_used, moe_w1[0], moe_b1[0].reshape(N_EXPERTS, 1, 2 * D_MODEL),
                          moe_w2[0], moe_b2[0].reshape(N_EXPERTS, 1, D_MODEL))
    pe_all = jnp.concatenate([p_prompt[0].reshape(n_p, PLE_DIM), p_sample[0].reshape(n_s, PLE_DIM)], axis=0)
    y_p, y_s = _tail(x1_all, y_rows, idx_rec, top_g.T, pe_all, tail_wts, n_p)

    kv_out_p = [feat_major(t).reshape(1, bp, t.shape[2], 2, HEADS_PER_GROUP, HEAD_DIM) for t in kvt_p]
    shift_p = zr_p.reshape(bp, seq, RWKV_COLS)[:, -1][None]
    shift_s = zr_s.reshape(bs, t_s, RWKV_COLS)[:, -1][None]
    return (y_p.reshape(bp, seq, D_MODEL), y_s.reshape(bs, t_s, D_MODEL),
            kv_out_p[0], kv_out_p[1], kv_out_p[2], shift_p, wkv_p[None],
            new_caches[0], new_caches[1], new_caches[2], shift_s, wkv_s[None])
```

```python
import functools
import math

import numpy as np
import jax
import jax.numpy as jnp
from jax import lax
from jax.experimental import pallas as pl
from jax.experimental.pallas import tpu as pltpu

F32 = jnp.float32
BF16 = jnp.bfloat16

LANE = 128
D_MODEL = 1024
N_GROUPS = 3
HEADS_PER_GROUP = 4
HEAD_DIM = 64
ATTN_GROUPS = ((128, 1), (512, 4), (2048, 16))
GROUP_W = HEADS_PER_GROUP * HEAD_DIM
Q_COLS = N_GROUPS * GROUP_W
KV_COLS = 2 * Q_COLS
BAND = 128

RWKV_HEADS = 8
RWKV_HEAD = 64
RWKV_W = 512
LORA_W = 128
GATE_LORA = 128
RWKV_COLS = 3 * RWKV_W + LORA_W + GATE_LORA
GN_EPS = 64e-5
GATE_COLS = 2 * D_MODEL
Z_RWKV0 = 3 * Q_COLS
Z_GATE0 = Z_RWKV0 + RWKV_COLS
IN_COLS = Z_GATE0 + GATE_COLS

N_EXPERTS = 32
TOP_K = 4
SWIGLU_LIMIT = 7.0
SWIGLU_ALPHA = 1.702
MOE_BLOCK = 512
MOE_TILE = 128
DISPATCH_LAG = 1
IDX_REC = 1024
PLE_DIM = 256
RMS_EPS = 1e-6

NEG_BIG = -1e30
VMEM_LIMIT = 56 * 1024 * 1024


def _cparams(sem):
    return pltpu.CompilerParams(dimension_semantics=sem, vmem_limit_bytes=VMEM_LIMIT)


def _rms(x, g):
    return x * lax.rsqrt(jnp.mean(x * x, axis=-1, keepdims=True) + RMS_EPS) * g


def _sigmoid(x):
    return 1.0 / (1.0 + jnp.exp(-x))


def _dot(a, b):
    return jnp.dot(a, b, preferred_element_type=F32)


def _dot_nt(a, b):
    return lax.dot_general(a, b, (((1,), (1,)), ((), ())), preferred_element_type=F32)


def _dot_tn(a, b):
    return lax.dot_general(a, b, (((0,), (0,)), ((), ())), preferred_element_type=F32)


def _split(x):
    hi = x.astype(BF16)
    lo = (x - hi.astype(F32)).astype(BF16)
    return hi, lo


def _mm3(dot, a, b):
    ah, al = _split(a)
    bh, bl = _split(b)
    return dot(ah, bh) + (dot(ah, bl) + dot(al, bh))


def _parts(x, n):
    out = []
    for _ in range(n - 1):
        hi = x.astype(BF16)
        out.append(hi)
        x = x - hi.astype(F32)
    out.append(x.astype(BF16))
    return out


def _mmp(dot, ap, bp):
    order = max(len(ap), len(bp))
    acc = None
    for i, a in enumerate(ap):
        for j, b in enumerate(bp):
            if i + j < order:
                t = dot(a, b)
                acc = t if acc is None else acc + t
    return acc


def _in_proj_body(x_ref, g_ref, w_ref, q_ref, kv_ref, zr_ref, gate_ref):
    h = _rms(x_ref[...], g_ref[...]).astype(BF16)

    def proj(lo, width):
        return _dot(h, w_ref[:, lo:lo + width])

    q_ref[...] = (proj(0, Q_COLS) * (1.0 / math.sqrt(HEAD_DIM))).astype(BF16)
    for g in range(N_GROUPS):
        kv_ref[:, 2 * g * GROUP_W:(2 * g + 1) * GROUP_W] = proj(Q_COLS + g * GROUP_W, GROUP_W)
        kv_ref[:, (2 * g + 1) * GROUP_W:(2 * g + 2) * GROUP_W] = proj(2 * Q_COLS + g * GROUP_W, GROUP_W)
    zr_ref[...] = proj(Z_RWKV0, RWKV_COLS)
    gate_ref[...] = _sigmoid(proj(Z_GATE0, GATE_COLS)).astype(BF16)


def _in_proj_dilated_body(x_ref, g_ref, w_ref, *rest, tiles_per_seq):
    qd_refs, kvd_refs, kvt_refs = rest[0:N_GROUPS], rest[N_GROUPS:2 * N_GROUPS], rest[2 * N_GROUPS:3 * N_GROUPS]
    zr_ref, gate_ref, st_ref = rest[3 * N_GROUPS:]
    tm = x_ref.shape[0]
    tile_in_seq = pl.program_id(0) % tiles_per_seq
    h = _rms(x_ref[...], g_ref[...]).astype(BF16)

    def proj(lo, width):
        return _dot(h, w_ref[:, lo:lo + width])

    n_q = Q_COLS // LANE

    def stage(slab0, val):
        for s in range(val.shape[1] // LANE):
            st_ref[slab0 + s] = val[:, s * LANE:(s + 1) * LANE]

    stage(0, proj(0, Q_COLS) * (1.0 / math.sqrt(HEAD_DIM)))
    for g in range(N_GROUPS):
        for part, src in ((0, Q_COLS), (1, 2 * Q_COLS)):
            col = (2 * g + part) * GROUP_W
            stage(n_q + col // LANE, proj(src + g * GROUP_W, GROUP_W))
    for g, (window, _) in enumerate(ATTN_GROUPS):
        rows = min(window, tm)
        first_tile = tiles_per_seq - max(window // tm, 1)

        @pl.when(tile_in_seq >= first_tile)
        def _():
            for s in range(2 * GROUP_W // LANE):
                slab = st_ref[n_q + 2 * g * GROUP_W // LANE + s, tm - rows:tm, :]
                kvt_refs[g][0, s * LANE:(s + 1) * LANE, :] = slab.T
    for g, (_, dil) in enumerate(ATTN_GROUPS):
        for r in range(dil):
            rows = pl.ds(r, tm // dil, stride=dil) if dil > 1 else slice(None)
            for s in range(GROUP_W // LANE):
                qd_refs[g][0, r, :, s * LANE:(s + 1) * LANE] = st_ref[g * GROUP_W // LANE + s, rows, :].astype(BF16)
            for s in range(2 * GROUP_W // LANE):
                kvd_refs[g][0, r, :, s * LANE:(s + 1) * LANE] = \
                    st_ref[n_q + 2 * g * GROUP_W // LANE + s, rows, :].astype(BF16)
    zr_ref[...] = proj(Z_RWKV0, RWKV_COLS)
    gate_ref[...] = _sigmoid(proj(Z_GATE0, GATE_COLS)).astype(BF16)


def _in_proj_dilated(x, g, w_bf16, tm, batch, seq):
    n = x.shape[0]
    tiles = seq // tm
    row = lambda i: (i, 0)
    const = lambda i: (0, 0)
    sub = lambda i: (i // tiles, 0, i % tiles, 0)
    dils = [d for _, d in ATTN_GROUPS]
    assert all(tm % (16 * d) == 0 for d in dils)
    wins = [min(w, seq) for w, _ in ATTN_GROUPS]
    assert all(w % tm == 0 or tm % w == 0 for w in wins)

    def tail_spec(w):
        cols = min(w, tm)
        first = tiles - max(w // tm, 1)
        return pl.BlockSpec((1, 2 * GROUP_W, cols), lambda i: (i // tiles, 0, jnp.maximum(i % tiles - first, 0)))

    outs = pl.pallas_call(
        functools.partial(_in_proj_dilated_body, tiles_per_seq=tiles),
        grid=(n // tm,),
        in_specs=[pl.BlockSpec((tm, D_MODEL), row),
                  pl.BlockSpec((1, D_MODEL), const),
                  pl.BlockSpec((D_MODEL, IN_COLS), const, pipeline_mode=pl.Buffered(1))],
        out_specs=[pl.BlockSpec((1, d, tm // d, GROUP_W), sub) for d in dils]
                  + [pl.BlockSpec((1, d, tm // d, 2 * GROUP_W), sub) for d in dils]
                  + [tail_spec(w) for w in wins]
                  + [pl.BlockSpec((tm, RWKV_COLS), row),
                     pl.BlockSpec((tm, GATE_COLS), row)],
        out_shape=[jax.ShapeDtypeStruct((batch, d, seq // d, GROUP_W), BF16) for d in dils]
                  + [jax.ShapeDtypeStruct((batch, d, seq // d, 2 * GROUP_W), BF16) for d in dils]
                  + [jax.ShapeDtypeStruct((batch, 2 * GROUP_W, w), F32) for w in wins]
                  + [jax.ShapeDtypeStruct((n, RWKV_COLS), F32),
                     jax.ShapeDtypeStruct((n, GATE_COLS), BF16)],
        scratch_shapes=[pltpu.VMEM(((Q_COLS + KV_COLS) // LANE, tm, LANE), F32)],
        compiler_params=_cparams(("arbitrary",)),
        name="in_proj_dilated",
    )(x, g, w_bf16)
    g3 = N_GROUPS
    return outs[0:g3], outs[g3:2 * g3], outs[2 * g3:3 * g3], outs[3 * g3], outs[3 * g3 + 1]


def _in_proj(x, g, w_bf16, tm):
    n = x.shape[0]
    row = lambda i: (i, 0)
    const = lambda i: (0, 0)
    return pl.pallas_call(
        _in_proj_body,
        grid=(n // tm,),
        in_specs=[pl.BlockSpec((tm, D_MODEL), row),
                  pl.BlockSpec((1, D_MODEL), const),
                  pl.BlockSpec((D_MODEL, IN_COLS), const)],
        out_specs=[pl.BlockSpec((tm, Q_COLS), row),
                   pl.BlockSpec((tm, KV_COLS), row),
                   pl.BlockSpec((tm, RWKV_COLS), row),
                   pl.BlockSpec((tm, GATE_COLS), row)],
        out_shape=[jax.ShapeDtypeStruct((n, Q_COLS), BF16),
                   jax.ShapeDtypeStruct((n, KV_COLS), F32),
                   jax.ShapeDtypeStruct((n, RWKV_COLS), F32),
                   jax.ShapeDtypeStruct((n, GATE_COLS), BF16)],
        compiler_params=_cparams(("parallel",)),
        name="in_proj",
    )(x, g, w_bf16)


def _alibi_slopes(gi):
    return [2.0 ** (-8.0 * (gi * HEADS_PER_GROUP + h + 1) / (N_GROUPS * HEADS_PER_GROUP))
            for h in range(HEADS_PER_GROUP)]


def _head_of_lane(shape):
    return lax.broadcasted_iota(jnp.int32, shape, len(shape) - 1) // HEAD_DIM


def _stack_heads(q):
    hl = _head_of_lane(q.shape)
    return jnp.concatenate([jnp.where(hl == h, q, jnp.zeros_like(q)) for h in range(HEADS_PER_GROUP)], axis=0)


def _unstack_heads(x4, rows):
    hl = _head_of_lane((rows, GROUP_W))
    out = x4[0:rows]
    for h in range(1, HEADS_PER_GROUP):
        out = jnp.where(hl == h, x4[h * rows:(h + 1) * rows], out)
    return out


def _attn_prompt_body(q_ref, kvp_ref, kvc_ref, o_ref, lse_ref, kb_ref, bias_ref, *, dil, slopes, n_sub):
    c = pl.program_id(2)
    kb_ref[0:BAND, :] = kvp_ref[0, 0]
    kb_ref[BAND:, :] = kvc_ref[0, 0]

    qi = lax.broadcasted_iota(jnp.int32, (BAND, 2 * BAND), 0) + BAND
    ki = lax.broadcasted_iota(jnp.int32, (BAND, 2 * BAND), 1)
    rel = qi - ki
    dist = jnp.where((rel >= 0) & (rel <= BAND), (dil * rel).astype(F32), -NEG_BIG / slopes[-1])
    for h in range(HEADS_PER_GROUP):
        bias_ref[h * BAND:(h + 1) * BAND, :] = -slopes[h] * dist

    def sub_block(n, carry):
        r0 = pl.multiple_of(n * BAND, BAND)
        q4 = _stack_heads(q_ref[0, 0, pl.ds(r0, BAND), :])
        kv = kb_ref[pl.ds(r0, 2 * BAND), :]
        s = _dot_nt(q4, kv[:, :GROUP_W]) + bias_ref[...]
        n_before_start = jnp.where((c == 0) & (n == 0), BAND, 0)
        kcol = lax.broadcasted_iota(jnp.int32, s.shape, 1)
        s = jnp.where(kcol < n_before_start, NEG_BIG, s)
        m = jnp.max(s, axis=-1, keepdims=True)
        e = jnp.exp(s - m)
        den = jnp.sum(e, axis=-1, keepdims=True)
        o4 = _dot(e.astype(BF16), kv[:, GROUP_W:]) * (1.0 / den)
        o_ref[0, 0, pl.ds(r0, BAND), :] = _unstack_heads(o4, BAND).astype(o_ref.dtype)
        lse4 = jnp.broadcast_to(m + jnp.log(den), (HEADS_PER_GROUP * BAND, GROUP_W))
        lse_ref[0, 0, pl.ds(r0, BAND), :] = _unstack_heads(lse4, BAND)
        return carry

    lax.fori_loop(0, n_sub, sub_block, 0, unroll=min(n_sub, 4))


def _attn_prompt(qd, kvd, gi):
    window, dil = ATTN_GROUPS[gi]
    assert window // dil == BAND
    batch, _, sub_len, _ = qd.shape
    chunk = min(sub_len, 1024)
    n_chunks = sub_len // chunk
    sub_per_chunk = chunk // BAND
    body = functools.partial(_attn_prompt_body, dil=dil, slopes=_alibi_slopes(gi), n_sub=sub_per_chunk)
    cur = lambda b, r, c: (b, r, c, 0)
    return pl.pallas_call(
        body,
        grid=(batch, dil, n_chunks),
        in_specs=[
            pl.BlockSpec((1, 1, chunk, GROUP_W), cur),
            pl.BlockSpec((1, 1, BAND, 2 * GROUP_W),
                         lambda b, r, c: (b, r, jnp.maximum(c * sub_per_chunk - 1, 0), 0)),
            pl.BlockSpec((1, 1, chunk, 2 * GROUP_W), cur),
        ],
        out_specs=[pl.BlockSpec((1, 1, chunk, GROUP_W), cur), pl.BlockSpec((1, 1, chunk, GROUP_W), cur)],
        out_shape=[jax.ShapeDtypeStruct(qd.shape, BF16), jax.ShapeDtypeStruct(qd.shape, F32)],
        scratch_shapes=[pltpu.VMEM((chunk + BAND, 2 * GROUP_W), BF16),
                        pltpu.VMEM((HEADS_PER_GROUP * BAND, 2 * BAND), F32)],
        compiler_params=_cparams(("parallel", "parallel", "arbitrary")),
        name=f"attn_prompt_g{gi}",
    )(qd, kvd, kvd)


def _attn_sample_body(q_ref, tail_ref, cache_ref, o_ref, lse_ref, newc_ref, *, dil, slopes, wc, t_new):
    tp = q_ref.shape[1]
    cache = cache_ref[0]
    tail = tail_ref[0]
    newc_ref[0] = pltpu.roll(cache, wc - t_new, 1)
    lane = lax.broadcasted_iota(jnp.int32, tail.shape, 1)
    newc_ref[0, :, wc - LANE:wc] = jnp.where(lane >= LANE - t_new, tail, newc_ref[0, :, wc - LANE:wc])

    q4 = _stack_heads(q_ref[0])
    cb = cache.astype(BF16)
    nb = tail.astype(BF16)
    t_row = lax.broadcasted_iota(jnp.int32, (tp, 1), 0)
    slope_col = jnp.concatenate([jnp.full((tp, 1), s, F32) for s in slopes], axis=0)
    t4 = jnp.concatenate([t_row] * HEADS_PER_GROUP, axis=0)

    def scores(keys_t_bf16, dist):
        ok = (dist >= 0) & ((dist & (dil - 1)) == 0) & (dist <= BAND * dil)
        s = _dot(q4, keys_t_bf16[:GROUP_W, :])
        return jnp.where(ok, s - slope_col * dist.astype(F32), NEG_BIG)

    pc = lax.broadcasted_iota(jnp.int32, (HEADS_PER_GROUP * tp, wc), 1)
    pn = lax.broadcasted_iota(jnp.int32, (HEADS_PER_GROUP * tp, LANE), 1) - (LANE - t_new)
    s_c = scores(cb, wc + t4 - pc)
    s_n = scores(nb, jnp.where(pn >= 0, t4 - pn, -1))
    m = jnp.maximum(jnp.max(s_c, axis=-1, keepdims=True), jnp.max(s_n, axis=-1, keepdims=True))
    e_c = jnp.exp(s_c - m)
    e_n = jnp.exp(s_n - m)
    den = jnp.sum(e_c, axis=-1, keepdims=True) + jnp.sum(e_n, axis=-1, keepdims=True)
    o4 = (_dot_nt(e_c.astype(BF16), cb[GROUP_W:, :]) + _dot_nt(e_n.astype(BF16), nb[GROUP_W:, :])) * (1.0 / den)
    o_ref[0] = _unstack_heads(o4, tp).astype(o_ref.dtype)
    lse4 = jnp.broadcast_to(m + jnp.log(den), (HEADS_PER_GROUP * tp, GROUP_W))
    lse_ref[0] = _unstack_heads(lse4, tp)


def _attn_sample(q, tail_t, cache_t, gi, t_new):
    window, dil = ATTN_GROUPS[gi]
    batch, tp, _ = q.shape
    wc = cache_t.shape[2]
    assert wc + 0 - dil * BAND >= 0
    body = functools.partial(_attn_sample_body, dil=dil, slopes=_alibi_slopes(gi), wc=wc, t_new=t_new)
    return pl.pallas_call(
        body,
        grid=(batch,),
        in_specs=[pl.BlockSpec((1, tp, GROUP_W), lambda b: (b, 0, gi)),
                  pl.BlockSpec((1, 2 * GROUP_W, LANE), lambda b: (b, gi, 0)),
                  pl.BlockSpec((1, 2 * GROUP_W, wc), lambda b: (b, 0, 0))],
        out_specs=[pl.BlockSpec((1, tp, GROUP_W), lambda b: (b, 0, 0)),
                   pl.BlockSpec((1, tp, GROUP_W), lambda b: (b, 0, 0)),
                   pl.BlockSpec((1, 2 * GROUP_W, wc), lambda b: (b, 0, 0))],
        out_shape=[jax.ShapeDtypeStruct((batch, tp, GROUP_W), BF16),
                   jax.ShapeDtypeStruct((batch, tp, GROUP_W), F32),
                   jax.ShapeDtypeStruct((batch, 2 * GROUP_W, wc), F32)],
        compiler_params=_cparams(("parallel",)),
        name=f"attn_sample_g{gi}",
    )(q, tail_t, cache_t)


def _rwkv_body(zr_ref, shift0_ref, wkv0_ref, mu_ref, w0_ref, w2_ref, a0_ref, a2_ref, g2_ref, kk_ref, ka_ref,
               rk_ref, lnw_ref, lnb_ref, o_ref, wkv_ref, prev_ref, *, chunk, t_valid):
    c = pl.program_id(1)
    L = chunk
    n_seq = zr_ref.shape[0]
    heads = range(n_seq * RWKV_HEADS)
    seq_of = [h // RWKV_HEADS for h in heads]
    head_of = [h % RWKV_HEADS for h in heads]
    sls = [slice(head_of[h] * RWKV_HEAD, (head_of[h] + 1) * RWKV_HEAD) for h in heads]

    @pl.when(c == 0)
    def _():
        for b in range(n_seq):
            prev_ref[b] = shift0_ref[b]
        wkv_ref[...] = wkv0_ref[...]

    row = lax.broadcasted_iota(jnp.int32, (L, 1), 0)
    ti = lax.broadcasted_iota(jnp.int32, (L, L), 0)
    si = lax.broadcasted_iota(jnp.int32, (L, L), 1)
    incl = ti >= si
    strict = ti > si
    seqs = [_rwkv_columns(zr_ref[b], prev_ref.at[b], row, incl, mu_ref, w0_ref, w2_ref, a0_ref, a2_ref, g2_ref,
                          kk_ref, ka_ref, rk_ref, L, t_valid) for b in range(n_seq)]
    col = lambda name, h: seqs[seq_of[h]][name][:, sls[h]]

    sp, yp = STATE_PASSES, Y_PASSES
    kkh = [col('kk_raw', h) for h in heads]
    kkn = [x / jnp.maximum(jnp.sqrt(jnp.sum(x * x, axis=-1, keepdims=True)), 1e-12) for x in kkh]
    at = [-kkn[h] * col('p_prev', h) for h in heads]
    bt = [kkn[h] * col('a', h) * col('p_inv', h) for h in heads]
    rt = [col('rt_all', h) for h in heads]
    kt = [col('kt_all', h) for h in heads]
    vh = [col('v', h) for h in heads]
    at_s = [_parts(x, sp) for x in at]
    bt_s = [_parts(x, sp) for x in bt]
    kt_s = [_parts(x, sp) for x in kt]
    vh_s = [_parts(x, sp) for x in vh]
    rt_y = [_parts(x, yp) for x in rt]
    a_ab = [jnp.where(strict, _mmp(_dot_nt, at_s[h], bt_s[h]), 0.0) for h in heads]
    a_ak = [jnp.where(strict, _mmp(_dot_nt, at_s[h], kt_s[h]), 0.0) for h in heads]
    a_rb = [jnp.where(incl, _mmp(_dot_nt, rt_y[h], bt_s[h][:yp]), 0.0) for h in heads]
    a_rk = [jnp.where(incl, _mmp(_dot_nt, rt_y[h], kt_s[h][:yp]), 0.0) for h in heads]
    x = [jnp.concatenate([at[h], _mmp(_dot, _parts(a_ak[h], sp), vh_s[h])], axis=1) for h in heads]
    p = a_ab
    for level in range(max(int(math.log2(L)), 1)):
        if level > 0:
            p = [_mmp(_dot, ps, ps) for ps in p_s]
        p_s = [_parts(q, sp) for q in p]
        x = [x[h] + _mmp(_dot, p_s[h], _parts(x[h], sp)) for h in heads]
    x_s = [_parts(q, sp) for q in x]
    qy = [_mmp(_dot, _parts(a_rb[h], yp), x_s[h][:yp]) for h in heads]
    y0 = [qy[h][:, RWKV_HEAD:] + _mmp(_dot, _parts(a_rk[h], yp), vh_s[h][:yp]) for h in heads]
    qh = [rt[h] + qy[h][:, :RWKV_HEAD] for h in heads]
    s0 = [wkv_ref[seq_of[h], head_of[h]] * col('p_mid', h) for h in heads]
    s0_s = [_parts(q, sp) for q in s0]
    y = [_mmp(_dot_nt, _parts(qh[h], yp), s0_s[h][:yp]) + y0[h] for h in heads]
    wtb = [_mmp(_dot_tn, [q[:, :RWKV_HEAD] for q in x_s[h]], bt_s[h]) for h in heads]
    uv_s = [[jnp.concatenate([x_s[h][i][:, RWKV_HEAD:], vh_s[h][i]], axis=0) for i in range(sp)] for h in heads]
    bk_s = [[jnp.concatenate([bt_s[h][i], kt_s[h][i]], axis=0) for i in range(sp)] for h in heads]
    for h in heads:
        s_new = s0[h] + _mmp(_dot, s0_s[h], _parts(wtb[h], sp)) + _mmp(_dot_tn, uv_s[h], bk_s[h])
        wkv_ref[seq_of[h], head_of[h]] = s_new * col('p_incl', h)[L - 1:L, :]
    for h in heads:
        mu_y = jnp.mean(y[h], axis=-1, keepdims=True)
        yc = y[h] - mu_y
        var = jnp.mean(yc * yc, axis=-1, keepdims=True)
        yn = yc * lax.rsqrt(var + GN_EPS) * lnw_ref[:, sls[h]] + lnb_ref[:, sls[h]]
        bonus = jnp.sum(col('rk_all', h), axis=-1, keepdims=True) * vh[h]
        o_ref[seq_of[h], :, sls[h]] = ((yn + bonus) * col('g', h)).astype(o_ref.dtype)


def _rwkv_columns(zr, prev_ref, row, incl, mu_ref, w0_ref, w2_ref, a0_ref, a2_ref, g2_ref, kk_ref, ka_ref, rk_ref,
                  L, t_valid):
    mm = functools.partial(_mm3, _dot)
    prev = jnp.where(row == 0, prev_ref[...], pltpu.roll(zr, 1, 0))
    prev_ref[...] = zr[L - 1:L, :]
    zm = zr + (prev - zr) * mu_ref[...]

    r = zm[:, 0:RWKV_W]
    k = zm[:, RWKV_W:2 * RWKV_W]
    v = zm[:, 2 * RWKV_W:3 * RWKV_W]
    xwa = zm[:, 3 * RWKV_W:3 * RWKV_W + LORA_W]
    xg = zm[:, 3 * RWKV_W + LORA_W:]
    w_pre = w0_ref[...] + mm(jnp.tanh(xwa), w2_ref[...])
    softplus = jnp.maximum(-w_pre, 0.0) + jnp.log(1.0 + jnp.exp(-jnp.abs(w_pre)))
    w_log = -softplus - 0.5
    lw = -jnp.exp(w_log)
    a = _sigmoid(a0_ref[...] + mm(xwa, a2_ref[...]))
    g = mm(_sigmoid(xg), g2_ref[...])
    kk_raw = k * kk_ref[...]
    k2 = k * (1.0 + (a - 1.0) * ka_ref[...])
    if t_valid < L:
        live = row < t_valid
        lw = jnp.where(live, lw, 0.0)
        kk_raw = jnp.where(live, kk_raw, 0.0)
        k2 = jnp.where(live, k2, 0.0)
        v = jnp.where(live, v, 0.0)

    clw = _mmp(_dot, [jnp.where(incl, 1.0, 0.0).astype(BF16)], _parts(lw, 3))
    mid = max(L // 2 - 1, 0)
    clw_mid = clw[mid:mid + 1, :]
    rel = clw - clw_mid
    p_incl = jnp.exp(rel)
    p_inv = jnp.exp(-rel)
    return dict(kk_raw=kk_raw, a=a, v=v, g=g, p_incl=p_incl, p_inv=p_inv, p_prev=jnp.exp(rel - lw),
                p_mid=jnp.exp(clw_mid),
                rt_all=r * p_incl, kt_all=k2 * p_inv, rk_all=r * k2 * rk_ref[...])


def _rwkv(zr, shift0, wkv0, wts, chunk, t_valid):
    batch, t, _ = zr.shape
    n_chunks = t // chunk
    n_seq = RWKV_SEQS_PER_STEP
    assert (t_valid == t or n_chunks == 1) and batch % n_seq == 0
    body = functools.partial(_rwkv_body, chunk=chunk, t_valid=min(t_valid, chunk))
    vec = lambda width: pl.BlockSpec((1, width), lambda b, c: (0, 0))
    mat = lambda rows: pl.BlockSpec((rows, RWKV_W), lambda b, c: (0, 0))
    return pl.pallas_call(
        body,
        grid=(batch // n_seq, n_chunks),
        in_specs=[pl.BlockSpec((n_seq, chunk, RWKV_COLS), lambda b, c: (b, c, 0)),
                  pl.BlockSpec((n_seq, 1, RWKV_COLS), lambda b, c: (b, 0, 0)),
                  pl.BlockSpec((n_seq, RWKV_HEADS, RWKV_HEAD, RWKV_HEAD), lambda b, c: (b, 0, 0, 0)),
                  vec(RWKV_COLS), vec(RWKV_W), mat(LORA_W), vec(RWKV_W), mat(LORA_W), mat(GATE_LORA),
                  vec(RWKV_W), vec(RWKV_W), vec(RWKV_W), vec(RWKV_W), vec(RWKV_W)],
        out_specs=[pl.BlockSpec((n_seq, chunk, RWKV_W), lambda b, c: (b, c, 0)),
                   pl.BlockSpec((n_seq, RWKV_HEADS, RWKV_HEAD, RWKV_HEAD), lambda b, c: (b, 0, 0, 0))],
        out_shape=[jax.ShapeDtypeStruct((batch, t, RWKV_W), BF16),
                   jax.ShapeDtypeStruct((batch, RWKV_HEADS, RWKV_HEAD, RWKV_HEAD), F32)],
        scratch_shapes=[pltpu.VMEM((n_seq, 1, RWKV_COLS), F32)],
        compiler_params=_cparams(("parallel", "arbitrary")),
        name="rwkv_scan",
    )(zr, shift0, wkv0, *wts)


def _mix_out_body(x_ref, o0_ref, o1_ref, o2_ref, l0_ref, l1_ref, l2_ref, orw_ref, gate_ref, woa_ref, wor_ref,
                  wo_ref, gffn_ref, rwt_ref, rb_ref, cnt0_ref, *rest, dilated, n_alias):
    x1_ref, h2_ref, te_ref, tg_ref, rk_ref, cnt_ref = rest[n_alias:n_alias + 6]
    stage = list(rest[n_alias + 6:])

    def token_major(ref, gi):
        if not dilated:
            return ref[...].astype(F32)
        dil = ATTN_GROUPS[gi][1]
        if dil == 1:
            return ref[0, 0].astype(F32)
        st_ref = stage.pop()
        for r in range(dil):
            sub = ref[0, r].astype(F32)
            for s in range(GROUP_W // LANE):
                st_ref[s, pl.ds(r, ref.shape[2], stride=dil), :] = sub[:, s * LANE:(s + 1) * LANE]
        return jnp.concatenate([st_ref[s] for s in range(GROUP_W // LANE)], axis=1)

    l0, l1, l2 = token_major(l0_ref, 0), token_major(l1_ref, 1), token_major(l2_ref, 2)
    m = jnp.maximum(jnp.maximum(l0, l1), l2)
    e0, e1, e2 = jnp.exp(l0 - m), jnp.exp(l1 - m), jnp.exp(l2 - m)
    o_att = (e0 * token_major(o0_ref, 0) + e1 * token_major(o1_ref, 1) + e2 * token_major(o2_ref, 2)) \
        * (1.0 / (e0 + e1 + e2))
    gates = gate_ref[...].astype(F32)
    merged = gates[:, :D_MODEL] * _dot(o_att.astype(BF16), woa_ref[...]) \
        + gates[:, D_MODEL:] * _dot(orw_ref[...], wor_ref[...])
    x1 = x_ref[...] + _dot(merged.astype(BF16), wo_ref[...])
    x1_ref[...] = x1
    h2 = _rms(x1, gffn_ref[...])
    h2_ref[...] = h2

    logits = _mm3(_dot_nt, rwt_ref[...], h2) + rb_ref[...]
    e_iota = lax.broadcasted_iota(jnp.int32, logits.shape, 0)
    vals, idxs = [], []
    for _ in range(TOP_K):
        top = jnp.max(logits, axis=0, keepdims=True)
        idx = jnp.min(jnp.where(logits == top, e_iota, N_EXPERTS), axis=0, keepdims=True)
        vals.append(top)
        idxs.append(idx)
        logits = jnp.where(e_iota == idx, -jnp.inf, logits)
    exps = [jnp.exp(t - vals[0]) for t in vals]
    inv = 1.0 / (exps[0] + exps[1] + exps[2] + exps[3])
    te_ref[...] = jnp.concatenate(idxs, axis=0)
    tg_ref[...] = jnp.concatenate([e * inv for e in exps], axis=0)

    @pl.when(pl.program_id(0) == 0)
    def _():
        cnt_ref[...] = cnt0_ref[...]

    hits = [e_iota == idx for idx in idxs]
    onehot = jnp.where(hits[0] | hits[1] | hits[2] | hits[3], 1.0, 0.0)
    tm = onehot.shape[1]
    earlier = lax.broadcasted_iota(jnp.int32, (tm, tm), 0) < lax.broadcasted_iota(jnp.int32, (tm, tm), 1)
    before = cnt_ref[...] + _dot(onehot.astype(BF16), jnp.where(earlier, 1.0, 0.0).astype(BF16))
    rk_ref[...] = jnp.concatenate([jnp.sum(jnp.where(h, before, 0.0), axis=0, keepdims=True) for h in hits],
                                  axis=0).astype(jnp.int32)
    cnt_ref[...] += jnp.sum(onehot, axis=1, keepdims=True)


def _mix_out(x, o_g, lse_g, o_rwkv, gates, wts, cnt0, tm, n_all, row0, bufs):
    n = x.shape[0]
    blk0 = row0 // tm
    row = lambda i: (i, 0)
    const = lambda i: (0, 0)
    out_row = lambda i: (i + blk0, 0)
    out_col = lambda i: (0, i + blk0)
    tok = lambda w: pl.BlockSpec((tm, w), row)
    dilated = o_g[0].ndim == 4
    if dilated:
        tiles = o_g[0].shape[1] * o_g[0].shape[2] // tm
        sub = lambda i: (i // tiles, 0, i % tiles, 0)
        att_specs = [pl.BlockSpec((1, d, tm // d, GROUP_W), sub) for _, d in ATTN_GROUPS] * 2
        stage = [pltpu.VMEM((GROUP_W // LANE, tm, LANE), F32) for _, d in ATTN_GROUPS if d > 1] * 2
    else:
        att_specs, stage = [tok(GROUP_W)] * 6, []
    in_specs = [tok(D_MODEL)] + att_specs + [tok(RWKV_W), tok(GATE_COLS),
                pl.BlockSpec((GROUP_W, D_MODEL), const), pl.BlockSpec((RWKV_W, D_MODEL), const),
                pl.BlockSpec((D_MODEL, D_MODEL), const), pl.BlockSpec((1, D_MODEL), const),
                pl.BlockSpec((N_EXPERTS, D_MODEL), const), pl.BlockSpec((N_EXPERTS, 1), const),
                pl.BlockSpec((N_EXPERTS, 1), const)]
    args = [x, *o_g, *lse_g, o_rwkv, gates, *wts, cnt0]
    aliases = {}
    if bufs is not None:
        in_specs += [pl.BlockSpec(memory_space=pl.ANY)] * 5
        aliases = {len(args) + j: j for j in range(5)}
        args += list(bufs)
    *new_bufs, cnt = pl.pallas_call(
        functools.partial(_mix_out_body, dilated=dilated, n_alias=len(aliases)),
        grid=(n // tm,),
        scratch_shapes=stage,
        in_specs=in_specs,
        out_specs=[pl.BlockSpec((tm, D_MODEL), out_row), pl.BlockSpec((tm, D_MODEL), out_row),
                   pl.BlockSpec((TOP_K, tm), out_col), pl.BlockSpec((TOP_K, tm), out_col),
                   pl.BlockSpec((TOP_K, tm), out_col), pl.BlockSpec((N_EXPERTS, 1), const)],
        out_shape=[jax.ShapeDtypeStruct((n_all, D_MODEL), F32), jax.ShapeDtypeStruct((n_all, D_MODEL), F32),
                   jax.ShapeDtypeStruct((TOP_K, n_all), jnp.int32), jax.ShapeDtypeStruct((TOP_K, n_all), F32),
                   jax.ShapeDtypeStruct((TOP_K, n_all), jnp.int32), jax.ShapeDtypeStruct((N_EXPERTS, 1), F32)],
        input_output_aliases=aliases,
        compiler_params=_cparams(("arbitrary",)),
        name="mix_out",
    )(*args)
    return new_bufs, cnt


def _moe_dispatch_body(tab_ref, idx_hbm, h2_hbm, x_hbm, idx_smem, hbuf, zrow, isem, fsem, dsem, zsem, *, n_tiles):
    i = pl.program_id(0)
    slot = i % 2
    n_buf, n_sem = DISPATCH_LAG + 2, DISPATCH_LAG + 1
    hslot = i % n_buf
    dslot = i % n_sem

    def tile_fetch(tile, hs):
        return pltpu.make_async_copy(h2_hbm.at[pl.ds(pl.multiple_of(tile * MOE_TILE, MOE_TILE), MOE_TILE)],
                                     hbuf.at[hs], fsem.at[hs])

    def idx_copy(rec, s):
        return pltpu.make_async_copy(idx_hbm.at[pl.ds(pl.multiple_of(rec * IDX_REC, IDX_REC), IDX_REC)],
                                     idx_smem.at[pl.ds(pl.multiple_of(s * IDX_REC, IDX_REC), IDX_REC)],
                                     isem.at[s])

    def row_copies_start(s):
        def one(t, carry):
            for k in range(TOP_K):
                d = idx_smem[s * IDX_REC + k * MOE_TILE + t]
                pltpu.make_async_copy(hbuf.at[hslot, pl.ds(t, 1)], x_hbm.at[pl.ds(d, 1)], dsem.at[dslot]).start()
            return carry
        for t in range(MOE_TILE):
            one(t, 0)

    @pl.when(i == 0)
    def _():
        idx_copy(0, 0).start()
        tile_fetch(0, 0).start()

    @pl.when(i + 1 < n_tiles)
    def _():
        idx_copy(i + 1, 1 - slot).start()
        tile_fetch(i + 1, (i + 1) % n_buf).start()

    idx_copy(i, slot).wait()
    tile_fetch(i, hslot).wait()
    row_copies_start(slot)

    def wait_rows(tile):
        for _ in range(TOP_K):
            pltpu.make_async_copy(hbuf.at[0], x_hbm.at[pl.ds(0, MOE_TILE)], dsem.at[tile % n_sem]).wait()

    @pl.when(i >= DISPATCH_LAG)
    def _():
        wait_rows(i - DISPATCH_LAG)

    @pl.when(i == n_tiles - 1)
    def _():
        for back in range(DISPATCH_LAG - 1, -1, -1):
            wait_rows(i - back)
        zrow[...] = jnp.zeros_like(zrow)

        def zero_copy(dst_row):
            return pltpu.make_async_copy(zrow.at[pl.ds(0, 1)], x_hbm.at[pl.ds(dst_row, 1)], zsem)

        def per_expert(e, carry):
            first = tab_ref[e] + tab_ref[N_EXPERTS + e]
            last = tab_ref[e] + tab_ref[2 * N_EXPERTS + e]
            lax.fori_loop(first, last, lambda r, c: (zero_copy(r).start(), c)[1], 0)
            lax.fori_loop(first, last, lambda r, c: (zero_copy(r).wait(), c)[1], 0)
            return carry
        lax.fori_loop(0, N_EXPERTS, per_expert, 0)


def _moe_dispatch(h2, idx_rec, tables, n_rows):
    n_tiles = h2.shape[0] // MOE_TILE
    grid_spec = pltpu.PrefetchScalarGridSpec(
        num_scalar_prefetch=1,
        grid=(n_tiles,),
        in_specs=[pl.BlockSpec(memory_space=pl.ANY),
                  pl.BlockSpec(memory_space=pl.ANY)],
        out_specs=pl.BlockSpec(memory_space=pl.ANY),
        scratch_shapes=[pltpu.SMEM((2 * IDX_REC,), jnp.int32),
                        pltpu.VMEM((DISPATCH_LAG + 2, MOE_TILE, D_MODEL), F32),
                        pltpu.VMEM((8, D_MODEL), F32),
                        pltpu.SemaphoreType.DMA((2,)),
                        pltpu.SemaphoreType.DMA((DISPATCH_LAG + 2,)),
                        pltpu.SemaphoreType.DMA((DISPATCH_LAG + 1,)),
                        pltpu.SemaphoreType.DMA],
    )
    return pl.pallas_call(
        functools.partial(_moe_dispatch_body, n_tiles=n_tiles),
        grid_spec=grid_spec,
        out_shape=jax.ShapeDtypeStruct((n_rows, D_MODEL), F32),
        compiler_params=_cparams(("arbitrary",)),
        name="moe_dispatch",
    )(tables, idx_rec, h2)


def _moe_experts_body(be_ref, nused_ref, x_ref, w1_ref, b1_ref, w2_ref, b2_ref, y_ref, w1b, w2b):
    i = pl.program_id(0)

    @pl.when(i < nused_ref[0])
    def _():
        @pl.when((i == 0) | (be_ref[i] != be_ref[jnp.maximum(i - 1, 0)]))
        def _():
            w1b[...] = w1_ref[0].astype(BF16)
            w2b[...] = w2_ref[0].astype(BF16)

        xb = x_ref[...].astype(BF16)
        glu = jnp.minimum(_dot(xb, w1b[:, :D_MODEL]) + b1_ref[0, :, :D_MODEL], SWIGLU_LIMIT)
        lin = jnp.clip(_dot(xb, w1b[:, D_MODEL:]) + b1_ref[0, :, D_MODEL:], -SWIGLU_LIMIT, SWIGLU_LIMIT)
        act = glu * _sigmoid(SWIGLU_ALPHA * glu) * (lin + 1.0)
        y_ref[...] = _dot(act.astype(BF16), w2b[...]) + b2_ref[0]

    @pl.when(i >= nused_ref[0])
    def _():
        y_ref[...] = jnp.zeros_like(y_ref)


def _moe_experts(x_rows, block_e, n_used, w1, b1, w2, b2):
    n_blocks = block_e.shape[0]
    by_expert = lambda i, be, nu: (be[i], 0, 0)
    grid_spec = pltpu.PrefetchScalarGridSpec(
        num_scalar_prefetch=2,
        grid=(n_blocks,),
        in_specs=[pl.BlockSpec((MOE_BLOCK, D_MODEL), lambda i, be, nu: (jnp.minimum(i, nu[0] - 1), 0)),
                  pl.BlockSpec((1, D_MODEL, 2 * D_MODEL), by_expert),
                  pl.BlockSpec((1, 1, 2 * D_MODEL), by_expert),
                  pl.BlockSpec((1, D_MODEL, D_MODEL), by_expert),
                  pl.BlockSpec((1, 1, D_MODEL), by_expert)],
        out_specs=pl.BlockSpec((MOE_BLOCK, D_MODEL), lambda i, be, nu: (i, 0)),
        scratch_shapes=[pltpu.VMEM((D_MODEL, 2 * D_MODEL), BF16),
                        pltpu.VMEM((D_MODEL, D_MODEL), BF16)],
    )
    return pl.pallas_call(
        _moe_experts_body,
        grid_spec=grid_spec,
        out_shape=jax.ShapeDtypeStruct((n_blocks * MOE_BLOCK, D_MODEL), F32),
        compiler_params=_cparams(("arbitrary",)),
        name="moe_experts",
    )(block_e, n_used, x_rows, w1, b1, w2, b2)


def _route(top_e, rank, counts):
    n_tok = top_e.shape[1]
    counts = counts.reshape(N_EXPERTS).astype(jnp.int32)
    padded = (counts + MOE_BLOCK - 1) // MOE_BLOCK * MOE_BLOCK
    pad_end = jnp.cumsum(padded)
    pad_start = pad_end - padded
    experts = jnp.arange(N_EXPERTS, dtype=jnp.int32)
    dest = rank + jnp.sum(jnp.where(top_e[..., None] == experts, pad_start, 0), axis=-1)
    n_blocks = -(-(n_tok * TOP_K + N_EXPERTS * (MOE_BLOCK - 1)) // MOE_BLOCK)
    blk_row0 = jnp.arange(n_blocks, dtype=jnp.int32) * MOE_BLOCK
    block_e = jnp.minimum(jnp.sum(blk_row0[:, None] >= pad_end[None, :], axis=1), N_EXPERTS - 1).astype(jnp.int32)
    n_used = (pad_end[-1] // MOE_BLOCK).astype(jnp.int32).reshape(1)
    n_tiles = n_tok // MOE_TILE
    rec = dest.reshape(TOP_K, n_tiles, MOE_TILE).transpose(1, 0, 2).reshape(n_tiles, TOP_K * MOE_TILE)
    idx_rec = jnp.concatenate([rec, jnp.zeros((n_tiles, IDX_REC - TOP_K * MOE_TILE), jnp.int32)], axis=1).reshape(-1)
    tables = jnp.concatenate([pad_start, counts, padded]).astype(jnp.int32)
    return idx_rec, tables, block_e, n_used, n_blocks * MOE_BLOCK


def _tail_body(idx_hbm, y_hbm, x1_ref, tg_ref, pe_ref, gple_ref, wpg_ref, wp_ref, gfin_ref, yp_ref, ys_ref,
               idx_smem, gbuf, isem, gsem, *, n_tiles, n_prompt_tiles):
    i = pl.program_id(0)
    slot = i % 2

    def idx_copy(rec, s):
        return pltpu.make_async_copy(idx_hbm.at[pl.ds(pl.multiple_of(rec * IDX_REC, IDX_REC), IDX_REC)],
                                     idx_smem.at[pl.ds(pl.multiple_of(s * IDX_REC, IDX_REC), IDX_REC)],
                                     isem.at[s])

    @pl.when(i == 0)
    def _():
        idx_copy(0, 0).start()

    @pl.when(i < n_tiles)
    def _():
        idx_copy(i, slot).wait()

        @pl.when(i + 1 < n_tiles)
        def _():
            idx_copy(i + 1, 1 - slot).start()

        for t in range(MOE_TILE):
            for k in range(TOP_K):
                d = idx_smem[slot * IDX_REC + k * MOE_TILE + t]
                pltpu.make_async_copy(y_hbm.at[pl.ds(d, 1)], gbuf.at[slot, k, pl.ds(t, 1)], gsem.at[slot]).start()

    @pl.when(i >= 1)
    def _():
        prev = 1 - slot
        for k in range(TOP_K):
            pltpu.make_async_copy(y_hbm.at[pl.ds(0, MOE_TILE)], gbuf.at[prev, k], gsem.at[prev]).wait()
        tg = tg_ref[...]
        moe = tg[:, 0:1] * gbuf[prev, 0]
        for k in range(1, TOP_K):
            moe = moe + tg[:, k:k + 1] * gbuf[prev, k]
        x2 = x1_ref[...] + moe
        gate = _sigmoid(_dot(_rms(x2, gple_ref[...]).astype(BF16), wpg_ref[...]))
        x3 = x2 + gate * _dot(pe_ref[...].astype(BF16), wp_ref[...])
        y = _rms(x3, gfin_ref[...])

        @pl.when(i - 1 < n_prompt_tiles)
        def _():
            yp_ref[...] = y

        @pl.when(i - 1 >= n_prompt_tiles)
        def _():
            ys_ref[...] = y


def _tail(x1_all, y_rows, idx_rec, tg_t, pe_all, wts, n_p):
    n_all = x1_all.shape[0]
    n_tiles = n_all // MOE_TILE
    n_prompt_tiles = n_p // MOE_TILE
    row = lambda i: (jnp.maximum(i - 1, 0), 0)
    const = lambda i: (0, 0)
    body = functools.partial(_tail_body, n_tiles=n_tiles, n_prompt_tiles=n_prompt_tiles)
    return pl.pallas_call(
        body,
        grid=(n_tiles + 1,),
        in_specs=[pl.BlockSpec(memory_space=pl.ANY), pl.BlockSpec(memory_space=pl.ANY),
                  pl.BlockSpec((MOE_TILE, D_MODEL), row), pl.BlockSpec((MOE_TILE, TOP_K), row),
                  pl.BlockSpec((MOE_TILE, PLE_DIM), row), pl.BlockSpec((1, D_MODEL), const),
                  pl.BlockSpec((D_MODEL, D_MODEL), const), pl.BlockSpec((PLE_DIM, D_MODEL), const),
                  pl.BlockSpec((1, D_MODEL), const)],
        out_specs=[pl.BlockSpec((MOE_TILE, D_MODEL),
                                lambda i: (jnp.clip(i - 1, 0, n_prompt_tiles - 1), 0)),
                   pl.BlockSpec((MOE_TILE, D_MODEL), lambda i: (jnp.maximum(i - 1 - n_prompt_tiles, 0), 0))],
        out_shape=[jax.ShapeDtypeStruct((n_p, D_MODEL), F32),
                   jax.ShapeDtypeStruct((n_all - n_p, D_MODEL), F32)],
        scratch_shapes=[pltpu.SMEM((2 * IDX_REC,), jnp.int32),
                        pltpu.VMEM((2, TOP_K, MOE_TILE, D_MODEL), F32),
                        pltpu.SemaphoreType.DMA((2,)),
                        pltpu.SemaphoreType.DMA((2,))],
        compiler_params=_cparams(("arbitrary",)),
        name="tail",
    )(idx_rec, y_rows, x1_all, tg_t, pe_all, *wts)


RWKV_CHUNK = 128
RWKV_CHUNK_SAMPLE = 8
RWKV_SEQS_PER_STEP = 4
STATE_PASSES = 1
Y_PASSES = 1
TM_PROMPT = 512


def kernel(x_prompt, x_sample, p_prompt, p_sample, cache_kv_w128, cache_kv_w512, cache_kv_w2048, state_rwkv_shift, state_rwkv_wkv, norm_mix_g, w_in, rwkv_mu, rwkv_w0, rwkv_w2, rwkv_a0, rwkv_a2, rwkv_g2, rwkv_k_k, rwkv_k_a, rwkv_r_k, rwkv_ln_w, rwkv_ln_b, w_out_attn, w_out_rwkv, w_out, norm_ffn_g, router_w, router_b, moe_w1, moe_b1, moe_w2, moe_b2, norm_ple_g, w_ple, w_ple_gate, norm_final_g):
    bp, seq, _ = x_prompt.shape
    bs, t_s, _ = x_sample.shape
    n_p, n_s = bp * seq, bs * t_s
    n_all = n_p + n_s
    assert w_in.shape[0] == 1, "single layer"
    caches = (cache_kv_w128, cache_kv_w512, cache_kv_w2048)

    row = lambda a: a.reshape(1, -1)
    w_in_b = w_in[0].astype(BF16)
    zeros64 = jnp.zeros((LORA_W // 2, RWKV_W), F32)
    rwkv_wts = (row(rwkv_mu[0]), row(rwkv_w0[0]), jnp.concatenate([rwkv_w2[0], zeros64], axis=0),
                row(rwkv_a0[0]), jnp.concatenate([zeros64, rwkv_a2[0]], axis=0), rwkv_g2[0],
                row(rwkv_k_k[0]), row(rwkv_k_a[0]), row(rwkv_r_k[0]), row(rwkv_ln_w[0]), row(rwkv_ln_b[0]))
    mix_wts = (w_out_attn[0].astype(BF16), w_out_rwkv[0].astype(BF16), w_out[0].astype(BF16),
               row(norm_ffn_g[0]), router_w[0].T, router_b[0].reshape(N_EXPERTS, 1))
    tail_wts = (row(norm_ple_g[0]), w_ple_gate[0].astype(BF16), w_ple[0].astype(BF16), row(norm_final_g))
    g_mix = row(norm_mix_g[0])

    xp = x_prompt.reshape(n_p, D_MODEL)
    qd_p, kvd_p, kvt_p, zr_p, gate_p = _in_proj_dilated(xp, g_mix, w_in_b, TM_PROMPT, bp, seq)
    att_p = [_attn_prompt(qd_p[gi], kvd_p[gi], gi) for gi in range(N_GROUPS)]
    orw_p, wkv_p = _rwkv(zr_p.reshape(bp, seq, RWKV_COLS), jnp.zeros((bp, 1, RWKV_COLS), F32),
                         jnp.zeros((bp, RWKV_HEADS, RWKV_HEAD, RWKV_HEAD), F32), rwkv_wts, RWKV_CHUNK, seq)
    bufs, counts = _mix_out(xp, [a[0] for a in att_p], [a[1] for a in att_p], orw_p.reshape(n_p, RWKV_W), gate_p,
                            mix_wts, jnp.zeros((N_EXPERTS, 1), F32), TM_PROMPT, n_all, 0, None)

    xs = x_sample.reshape(n_s, D_MODEL)
    q_s, kv_s, zr_s, gate_s = _in_proj(xs, g_mix, w_in_b, n_s)
    t_pad = 8
    q_s3 = jnp.pad(q_s.reshape(bs, t_s, Q_COLS), ((0, 0), (0, t_pad - t_s), (0, 0)))
    feat_major = lambda a: jnp.swapaxes(a, 1, 2)
    tail_t = jnp.pad(feat_major(kv_s.reshape(bs, t_s, KV_COLS)), ((0, 0), (0, 0), (LANE - t_s, 0)))
    att_s, new_caches = [], []
    for gi in range(N_GROUPS):
        wc = caches[gi].shape[2]
        o, lse, newc = _attn_sample(q_s3, tail_t, feat_major(caches[gi].reshape(bs, wc, 2 * GROUP_W)), gi, t_s)
        att_s.append((o[:, :t_s].reshape(n_s, GROUP_W), lse[:, :t_s].reshape(n_s, GROUP_W)))
        new_caches.append(feat_major(newc).reshape(1, bs, wc, 2, HEADS_PER_GROUP, HEAD_DIM))
    zr_s3 = jnp.pad(zr_s.reshape(bs, t_s, RWKV_COLS), ((0, 0), (0, RWKV_CHUNK_SAMPLE - t_s), (0, 0)))
    orw_s, wkv_s = _rwkv(zr_s3, state_rwkv_shift[0].reshape(bs, 1, RWKV_COLS), state_rwkv_wkv[0], rwkv_wts,
                         RWKV_CHUNK_SAMPLE, t_s)
    bufs, counts = _mix_out(xs, [a[0] for a in att_s], [a[1] for a in att_s], orw_s[:, :t_s].reshape(n_s, RWKV_W),
                            gate_s, mix_wts, counts, n_s, n_all, n_p, bufs)
    x1_all, h2_all, top_e, top_g, rank = bufs

    idx_rec, tables, block_e, n_used, n_rows = _route(top_e, rank, counts)
    x_rows = _moe_dispatch(h2_all, idx_rec, tables, n_rows)
    y_rows = _moe_experts(x_rows, block_e, n_used, moe_w1[0], moe_b1[0].reshape(N_EXPERTS, 1, 2 * D_MODEL),
                          moe_w2[0], moe_b2[0].reshape(N_EXPERTS, 1, D_MODEL))
    pe_all = jnp.concatenate([p_prompt[0].reshape(n_p, PLE_DIM), p_sample[0].reshape(n_s, PLE_DIM)], axis=0)
    y_p, y_s = _tail(x1_all, y_rows, idx_rec, top_g.T, pe_all, tail_wts, n_p)

    kv_out_p = [feat_major(t).reshape(1, bp, t.shape[2], 2, HEADS_PER_GROUP, HEAD_DIM) for t in kvt_p]
    shift_p = zr_p.reshape(bp, seq, RWKV_COLS)[:, -1][None]
    shift_s = zr_s.reshape(bs, t_s, RWKV_COLS)[:, -1][None]
    return (y_p.reshape(bp, seq, D_MODEL), y_s.reshape(bs, t_s, D_MODEL),
            kv_out_p[0], kv_out_p[1], kv_out_p[2], shift_p, wkv_p[None],
            new_caches[0], new_caches[1], new_caches[2], shift_s, wkv_s[None])
```

```python
import functools
import math

import numpy as np
import jax
import jax.numpy as jnp
from jax import lax
from jax.experimental import pallas as pl
from jax.experimental.pallas import tpu as pltpu

F32 = jnp.float32
BF16 = jnp.bfloat16

LANE = 128
D_MODEL = 1024
N_GROUPS = 3
HEADS_PER_GROUP = 4
HEAD_DIM = 64
ATTN_GROUPS = ((128, 1), (512, 4), (2048, 16))
GROUP_W = HEADS_PER_GROUP * HEAD_DIM
Q_COLS = N_GROUPS * GROUP_W
KV_COLS = 2 * Q_COLS
BAND = 128
ATTN_CHAINS = 4

RWKV_HEADS = 8
RWKV_HEAD = 64
RWKV_W = 512
LORA_W = 128
GATE_LORA = 128
RWKV_COLS = 3 * RWKV_W + LORA_W + GATE_LORA
GN_EPS = 64e-5
GATE_COLS = 2 * D_MODEL
Z_RWKV0 = 3 * Q_COLS
Z_GATE0 = Z_RWKV0 + RWKV_COLS
IN_COLS = Z_GATE0 + GATE_COLS

N_EXPERTS = 32
TOP_K = 4
SWIGLU_LIMIT = 7.0
SWIGLU_ALPHA = 1.702
MOE_BLOCK = 512
MOE_TILE = 128
DISPATCH_LAG = 1
IDX_REC = 1024
PLE_DIM = 256
RMS_EPS = 1e-6

NEG_BIG = -1e30
VMEM_LIMIT = 56 * 1024 * 1024


def _cparams(sem):
    return pltpu.CompilerParams(dimension_semantics=sem, vmem_limit_bytes=VMEM_LIMIT)


def _rms(x, g):
    return x * lax.rsqrt(jnp.mean(x * x, axis=-1, keepdims=True) + RMS_EPS) * g


def _sigmoid(x):
    return 1.0 / (1.0 + jnp.exp(-x))


def _dot(a, b):
    return jnp.dot(a, b, preferred_element_type=F32)


def _dot_nt(a, b):
    return lax.dot_general(a, b, (((1,), (1,)), ((), ())), preferred_element_type=F32)


def _dot_tn(a, b):
    return lax.dot_general(a, b, (((0,), (0,)), ((), ())), preferred_element_type=F32)


def _split(x):
    hi = x.astype(BF16)
    lo = (x - hi.astype(F32)).astype(BF16)
    return hi, lo


def _mm3(dot, a, b):
    ah, al = _split(a)
    bh, bl = _split(b)
    return dot(ah, bh) + (dot(ah, bl) + dot(al, bh))


def _parts(x, n):
    out = []
    for _ in range(n - 1):
        hi = x.astype(BF16)
        out.append(hi)
        x = x - hi.astype(F32)
    out.append(x.astype(BF16))
    return out


def _mmp(dot, ap, bp):
    order = max(len(ap), len(bp))
    acc = None
    for i, a in enumerate(ap):
        for j, b in enumerate(bp):
            if i + j < order:
                t = dot(a, b)
                acc = t if acc is None else acc + t
    return acc


def _in_proj_body(x_ref, g_ref, w_ref, q_ref, kv_ref, zr_ref, gate_ref):
    h = _rms(x_ref[...], g_ref[...]).astype(BF16)

    def proj(lo, width):
        return _dot(h, w_ref[:, lo:lo + width])

    q_ref[...] = (proj(0, Q_COLS) * (1.0 / math.sqrt(HEAD_DIM))).astype(BF16)
    for g in range(N_GROUPS):
        kv_ref[:, 2 * g * GROUP_W:(2 * g + 1) * GROUP_W] = proj(Q_COLS + g * GROUP_W, GROUP_W)
        kv_ref[:, (2 * g + 1) * GROUP_W:(2 * g + 2) * GROUP_W] = proj(2 * Q_COLS + g * GROUP_W, GROUP_W)
    zr_ref[...] = proj(Z_RWKV0, RWKV_COLS)
    gate_ref[...] = _sigmoid(proj(Z_GATE0, GATE_COLS)).astype(BF16)


def _in_proj_dilated_body(x_ref, g_ref, w_ref, *rest, tiles_per_seq):
    qd_refs, kvd_refs, kvt_refs = rest[0:N_GROUPS], rest[N_GROUPS:2 * N_GROUPS], rest[2 * N_GROUPS:3 * N_GROUPS]
    zr_ref, gate_ref, st_ref = rest[3 * N_GROUPS:]
    tm = x_ref.shape[0]
    tile_in_seq = pl.program_id(0) % tiles_per_seq
    h = _rms(x_ref[...], g_ref[...]).astype(BF16)

    def proj(lo, width):
        return _dot(h, w_ref[:, lo:lo + width])

    n_q = Q_COLS // LANE

    def stage(slab0, val):
        for s in range(val.shape[1] // LANE):
            st_ref[slab0 + s] = val[:, s * LANE:(s + 1) * LANE]

    stage(0, proj(0, Q_COLS) * (1.0 / math.sqrt(HEAD_DIM)))
    for g in range(N_GROUPS):
        for part, src in ((0, Q_COLS), (1, 2 * Q_COLS)):
            col = (2 * g + part) * GROUP_W
            stage(n_q + col // LANE, proj(src + g * GROUP_W, GROUP_W))
    for g, (window, _) in enumerate(ATTN_GROUPS):
        rows = min(window, tm)
        first_tile = tiles_per_seq - max(window // tm, 1)

        @pl.when(tile_in_seq >= first_tile)
        def _():
            for s in range(2 * GROUP_W // LANE):
                slab = st_ref[n_q + 2 * g * GROUP_W // LANE + s, tm - rows:tm, :]
                kvt_refs[g][0, s * LANE:(s + 1) * LANE, :] = slab.T
    for g, (_, dil) in enumerate(ATTN_GROUPS):
        for r in range(dil):
            rows = pl.ds(r, tm // dil, stride=dil) if dil > 1 else slice(None)
            for s in range(GROUP_W // LANE):
                qd_refs[g][0, r, :, s * LANE:(s + 1) * LANE] = st_ref[g * GROUP_W // LANE + s, rows, :].astype(BF16)
            for s in range(2 * GROUP_W // LANE):
                kvd_refs[g][0, r, :, s * LANE:(s + 1) * LANE] = \
                    st_ref[n_q + 2 * g * GROUP_W // LANE + s, rows, :].astype(BF16)
    zr_ref[...] = proj(Z_RWKV0, RWKV_COLS)
    gate_ref[...] = _sigmoid(proj(Z_GATE0, GATE_COLS)).astype(BF16)


def _in_proj_dilated(x, g, w_bf16, tm, batch, seq):
    n = x.shape[0]
    tiles = seq // tm
    row = lambda i: (i, 0)
    const = lambda i: (0, 0)
    sub = lambda i: (i // tiles, 0, i % tiles, 0)
    dils = [d for _, d in ATTN_GROUPS]
    assert all(tm % (16 * d) == 0 for d in dils)
    wins = [min(w, seq) for w, _ in ATTN_GROUPS]
    assert all(w % tm == 0 or tm % w == 0 for w in wins)

    def tail_spec(w):
        cols = min(w, tm)
        first = tiles - max(w // tm, 1)
        return pl.BlockSpec((1, 2 * GROUP_W, cols), lambda i: (i // tiles, 0, jnp.maximum(i % tiles - first, 0)))

    outs = pl.pallas_call(
        functools.partial(_in_proj_dilated_body, tiles_per_seq=tiles),
        grid=(n // tm,),
        in_specs=[pl.BlockSpec((tm, D_MODEL), row),
                  pl.BlockSpec((1, D_MODEL), const),
                  pl.BlockSpec((D_MODEL, IN_COLS), const, pipeline_mode=pl.Buffered(1))],
        out_specs=[pl.BlockSpec((1, d, tm // d, GROUP_W), sub) for d in dils]
                  + [pl.BlockSpec((1, d, tm // d, 2 * GROUP_W), sub) for d in dils]
                  + [tail_spec(w) for w in wins]
                  + [pl.BlockSpec((tm, RWKV_COLS), row),
                     pl.BlockSpec((tm, GATE_COLS), row)],
        out_shape=[jax.ShapeDtypeStruct((batch, d, seq // d, GROUP_W), BF16) for d in dils]
                  + [jax.ShapeDtypeStruct((batch, d, seq // d, 2 * GROUP_W), BF16) for d in dils]
                  + [jax.ShapeDtypeStruct((batch, 2 * GROUP_W, w), F32) for w in wins]
                  + [jax.ShapeDtypeStruct((n, RWKV_COLS), F32),
                     jax.ShapeDtypeStruct((n, GATE_COLS), BF16)],
        scratch_shapes=[pltpu.VMEM(((Q_COLS + KV_COLS) // LANE, tm, LANE), F32)],
        compiler_params=_cparams(("arbitrary",)),
        name="in_proj_dilated",
    )(x, g, w_bf16)
    g3 = N_GROUPS
    return outs[0:g3], outs[g3:2 * g3], outs[2 * g3:3 * g3], outs[3 * g3], outs[3 * g3 + 1]


def _in_proj(x, g, w_bf16, tm):
    n = x.shape[0]
    row = lambda i: (i, 0)
    const = lambda i: (0, 0)
    return pl.pallas_call(
        _in_proj_body,
        grid=(n // tm,),
        in_specs=[pl.BlockSpec((tm, D_MODEL), row),
                  pl.BlockSpec((1, D_MODEL), const),
                  pl.BlockSpec((D_MODEL, IN_COLS), const)],
        out_specs=[pl.BlockSpec((tm, Q_COLS), row),
                   pl.BlockSpec((tm, KV_COLS), row),
                   pl.BlockSpec((tm, RWKV_COLS), row),
                   pl.BlockSpec((tm, GATE_COLS), row)],
        out_shape=[jax.ShapeDtypeStruct((n, Q_COLS), BF16),
                   jax.ShapeDtypeStruct((n, KV_COLS), F32),
                   jax.ShapeDtypeStruct((n, RWKV_COLS), F32),
                   jax.ShapeDtypeStruct((n, GATE_COLS), BF16)],
        compiler_params=_cparams(("parallel",)),
        name="in_proj",
    )(x, g, w_bf16)


def _alibi_slopes(gi):
    return [2.0 ** (-8.0 * (gi * HEADS_PER_GROUP + h + 1) / (N_GROUPS * HEADS_PER_GROUP))
            for h in range(HEADS_PER_GROUP)]


def _head_of_lane(shape):
    return lax.broadcasted_iota(jnp.int32, shape, len(shape) - 1) // HEAD_DIM


def _stack_heads(q):
    hl = _head_of_lane(q.shape)
    return jnp.concatenate([jnp.where(hl == h, q, jnp.zeros_like(q)) for h in range(HEADS_PER_GROUP)], axis=0)


def _unstack_heads(x4, rows):
    hl = _head_of_lane((rows, GROUP_W))
    out = x4[0:rows]
    for h in range(1, HEADS_PER_GROUP):
        out = jnp.where(hl == h, x4[h * rows:(h + 1) * rows], out)
    return out


def _attn_prompt_body(q_ref, kvp_ref, kvc_ref, o_ref, lse_ref, kb_ref, bias_ref, *, dil, slopes, n_sub):
    c = pl.program_id(2)
    n_res = q_ref.shape[1]
    for r in range(n_res):
        kb_ref[r, 0:BAND, :] = kvp_ref[0, r]
        kb_ref[r, BAND:, :] = kvc_ref[0, r]

    qi = lax.broadcasted_iota(jnp.int32, (BAND, 2 * BAND), 0) + BAND
    ki = lax.broadcasted_iota(jnp.int32, (BAND, 2 * BAND), 1)
    rel = qi - ki
    dist = jnp.where((rel >= 0) & (rel <= BAND), (dil * rel).astype(F32), -NEG_BIG / slopes[-1])
    for h in range(HEADS_PER_GROUP):
        bias_ref[h * BAND:(h + 1) * BAND, :] = -slopes[h] * dist

    def sub_block(n, carry):
        r0 = pl.multiple_of(n * BAND, BAND)
        n_before_start = jnp.where((c == 0) & (n == 0), BAND, 0)
        for r in range(n_res):
            q4 = _stack_heads(q_ref[0, r, pl.ds(r0, BAND), :])
            kv = kb_ref[r, pl.ds(r0, 2 * BAND), :]
            s = _dot_nt(q4, kv[:, :GROUP_W]) + bias_ref[...]
            kcol = lax.broadcasted_iota(jnp.int32, s.shape, 1)
            s = jnp.where(kcol < n_before_start, NEG_BIG, s)
            m = jnp.max(s, axis=-1, keepdims=True)
            e = jnp.exp(s - m)
            den = jnp.sum(e, axis=-1, keepdims=True)
            o4 = _dot(e.astype(BF16), kv[:, GROUP_W:]) * (1.0 / den)
            o_ref[0, r, pl.ds(r0, BAND), :] = _unstack_heads(o4, BAND).astype(o_ref.dtype)
            lse4 = jnp.broadcast_to(m + jnp.log(den), (HEADS_PER_GROUP * BAND, GROUP_W))
            lse_ref[0, r, pl.ds(r0, BAND), :] = _unstack_heads(lse4, BAND)
        return carry

    lax.fori_loop(0, n_sub, sub_block, 0, unroll=min(n_sub, max(ATTN_CHAINS // n_res, 1)))


def _attn_prompt(qd, kvd, gi):
    window, dil = ATTN_GROUPS[gi]
    assert window // dil == BAND
    batch, _, sub_len, _ = qd.shape
    chunk = min(sub_len, 1024)
    n_chunks = sub_len // chunk
    sub_per_chunk = chunk // BAND
    n_res = max(min(ATTN_CHAINS // sub_per_chunk, dil), 1)
    assert dil % n_res == 0
    body = functools.partial(_attn_prompt_body, dil=dil, slopes=_alibi_slopes(gi), n_sub=sub_per_chunk)
    cur = lambda b, r, c: (b, r, c, 0)
    return pl.pallas_call(
        body,
        grid=(batch, dil // n_res, n_chunks),
        in_specs=[
            pl.BlockSpec((1, n_res, chunk, GROUP_W), cur),
            pl.BlockSpec((1, n_res, BAND, 2 * GROUP_W),
                         lambda b, r, c: (b, r, jnp.maximum(c * sub_per_chunk - 1, 0), 0)),
            pl.BlockSpec((1, n_res, chunk, 2 * GROUP_W), cur),
        ],
        out_specs=[pl.BlockSpec((1, n_res, chunk, GROUP_W), cur), pl.BlockSpec((1, n_res, chunk, GROUP_W), cur)],
        out_shape=[jax.ShapeDtypeStruct(qd.shape, BF16), jax.ShapeDtypeStruct(qd.shape, F32)],
        scratch_shapes=[pltpu.VMEM((n_res, chunk + BAND, 2 * GROUP_W), BF16),
                        pltpu.VMEM((HEADS_PER_GROUP * BAND, 2 * BAND), F32)],
        compiler_params=_cparams(("parallel", "parallel", "arbitrary")),
        name=f"attn_prompt_g{gi}",
    )(qd, kvd, kvd)


def _attn_sample_body(q_ref, tail_ref, cache_ref, o_ref, lse_ref, newc_ref, *, dil, slopes, wc, t_new):
    tp = q_ref.shape[1]
    cache = cache_ref[0]
    tail = tail_ref[0]
    newc_ref[0] = pltpu.roll(cache, wc - t_new, 1)
    lane = lax.broadcasted_iota(jnp.int32, tail.shape, 1)
    newc_ref[0, :, wc - LANE:wc] = jnp.where(lane >= LANE - t_new, tail, newc_ref[0, :, wc - LANE:wc])

    q4 = _stack_heads(q_ref[0])
    cb = cache.astype(BF16)
    nb = tail.astype(BF16)
    t_row = lax.broadcasted_iota(jnp.int32, (tp, 1), 0)
    slope_col = jnp.concatenate([jnp.full((tp, 1), s, F32) for s in slopes], axis=0)
    t4 = jnp.concatenate([t_row] * HEADS_PER_GROUP, axis=0)

    def scores(keys_t_bf16, dist):
        ok = (dist >= 0) & ((dist & (dil - 1)) == 0) & (dist <= BAND * dil)
        s = _dot(q4, keys_t_bf16[:GROUP_W, :])
        return jnp.where(ok, s - slope_col * dist.astype(F32), NEG_BIG)

    pc = lax.broadcasted_iota(jnp.int32, (HEADS_PER_GROUP * tp, wc), 1)
    pn = lax.broadcasted_iota(jnp.int32, (HEADS_PER_GROUP * tp, LANE), 1) - (LANE - t_new)
    s_c = scores(cb, wc + t4 - pc)
    s_n = scores(nb, jnp.where(pn >= 0, t4 - pn, -1))
    m = jnp.maximum(jnp.max(s_c, axis=-1, keepdims=True), jnp.max(s_n, axis=-1, keepdims=True))
    e_c = jnp.exp(s_c - m)
    e_n = jnp.exp(s_n - m)
    den = jnp.sum(e_c, axis=-1, keepdims=True) + jnp.sum(e_n, axis=-1, keepdims=True)
    o4 = (_dot_nt(e_c.astype(BF16), cb[GROUP_W:, :]) + _dot_nt(e_n.astype(BF16), nb[GROUP_W:, :])) * (1.0 / den)
    o_ref[0] = _unstack_heads(o4, tp).astype(o_ref.dtype)
    lse4 = jnp.broadcast_to(m + jnp.log(den), (HEADS_PER_GROUP * tp, GROUP_W))
    lse_ref[0] = _unstack_heads(lse4, tp)


def _attn_sample(q, tail_t, cache_t, gi, t_new):
    window, dil = ATTN_GROUPS[gi]
    batch, tp, _ = q.shape
    wc = cache_t.shape[2]
    assert wc + 0 - dil * BAND >= 0
    body = functools.partial(_attn_sample_body, dil=dil, slopes=_alibi_slopes(gi), wc=wc, t_new=t_new)
    return pl.pallas_call(
        body,
        grid=(batch,),
        in_specs=[pl.BlockSpec((1, tp, GROUP_W), lambda b: (b, 0, gi)),
                  pl.BlockSpec((1, 2 * GROUP_W, LANE), lambda b: (b, gi, 0)),
                  pl.BlockSpec((1, 2 * GROUP_W, wc), lambda b: (b, 0, 0))],
        out_specs=[pl.BlockSpec((1, tp, GROUP_W), lambda b: (b, 0, 0)),
                   pl.BlockSpec((1, tp, GROUP_W), lambda b: (b, 0, 0)),
                   pl.BlockSpec((1, 2 * GROUP_W, wc), lambda b: (b, 0, 0))],
        out_shape=[jax.ShapeDtypeStruct((batch, tp, GROUP_W), BF16),
                   jax.ShapeDtypeStruct((batch, tp, GROUP_W), F32),
                   jax.ShapeDtypeStruct((batch, 2 * GROUP_W, wc), F32)],
        compiler_params=_cparams(("parallel",)),
        name=f"attn_sample_g{gi}",
    )(q, tail_t, cache_t)


def _rwkv_body(zr_ref, shift0_ref, wkv0_ref, mu_ref, w0_ref, w2_ref, a0_ref, a2_ref, g2_ref, kk_ref, ka_ref,
               rk_ref, lnw_ref, lnb_ref, o_ref, wkv_ref, prev_ref, *, chunk, t_valid):
    c = pl.program_id(1)
    L = chunk
    n_seq = zr_ref.shape[0]
    heads = range(n_seq * RWKV_HEADS)
    seq_of = [h // RWKV_HEADS for h in heads]
    head_of = [h % RWKV_HEADS for h in heads]
    sls = [slice(head_of[h] * RWKV_HEAD, (head_of[h] + 1) * RWKV_HEAD) for h in heads]

    @pl.when(c == 0)
    def _():
        for b in range(n_seq):
            prev_ref[b] = shift0_ref[b]
        wkv_ref[...] = wkv0_ref[...]

    row = lax.broadcasted_iota(jnp.int32, (L, 1), 0)
    ti = lax.broadcasted_iota(jnp.int32, (L, L), 0)
    si = lax.broadcasted_iota(jnp.int32, (L, L), 1)
    incl = ti >= si
    strict = ti > si
    seqs = [_rwkv_columns(zr_ref[b], prev_ref.at[b], row, incl, mu_ref, w0_ref, w2_ref, a0_ref, a2_ref, g2_ref,
                          kk_ref, ka_ref, rk_ref, L, t_valid) for b in range(n_seq)]
    col = lambda name, h: seqs[seq_of[h]][name][:, sls[h]]

    sp, yp = STATE_PASSES, Y_PASSES
    kkh = [col('kk_raw', h) for h in heads]
    kkn = [x / jnp.maximum(jnp.sqrt(jnp.sum(x * x, axis=-1, keepdims=True)), 1e-12) for x in kkh]
    at = [-kkn[h] * col('p_prev', h) for h in heads]
    bt = [kkn[h] * col('a', h) * col('p_inv', h) for h in heads]
    rt = [col('rt_all', h) for h in heads]
    kt = [col('kt_all', h) for h in heads]
    vh = [col('v', h) for h in heads]
    at_s = [_parts(x, sp) for x in at]
    bt_s = [_parts(x, sp) for x in bt]
    kt_s = [_parts(x, sp) for x in kt]
    vh_s = [_parts(x, sp) for x in vh]
    rt_y = [_parts(x, yp) for x in rt]
    a_ab = [jnp.where(strict, _mmp(_dot_nt, at_s[h], bt_s[h]), 0.0) for h in heads]
    a_ak = [jnp.where(strict, _mmp(_dot_nt, at_s[h], kt_s[h]), 0.0) for h in heads]
    a_rb = [jnp.where(incl, _mmp(_dot_nt, rt_y[h], bt_s[h][:yp]), 0.0) for h in heads]
    a_rk = [jnp.where(incl, _mmp(_dot_nt, rt_y[h], kt_s[h][:yp]), 0.0) for h in heads]
    x = [jnp.concatenate([at[h], _mmp(_dot, _parts(a_ak[h], sp), vh_s[h])], axis=1) for h in heads]
    p = a_ab
    for level in range(max(int(math.log2(L)), 1)):
        if level > 0:
            p = [_mmp(_dot, ps, ps) for ps in p_s]
        p_s = [_parts(q, sp) for q in p]
        x = [x[h] + _mmp(_dot, p_s[h], _parts(x[h], sp)) for h in heads]
    x_s = [_parts(q, sp) for q in x]
    qy = [_mmp(_dot, _parts(a_rb[h], yp), x_s[h][:yp]) for h in heads]
    y0 = [qy[h][:, RWKV_HEAD:] + _mmp(_dot, _parts(a_rk[h], yp), vh_s[h][:yp]) for h in heads]
    qh = [rt[h] + qy[h][:, :RWKV_HEAD] for h in heads]
    s0 = [wkv_ref[seq_of[h], head_of[h]] * col('p_mid', h) for h in heads]
    s0_s = [_parts(q, sp) for q in s0]
    y = [_mmp(_dot_nt, _parts(qh[h], yp), s0_s[h][:yp]) + y0[h] for h in heads]
    wtb = [_mmp(_dot_tn, [q[:, :RWKV_HEAD] for q in x_s[h]], bt_s[h]) for h in heads]
    uv_s = [[jnp.concatenate([x_s[h][i][:, RWKV_HEAD:], vh_s[h][i]], axis=0) for i in range(sp)] for h in heads]
    bk_s = [[jnp.concatenate([bt_s[h][i], kt_s[h][i]], axis=0) for i in range(sp)] for h in heads]
    for h in heads:
        s_new = s0[h] + _mmp(_dot, s0_s[h], _parts(wtb[h], sp)) + _mmp(_dot_tn, uv_s[h], bk_s[h])
        wkv_ref[seq_of[h], head_of[h]] = s_new * col('p_incl', h)[L - 1:L, :]
    for h in heads:
        mu_y = jnp.mean(y[h], axis=-1, keepdims=True)
        yc = y[h] - mu_y
        var = jnp.mean(yc * yc, axis=-1, keepdims=True)
        yn = yc * lax.rsqrt(var + GN_EPS) * lnw_ref[:, sls[h]] + lnb_ref[:, sls[h]]
        bonus = jnp.sum(col('rk_all', h), axis=-1, keepdims=True) * vh[h]
        o_ref[seq_of[h], :, sls[h]] = ((yn + bonus) * col('g', h)).astype(o_ref.dtype)


def _rwkv_columns(zr, prev_ref, row, incl, mu_ref, w0_ref, w2_ref, a0_ref, a2_ref, g2_ref, kk_ref, ka_ref, rk_ref,
                  L, t_valid):
    mm = functools.partial(_mm3, _dot)
    prev = jnp.where(row == 0, prev_ref[...], pltpu.roll(zr, 1, 0))
    prev_ref[...] = zr[L - 1:L, :]
    zm = zr + (prev - zr) * mu_ref[...]

    r = zm[:, 0:RWKV_W]
    k = zm[:, RWKV_W:2 * RWKV_W]
    v = zm[:, 2 * RWKV_W:3 * RWKV_W]
    xwa = zm[:, 3 * RWKV_W:3 * RWKV_W + LORA_W]
    xg = zm[:, 3 * RWKV_W + LORA_W:]
    w_pre = w0_ref[...] + mm(jnp.tanh(xwa), w2_ref[...])
    softplus = jnp.maximum(-w_pre, 0.0) + jnp.log(1.0 + jnp.exp(-jnp.abs(w_pre)))
    w_log = -softplus - 0.5
    lw = -jnp.exp(w_log)
    a = _sigmoid(a0_ref[...] + mm(xwa, a2_ref[...]))
    g = mm(_sigmoid(xg), g2_ref[...])
    kk_raw = k * kk_ref[...]
    k2 = k * (1.0 + (a - 1.0) * ka_ref[...])
    if t_valid < L:
        live = row < t_valid
        lw = jnp.where(live, lw, 0.0)
        kk_raw = jnp.where(live, kk_raw, 0.0)
        k2 = jnp.where(live, k2, 0.0)
        v = jnp.where(live, v, 0.0)

    clw = _mmp(_dot, [jnp.where(incl, 1.0, 0.0).astype(BF16)], _parts(lw, 3))
    mid = max(L // 2 - 1, 0)
    clw_mid = clw[mid:mid + 1, :]
    rel = clw - clw_mid
    p_incl = jnp.exp(rel)
    p_inv = jnp.exp(-rel)
    return dict(kk_raw=kk_raw, a=a, v=v, g=g, p_incl=p_incl, p_inv=p_inv, p_prev=jnp.exp(rel - lw),
                p_mid=jnp.exp(clw_mid),
                rt_all=r * p_incl, kt_all=k2 * p_inv, rk_all=r * k2 * rk_ref[...])


def _rwkv(zr, shift0, wkv0, wts, chunk, t_valid):
    batch, t, _ = zr.shape
    n_chunks = t // chunk
    n_seq = RWKV_SEQS_PER_STEP
    assert (t_valid == t or n_chunks == 1) and batch % n_seq == 0
    body = functools.partial(_rwkv_body, chunk=chunk, t_valid=min(t_valid, chunk))
    vec = lambda width: pl.BlockSpec((1, width), lambda b, c: (0, 0))
    mat = lambda rows: pl.BlockSpec((rows, RWKV_W), lambda b, c: (0, 0))
    return pl.pallas_call(
        body,
        grid=(batch // n_seq, n_chunks),
        in_specs=[pl.BlockSpec((n_seq, chunk, RWKV_COLS), lambda b, c: (b, c, 0)),
                  pl.BlockSpec((n_seq, 1, RWKV_COLS), lambda b, c: (b, 0, 0)),
                  pl.BlockSpec((n_seq, RWKV_HEADS, RWKV_HEAD, RWKV_HEAD), lambda b, c: (b, 0, 0, 0)),
                  vec(RWKV_COLS), vec(RWKV_W), mat(LORA_W), vec(RWKV_W), mat(LORA_W), mat(GATE_LORA),
                  vec(RWKV_W), vec(RWKV_W), vec(RWKV_W), vec(RWKV_W), vec(RWKV_W)],
        out_specs=[pl.BlockSpec((n_seq, chunk, RWKV_W), lambda b, c: (b, c, 0)),
                   pl.BlockSpec((n_seq, RWKV_HEADS, RWKV_HEAD, RWKV_HEAD), lambda b, c: (b, 0, 0, 0))],
        out_shape=[jax.ShapeDtypeStruct((batch, t, RWKV_W), BF16),
                   jax.ShapeDtypeStruct((batch, RWKV_HEADS, RWKV_HEAD, RWKV_HEAD), F32)],
        scratch_shapes=[pltpu.VMEM((n_seq, 1, RWKV_COLS), F32)],
        compiler_params=_cparams(("parallel", "arbitrary")),
        name="rwkv_scan",
    )(zr, shift0, wkv0, *wts)


def _mix_out_body(x_ref, o0_ref, o1_ref, o2_ref, l0_ref, l1_ref, l2_ref, orw_ref, gate_ref, woa_ref, wor_ref,
                  wo_ref, gffn_ref, rwt_ref, rb_ref, cnt0_ref, *rest, dilated, n_alias):
    x1_ref, h2_ref, te_ref, tg_ref, rk_ref, cnt_ref = rest[n_alias:n_alias + 6]
    stage = list(rest[n_alias + 6:])

    def token_major(ref, gi):
        if not dilated:
            return ref[...].astype(F32)
        dil = ATTN_GROUPS[gi][1]
        if dil == 1:
            return ref[0, 0].astype(F32)
        st_ref = stage.pop()
        for r in range(dil):
            sub = ref[0, r].astype(F32)
            for s in range(GROUP_W // LANE):
                st_ref[s, pl.ds(r, ref.shape[2], stride=dil), :] = sub[:, s * LANE:(s + 1) * LANE]
        return jnp.concatenate([st_ref[s] for s in range(GROUP_W // LANE)], axis=1)

    l0, l1, l2 = token_major(l0_ref, 0), token_major(l1_ref, 1), token_major(l2_ref, 2)
    m = jnp.maximum(jnp.maximum(l0, l1), l2)
    e0, e1, e2 = jnp.exp(l0 - m), jnp.exp(l1 - m), jnp.exp(l2 - m)
    o_att = (e0 * token_major(o0_ref, 0) + e1 * token_major(o1_ref, 1) + e2 * token_major(o2_ref, 2)) \
        * (1.0 / (e0 + e1 + e2))
    gates = gate_ref[...].astype(F32)
    merged = gates[:, :D_MODEL] * _dot(o_att.astype(BF16), woa_ref[...]) \
        + gates[:, D_MODEL:] * _dot(orw_ref[...], wor_ref[...])
    x1 = x_ref[...] + _dot(merged.astype(BF16), wo_ref[...])
    x1_ref[...] = x1
    h2 = _rms(x1, gffn_ref[...])
    h2_ref[...] = h2

    logits = _mm3(_dot_nt, rwt_ref[...], h2) + rb_ref[...]
    e_iota = lax.broadcasted_iota(jnp.int32, logits.shape, 0)
    vals, idxs = [], []
    for _ in range(TOP_K):
        top = jnp.max(logits, axis=0, keepdims=True)
        idx = jnp.min(jnp.where(logits == top, e_iota, N_EXPERTS), axis=0, keepdims=True)
        vals.append(top)
        idxs.append(idx)
        logits = jnp.where(e_iota == idx, -jnp.inf, logits)
    exps = [jnp.exp(t - vals[0]) for t in vals]
    inv = 1.0 / (exps[0] + exps[1] + exps[2] + exps[3])
    te_ref[...] = jnp.concatenate(idxs, axis=0)
    tg_ref[...] = jnp.concatenate([e * inv for e in exps], axis=0)

    @pl.when(pl.program_id(0) == 0)
    def _():
        cnt_ref[...] = cnt0_ref[...]

    hits = [e_iota == idx for idx in idxs]
    onehot = jnp.where(hits[0] | hits[1] | hits[2] | hits[3], 1.0, 0.0)
    tm = onehot.shape[1]
    earlier = lax.broadcasted_iota(jnp.int32, (tm, tm), 0) < lax.broadcasted_iota(jnp.int32, (tm, tm), 1)
    before = cnt_ref[...] + _dot(onehot.astype(BF16), jnp.where(earlier, 1.0, 0.0).astype(BF16))
    rk_ref[...] = jnp.concatenate([jnp.sum(jnp.where(h, before, 0.0), axis=0, keepdims=True) for h in hits],
                                  axis=0).astype(jnp.int32)
    cnt_ref[...] += jnp.sum(onehot, axis=1, keepdims=True)


def _mix_out(x, o_g, lse_g, o_rwkv, gates, wts, cnt0, tm, n_all, row0, bufs):
    n = x.shape[0]
    blk0 = row0 // tm
    row = lambda i: (i, 0)
    const = lambda i: (0, 0)
    out_row = lambda i: (i + blk0, 0)
    out_col = lambda i: (0, i + blk0)
    tok = lambda w: pl.BlockSpec((tm, w), row)
    dilated = o_g[0].ndim == 4
    if dilated:
        tiles = o_g[0].shape[1] * o_g[0].shape[2] // tm
        sub = lambda i: (i // tiles, 0, i % tiles, 0)
        att_specs = [pl.BlockSpec((1, d, tm // d, GROUP_W), sub) for _, d in ATTN_GROUPS] * 2
        stage = [pltpu.VMEM((GROUP_W // LANE, tm, LANE), F32) for _, d in ATTN_GROUPS if d > 1] * 2
    else:
        att_specs, stage = [tok(GROUP_W)] * 6, []
    in_specs = [tok(D_MODEL)] + att_specs + [tok(RWKV_W), tok(GATE_COLS),
                pl.BlockSpec((GROUP_W, D_MODEL), const), pl.BlockSpec((RWKV_W, D_MODEL), const),
                pl.BlockSpec((D_MODEL, D_MODEL), const), pl.BlockSpec((1, D_MODEL), const),
                pl.BlockSpec((N_EXPERTS, D_MODEL), const), pl.BlockSpec((N_EXPERTS, 1), const),
                pl.BlockSpec((N_EXPERTS, 1), const)]
    args = [x, *o_g, *lse_g, o_rwkv, gates, *wts, cnt0]
    aliases = {}
    if bufs is not None:
        in_specs += [pl.BlockSpec(memory_space=pl.ANY)] * 5
        aliases = {len(args) + j: j for j in range(5)}
        args += list(bufs)
    *new_bufs, cnt = pl.pallas_call(
        functools.partial(_mix_out_body, dilated=dilated, n_alias=len(aliases)),
        grid=(n // tm,),
        scratch_shapes=stage,
        in_specs=in_specs,
        out_specs=[pl.BlockSpec((tm, D_MODEL), out_row), pl.BlockSpec((tm, D_MODEL), out_row),
                   pl.BlockSpec((TOP_K, tm), out_col), pl.BlockSpec((TOP_K, tm), out_col),
                   pl.BlockSpec((TOP_K, tm), out_col), pl.BlockSpec((N_EXPERTS, 1), const)],
        out_shape=[jax.ShapeDtypeStruct((n_all, D_MODEL), F32), jax.ShapeDtypeStruct((n_all, D_MODEL), F32),
                   jax.ShapeDtypeStruct((TOP_K, n_all), jnp.int32), jax.ShapeDtypeStruct((TOP_K, n_all), F32),
                   jax.ShapeDtypeStruct((TOP_K, n_all), jnp.int32), jax.ShapeDtypeStruct((N_EXPERTS, 1), F32)],
        input_output_aliases=aliases,
        compiler_params=_cparams(("arbitrary",)),
        name="mix_out",
    )(*args)
    return new_bufs, cnt


def _moe_dispatch_body(tab_ref, idx_hbm, h2_hbm, x_hbm, idx_smem, hbuf, zrow, isem, fsem, dsem, zsem, *, n_tiles):
    i = pl.program_id(0)
    slot = i % 2
    n_buf, n_sem = DISPATCH_LAG + 2, DISPATCH_LAG + 1
    hslot = i % n_buf
    dslot = i % n_sem

    def tile_fetch(tile, hs):
        return pltpu.make_async_copy(h2_hbm.at[pl.ds(pl.multiple_of(tile * MOE_TILE, MOE_TILE), MOE_TILE)],
                                     hbuf.at[hs], fsem.at[hs])

    def idx_copy(rec, s):
        return pltpu.make_async_copy(idx_hbm.at[pl.ds(pl.multiple_of(rec * IDX_REC, IDX_REC), IDX_REC)],
                                     idx_smem.at[pl.ds(pl.multiple_of(s * IDX_REC, IDX_REC), IDX_REC)],
                                     isem.at[s])

    def row_copies_start(s):
        def one(t, carry):
            for k in range(TOP_K):
                d = idx_smem[s * IDX_REC + k * MOE_TILE + t]
                pltpu.make_async_copy(hbuf.at[hslot, pl.ds(t, 1)], x_hbm.at[pl.ds(d, 1)], dsem.at[dslot]).start()
            return carry
        for t in range(MOE_TILE):
            one(t, 0)

    @pl.when(i == 0)
    def _():
        idx_copy(0, 0).start()
        tile_fetch(0, 0).start()

    @pl.when(i + 1 < n_tiles)
    def _():
        idx_copy(i + 1, 1 - slot).start()
        tile_fetch(i + 1, (i + 1) % n_buf).start()

    idx_copy(i, slot).wait()
    tile_fetch(i, hslot).wait()
    row_copies_start(slot)

    def wait_rows(tile):
        for _ in range(TOP_K):
            pltpu.make_async_copy(hbuf.at[0], x_hbm.at[pl.ds(0, MOE_TILE)], dsem.at[tile % n_sem]).wait()

    @pl.when(i >= DISPATCH_LAG)
    def _():
        wait_rows(i - DISPATCH_LAG)

    @pl.when(i == n_tiles - 1)
    def _():
        for back in range(DISPATCH_LAG - 1, -1, -1):
            wait_rows(i - back)
        zrow[...] = jnp.zeros_like(zrow)

        def zero_copy(dst_row):
            return pltpu.make_async_copy(zrow.at[pl.ds(0, 1)], x_hbm.at[pl.ds(dst_row, 1)], zsem)

        def per_expert(e, carry):
            first = tab_ref[e] + tab_ref[N_EXPERTS + e]
            last = tab_ref[e] + tab_ref[2 * N_EXPERTS + e]
            lax.fori_loop(first, last, lambda r, c: (zero_copy(r).start(), c)[1], 0)
            lax.fori_loop(first, last, lambda r, c: (zero_copy(r).wait(), c)[1], 0)
            return carry
        lax.fori_loop(0, N_EXPERTS, per_expert, 0)


def _moe_dispatch(h2, idx_rec, tables, n_rows):
    n_tiles = h2.shape[0] // MOE_TILE
    grid_spec = pltpu.PrefetchScalarGridSpec(
        num_scalar_prefetch=1,
        grid=(n_tiles,),
        in_specs=[pl.BlockSpec(memory_space=pl.ANY),
                  pl.BlockSpec(memory_space=pl.ANY)],
        out_specs=pl.BlockSpec(memory_space=pl.ANY),
        scratch_shapes=[pltpu.SMEM((2 * IDX_REC,), jnp.int32),
                        pltpu.VMEM((DISPATCH_LAG + 2, MOE_TILE, D_MODEL), F32),
                        pltpu.VMEM((8, D_MODEL), F32),
                        pltpu.SemaphoreType.DMA((2,)),
                        pltpu.SemaphoreType.DMA((DISPATCH_LAG + 2,)),
                        pltpu.SemaphoreType.DMA((DISPATCH_LAG + 1,)),
                        pltpu.SemaphoreType.DMA],
    )
    return pl.pallas_call(
        functools.partial(_moe_dispatch_body, n_tiles=n_tiles),
        grid_spec=grid_spec,
        out_shape=jax.ShapeDtypeStruct((n_rows, D_MODEL), F32),
        compiler_params=_cparams(("arbitrary",)),
        name="moe_dispatch",
    )(tables, idx_rec, h2)


def _moe_experts_body(be_ref, nused_ref, x_ref, w1_ref, b1_ref, w2_ref, b2_ref, y_ref, w1b, w2b):
    i = pl.program_id(0)

    @pl.when(i < nused_ref[0])
    def _():
        @pl.when((i == 0) | (be_ref[i] != be_ref[jnp.maximum(i - 1, 0)]))
        def _():
            w1b[...] = w1_ref[0].astype(BF16)
            w2b[...] = w2_ref[0].astype(BF16)

        xb = x_ref[...].astype(BF16)
        glu = jnp.minimum(_dot(xb, w1b[:, :D_MODEL]) + b1_ref[0, :, :D_MODEL], SWIGLU_LIMIT)
        lin = jnp.clip(_dot(xb, w1b[:, D_MODEL:]) + b1_ref[0, :, D_MODEL:], -SWIGLU_LIMIT, SWIGLU_LIMIT)
        act = glu * _sigmoid(SWIGLU_ALPHA * glu) * (lin + 1.0)
        y_ref[...] = _dot(act.astype(BF16), w2b[...]) + b2_ref[0]

    @pl.when(i >= nused_ref[0])
    def _():
        y_ref[...] = jnp.zeros_like(y_ref)


def _moe_experts(x_rows, block_e, n_used, w1, b1, w2, b2):
    n_blocks = block_e.shape[0]
    by_expert = lambda i, be, nu: (be[i], 0, 0)
    grid_spec = pltpu.PrefetchScalarGridSpec(
        num_scalar_prefetch=2,
        grid=(n_blocks,),
        in_specs=[pl.BlockSpec((MOE_BLOCK, D_MODEL), lambda i, be, nu: (jnp.minimum(i, nu[0] - 1), 0)),
                  pl.BlockSpec((1, D_MODEL, 2 * D_MODEL), by_expert),
                  pl.BlockSpec((1, 1, 2 * D_MODEL), by_expert),
                  pl.BlockSpec((1, D_MODEL, D_MODEL), by_expert),
                  pl.BlockSpec((1, 1, D_MODEL), by_expert)],
        out_specs=pl.BlockSpec((MOE_BLOCK, D_MODEL), lambda i, be, nu: (i, 0)),
        scratch_shapes=[pltpu.VMEM((D_MODEL, 2 * D_MODEL), BF16),
                        pltpu.VMEM((D_MODEL, D_MODEL), BF16)],
    )
    return pl.pallas_call(
        _moe_experts_body,
        grid_spec=grid_spec,
        out_shape=jax.ShapeDtypeStruct((n_blocks * MOE_BLOCK, D_MODEL), F32),
        compiler_params=_cparams(("arbitrary",)),
        name="moe_experts",
    )(block_e, n_used, x_rows, w1, b1, w2, b2)


def _route(top_e, rank, counts):
    n_tok = top_e.shape[1]
    counts = counts.reshape(N_EXPERTS).astype(jnp.int32)
    padded = (counts + MOE_BLOCK - 1) // MOE_BLOCK * MOE_BLOCK
    pad_end = jnp.cumsum(padded)
    pad_start = pad_end - padded
    experts = jnp.arange(N_EXPERTS, dtype=jnp.int32)
    dest = rank + jnp.sum(jnp.where(top_e[..., None] == experts, pad_start, 0), axis=-1)
    n_blocks = -(-(n_tok * TOP_K + N_EXPERTS * (MOE_BLOCK - 1)) // MOE_BLOCK)
    blk_row0 = jnp.arange(n_blocks, dtype=jnp.int32) * MOE_BLOCK
    block_e = jnp.minimum(jnp.sum(blk_row0[:, None] >= pad_end[None, :], axis=1), N_EXPERTS - 1).astype(jnp.int32)
    n_used = (pad_end[-1] // MOE_BLOCK).astype(jnp.int32).reshape(1)
    n_tiles = n_tok // MOE_TILE
    rec = dest.reshape(TOP_K, n_tiles, MOE_TILE).transpose(1, 0, 2).reshape(n_tiles, TOP_K * MOE_TILE)
    idx_rec = jnp.concatenate([rec, jnp.zeros((n_tiles, IDX_REC - TOP_K * MOE_TILE), jnp.int32)], axis=1).reshape(-1)
    tables = jnp.concatenate([pad_start, counts, padded]).astype(jnp.int32)
    return idx_rec, tables, block_e, n_used, n_blocks * MOE_BLOCK


def _tail_body(idx_hbm, y_hbm, x1_ref, tg_ref, pe_ref, gple_ref, wpg_ref, wp_ref, gfin_ref, yp_ref, ys_ref,
               idx_smem, gbuf, isem, gsem, *, n_tiles, n_prompt_tiles):
    i = pl.program_id(0)
    slot = i % 2

    def idx_copy(rec, s):
        return pltpu.make_async_copy(idx_hbm.at[pl.ds(pl.multiple_of(rec * IDX_REC, IDX_REC), IDX_REC)],
                                     idx_smem.at[pl.ds(pl.multiple_of(s * IDX_REC, IDX_REC), IDX_REC)],
                                     isem.at[s])

    @pl.when(i == 0)
    def _():
        idx_copy(0, 0).start()

    @pl.when(i < n_tiles)
    def _():
        idx_copy(i, slot).wait()

        @pl.when(i + 1 < n_tiles)
        def _():
            idx_copy(i + 1, 1 - slot).start()

        for t in range(MOE_TILE):
            for k in range(TOP_K):
                d = idx_smem[slot * IDX_REC + k * MOE_TILE + t]
                pltpu.make_async_copy(y_hbm.at[pl.ds(d, 1)], gbuf.at[slot, k, pl.ds(t, 1)], gsem.at[slot]).start()

    @pl.when(i >= 1)
    def _():
        prev = 1 - slot
        for k in range(TOP_K):
            pltpu.make_async_copy(y_hbm.at[pl.ds(0, MOE_TILE)], gbuf.at[prev, k], gsem.at[prev]).wait()
        tg = tg_ref[...]
        moe = tg[:, 0:1] * gbuf[prev, 0]
        for k in range(1, TOP_K):
            moe = moe + tg[:, k:k + 1] * gbuf[prev, k]
        x2 = x1_ref[...] + moe
        gate = _sigmoid(_dot(_rms(x2, gple_ref[...]).astype(BF16), wpg_ref[...]))
        x3 = x2 + gate * _dot(pe_ref[...].astype(BF16), wp_ref[...])
        y = _rms(x3, gfin_ref[...])

        @pl.when(i - 1 < n_prompt_tiles)
        def _():
            yp_ref[...] = y

        @pl.when(i - 1 >= n_prompt_tiles)
        def _():
            ys_ref[...] = y


def _tail(x1_all, y_rows, idx_rec, tg_t, pe_all, wts, n_p):
    n_all = x1_all.shape[0]
    n_tiles = n_all // MOE_TILE
    n_prompt_tiles = n_p // MOE_TILE
    row = lambda i: (jnp.maximum(i - 1, 0), 0)
    const = lambda i: (0, 0)
    body = functools.partial(_tail_body, n_tiles=n_tiles, n_prompt_tiles=n_prompt_tiles)
    return pl.pallas_call(
        body,
        grid=(n_tiles + 1,),
        in_specs=[pl.BlockSpec(memory_space=pl.ANY), pl.BlockSpec(memory_space=pl.ANY),
                  pl.BlockSpec((MOE_TILE, D_MODEL), row), pl.BlockSpec((MOE_TILE, TOP_K), row),
                  pl.BlockSpec((MOE_TILE, PLE_DIM), row), pl.BlockSpec((1, D_MODEL), const),
                  pl.BlockSpec((D_MODEL, D_MODEL), const), pl.BlockSpec((PLE_DIM, D_MODEL), const),
                  pl.BlockSpec((1, D_MODEL), const)],
        out_specs=[pl.BlockSpec((MOE_TILE, D_MODEL),
                                lambda i: (jnp.clip(i - 1, 0, n_prompt_tiles - 1), 0)),
                   pl.BlockSpec((MOE_TILE, D_MODEL), lambda i: (jnp.maximum(i - 1 - n_prompt_tiles, 0), 0))],
        out_shape=[jax.ShapeDtypeStruct((n_p, D_MODEL), F32),
                   jax.ShapeDtypeStruct((n_all - n_p, D_MODEL), F32)],
        scratch_shapes=[pltpu.SMEM((2 * IDX_REC,), jnp.int32),
                        pltpu.VMEM((2, TOP_K, MOE_TILE, D_MODEL), F32),
                        pltpu.SemaphoreType.DMA((2,)),
                        pltpu.SemaphoreType.DMA((2,))],
        compiler_params=_cparams(("arbitrary",)),
        name="tail",
    )(idx_rec, y_rows, x1_all, tg_t, pe_all, *wts)


RWKV_CHUNK = 128
RWKV_CHUNK_SAMPLE = 8
RWKV_SEQS_PER_STEP = 4
STATE_PASSES = 1
Y_PASSES = 1
TM_PROMPT = 512


def kernel(x_prompt, x_sample, p_prompt, p_sample, cache_kv_w128, cache_kv_w512, cache_kv_w2048, state_rwkv_shift, state_rwkv_wkv, norm_mix_g, w_in, rwkv_mu, rwkv_w0, rwkv_w2, rwkv_a0, rwkv_a2, rwkv_g2, rwkv_k_k, rwkv_k_a, rwkv_r_k, rwkv_ln_w, rwkv_ln_b, w_out_attn, w_out_rwkv, w_out, norm_ffn_g, router_w, router_b, moe_w1, moe_b1, moe_w2, moe_b2, norm_ple_g, w_ple, w_ple_gate, norm_final_g):
    bp, seq, _ = x_prompt.shape
    bs, t_s, _ = x_sample.shape
    n_p, n_s = bp * seq, bs * t_s
    n_all = n_p + n_s
    assert w_in.shape[0] == 1, "single layer"
    caches = (cache_kv_w128, cache_kv_w512, cache_kv_w2048)

    row = lambda a: a.reshape(1, -1)
    w_in_b = w_in[0].astype(BF16)
    zeros64 = jnp.zeros((LORA_W // 2, RWKV_W), F32)
    rwkv_wts = (row(rwkv_mu[0]), row(rwkv_w0[0]), jnp.concatenate([rwkv_w2[0], zeros64], axis=0),
                row(rwkv_a0[0]), jnp.concatenate([zeros64, rwkv_a2[0]], axis=0), rwkv_g2[0],
                row(rwkv_k_k[0]), row(rwkv_k_a[0]), row(rwkv_r_k[0]), row(rwkv_ln_w[0]), row(rwkv_ln_b[0]))
    mix_wts = (w_out_attn[0].astype(BF16), w_out_rwkv[0].astype(BF16), w_out[0].astype(BF16),
               row(norm_ffn_g[0]), router_w[0].T, router_b[0].reshape(N_EXPERTS, 1))
    tail_wts = (row(norm_ple_g[0]), w_ple_gate[0].astype(BF16), w_ple[0].astype(BF16), row(norm_final_g))
    g_mix = row(norm_mix_g[0])

    xp = x_prompt.reshape(n_p, D_MODEL)
    qd_p, kvd_p, kvt_p, zr_p, gate_p = _in_proj_dilated(xp, g_mix, w_in_b, TM_PROMPT, bp, seq)
    att_p = [_attn_prompt(qd_p[gi], kvd_p[gi], gi) for gi in range(N_GROUPS)]
    orw_p, wkv_p = _rwkv(zr_p.reshape(bp, seq, RWKV_COLS), jnp.zeros((bp, 1, RWKV_COLS), F32),
                         jnp.zeros((bp, RWKV_HEADS, RWKV_HEAD, RWKV_HEAD), F32), rwkv_wts, RWKV_CHUNK, seq)
    bufs, counts = _mix_out(xp, [a[0] for a in att_p], [a[1] for a in att_p], orw_p.reshape(n_p, RWKV_W), gate_p,
                            mix_wts, jnp.zeros((N_EXPERTS, 1), F32), TM_PROMPT, n_all, 0, None)

    xs = x_sample.reshape(n_s, D_MODEL)
    q_s, kv_s, zr_s, gate_s = _in_proj(xs, g_mix, w_in_b, n_s)
    t_pad = 8
    q_s3 = jnp.pad(q_s.reshape(bs, t_s, Q_COLS), ((0, 0), (0, t_pad - t_s), (0, 0)))
    feat_major = lambda a: jnp.swapaxes(a, 1, 2)
    tail_t = jnp.pad(feat_major(kv_s.reshape(bs, t_s, KV_COLS)), ((0, 0), (0, 0), (LANE - t_s, 0)))
    att_s, new_caches = [], []
    for gi in range(N_GROUPS):
        wc = caches[gi].shape[2]
        o, lse, newc = _attn_sample(q_s3, tail_t, feat_major(caches[gi].reshape(bs, wc, 2 * GROUP_W)), gi, t_s)
        att_s.append((o[:, :t_s].reshape(n_s, GROUP_W), lse[:, :t_s].reshape(n_s, GROUP_W)))
        new_caches.append(feat_major(newc).reshape(1, bs, wc, 2, HEADS_PER_GROUP, HEAD_DIM))
    zr_s3 = jnp.pad(zr_s.reshape(bs, t_s, RWKV_COLS), ((0, 0), (0, RWKV_CHUNK_SAMPLE - t_s), (0, 0)))
    orw_s, wkv_s = _rwkv(zr_s3, state_rwkv_shift[0].reshape(bs, 1, RWKV_COLS), state_rwkv_wkv[0], rwkv_wts,
                         RWKV_CHUNK_SAMPLE, t_s)
    bufs, counts = _mix_out(xs, [a[0] for a in att_s], [a[1] for a in att_s], orw_s[:, :t_s].reshape(n_s, RWKV_W),
                            gate_s, mix_wts, counts, n_s, n_all, n_p, bufs)
    x1_all, h2_all, top_e, top_g, rank = bufs

    idx_rec, tables, block_e, n_used, n_rows = _route(top_e, rank, counts)
    x_rows = _moe_dispatch(h2_all, idx_rec, tables, n_rows)
    y_rows = _moe_experts(x_rows, block_e, n_used, moe_w1[0], moe_b1[0].reshape(N_EXPERTS, 1, 2 * D_MODEL),
                          moe_w2[0], moe_b2[0].reshape(N_EXPERTS, 1, D_MODEL))
    pe_all = jnp.concatenate([p_prompt[0].reshape(n_p, PLE_DIM), p_sample[0].reshape(n_s, PLE_DIM)], axis=0)
    y_p, y_s = _tail(x1_all, y_rows, idx_rec, top_g.T, pe_all, tail_wts, n_p)

    kv_out_p = [feat_major(t).reshape(1, bp, t.shape[2], 2, HEADS_PER_GROUP, HEAD_DIM) for t in kvt_p]
    shift_p = zr_p.reshape(bp, seq, RWKV_COLS)[:, -1][None]
    shift_s = zr_s.reshape(bs, t_s, RWKV_COLS)[:, -1][None]
    return (y_p.reshape(bp, seq, D_MODEL), y_s.reshape(bs, t_s, D_MODEL),
            kv_out_p[0], kv_out_p[1], kv_out_p[2], shift_p, wkv_p[None],
            new_caches[0], new_caches[1], new_caches[2], shift_s, wkv_s[None])
```

```python
import functools
import math

import numpy as np
import jax
import jax.numpy as jnp
from jax import lax
from jax.experimental import pallas as pl
from jax.experimental.pallas import tpu as pltpu

F32 = jnp.float32
BF16 = jnp.bfloat16

LANE = 128
D_MODEL = 1024
N_GROUPS = 3
HEADS_PER_GROUP = 4
HEAD_DIM = 64
ATTN_GROUPS = ((128, 1), (512, 4), (2048, 16))
GROUP_W = HEADS_PER_GROUP * HEAD_DIM
Q_COLS = N_GROUPS * GROUP_W
KV_COLS = 2 * Q_COLS
BAND = 128
ATTN_CHAINS = 8

RWKV_HEADS = 8
RWKV_HEAD = 64
RWKV_W = 512
LORA_W = 128
GATE_LORA = 128
RWKV_COLS = 3 * RWKV_W + LORA_W + GATE_LORA
GN_EPS = 64e-5
GATE_COLS = 2 * D_MODEL
Z_RWKV0 = 3 * Q_COLS
Z_GATE0 = Z_RWKV0 + RWKV_COLS
IN_COLS = Z_GATE0 + GATE_COLS

N_EXPERTS = 32
TOP_K = 4
SWIGLU_LIMIT = 7.0
SWIGLU_ALPHA = 1.702
MOE_BLOCK = 512
MOE_TILE = 128
DISPATCH_LAG = 1
IDX_REC = 1024
PLE_DIM = 256
RMS_EPS = 1e-6

NEG_BIG = -1e30
VMEM_LIMIT = 56 * 1024 * 1024


def _cparams(sem):
    return pltpu.CompilerParams(dimension_semantics=sem, vmem_limit_bytes=VMEM_LIMIT)


def _rms(x, g):
    return x * lax.rsqrt(jnp.mean(x * x, axis=-1, keepdims=True) + RMS_EPS) * g


def _sigmoid(x):
    return 1.0 / (1.0 + jnp.exp(-x))


def _dot(a, b):
    return jnp.dot(a, b, preferred_element_type=F32)


def _dot_nt(a, b):
    return lax.dot_general(a, b, (((1,), (1,)), ((), ())), preferred_element_type=F32)


def _dot_tn(a, b):
    return lax.dot_general(a, b, (((0,), (0,)), ((), ())), preferred_element_type=F32)


def _split(x):
    hi = x.astype(BF16)
    lo = (x - hi.astype(F32)).astype(BF16)
    return hi, lo


def _mm3(dot, a, b):
    ah, al = _split(a)
    bh, bl = _split(b)
    return dot(ah, bh) + (dot(ah, bl) + dot(al, bh))


def _parts(x, n):
    out = []
    for _ in range(n - 1):
        hi = x.astype(BF16)
        out.append(hi)
        x = x - hi.astype(F32)
    out.append(x.astype(BF16))
    return out


def _mmp(dot, ap, bp):
    order = max(len(ap), len(bp))
    acc = None
    for i, a in enumerate(ap):
        for j, b in enumerate(bp):
            if i + j < order:
                t = dot(a, b)
                acc = t if acc is None else acc + t
    return acc


def _in_proj_body(x_ref, g_ref, w_ref, q_ref, kv_ref, zr_ref, gate_ref):
    h = _rms(x_ref[...], g_ref[...]).astype(BF16)

    def proj(lo, width):
        return _dot(h, w_ref[:, lo:lo + width])

    q_ref[...] = (proj(0, Q_COLS) * (1.0 / math.sqrt(HEAD_DIM))).astype(BF16)
    for g in range(N_GROUPS):
        kv_ref[:, 2 * g * GROUP_W:(2 * g + 1) * GROUP_W] = proj(Q_COLS + g * GROUP_W, GROUP_W)
        kv_ref[:, (2 * g + 1) * GROUP_W:(2 * g + 2) * GROUP_W] = proj(2 * Q_COLS + g * GROUP_W, GROUP_W)
    zr_ref[...] = proj(Z_RWKV0, RWKV_COLS)
    gate_ref[...] = _sigmoid(proj(Z_GATE0, GATE_COLS)).astype(BF16)


def _in_proj_dilated_body(x_ref, g_ref, w_ref, *rest, tiles_per_seq):
    qd_refs, kvd_refs, kvt_refs = rest[0:N_GROUPS], rest[N_GROUPS:2 * N_GROUPS], rest[2 * N_GROUPS:3 * N_GROUPS]
    zr_ref, gate_ref, st_ref = rest[3 * N_GROUPS:]
    tm = x_ref.shape[0]
    tile_in_seq = pl.program_id(0) % tiles_per_seq
    h = _rms(x_ref[...], g_ref[...]).astype(BF16)

    def proj(lo, width):
        return _dot(h, w_ref[:, lo:lo + width])

    n_q = Q_COLS // LANE

    def stage(slab0, val):
        for s in range(val.shape[1] // LANE):
            st_ref[slab0 + s] = val[:, s * LANE:(s + 1) * LANE]

    stage(0, proj(0, Q_COLS) * (1.0 / math.sqrt(HEAD_DIM)))
    for g in range(N_GROUPS):
        for part, src in ((0, Q_COLS), (1, 2 * Q_COLS)):
            col = (2 * g + part) * GROUP_W
            stage(n_q + col // LANE, proj(src + g * GROUP_W, GROUP_W))
    for g, (window, _) in enumerate(ATTN_GROUPS):
        rows = min(window, tm)
        first_tile = tiles_per_seq - max(window // tm, 1)

        @pl.when(tile_in_seq >= first_tile)
        def _():
            for s in range(2 * GROUP_W // LANE):
                slab = st_ref[n_q + 2 * g * GROUP_W // LANE + s, tm - rows:tm, :]
                kvt_refs[g][0, s * LANE:(s + 1) * LANE, :] = slab.T
    for g, (_, dil) in enumerate(ATTN_GROUPS):
        for r in range(dil):
            rows = pl.ds(r, tm // dil, stride=dil) if dil > 1 else slice(None)
            for s in range(GROUP_W // LANE):
                qd_refs[g][0, r, :, s * LANE:(s + 1) * LANE] = st_ref[g * GROUP_W // LANE + s, rows, :].astype(BF16)
            for s in range(2 * GROUP_W // LANE):
                kvd_refs[g][0, r, :, s * LANE:(s + 1) * LANE] = \
                    st_ref[n_q + 2 * g * GROUP_W // LANE + s, rows, :].astype(BF16)
    zr_ref[...] = proj(Z_RWKV0, RWKV_COLS)
    gate_ref[...] = _sigmoid(proj(Z_GATE0, GATE_COLS)).astype(BF16)


def _in_proj_dilated(x, g, w_bf16, tm, batch, seq):
    n = x.shape[0]
    tiles = seq // tm
    row = lambda i: (i, 0)
    const = lambda i: (0, 0)
    sub = lambda i: (i // tiles, 0, i % tiles, 0)
    dils = [d for _, d in ATTN_GROUPS]
    assert all(tm % (16 * d) == 0 for d in dils)
    wins = [min(w, seq) for w, _ in ATTN_GROUPS]
    assert all(w % tm == 0 or tm % w == 0 for w in wins)

    def tail_spec(w):
        cols = min(w, tm)
        first = tiles - max(w // tm, 1)
        return pl.BlockSpec((1, 2 * GROUP_W, cols), lambda i: (i // tiles, 0, jnp.maximum(i % tiles - first, 0)))

    outs = pl.pallas_call(
        functools.partial(_in_proj_dilated_body, tiles_per_seq=tiles),
        grid=(n // tm,),
        in_specs=[pl.BlockSpec((tm, D_MODEL), row),
                  pl.BlockSpec((1, D_MODEL), const),
                  pl.BlockSpec((D_MODEL, IN_COLS), const, pipeline_mode=pl.Buffered(1))],
        out_specs=[pl.BlockSpec((1, d, tm // d, GROUP_W), sub) for d in dils]
                  + [pl.BlockSpec((1, d, tm // d, 2 * GROUP_W), sub) for d in dils]
                  + [tail_spec(w) for w in wins]
                  + [pl.BlockSpec((tm, RWKV_COLS), row),
                     pl.BlockSpec((tm, GATE_COLS), row)],
        out_shape=[jax.ShapeDtypeStruct((batch, d, seq // d, GROUP_W), BF16) for d in dils]
                  + [jax.ShapeDtypeStruct((batch, d, seq // d, 2 * GROUP_W), BF16) for d in dils]
                  + [jax.ShapeDtypeStruct((batch, 2 * GROUP_W, w), F32) for w in wins]
                  + [jax.ShapeDtypeStruct((n, RWKV_COLS), F32),
                     jax.ShapeDtypeStruct((n, GATE_COLS), BF16)],
        scratch_shapes=[pltpu.VMEM(((Q_COLS + KV_COLS) // LANE, tm, LANE), F32)],
        compiler_params=_cparams(("arbitrary",)),
        name="in_proj_dilated",
    )(x, g, w_bf16)
    g3 = N_GROUPS
    return outs[0:g3], outs[g3:2 * g3], outs[2 * g3:3 * g3], outs[3 * g3], outs[3 * g3 + 1]


def _in_proj(x, g, w_bf16, tm):
    n = x.shape[0]
    row = lambda i: (i, 0)
    const = lambda i: (0, 0)
    return pl.pallas_call(
        _in_proj_body,
        grid=(n // tm,),
        in_specs=[pl.BlockSpec((tm, D_MODEL), row),
                  pl.BlockSpec((1, D_MODEL), const),
                  pl.BlockSpec((D_MODEL, IN_COLS), const)],
        out_specs=[pl.BlockSpec((tm, Q_COLS), row),
                   pl.BlockSpec((tm, KV_COLS), row),
                   pl.BlockSpec((tm, RWKV_COLS), row),
                   pl.BlockSpec((tm, GATE_COLS), row)],
        out_shape=[jax.ShapeDtypeStruct((n, Q_COLS), BF16),
                   jax.ShapeDtypeStruct((n, KV_COLS), F32),
                   jax.ShapeDtypeStruct((n, RWKV_COLS), F32),
                   jax.ShapeDtypeStruct((n, GATE_COLS), BF16)],
        compiler_params=_cparams(("parallel",)),
        name="in_proj",
    )(x, g, w_bf16)


def _alibi_slopes(gi):
    return [2.0 ** (-8.0 * (gi * HEADS_PER_GROUP + h + 1) / (N_GROUPS * HEADS_PER_GROUP))
            for h in range(HEADS_PER_GROUP)]


def _head_of_lane(shape):
    return lax.broadcasted_iota(jnp.int32, shape, len(shape) - 1) // HEAD_DIM


def _stack_heads(q):
    hl = _head_of_lane(q.shape)
    return jnp.concatenate([jnp.where(hl == h, q, jnp.zeros_like(q)) for h in range(HEADS_PER_GROUP)], axis=0)


def _unstack_heads(x4, rows):
    hl = _head_of_lane((rows, GROUP_W))
    out = x4[0:rows]
    for h in range(1, HEADS_PER_GROUP):
        out = jnp.where(hl == h, x4[h * rows:(h + 1) * rows], out)
    return out


def _attn_prompt_body(q_ref, kvp_ref, kvc_ref, o_ref, lse_ref, kb_ref, bias_ref, *, dil, slopes, n_sub):
    c = pl.program_id(2)
    n_res = q_ref.shape[1]
    for r in range(n_res):
        kb_ref[r, 0:BAND, :] = kvp_ref[0, r]
        kb_ref[r, BAND:, :] = kvc_ref[0, r]

    qi = lax.broadcasted_iota(jnp.int32, (BAND, 2 * BAND), 0) + BAND
    ki = lax.broadcasted_iota(jnp.int32, (BAND, 2 * BAND), 1)
    rel = qi - ki
    dist = jnp.where((rel >= 0) & (rel <= BAND), (dil * rel).astype(F32), -NEG_BIG / slopes[-1])
    for h in range(HEADS_PER_GROUP):
        bias_ref[h * BAND:(h + 1) * BAND, :] = -slopes[h] * dist

    def sub_block(n, carry):
        r0 = pl.multiple_of(n * BAND, BAND)
        n_before_start = jnp.where((c == 0) & (n == 0), BAND, 0)
        for r in range(n_res):
            q4 = _stack_heads(q_ref[0, r, pl.ds(r0, BAND), :])
            kv = kb_ref[r, pl.ds(r0, 2 * BAND), :]
            s = _dot_nt(q4, kv[:, :GROUP_W]) + bias_ref[...]
            kcol = lax.broadcasted_iota(jnp.int32, s.shape, 1)
            s = jnp.where(kcol < n_before_start, NEG_BIG, s)
            m = jnp.max(s, axis=-1, keepdims=True)
            e = jnp.exp(s - m)
            den = jnp.sum(e, axis=-1, keepdims=True)
            o4 = _dot(e.astype(BF16), kv[:, GROUP_W:]) * (1.0 / den)
            o_ref[0, r, pl.ds(r0, BAND), :] = _unstack_heads(o4, BAND).astype(o_ref.dtype)
            lse4 = jnp.broadcast_to(m + jnp.log(den), (HEADS_PER_GROUP * BAND, GROUP_W))
            lse_ref[0, r, pl.ds(r0, BAND), :] = _unstack_heads(lse4, BAND)
        return carry

    lax.fori_loop(0, n_sub, sub_block, 0, unroll=min(n_sub, max(ATTN_CHAINS // n_res, 1)))


def _attn_prompt(qd, kvd, gi):
    window, dil = ATTN_GROUPS[gi]
    assert window // dil == BAND
    batch, _, sub_len, _ = qd.shape
    chunk = min(sub_len, 1024)
    n_chunks = sub_len // chunk
    sub_per_chunk = chunk // BAND
    n_res = max(min(ATTN_CHAINS // sub_per_chunk, dil), 1)
    assert dil % n_res == 0
    body = functools.partial(_attn_prompt_body, dil=dil, slopes=_alibi_slopes(gi), n_sub=sub_per_chunk)
    cur = lambda b, r, c: (b, r, c, 0)
    return pl.pallas_call(
        body,
        grid=(batch, dil // n_res, n_chunks),
        in_specs=[
            pl.BlockSpec((1, n_res, chunk, GROUP_W), cur),
            pl.BlockSpec((1, n_res, BAND, 2 * GROUP_W),
                         lambda b, r, c: (b, r, jnp.maximum(c * sub_per_chunk - 1, 0), 0)),
            pl.BlockSpec((1, n_res, chunk, 2 * GROUP_W), cur),
        ],
        out_specs=[pl.BlockSpec((1, n_res, chunk, GROUP_W), cur), pl.BlockSpec((1, n_res, chunk, GROUP_W), cur)],
        out_shape=[jax.ShapeDtypeStruct(qd.shape, BF16), jax.ShapeDtypeStruct(qd.shape, F32)],
        scratch_shapes=[pltpu.VMEM((n_res, chunk + BAND, 2 * GROUP_W), BF16),
                        pltpu.VMEM((HEADS_PER_GROUP * BAND, 2 * BAND), F32)],
        compiler_params=_cparams(("parallel", "parallel", "arbitrary")),
        name=f"attn_prompt_g{gi}",
    )(qd, kvd, kvd)


def _attn_sample_body(q_ref, tail_ref, cache_ref, o_ref, lse_ref, newc_ref, *, dil, slopes, wc, t_new):
    tp = q_ref.shape[1]
    cache = cache_ref[0]
    tail = tail_ref[0]
    newc_ref[0] = pltpu.roll(cache, wc - t_new, 1)
    lane = lax.broadcasted_iota(jnp.int32, tail.shape, 1)
    newc_ref[0, :, wc - LANE:wc] = jnp.where(lane >= LANE - t_new, tail, newc_ref[0, :, wc - LANE:wc])

    q4 = _stack_heads(q_ref[0])
    cb = cache.astype(BF16)
    nb = tail.astype(BF16)
    t_row = lax.broadcasted_iota(jnp.int32, (tp, 1), 0)
    slope_col = jnp.concatenate([jnp.full((tp, 1), s, F32) for s in slopes], axis=0)
    t4 = jnp.concatenate([t_row] * HEADS_PER_GROUP, axis=0)

    def scores(keys_t_bf16, dist):
        ok = (dist >= 0) & ((dist & (dil - 1)) == 0) & (dist <= BAND * dil)
        s = _dot(q4, keys_t_bf16[:GROUP_W, :])
        return jnp.where(ok, s - slope_col * dist.astype(F32), NEG_BIG)

    pc = lax.broadcasted_iota(jnp.int32, (HEADS_PER_GROUP * tp, wc), 1)
    pn = lax.broadcasted_iota(jnp.int32, (HEADS_PER_GROUP * tp, LANE), 1) - (LANE - t_new)
    s_c = scores(cb, wc + t4 - pc)
    s_n = scores(nb, jnp.where(pn >= 0, t4 - pn, -1))
    m = jnp.maximum(jnp.max(s_c, axis=-1, keepdims=True), jnp.max(s_n, axis=-1, keepdims=True))
    e_c = jnp.exp(s_c - m)
    e_n = jnp.exp(s_n - m)
    den = jnp.sum(e_c, axis=-1, keepdims=True) + jnp.sum(e_n, axis=-1, keepdims=True)
    o4 = (_dot_nt(e_c.astype(BF16), cb[GROUP_W:, :]) + _dot_nt(e_n.astype(BF16), nb[GROUP_W:, :])) * (1.0 / den)
    o_ref[0] = _unstack_heads(o4, tp).astype(o_ref.dtype)
    lse4 = jnp.broadcast_to(m + jnp.log(den), (HEADS_PER_GROUP * tp, GROUP_W))
    lse_ref[0] = _unstack_heads(lse4, tp)


def _attn_sample(q, tail_t, cache_t, gi, t_new):
    window, dil = ATTN_GROUPS[gi]
    batch, tp, _ = q.shape
    wc = cache_t.shape[2]
    assert wc + 0 - dil * BAND >= 0
    body = functools.partial(_attn_sample_body, dil=dil, slopes=_alibi_slopes(gi), wc=wc, t_new=t_new)
    return pl.pallas_call(
        body,
        grid=(batch,),
        in_specs=[pl.BlockSpec((1, tp, GROUP_W), lambda b: (b, 0, gi)),
                  pl.BlockSpec((1, 2 * GROUP_W, LANE), lambda b: (b, gi, 0)),
                  pl.BlockSpec((1, 2 * GROUP_W, wc), lambda b: (b, 0, 0))],
        out_specs=[pl.BlockSpec((1, tp, GROUP_W), lambda b: (b, 0, 0)),
                   pl.BlockSpec((1, tp, GROUP_W), lambda b: (b, 0, 0)),
                   pl.BlockSpec((1, 2 * GROUP_W, wc), lambda b: (b, 0, 0))],
        out_shape=[jax.ShapeDtypeStruct((batch, tp, GROUP_W), BF16),
                   jax.ShapeDtypeStruct((batch, tp, GROUP_W), F32),
                   jax.ShapeDtypeStruct((batch, 2 * GROUP_W, wc), F32)],
        compiler_params=_cparams(("parallel",)),
        name=f"attn_sample_g{gi}",
    )(q, tail_t, cache_t)


def _rwkv_body(zr_ref, shift0_ref, wkv0_ref, mu_ref, w0_ref, w2_ref, a0_ref, a2_ref, g2_ref, kk_ref, ka_ref,
               rk_ref, lnw_ref, lnb_ref, o_ref, wkv_ref, prev_ref, *, chunk, t_valid):
    c = pl.program_id(1)
    L = chunk
    n_seq = zr_ref.shape[0]
    heads = range(n_seq * RWKV_HEADS)
    seq_of = [h // RWKV_HEADS for h in heads]
    head_of = [h % RWKV_HEADS for h in heads]
    sls = [slice(head_of[h] * RWKV_HEAD, (head_of[h] + 1) * RWKV_HEAD) for h in heads]

    @pl.when(c == 0)
    def _():
        for b in range(n_seq):
            prev_ref[b] = shift0_ref[b]
        wkv_ref[...] = wkv0_ref[...]

    row = lax.broadcasted_iota(jnp.int32, (L, 1), 0)
    ti = lax.broadcasted_iota(jnp.int32, (L, L), 0)
    si = lax.broadcasted_iota(jnp.int32, (L, L), 1)
    incl = ti >= si
    strict = ti > si
    seqs = [_rwkv_columns(zr_ref[b], prev_ref.at[b], row, incl, mu_ref, w0_ref, w2_ref, a0_ref, a2_ref, g2_ref,
                          kk_ref, ka_ref, rk_ref, L, t_valid) for b in range(n_seq)]
    col = lambda name, h: seqs[seq_of[h]][name][:, sls[h]]

    sp, yp = STATE_PASSES, Y_PASSES
    kkh = [col('kk_raw', h) for h in heads]
    kkn = [x / jnp.maximum(jnp.sqrt(jnp.sum(x * x, axis=-1, keepdims=True)), 1e-12) for x in kkh]
    at = [-kkn[h] * col('p_prev', h) for h in heads]
    bt = [kkn[h] * col('a', h) * col('p_inv', h) for h in heads]
    rt = [col('rt_all', h) for h in heads]
    kt = [col('kt_all', h) for h in heads]
    vh = [col('v', h) for h in heads]
    at_s = [_parts(x, sp) for x in at]
    bt_s = [_parts(x, sp) for x in bt]
    kt_s = [_parts(x, sp) for x in kt]
    vh_s = [_parts(x, sp) for x in vh]
    rt_y = [_parts(x, yp) for x in rt]
    a_ab = [jnp.where(strict, _mmp(_dot_nt, at_s[h], bt_s[h]), 0.0) for h in heads]
    a_ak = [jnp.where(strict, _mmp(_dot_nt, at_s[h], kt_s[h]), 0.0) for h in heads]
    a_rb = [jnp.where(incl, _mmp(_dot_nt, rt_y[h], bt_s[h][:yp]), 0.0) for h in heads]
    a_rk = [jnp.where(incl, _mmp(_dot_nt, rt_y[h], kt_s[h][:yp]), 0.0) for h in heads]
    x = [jnp.concatenate([at[h], _mmp(_dot, _parts(a_ak[h], sp), vh_s[h])], axis=1) for h in heads]
    p = a_ab
    for level in range(max(int(math.log2(L)), 1)):
        if level > 0:
            p = [_mmp(_dot, ps, ps) for ps in p_s]
        p_s = [_parts(q, sp) for q in p]
        x = [x[h] + _mmp(_dot, p_s[h], _parts(x[h], sp)) for h in heads]
    x_s = [_parts(q, sp) for q in x]
    qy = [_mmp(_dot, _parts(a_rb[h], yp), x_s[h][:yp]) for h in heads]
    y0 = [qy[h][:, RWKV_HEAD:] + _mmp(_dot, _parts(a_rk[h], yp), vh_s[h][:yp]) for h in heads]
    qh = [rt[h] + qy[h][:, :RWKV_HEAD] for h in heads]
    s0 = [wkv_ref[seq_of[h], head_of[h]] * col('p_mid', h) for h in heads]
    s0_s = [_parts(q, sp) for q in s0]
    y = [_mmp(_dot_nt, _parts(qh[h], yp), s0_s[h][:yp]) + y0[h] for h in heads]
    wtb = [_mmp(_dot_tn, [q[:, :RWKV_HEAD] for q in x_s[h]], bt_s[h]) for h in heads]
    uv_s = [[jnp.concatenate([x_s[h][i][:, RWKV_HEAD:], vh_s[h][i]], axis=0) for i in range(sp)] for h in heads]
    bk_s = [[jnp.concatenate([bt_s[h][i], kt_s[h][i]], axis=0) for i in range(sp)] for h in heads]
    for h in heads:
        s_new = s0[h] + _mmp(_dot, s0_s[h], _parts(wtb[h], sp)) + _mmp(_dot_tn, uv_s[h], bk_s[h])
        wkv_ref[seq_of[h], head_of[h]] = s_new * col('p_incl', h)[L - 1:L, :]
    for h in heads:
        mu_y = jnp.mean(y[h], axis=-1, keepdims=True)
        yc = y[h] - mu_y
        var = jnp.mean(yc * yc, axis=-1, keepdims=True)
        yn = yc * lax.rsqrt(var + GN_EPS) * lnw_ref[:, sls[h]] + lnb_ref[:, sls[h]]
        bonus = jnp.sum(col('rk_all', h), axis=-1, keepdims=True) * vh[h]
        o_ref[seq_of[h], :, sls[h]] = ((yn + bonus) * col('g', h)).astype(o_ref.dtype)


def _rwkv_columns(zr, prev_ref, row, incl, mu_ref, w0_ref, w2_ref, a0_ref, a2_ref, g2_ref, kk_ref, ka_ref, rk_ref,
                  L, t_valid):
    mm = functools.partial(_mm3, _dot)
    prev = jnp.where(row == 0, prev_ref[...], pltpu.roll(zr, 1, 0))
    prev_ref[...] = zr[L - 1:L, :]
    zm = zr + (prev - zr) * mu_ref[...]

    r = zm[:, 0:RWKV_W]
    k = zm[:, RWKV_W:2 * RWKV_W]
    v = zm[:, 2 * RWKV_W:3 * RWKV_W]
    xwa = zm[:, 3 * RWKV_W:3 * RWKV_W + LORA_W]
    xg = zm[:, 3 * RWKV_W + LORA_W:]
    w_pre = w0_ref[...] + mm(jnp.tanh(xwa), w2_ref[...])
    softplus = jnp.maximum(-w_pre, 0.0) + jnp.log(1.0 + jnp.exp(-jnp.abs(w_pre)))
    w_log = -softplus - 0.5
    lw = -jnp.exp(w_log)
    a = _sigmoid(a0_ref[...] + mm(xwa, a2_ref[...]))
    g = mm(_sigmoid(xg), g2_ref[...])
    kk_raw = k * kk_ref[...]
    k2 = k * (1.0 + (a - 1.0) * ka_ref[...])
    if t_valid < L:
        live = row < t_valid
        lw = jnp.where(live, lw, 0.0)
        kk_raw = jnp.where(live, kk_raw, 0.0)
        k2 = jnp.where(live, k2, 0.0)
        v = jnp.where(live, v, 0.0)

    clw = _mmp(_dot, [jnp.where(incl, 1.0, 0.0).astype(BF16)], _parts(lw, 3))
    mid = max(L // 2 - 1, 0)
    clw_mid = clw[mid:mid + 1, :]
    rel = clw - clw_mid
    p_incl = jnp.exp(rel)
    p_inv = jnp.exp(-rel)
    return dict(kk_raw=kk_raw, a=a, v=v, g=g, p_incl=p_incl, p_inv=p_inv, p_prev=jnp.exp(rel - lw),
                p_mid=jnp.exp(clw_mid),
                rt_all=r * p_incl, kt_all=k2 * p_inv, rk_all=r * k2 * rk_ref[...])


def _rwkv(zr, shift0, wkv0, wts, chunk, t_valid):
    batch, t, _ = zr.shape
    n_chunks = t // chunk
    n_seq = RWKV_SEQS_PER_STEP
    assert (t_valid == t or n_chunks == 1) and batch % n_seq == 0
    body = functools.partial(_rwkv_body, chunk=chunk, t_valid=min(t_valid, chunk))
    vec = lambda width: pl.BlockSpec((1, width), lambda b, c: (0, 0))
    mat = lambda rows: pl.BlockSpec((rows, RWKV_W), lambda b, c: (0, 0))
    return pl.pallas_call(
        body,
        grid=(batch // n_seq, n_chunks),
        in_specs=[pl.BlockSpec((n_seq, chunk, RWKV_COLS), lambda b, c: (b, c, 0)),
                  pl.BlockSpec((n_seq, 1, RWKV_COLS), lambda b, c: (b, 0, 0)),
                  pl.BlockSpec((n_seq, RWKV_HEADS, RWKV_HEAD, RWKV_HEAD), lambda b, c: (b, 0, 0, 0)),
                  vec(RWKV_COLS), vec(RWKV_W), mat(LORA_W), vec(RWKV_W), mat(LORA_W), mat(GATE_LORA),
                  vec(RWKV_W), vec(RWKV_W), vec(RWKV_W), vec(RWKV_W), vec(RWKV_W)],
        out_specs=[pl.BlockSpec((n_seq, chunk, RWKV_W), lambda b, c: (b, c, 0)),
                   pl.BlockSpec((n_seq, RWKV_HEADS, RWKV_HEAD, RWKV_HEAD), lambda b, c: (b, 0, 0, 0))],
        out_shape=[jax.ShapeDtypeStruct((batch, t, RWKV_W), BF16),
                   jax.ShapeDtypeStruct((batch, RWKV_HEADS, RWKV_HEAD, RWKV_HEAD), F32)],
        scratch_shapes=[pltpu.VMEM((n_seq, 1, RWKV_COLS), F32)],
        compiler_params=_cparams(("parallel", "arbitrary")),
        name="rwkv_scan",
    )(zr, shift0, wkv0, *wts)


def _mix_out_body(x_ref, o0_ref, o1_ref, o2_ref, l0_ref, l1_ref, l2_ref, orw_ref, gate_ref, woa_ref, wor_ref,
                  wo_ref, gffn_ref, rwt_ref, rb_ref, cnt0_ref, *rest, dilated, n_alias):
    x1_ref, h2_ref, te_ref, tg_ref, rk_ref, cnt_ref = rest[n_alias:n_alias + 6]
    stage = list(rest[n_alias + 6:])

    def token_major(ref, gi):
        if not dilated:
            return ref[...].astype(F32)
        dil = ATTN_GROUPS[gi][1]
        if dil == 1:
            return ref[0, 0].astype(F32)
        st_ref = stage.pop()
        for r in range(dil):
            sub = ref[0, r].astype(F32)
            for s in range(GROUP_W // LANE):
                st_ref[s, pl.ds(r, ref.shape[2], stride=dil), :] = sub[:, s * LANE:(s + 1) * LANE]
        return jnp.concatenate([st_ref[s] for s in range(GROUP_W // LANE)], axis=1)

    l0, l1, l2 = token_major(l0_ref, 0), token_major(l1_ref, 1), token_major(l2_ref, 2)
    m = jnp.maximum(jnp.maximum(l0, l1), l2)
    e0, e1, e2 = jnp.exp(l0 - m), jnp.exp(l1 - m), jnp.exp(l2 - m)
    o_att = (e0 * token_major(o0_ref, 0) + e1 * token_major(o1_ref, 1) + e2 * token_major(o2_ref, 2)) \
        * (1.0 / (e0 + e1 + e2))
    gates = gate_ref[...].astype(F32)
    merged = gates[:, :D_MODEL] * _dot(o_att.astype(BF16), woa_ref[...]) \
        + gates[:, D_MODEL:] * _dot(orw_ref[...], wor_ref[...])
    x1 = x_ref[...] + _dot(merged.astype(BF16), wo_ref[...])
    x1_ref[...] = x1
    h2 = _rms(x1, gffn_ref[...])
    h2_ref[...] = h2

    logits = _mm3(_dot_nt, rwt_ref[...], h2) + rb_ref[...]
    e_iota = lax.broadcasted_iota(jnp.int32, logits.shape, 0)
    vals, idxs = [], []
    for _ in range(TOP_K):
        top = jnp.max(logits, axis=0, keepdims=True)
        idx = jnp.min(jnp.where(logits == top, e_iota, N_EXPERTS), axis=0, keepdims=True)
        vals.append(top)
        idxs.append(idx)
        logits = jnp.where(e_iota == idx, -jnp.inf, logits)
    exps = [jnp.exp(t - vals[0]) for t in vals]
    inv = 1.0 / (exps[0] + exps[1] + exps[2] + exps[3])
    te_ref[...] = jnp.concatenate(idxs, axis=0)
    tg_ref[...] = jnp.concatenate([e * inv for e in exps], axis=0)

    @pl.when(pl.program_id(0) == 0)
    def _():
        cnt_ref[...] = cnt0_ref[...]

    hits = [e_iota == idx for idx in idxs]
    onehot = jnp.where(hits[0] | hits[1] | hits[2] | hits[3], 1.0, 0.0)
    tm = onehot.shape[1]
    earlier = lax.broadcasted_iota(jnp.int32, (tm, tm), 0) < lax.broadcasted_iota(jnp.int32, (tm, tm), 1)
    before = cnt_ref[...] + _dot(onehot.astype(BF16), jnp.where(earlier, 1.0, 0.0).astype(BF16))
    rk_ref[...] = jnp.concatenate([jnp.sum(jnp.where(h, before, 0.0), axis=0, keepdims=True) for h in hits],
                                  axis=0).astype(jnp.int32)
    cnt_ref[...] += jnp.sum(onehot, axis=1, keepdims=True)


def _mix_out(x, o_g, lse_g, o_rwkv, gates, wts, cnt0, tm, n_all, row0, bufs):
    n = x.shape[0]
    blk0 = row0 // tm
    row = lambda i: (i, 0)
    const = lambda i: (0, 0)
    out_row = lambda i: (i + blk0, 0)
    out_col = lambda i: (0, i + blk0)
    tok = lambda w: pl.BlockSpec((tm, w), row)
    dilated = o_g[0].ndim == 4
    if dilated:
        tiles = o_g[0].shape[1] * o_g[0].shape[2] // tm
        sub = lambda i: (i // tiles, 0, i % tiles, 0)
        att_specs = [pl.BlockSpec((1, d, tm // d, GROUP_W), sub) for _, d in ATTN_GROUPS] * 2
        stage = [pltpu.VMEM((GROUP_W // LANE, tm, LANE), F32) for _, d in ATTN_GROUPS if d > 1] * 2
    else:
        att_specs, stage = [tok(GROUP_W)] * 6, []
    in_specs = [tok(D_MODEL)] + att_specs + [tok(RWKV_W), tok(GATE_COLS),
                pl.BlockSpec((GROUP_W, D_MODEL), const), pl.BlockSpec((RWKV_W, D_MODEL), const),
                pl.BlockSpec((D_MODEL, D_MODEL), const), pl.BlockSpec((1, D_MODEL), const),
                pl.BlockSpec((N_EXPERTS, D_MODEL), const), pl.BlockSpec((N_EXPERTS, 1), const),
                pl.BlockSpec((N_EXPERTS, 1), const)]
    args = [x, *o_g, *lse_g, o_rwkv, gates, *wts, cnt0]
    aliases = {}
    if bufs is not None:
        in_specs += [pl.BlockSpec(memory_space=pl.ANY)] * 5
        aliases = {len(args) + j: j for j in range(5)}
        args += list(bufs)
    *new_bufs, cnt = pl.pallas_call(
        functools.partial(_mix_out_body, dilated=dilated, n_alias=len(aliases)),
        grid=(n // tm,),
        scratch_shapes=stage,
        in_specs=in_specs,
        out_specs=[pl.BlockSpec((tm, D_MODEL), out_row), pl.BlockSpec((tm, D_MODEL), out_row),
                   pl.BlockSpec((TOP_K, tm), out_col), pl.BlockSpec((TOP_K, tm), out_col),
                   pl.BlockSpec((TOP_K, tm), out_col), pl.BlockSpec((N_EXPERTS, 1), const)],
        out_shape=[jax.ShapeDtypeStruct((n_all, D_MODEL), F32), jax.ShapeDtypeStruct((n_all, D_MODEL), F32),
                   jax.ShapeDtypeStruct((TOP_K, n_all), jnp.int32), jax.ShapeDtypeStruct((TOP_K, n_all), F32),
                   jax.ShapeDtypeStruct((TOP_K, n_all), jnp.int32), jax.ShapeDtypeStruct((N_EXPERTS, 1), F32)],
        input_output_aliases=aliases,
        compiler_params=_cparams(("arbitrary",)),
        name="mix_out",
    )(*args)
    return new_bufs, cnt


def _moe_dispatch_body(tab_ref, idx_hbm, h2_hbm, x_hbm, idx_smem, hbuf, zrow, isem, fsem, dsem, zsem, *, n_tiles):
    i = pl.program_id(0)
    slot = i % 2
    n_buf, n_sem = DISPATCH_LAG + 2, DISPATCH_LAG + 1
    hslot = i % n_buf
    dslot = i % n_sem

    def tile_fetch(tile, hs):
        return pltpu.make_async_copy(h2_hbm.at[pl.ds(pl.multiple_of(tile * MOE_TILE, MOE_TILE), MOE_TILE)],
                                     hbuf.at[hs], fsem.at[hs])

    def idx_copy(rec, s):
        return pltpu.make_async_copy(idx_hbm.at[pl.ds(pl.multiple_of(rec * IDX_REC, IDX_REC), IDX_REC)],
                                     idx_smem.at[pl.ds(pl.multiple_of(s * IDX_REC, IDX_REC), IDX_REC)],
                                     isem.at[s])

    def row_copies_start(s):
        def one(t, carry):
            for k in range(TOP_K):
                d = idx_smem[s * IDX_REC + k * MOE_TILE + t]
                pltpu.make_async_copy(hbuf.at[hslot, pl.ds(t, 1)], x_hbm.at[pl.ds(d, 1)], dsem.at[dslot]).start()
            return carry
        for t in range(MOE_TILE):
            one(t, 0)

    @pl.when(i == 0)
    def _():
        idx_copy(0, 0).start()
        tile_fetch(0, 0).start()

    @pl.when(i + 1 < n_tiles)
    def _():
        idx_copy(i + 1, 1 - slot).start()
        tile_fetch(i + 1, (i + 1) % n_buf).start()

    idx_copy(i, slot).wait()
    tile_fetch(i, hslot).wait()
    row_copies_start(slot)

    def wait_rows(tile):
        for _ in range(TOP_K):
            pltpu.make_async_copy(hbuf.at[0], x_hbm.at[pl.ds(0, MOE_TILE)], dsem.at[tile % n_sem]).wait()

    @pl.when(i >= DISPATCH_LAG)
    def _():
        wait_rows(i - DISPATCH_LAG)

    @pl.when(i == n_tiles - 1)
    def _():
        for back in range(DISPATCH_LAG - 1, -1, -1):
            wait_rows(i - back)
        zrow[...] = jnp.zeros_like(zrow)

        def zero_copy(dst_row):
            return pltpu.make_async_copy(zrow.at[pl.ds(0, 1)], x_hbm.at[pl.ds(dst_row, 1)], zsem)

        def per_expert(e, carry):
            first = tab_ref[e] + tab_ref[N_EXPERTS + e]
            last = tab_ref[e] + tab_ref[2 * N_EXPERTS + e]
            lax.fori_loop(first, last, lambda r, c: (zero_copy(r).start(), c)[1], 0)
            lax.fori_loop(first, last, lambda r, c: (zero_copy(r).wait(), c)[1], 0)
            return carry
        lax.fori_loop(0, N_EXPERTS, per_expert, 0)


def _moe_dispatch(h2, idx_rec, tables, n_rows):
    n_tiles = h2.shape[0] // MOE_TILE
    grid_spec = pltpu.PrefetchScalarGridSpec(
        num_scalar_prefetch=1,
        grid=(n_tiles,),
        in_specs=[pl.BlockSpec(memory_space=pl.ANY),
                  pl.BlockSpec(memory_space=pl.ANY)],
        out_specs=pl.BlockSpec(memory_space=pl.ANY),
        scratch_shapes=[pltpu.SMEM((2 * IDX_REC,), jnp.int32),
                        pltpu.VMEM((DISPATCH_LAG + 2, MOE_TILE, D_MODEL), F32),
                        pltpu.VMEM((8, D_MODEL), F32),
                        pltpu.SemaphoreType.DMA((2,)),
                        pltpu.SemaphoreType.DMA((DISPATCH_LAG + 2,)),
                        pltpu.SemaphoreType.DMA((DISPATCH_LAG + 1,)),
                        pltpu.SemaphoreType.DMA],
    )
    return pl.pallas_call(
        functools.partial(_moe_dispatch_body, n_tiles=n_tiles),
        grid_spec=grid_spec,
        out_shape=jax.ShapeDtypeStruct((n_rows, D_MODEL), F32),
        compiler_params=_cparams(("arbitrary",)),
        name="moe_dispatch",
    )(tables, idx_rec, h2)


def _moe_experts_body(be_ref, nused_ref, x_ref, w1_ref, b1_ref, w2_ref, b2_ref, y_ref, w1b, w2b):
    i = pl.program_id(0)

    @pl.when(i < nused_ref[0])
    def _():
        @pl.when((i == 0) | (be_ref[i] != be_ref[jnp.maximum(i - 1, 0)]))
        def _():
            w1b[...] = w1_ref[0].astype(BF16)
            w2b[...] = w2_ref[0].astype(BF16)

        xb = x_ref[...].astype(BF16)
        glu = jnp.minimum(_dot(xb, w1b[:, :D_MODEL]) + b1_ref[0, :, :D_MODEL], SWIGLU_LIMIT)
        lin = jnp.clip(_dot(xb, w1b[:, D_MODEL:]) + b1_ref[0, :, D_MODEL:], -SWIGLU_LIMIT, SWIGLU_LIMIT)
        act = glu * _sigmoid(SWIGLU_ALPHA * glu) * (lin + 1.0)
        y_ref[...] = _dot(act.astype(BF16), w2b[...]) + b2_ref[0]

    @pl.when(i >= nused_ref[0])
    def _():
        y_ref[...] = jnp.zeros_like(y_ref)


def _moe_experts(x_rows, block_e, n_used, w1, b1, w2, b2):
    n_blocks = block_e.shape[0]
    by_expert = lambda i, be, nu: (be[i], 0, 0)
    grid_spec = pltpu.PrefetchScalarGridSpec(
        num_scalar_prefetch=2,
        grid=(n_blocks,),
        in_specs=[pl.BlockSpec((MOE_BLOCK, D_MODEL), lambda i, be, nu: (jnp.minimum(i, nu[0] - 1), 0)),
                  pl.BlockSpec((1, D_MODEL, 2 * D_MODEL), by_expert),
                  pl.BlockSpec((1, 1, 2 * D_MODEL), by_expert),
                  pl.BlockSpec((1, D_MODEL, D_MODEL), by_expert),
                  pl.BlockSpec((1, 1, D_MODEL), by_expert)],
        out_specs=pl.BlockSpec((MOE_BLOCK, D_MODEL), lambda i, be, nu: (i, 0)),
        scratch_shapes=[pltpu.VMEM((D_MODEL, 2 * D_MODEL), BF16),
                        pltpu.VMEM((D_MODEL, D_MODEL), BF16)],
    )
    return pl.pallas_call(
        _moe_experts_body,
        grid_spec=grid_spec,
        out_shape=jax.ShapeDtypeStruct((n_blocks * MOE_BLOCK, D_MODEL), F32),
        compiler_params=_cparams(("arbitrary",)),
        name="moe_experts",
    )(block_e, n_used, x_rows, w1, b1, w2, b2)


def _route(top_e, rank, counts):
    n_tok = top_e.shape[1]
    counts = counts.reshape(N_EXPERTS).astype(jnp.int32)
    padded = (counts + MOE_BLOCK - 1) // MOE_BLOCK * MOE_BLOCK
    pad_end = jnp.cumsum(padded)
    pad_start = pad_end - padded
    experts = jnp.arange(N_EXPERTS, dtype=jnp.int32)
    dest = rank + jnp.sum(jnp.where(top_e[..., None] == experts, pad_start, 0), axis=-1)
    n_blocks = -(-(n_tok * TOP_K + N_EXPERTS * (MOE_BLOCK - 1)) // MOE_BLOCK)
    blk_row0 = jnp.arange(n_blocks, dtype=jnp.int32) * MOE_BLOCK
    block_e = jnp.minimum(jnp.sum(blk_row0[:, None] >= pad_end[None, :], axis=1), N_EXPERTS - 1).astype(jnp.int32)
    n_used = (pad_end[-1] // MOE_BLOCK).astype(jnp.int32).reshape(1)
    n_tiles = n_tok // MOE_TILE
    rec = dest.reshape(TOP_K, n_tiles, MOE_TILE).transpose(1, 0, 2).reshape(n_tiles, TOP_K * MOE_TILE)
    idx_rec = jnp.concatenate([rec, jnp.zeros((n_tiles, IDX_REC - TOP_K * MOE_TILE), jnp.int32)], axis=1).reshape(-1)
    tables = jnp.concatenate([pad_start, counts, padded]).astype(jnp.int32)
    return idx_rec, tables, block_e, n_used, n_blocks * MOE_BLOCK


def _tail_body(idx_hbm, y_hbm, x1_ref, tg_ref, pe_ref, gple_ref, wpg_ref, wp_ref, gfin_ref, yp_ref, ys_ref,
               idx_smem, gbuf, isem, gsem, *, n_tiles, n_prompt_tiles):
    i = pl.program_id(0)
    slot = i % 2

    def idx_copy(rec, s):
        return pltpu.make_async_copy(idx_hbm.at[pl.ds(pl.multiple_of(rec * IDX_REC, IDX_REC), IDX_REC)],
                                     idx_smem.at[pl.ds(pl.multiple_of(s * IDX_REC, IDX_REC), IDX_REC)],
                                     isem.at[s])

    @pl.when(i == 0)
    def _():
        idx_copy(0, 0).start()

    @pl.when(i < n_tiles)
    def _():
        idx_copy(i, slot).wait()

        @pl.when(i + 1 < n_tiles)
        def _():
            idx_copy(i + 1, 1 - slot).start()

        for t in range(MOE_TILE):
            for k in range(TOP_K):
                d = idx_smem[slot * IDX_REC + k * MOE_TILE + t]
                pltpu.make_async_copy(y_hbm.at[pl.ds(d, 1)], gbuf.at[slot, k, pl.ds(t, 1)], gsem.at[slot]).start()

    @pl.when(i >= 1)
    def _():
        prev = 1 - slot
        for k in range(TOP_K):
            pltpu.make_async_copy(y_hbm.at[pl.ds(0, MOE_TILE)], gbuf.at[prev, k], gsem.at[prev]).wait()
        tg = tg_ref[...]
        moe = tg[:, 0:1] * gbuf[prev, 0]
        for k in range(1, TOP_K):
            moe = moe + tg[:, k:k + 1] * gbuf[prev, k]
        x2 = x1_ref[...] + moe
        gate = _sigmoid(_dot(_rms(x2, gple_ref[...]).astype(BF16), wpg_ref[...]))
        x3 = x2 + gate * _dot(pe_ref[...].astype(BF16), wp_ref[...])
        y = _rms(x3, gfin_ref[...])

        @pl.when(i - 1 < n_prompt_tiles)
        def _():
            yp_ref[...] = y

        @pl.when(i - 1 >= n_prompt_tiles)
        def _():
            ys_ref[...] = y


def _tail(x1_all, y_rows, idx_rec, tg_t, pe_all, wts, n_p):
    n_all = x1_all.shape[0]
    n_tiles = n_all // MOE_TILE
    n_prompt_tiles = n_p // MOE_TILE
    row = lambda i: (jnp.maximum(i - 1, 0), 0)
    const = lambda i: (0, 0)
    body = functools.partial(_tail_body, n_tiles=n_tiles, n_prompt_tiles=n_prompt_tiles)
    return pl.pallas_call(
        body,
        grid=(n_tiles + 1,),
        in_specs=[pl.BlockSpec(memory_space=pl.ANY), pl.BlockSpec(memory_space=pl.ANY),
                  pl.BlockSpec((MOE_TILE, D_MODEL), row), pl.BlockSpec((MOE_TILE, TOP_K), row),
                  pl.BlockSpec((MOE_TILE, PLE_DIM), row), pl.BlockSpec((1, D_MODEL), const),
                  pl.BlockSpec((D_MODEL, D_MODEL), const), pl.BlockSpec((PLE_DIM, D_MODEL), const),
                  pl.BlockSpec((1, D_MODEL), const)],
        out_specs=[pl.BlockSpec((MOE_TILE, D_MODEL),
                                lambda i: (jnp.clip(i - 1, 0, n_prompt_tiles - 1), 0)),
                   pl.BlockSpec((MOE_TILE, D_MODEL), lambda i: (jnp.maximum(i - 1 - n_prompt_tiles, 0), 0))],
        out_shape=[jax.ShapeDtypeStruct((n_p, D_MODEL), F32),
                   jax.ShapeDtypeStruct((n_all - n_p, D_MODEL), F32)],
        scratch_shapes=[pltpu.SMEM((2 * IDX_REC,), jnp.int32),
                        pltpu.VMEM((2, TOP_K, MOE_TILE, D_MODEL), F32),
                        pltpu.SemaphoreType.DMA((2,)),
                        pltpu.SemaphoreType.DMA((2,))],
        compiler_params=_cparams(("arbitrary",)),
        name="tail",
    )(idx_rec, y_rows, x1_all, tg_t, pe_all, *wts)


RWKV_CHUNK = 128
RWKV_CHUNK_SAMPLE = 8
RWKV_SEQS_PER_STEP = 4
STATE_PASSES = 1
Y_PASSES = 1
TM_PROMPT = 512


def kernel(x_prompt, x_sample, p_prompt, p_sample, cache_kv_w128, cache_kv_w512, cache_kv_w2048, state_rwkv_shift, state_rwkv_wkv, norm_mix_g, w_in, rwkv_mu, rwkv_w0, rwkv_w2, rwkv_a0, rwkv_a2, rwkv_g2, rwkv_k_k, rwkv_k_a, rwkv_r_k, rwkv_ln_w, rwkv_ln_b, w_out_attn, w_out_rwkv, w_out, norm_ffn_g, router_w, router_b, moe_w1, moe_b1, moe_w2, moe_b2, norm_ple_g, w_ple, w_ple_gate, norm_final_g):
    bp, seq, _ = x_prompt.shape
    bs, t_s, _ = x_sample.shape
    n_p, n_s = bp * seq, bs * t_s
    n_all = n_p + n_s
    assert w_in.shape[0] == 1, "single layer"
    caches = (cache_kv_w128, cache_kv_w512, cache_kv_w2048)

    row = lambda a: a.reshape(1, -1)
    w_in_b = w_in[0].astype(BF16)
    zeros64 = jnp.zeros((LORA_W // 2, RWKV_W), F32)
    rwkv_wts = (row(rwkv_mu[0]), row(rwkv_w0[0]), jnp.concatenate([rwkv_w2[0], zeros64], axis=0),
                row(rwkv_a0[0]), jnp.concatenate([zeros64, rwkv_a2[0]], axis=0), rwkv_g2[0],
                row(rwkv_k_k[0]), row(rwkv_k_a[0]), row(rwkv_r_k[0]), row(rwkv_ln_w[0]), row(rwkv_ln_b[0]))
    mix_wts = (w_out_attn[0].astype(BF16), w_out_rwkv[0].astype(BF16), w_out[0].astype(BF16),
               row(norm_ffn_g[0]), router_w[0].T, router_b[0].reshape(N_EXPERTS, 1))
    tail_wts = (row(norm_ple_g[0]), w_ple_gate[0].astype(BF16), w_ple[0].astype(BF16), row(norm_final_g))
    g_mix = row(norm_mix_g[0])

    xp = x_prompt.reshape(n_p, D_MODEL)
    qd_p, kvd_p, kvt_p, zr_p, gate_p = _in_proj_dilated(xp, g_mix, w_in_b, TM_PROMPT, bp, seq)
    att_p = [_attn_prompt(qd_p[gi], kvd_p[gi], gi) for gi in range(N_GROUPS)]
    orw_p, wkv_p = _rwkv(zr_p.reshape(bp, seq, RWKV_COLS), jnp.zeros((bp, 1, RWKV_COLS), F32),
                         jnp.zeros((bp, RWKV_HEADS, RWKV_HEAD, RWKV_HEAD), F32), rwkv_wts, RWKV_CHUNK, seq)
    bufs, counts = _mix_out(xp, [a[0] for a in att_p], [a[1] for a in att_p], orw_p.reshape(n_p, RWKV_W), gate_p,
                            mix_wts, jnp.zeros((N_EXPERTS, 1), F32), TM_PROMPT, n_all, 0, None)

    xs = x_sample.reshape(n_s, D_MODEL)
    q_s, kv_s, zr_s, gate_s = _in_proj(xs, g_mix, w_in_b, n_s)
    t_pad = 8
    q_s3 = jnp.pad(q_s.reshape(bs, t_s, Q_COLS), ((0, 0), (0, t_pad - t_s), (0, 0)))
    feat_major = lambda a: jnp.swapaxes(a, 1, 2)
    tail_t = jnp.pad(feat_major(kv_s.reshape(bs, t_s, KV_COLS)), ((0, 0), (0, 0), (LANE - t_s, 0)))
    att_s, new_caches = [], []
    for gi in range(N_GROUPS):
        wc = caches[gi].shape[2]
        o, lse, newc = _attn_sample(q_s3, tail_t, feat_major(caches[gi].reshape(bs, wc, 2 * GROUP_W)), gi, t_s)
        att_s.append((o[:, :t_s].reshape(n_s, GROUP_W), lse[:, :t_s].reshape(n_s, GROUP_W)))
        new_caches.append(feat_major(newc).reshape(1, bs, wc, 2, HEADS_PER_GROUP, HEAD_DIM))
    zr_s3 = jnp.pad(zr_s.reshape(bs, t_s, RWKV_COLS), ((0, 0), (0, RWKV_CHUNK_SAMPLE - t_s), (0, 0)))
    orw_s, wkv_s = _rwkv(zr_s3, state_rwkv_shift[0].reshape(bs, 1, RWKV_COLS), state_rwkv_wkv[0], rwkv_wts,
                         RWKV_CHUNK_SAMPLE, t_s)
    bufs, counts = _mix_out(xs, [a[0] for a in att_s], [a[1] for a in att_s], orw_s[:, :t_s].reshape(n_s, RWKV_W),
                            gate_s, mix_wts, counts, n_s, n_all, n_p, bufs)
    x1_all, h2_all, top_e, top_g, rank = bufs

    idx_rec, tables, block_e, n_used, n_rows = _route(top_e, rank, counts)
    x_rows = _moe_dispatch(h2_all, idx_rec, tables, n_rows)
    y_rows = _moe_experts(x_rows, block_e, n_used, moe_w1[0], moe_b1[0].reshape(N_EXPERTS, 1, 2 * D_MODEL),
                          moe_w2[0], moe_b2[0].reshape(N_EXPERTS, 1, D_MODEL))
    pe_all = jnp.concatenate([p_prompt[0].reshape(n_p, PLE_DIM), p_sample[0].reshape(n_s, PLE_DIM)], axis=0)
    y_p, y_s = _tail(x1_all, y_rows, idx_rec, top_g.T, pe_all, tail_wts, n_p)

    kv_out_p = [feat_major(t).reshape(1, bp, t.shape[2], 2, HEADS_PER_GROUP, HEAD_DIM) for t in kvt_p]
    shift_p = zr_p.reshape(bp, seq, RWKV_COLS)[:, -1][None]
    shift_s = zr_s.reshape(bs, t_s, RWKV_COLS)[:, -1][None]
    return (y_p.reshape(bp, seq, D_MODEL), y_s.reshape(bs, t_s, D_MODEL),
            kv_out_p[0], kv_out_p[1], kv_out_p[2], shift_p, wkv_p[None],
            new_caches[0], new_caches[1], new_caches[2], shift_s, wkv_s[None])
```
